```python
import jax, jax.numpy as jnp
from jax import lax
import numpy as np

D_MODEL = 1024
BATCH = 4
SEQ = 4096
DEPTH = 1

MEM_LEN = 256
EPS = 1e-5
NEG = -1e30

RET_HEADS = 4
RET_DK = 64
RET_DV = 128
RET_CHUNK = 128
MOBA_HEADS = 8
MOBA_HD = 64
MOBA_BLOCK = 256
MOBA_TOPK = 3
MOBA_QCHUNK = 32
MEM_HEADS = 4
MEM_HD = 128
N_BRANCH = 3
N_EXPERTS = 32
TOP_K = 4
D_FF = D_MODEL
SWIGLU_LIMIT = 7.0
SWIGLU_ALPHA = 1.702
MOE_BLOCK = 256

RET_Q = RET_HEADS * RET_DK
RET_V = RET_HEADS * RET_DV
MOBA_W = MOBA_HEADS * MOBA_HD
MEM_W = MEM_HEADS * MEM_HD
IN_SPLITS = (RET_Q, RET_Q, RET_V, RET_V, MOBA_W, MOBA_W, MOBA_W, MEM_W, N_BRANCH * D_MODEL)
D_IN = sum(IN_SPLITS)
SPLIT_POINTS = tuple(int(v) for v in np.cumsum(IN_SPLITS)[:-1])

kernel_name = "hybrid_retention_moba_memory_moe"


def rms_norm(x, g):
    xf = x.astype(jnp.float32)
    y = xf * lax.rsqrt(jnp.mean(xf * xf, axis=-1, keepdims=True) + EPS)
    return (y * g.astype(jnp.float32)).astype(x.dtype)


def retention(q, k, v):
    B, S, H, dk = q.shape
    dv = v.shape[-1]
    C = RET_CHUNK
    nc = S // C
    log_g = jnp.log1p(-jnp.exp2(-5.0 - jnp.arange(H, dtype=jnp.float32)))
    i = jnp.arange(C, dtype=jnp.float32)
    diff = i[:, None] - i[None, :]
    decay_in = jnp.where(diff >= 0, jnp.exp(jnp.maximum(diff, 0.0)[None] * log_g[:, None, None]), 0.0).astype(q.dtype)
    decay_k = jnp.exp((C - 1 - i)[:, None] * log_g[None, :]).astype(q.dtype)
    decay_q = jnp.exp((i + 1)[:, None] * log_g[None, :]).astype(q.dtype)
    decay_chunk = jnp.exp(C * log_g).astype(v.dtype)
    qc = q.reshape(B, nc, C, H, dk)
    kc = k.reshape(B, nc, C, H, dk)
    vc = v.reshape(B, nc, C, H, dv)
    scores = jnp.einsum("bnihd,bnjhd->bnhij", qc, kc) * decay_in
    intra = jnp.einsum("bnhij,bnjhe->bnihe", scores, vc)
    upd = jnp.einsum("bnjhd,bnjhe->nbhde", kc * decay_k[:, :, None], vc)

    def step(state, u):
        return decay_chunk[None, :, None, None] * state + u, state

    _, prev = lax.scan(step, jnp.zeros((B, H, dk, dv), v.dtype), upd)
    cross = jnp.einsum("bnihd,nbhde->bnihe", qc * decay_q[:, :, None], prev)
    return (intra + cross).reshape(B, S, H, dv)


def moba_attention(q, k, v):
    B, H, S, d = q.shape
    BS = MOBA_BLOCK
    QC = MOBA_QCHUNK
    nb = -(-S // BS)
    n_sel = min(MOBA_TOPK, nb)
    pad = ((0, 0), (0, 0), (0, nb * BS - S), (0, 0))
    kb = jnp.pad(k, pad).reshape(B, H, nb, BS, d)
    vb = jnp.pad(v, pad).reshape(B, H, nb, BS, d)
    k_mean = jnp.mean(kb.astype(jnp.float32), axis=3)
    q_blk = jnp.arange(S) // BS
    gate = jnp.einsum("bhsd,bhnd->bhsn", q.astype(jnp.float32), k_mean)
    past = jnp.arange(nb)[None, :] < q_blk[:, None]
    gate = jnp.where(past, gate, -jnp.inf)
    _, idx = lax.top_k(gate, n_sel)
    valid = idx < q_blk[:, None]
    slopes = jnp.exp2(-8.0 * (jnp.arange(H, dtype=jnp.float32) + 1.0) / H)
    scale = d ** -0.5
    nq = S // QC
    q_ch = q.reshape(B, H, nq, QC, d).transpose(2, 0, 1, 3, 4)
    idx_ch = idx.reshape(B, H, nq, QC, n_sel).transpose(2, 0, 1, 3, 4)
    valid_ch = valid.reshape(B, H, nq, QC, n_sel).transpose(2, 0, 1, 3, 4)
    starts = jnp.arange(nq, dtype=jnp.int32) * QC
    bi = jnp.arange(B)[:, None, None, None]
    hi = jnp.arange(H)[None, :, None, None]
    kpos = jnp.arange(BS)

    def chunk(args):
        qc, idc, vld, start = args
        t = start + jnp.arange(QC)
        bid = start // BS
        k_own = lax.dynamic_index_in_dim(kb, bid, axis=2, keepdims=False)
        v_own = lax.dynamic_index_in_dim(vb, bid, axis=2, keepdims=False)
        dist_own = (t[:, None] - (bid * BS + kpos)[None, :]).astype(jnp.float32)
        sc_own = jnp.einsum("bhqd,bhkd->bhqk", qc, k_own).astype(jnp.float32) * scale - slopes[:, None, None] * dist_own
        sc_own = jnp.where(dist_own >= 0, sc_own, NEG)
        k_sel = kb[bi, hi, idc]
        v_sel = vb[bi, hi, idc]
        dist_sel = (t[None, None, :, None, None] - (idc[..., None] * BS + kpos)).astype(jnp.float32)
        sc_sel = jnp.einsum("bhqd,bhqjkd->bhqjk", qc, k_sel).astype(jnp.float32) * scale - slopes[None, :, None, None, None] * dist_sel
        sc_sel = jnp.where(vld[..., None], sc_sel, NEG)
        sc = jnp.concatenate([sc_own, sc_sel.reshape(B, H, QC, n_sel * BS)], axis=-1)
        p = jax.nn.softmax(sc, axis=-1).astype(v.dtype)
        p_own = p[..., :BS]
        p_sel = p[..., BS:].reshape(B, H, QC, n_sel, BS)
        return jnp.einsum("bhqk,bhkd->bhqd", p_own, v_own) + jnp.einsum("bhqjk,bhqjkd->bhqd", p_sel, v_sel)

    out = lax.map(chunk, (q_ch, idx_ch, valid_ch, starts))
    return out.transpose(1, 0, 3, 2, 4).reshape(B, S, H * d)


def hybrid_mixer(x, mem, g_mix, w_in, b_gate, g_ret_out, g_moba_q, g_moba_k, g_mem, w_mem_kv, g_mem_q, g_mem_k, w_br_ret, w_br_moba, w_br_mem, w_out):
    B, S, D = x.shape
    h = rms_norm(x, g_mix)
    rq, rk, rv, rg, mq, mk, mv, cq, gl = jnp.split(h @ w_in, SPLIT_POINTS, axis=-1)
    o_ret = retention(rq.reshape(B, S, RET_HEADS, RET_DK), rk.reshape(B, S, RET_HEADS, RET_DK) * (RET_DK ** -0.5), rv.reshape(B, S, RET_HEADS, RET_DV))
    o_ret = (rms_norm(o_ret, g_ret_out) * jax.nn.silu(rg.reshape(B, S, RET_HEADS, RET_DV))).reshape(B, S, RET_V)
    qm = rms_norm(mq.reshape(B, S, MOBA_HEADS, MOBA_HD), g_moba_q).transpose(0, 2, 1, 3)
    km = rms_norm(mk.reshape(B, S, MOBA_HEADS, MOBA_HD), g_moba_k).transpose(0, 2, 1, 3)
    vm = mv.reshape(B, S, MOBA_HEADS, MOBA_HD).transpose(0, 2, 1, 3)
    o_moba = moba_attention(qm, km, vm)
    M = mem.shape[1]
    mk_, mv_ = jnp.split(rms_norm(mem, g_mem) @ w_mem_kv, 2, axis=-1)
    qc = rms_norm(cq.reshape(B, S, MEM_HEADS, MEM_HD), g_mem_q)
    kc = rms_norm(mk_.reshape(B, M, MEM_HEADS, MEM_HD), g_mem_k)
    vc = mv_.reshape(B, M, MEM_HEADS, MEM_HD)
    sc = jnp.einsum("bshd,bmhd->bhsm", qc, kc).astype(jnp.float32) * (MEM_HD ** -0.5)
    p = jax.nn.softmax(sc, axis=-1).astype(vc.dtype)
    o_mem = jnp.einsum("bhsm,bmhd->bshd", p, vc).reshape(B, S, MEM_W)
    gates = jax.nn.sigmoid((gl + b_gate).astype(jnp.float32)).astype(x.dtype).reshape(B, S, N_BRANCH, D)
    y = gates[:, :, 0] * (o_ret @ w_br_ret) + gates[:, :, 1] * (o_moba @ w_br_moba) + gates[:, :, 2] * (o_mem @ w_br_mem)
    return x + y @ w_out


def moe_ffn(x, g_ffn, w_router, b_router, w_mlp1, b_mlp1, w_mlp2, b_mlp2):
    B, S, D = x.shape
    N = B * S
    xt = rms_norm(x, g_ffn).reshape(N, D)
    logits = (xt @ w_router + b_router).astype(jnp.float32)
    top_v, top_e = lax.top_k(logits, TOP_K)
    wts = jax.nn.softmax(top_v, axis=-1).astype(x.dtype)
    P = N * TOP_K
    T = MOE_BLOCK
    e_flat = top_e.reshape(P)
    tok_flat = jnp.arange(P, dtype=jnp.int32) // TOP_K
    w_flat = wts.reshape(P)
    order = jnp.argsort(e_flat)
    e_sorted = e_flat[order]
    tok_sorted = tok_flat[order]
    w_sorted = w_flat[order]
    counts = jnp.bincount(e_flat, length=N_EXPERTS)
    starts = jnp.cumsum(counts) - counts
    padded = ((counts + T - 1) // T) * T
    pad_ends = jnp.cumsum(padded)
    pad_starts = pad_ends - padded
    dest = pad_starts[e_sorted] + (jnp.arange(P) - starts[e_sorted])
    NB = -(-P // T) + N_EXPERTS
    R = NB * T
    slot_tok = jnp.full((R,), N, dtype=jnp.int32).at[dest].set(tok_sorted)
    slot_w = jnp.zeros((R,), x.dtype).at[dest].set(w_sorted)
    block_e = jnp.minimum(jnp.searchsorted(pad_ends, jnp.arange(NB) * T, side="right"), N_EXPERTS - 1)
    x_pad = jnp.concatenate([xt, jnp.zeros((1, D), xt.dtype)], axis=0)
    xb = x_pad[slot_tok].reshape(NB, T, D)

    def expert_block(args):
        xblk, e = args
        hdn = xblk @ w_mlp1[e] + b_mlp1[e]
        x_glu = jnp.minimum(hdn[:, ::2], SWIGLU_LIMIT)
        x_lin = jnp.clip(hdn[:, 1::2], -SWIGLU_LIMIT, SWIGLU_LIMIT)
        act = x_glu * jax.nn.sigmoid(SWIGLU_ALPHA * x_glu) * (x_lin + 1.0)
        return act @ w_mlp2[e] + b_mlp2[e]

    yb = lax.map(expert_block, (xb, block_e)).reshape(R, D) * slot_w[:, None]
    y = jnp.zeros((N + 1, D), x.dtype).at[slot_tok].add(yb)[:N]
    return x + y.reshape(B, S, D)


def setup_inputs(seed: int = 0) -> dict:
    key = jax.random.key(seed)
    ks = jax.random.split(key, 24)
    f32 = jnp.float32

    def nrm(k, shape, fan_in):
        return jax.random.normal(k, shape, f32) * (fan_in ** -0.5)

    def gain(k, shape):
        return 1.0 + 0.02 * jax.random.normal(k, shape, f32)

    L = DEPTH
    return {
        "x": jax.random.normal(ks[0], (BATCH, SEQ, D_MODEL), f32),
        "mem": jax.random.normal(ks[1], (BATCH, MEM_LEN, D_MODEL), f32),
        "g_mix": gain(ks[2], (L, D_MODEL)),
        "w_in": nrm(ks[3], (L, D_MODEL, D_IN), D_MODEL),
        "b_gate": 0.01 * jax.random.normal(ks[4], (L, N_BRANCH * D_MODEL), f32),
        "g_ret_out": gain(ks[5], (L, RET_DV)),
        "g_moba_q": gain(ks[6], (L, MOBA_HD)),
        "g_moba_k": gain(ks[7], (L, MOBA_HD)),
        "g_mem": gain(ks[8], (L, D_MODEL)),
        "w_mem_kv": nrm(ks[9], (L, D_MODEL, 2 * MEM_W), D_MODEL),
        "g_mem_q": gain(ks[10], (L, MEM_HD)),
        "g_mem_k": gain(ks[11], (L, MEM_HD)),
        "w_br_ret": nrm(ks[12], (L, RET_V, D_MODEL), RET_V),
        "w_br_moba": nrm(ks[13], (L, MOBA_W, D_MODEL), MOBA_W),
        "w_br_mem": nrm(ks[14], (L, MEM_W, D_MODEL), MEM_W),
        "w_out": nrm(ks[15], (L, D_MODEL, D_MODEL), D_MODEL),
        "g_ffn": gain(ks[16], (L, D_MODEL)),
        "w_router": nrm(ks[17], (L, D_MODEL, N_EXPERTS), D_MODEL),
        "b_router": 0.01 * jax.random.normal(ks[18], (L, N_EXPERTS), f32),
        "w_mlp1": nrm(ks[19], (L, N_EXPERTS, D_MODEL, 2 * D_FF), D_MODEL),
        "b_mlp1": 0.01 * jax.random.normal(ks[20], (L, N_EXPERTS, 2 * D_FF), f32),
        "w_mlp2": nrm(ks[21], (L, N_EXPERTS, D_FF, D_MODEL), D_FF),
        "b_mlp2": 0.01 * jax.random.normal(ks[22], (L, N_EXPERTS, D_MODEL), f32),
    }


def reference(x, mem, g_mix, w_in, b_gate, g_ret_out, g_moba_q, g_moba_k, g_mem, w_mem_kv, g_mem_q, g_mem_k, w_br_ret, w_br_moba, w_br_mem, w_out, g_ffn, w_router, b_router, w_mlp1, b_mlp1, w_mlp2, b_mlp2):
    for l in range(DEPTH):
        x = hybrid_mixer(x, mem, g_mix[l], w_in[l], b_gate[l], g_ret_out[l], g_moba_q[l], g_moba_k[l], g_mem[l], w_mem_kv[l], g_mem_q[l], g_mem_k[l], w_br_ret[l], w_br_moba[l], w_br_mem[l], w_out[l])
        x = moe_ffn(x, g_ffn[l], w_router[l], b_router[l], w_mlp1[l], b_mlp1[l], w_mlp2[l], b_mlp2[l])
    return x
```

```python
import functools

import jax
import jax.numpy as jnp
import numpy as np
from jax import lax
from jax.experimental import pallas as pl
from jax.experimental.pallas import tpu as pltpu

F32 = jnp.float32
BF16 = jnp.bfloat16
I32 = jnp.int32

EPS = 1e-5
NEG = -1e30

RET_HEADS = 4
RET_DK = 64
RET_DV = 128
RET_CHUNK = 128
MOBA_HEADS = 8
MOBA_HD = 64
MOBA_BLOCK = 256
MOBA_TOPK = 3
MEM_HEADS = 4
MEM_HD = 128
N_BRANCH = 3
N_EXPERTS = 32
TOP_K = 4
SWIGLU_LIMIT = 7.0
SWIGLU_ALPHA = 1.702

RET_Q = RET_HEADS * RET_DK
RET_V = RET_HEADS * RET_DV
MOBA_W = MOBA_HEADS * MOBA_HD
MEM_W = MEM_HEADS * MEM_HD
MIX_W = 2 * RET_Q + 2 * RET_V + 3 * MOBA_W + MEM_W

LANES = 128
MOBA_PAIRS = MOBA_HEADS // 2
BIAS_LANE0 = MOBA_HD
ONE_LANE = 80
VSUM_LANE = MOBA_HD

MOE_T = 256
TM_PROJ = 512
TM_MERGE = 512
TQ_RET = 512
T_DISPATCH = 256
T_COMBINE = 256

_NT = (((1,), (1,)), ((), ()))
_TN = (((0,), (0,)), ((), ()))


def _rms(x, g):
    return x * lax.rsqrt(jnp.mean(x * x, axis=-1, keepdims=True) + EPS) * g


def _dot(a, b):
    return jnp.dot(a, b, preferred_element_type=F32)


def _cparams(sem, vmem_mb):
    return pltpu.CompilerParams(dimension_semantics=sem, vmem_limit_bytes=vmem_mb * 1024 * 1024)


def _memkv_kernel(mem_ref, g_ref, w_ref, gk_ref, k_ref, v_ref):
    m = _rms(mem_ref[0], g_ref[...]).astype(BF16)
    kv = _dot(m, w_ref[...])
    ks = [_rms(kv[:, h * MEM_HD:(h + 1) * MEM_HD], gk_ref[...]) for h in range(MEM_HEADS)]
    k_ref[0] = jnp.concatenate(ks, axis=-1).astype(BF16)
    v_ref[0] = kv[:, MEM_W:].astype(BF16)


def _memkv(mem, g_mem, w_mem_kv, g_mem_k):
    B, M, D = mem.shape
    return pl.pallas_call(
        _memkv_kernel,
        grid=(B,),
        in_specs=[
            pl.BlockSpec((1, M, D), lambda b: (b, 0, 0)),
            pl.BlockSpec((1, D), lambda b: (0, 0)),
            pl.BlockSpec((D, 2 * MEM_W), lambda b: (0, 0)),
            pl.BlockSpec((1, MEM_HD), lambda b: (0, 0)),
        ],
        out_specs=[
            pl.BlockSpec((1, M, MEM_W), lambda b: (b, 0, 0)),
            pl.BlockSpec((1, M, MEM_W), lambda b: (b, 0, 0)),
        ],
        out_shape=[jax.ShapeDtypeStruct((B, M, MEM_W), BF16)] * 2,
        compiler_params=_cparams(("arbitrary",), 32),
        name="memkv",
    )(mem, g_mem.reshape(1, D), w_mem_kv.astype(BF16), g_mem_k.reshape(1, MEM_HD))


def _head_pair_norm(a2, g2, lane):
    sq = a2 * a2
    lo = lane < MOBA_HD
    ss_lo = jnp.sum(jnp.where(lo, sq, 0.0), axis=-1, keepdims=True)
    ss_hi = jnp.sum(jnp.where(lo, 0.0, sq), axis=-1, keepdims=True)
    inv = jnp.where(lo, lax.rsqrt(ss_lo / MOBA_HD + EPS), lax.rsqrt(ss_hi / MOBA_HD + EPS))
    return a2 * inv * g2


def _inproj_kernel(slopes, seq_tiles, x_ref, gmix_ref, w_ref, gq_ref, gk_ref, gc_ref,
                   rq_ref, rk_ref, rv_ref, rg_ref, mq_ref, mk_ref, mv_ref, cq_ref, kmean_ref):
    tm = x_ref.shape[0]
    blocks_per_tile = tm // MOBA_BLOCK
    h = _rms(x_ref[...], gmix_ref[...]).astype(BF16)
    col = [0]

    def proj(width):
        a = _dot(h, w_ref[:, col[0]:col[0] + width])
        col[0] += width
        return a

    a = proj(2 * RET_Q)
    for hh in range(RET_HEADS):
        rq_ref[0, hh] = a[:, hh * RET_DK:(hh + 1) * RET_DK].astype(BF16)
        rk_ref[0, hh] = (a[:, RET_Q + hh * RET_DK:RET_Q + (hh + 1) * RET_DK] * (RET_DK ** -0.5)).astype(BF16)
    rv_ref[...] = proj(RET_V).astype(BF16)
    rg_ref[...] = proj(RET_V).astype(BF16)

    lane = lax.broadcasted_iota(I32, (tm, LANES), 1)
    row = lax.broadcasted_iota(I32, (tm, LANES), 0)
    lo = lane < MOBA_HD
    q_tail = jnp.where(lane == ONE_LANE, 1.0, 0.0)
    a = proj(MOBA_W)
    for p in range(MOBA_PAIRS):
        n2 = _head_pair_norm(a[:, p * LANES:(p + 1) * LANES], gq_ref[...], lane) * (MOBA_HD ** -0.5)
        mq_ref[0, 2 * p] = jnp.where(lo, n2, q_tail).astype(BF16)
        mq_ref[0, 2 * p + 1] = jnp.where(lo, pltpu.roll(n2, MOBA_HD, 1), q_tail).astype(BF16)
    blk = (pl.program_id(0) % seq_tiles) * blocks_per_tile + row // MOBA_BLOCK
    onehot_tail = jnp.where(lane == BIAS_LANE0 + blk, 1.0, 0.0)
    off = (row % MOBA_BLOCK).astype(F32)
    a = proj(MOBA_W)
    for p in range(MOBA_PAIRS):
        n2 = _head_pair_norm(a[:, p * LANES:(p + 1) * LANES], gk_ref[...], lane)
        for j in range(blocks_per_tile):
            kmean_ref[0, 0, p, j:j + 1, :] = jnp.mean(n2[j * MOBA_BLOCK:(j + 1) * MOBA_BLOCK], axis=0, keepdims=True)
        for s, src in ((0, n2), (1, pltpu.roll(n2, MOBA_HD, 1))):
            tail = jnp.where(lane == ONE_LANE, slopes[2 * p + s] * off, onehot_tail)
            mk_ref[0, 2 * p + s] = jnp.where(lo, src, tail).astype(BF16)
    v_tail = jnp.where(lane == VSUM_LANE, 1.0, 0.0)
    a = proj(MOBA_W)
    for p in range(MOBA_PAIRS):
        a2 = a[:, p * LANES:(p + 1) * LANES]
        mv_ref[0, 2 * p] = jnp.where(lo, a2, v_tail).astype(BF16)
        mv_ref[0, 2 * p + 1] = jnp.where(lo, pltpu.roll(a2, MOBA_HD, 1), v_tail).astype(BF16)
    a = proj(MEM_W)
    cq = [_rms(a[:, hh * MEM_HD:(hh + 1) * MEM_HD], gc_ref[...]) for hh in range(MEM_HEADS)]
    cq_ref[...] = jnp.concatenate(cq, axis=-1).astype(BF16)


def _inproj(x2, B, S, g_mix, w_mix, g_moba_q, g_moba_k, g_mem_q, slopes):
    N, D = x2.shape
    tm = TM_PROJ
    nS = S // tm
    bpt = tm // MOBA_BLOCK
    tok = lambda i: (i, 0)
    headmaj = lambda i: (i // nS, 0, i % nS, 0)
    g2 = lambda g: jnp.concatenate([g, g]).reshape(1, LANES)
    outs = pl.pallas_call(
        functools.partial(_inproj_kernel, slopes, nS),
        grid=(N // tm,),
        in_specs=[
            pl.BlockSpec((tm, D), tok),
            pl.BlockSpec((1, D), lambda i: (0, 0)),
            pl.BlockSpec((D, MIX_W), lambda i: (0, 0)),
            pl.BlockSpec((1, LANES), lambda i: (0, 0)),
            pl.BlockSpec((1, LANES), lambda i: (0, 0)),
            pl.BlockSpec((1, MEM_HD), lambda i: (0, 0)),
        ],
        out_specs=[
            pl.BlockSpec((1, RET_HEADS, tm, RET_DK), headmaj),
            pl.BlockSpec((1, RET_HEADS, tm, RET_DK), headmaj),
            pl.BlockSpec((tm, RET_V), tok),
            pl.BlockSpec((tm, RET_V), tok),
            pl.BlockSpec((1, MOBA_HEADS, tm, LANES), headmaj),
            pl.BlockSpec((1, MOBA_HEADS, tm, LANES), headmaj),
            pl.BlockSpec((1, MOBA_HEADS, tm, LANES), headmaj),
            pl.BlockSpec((tm, MEM_W), tok),
            pl.BlockSpec((1, 1, MOBA_PAIRS, bpt, LANES), lambda i: (i // nS, i % nS, 0, 0, 0)),
        ],
        out_shape=[
            jax.ShapeDtypeStruct((B, RET_HEADS, S, RET_DK), BF16),
            jax.ShapeDtypeStruct((B, RET_HEADS, S, RET_DK), BF16),
            jax.ShapeDtypeStruct((N, RET_V), BF16),
            jax.ShapeDtypeStruct((N, RET_V), BF16),
            jax.ShapeDtypeStruct((B, MOBA_HEADS, S, LANES), BF16),
            jax.ShapeDtypeStruct((B, MOBA_HEADS, S, LANES), BF16),
            jax.ShapeDtypeStruct((B, MOBA_HEADS, S, LANES), BF16),
            jax.ShapeDtypeStruct((N, MEM_W), BF16),
            jax.ShapeDtypeStruct((B, nS, MOBA_PAIRS, bpt, LANES), F32),
        ],
        compiler_params=_cparams(("arbitrary",), 56),
        name="inproj",
    )(x2, g_mix.reshape(1, D), w_mix, g2(g_moba_q), g2(g_moba_k), g_mem_q.reshape(1, MEM_HD))
    rq, rk, rv, rg, mq, mk, mv, cq, kmean_pairs = outs
    km = kmean_pairs.reshape(B, nS, MOBA_PAIRS, bpt, 2, MOBA_HD).transpose(0, 2, 4, 1, 3, 5)
    km = km.reshape(B, MOBA_HEADS, nS * bpt, MOBA_HD)
    km = jnp.pad(km, ((0, 0), (0, 0), (0, 0), (0, LANES - MOBA_HD)))
    return rq, rk, rv, rg, mq, mk, mv, cq, km


def _retention_kernel(q_ref, k_ref, v_ref, rg_ref, din_ref, dq_ref, dk_ref, dc_ref, g_ref, o_ref, state_ref):
    @pl.when(pl.program_id(2) == 0)
    def _():
        state_ref[...] = jnp.zeros_like(state_ref)

    C = RET_CHUNK
    for c in range(q_ref.shape[2] // C):
        rows = slice(c * C, (c + 1) * C)
        q = q_ref[0, 0, rows, :]
        k = k_ref[0, 0, rows, :]
        v = v_ref[rows, :]
        state = state_ref[...]
        scores = lax.dot_general(q, k, _NT, preferred_element_type=F32) * din_ref[0]
        intra = _dot(scores.astype(BF16), v)
        cross = _dot(q, state.astype(BF16)) * dq_ref[0]
        kd = (k.astype(F32) * dk_ref[0]).astype(BF16)
        state_ref[...] = dc_ref[0] * state + lax.dot_general(kd, v, _TN, preferred_element_type=F32)
        o = _rms(intra + cross, g_ref[...])
        o_ref[rows, :] = (o * jax.nn.silu(rg_ref[rows, :].astype(F32))).astype(BF16)


def _retention_decays():
    H, C = RET_HEADS, RET_CHUNK
    log_g = jnp.log1p(-jnp.exp2(-5.0 - jnp.arange(H, dtype=F32)))
    i = jnp.arange(C, dtype=F32)
    diff = i[:, None] - i[None, :]
    decay_in = jnp.where(diff >= 0, jnp.exp(jnp.maximum(diff, 0.0)[None] * log_g[:, None, None]), 0.0)
    decay_k = jnp.exp((C - 1 - i)[None, :] * log_g[:, None])
    decay_q = jnp.exp((i + 1)[None, :] * log_g[:, None])
    decay_chunk = jnp.exp(C * log_g)
    dq = jnp.broadcast_to(decay_q[:, :, None], (H, C, RET_DV))
    dk = jnp.broadcast_to(decay_k[:, :, None], (H, C, RET_DK))
    dc = jnp.broadcast_to(decay_chunk[:, None, None], (H, RET_DK, RET_DV))
    return decay_in, dq, dk, dc


def _retention(rq, rk, rv, rg, g_ret_out):
    B, H, S, dk = rq.shape
    tq = TQ_RET
    nT = S // tq
    C = RET_CHUNK
    din, dq, dk_, dc = _retention_decays()
    qk_spec = pl.BlockSpec((1, 1, tq, dk), lambda b, h, t: (b, h, t, 0))
    tok_spec = pl.BlockSpec((tq, RET_DV), lambda b, h, t: (b * nT + t, h))
    per_head = lambda r, c: pl.BlockSpec((1, r, c), lambda b, h, t: (h, 0, 0))
    return pl.pallas_call(
        _retention_kernel,
        grid=(B, H, nT),
        in_specs=[qk_spec, qk_spec, tok_spec, tok_spec,
                  per_head(C, C), per_head(C, RET_DV), per_head(C, RET_DK), per_head(RET_DK, RET_DV),
                  pl.BlockSpec((1, RET_DV), lambda b, h, t: (0, 0))],
        out_specs=tok_spec,
        out_shape=jax.ShapeDtypeStruct((B * S, RET_V), BF16),
        scratch_shapes=[pltpu.VMEM((RET_DK, RET_DV), F32)],
        compiler_params=_cparams(("arbitrary", "arbitrary", "arbitrary"), 32),
        name="retention",
    )(rq, rk, rv, rg, din, dq, dk_, dc, g_ret_out.reshape(1, RET_DV))


def _moba_kernel(slopes_ref, q_ref, k_ref, v_ref, km_ref, shift_ref, o_ref):
    pr = pl.program_id(1)
    i = pl.program_id(2)
    BS = MOBA_BLOCK
    nb = km_ref.shape[2]
    n_iota = lax.broadcasted_iota(I32, (nb, BS), 0)
    past = n_iota < i
    outs = []
    for s in range(2):
        slope = slopes_ref[2 * pr + s]
        qa = q_ref[0, s]
        gate = lax.dot_general(km_ref[0, s], qa.astype(F32), _NT, precision=lax.Precision.HIGHEST,
                               preferred_element_type=F32)
        g = jnp.where(past, gate, -jnp.inf)
        rank = jnp.zeros((nb, BS), I32)
        for m in range(nb):
            gm = g[m:m + 1, :]
            beats = jnp.where(gm > g, 1, jnp.where(gm == g, jnp.where(m < n_iota, 1, 0), 0))
            rank = rank + beats
        sel = jnp.where(past, rank, MOBA_TOPK) < MOBA_TOPK
        bias = jnp.where(sel, (n_iota - i).astype(F32) * (slope * BS), NEG).astype(BF16)
        q_aug = (qa.astype(F32) + lax.dot_general(bias, shift_ref[...], _TN, preferred_element_type=F32)).astype(BF16)

        k_own = k_ref[0, s, pl.ds(pl.multiple_of(i * BS, BS), BS), :]
        v_own = v_ref[0, s, pl.ds(pl.multiple_of(i * BS, BS), BS), :]
        sc = lax.dot_general(qa, k_own, _NT, preferred_element_type=F32)
        r_io = lax.broadcasted_iota(I32, (BS, BS), 0)
        c_io = lax.broadcasted_iota(I32, (BS, BS), 1)
        sc = jnp.where(r_io >= c_io, sc, NEG)
        m0 = jnp.max(sc, axis=-1, keepdims=True)
        p = jnp.exp(sc - m0)
        acc0 = _dot(p.astype(BF16), v_own)

        def body(j, carry):
            m_prev, acc = carry
            kj = k_ref[0, s, pl.ds(pl.multiple_of(j * BS, BS), BS), :]
            vj = v_ref[0, s, pl.ds(pl.multiple_of(j * BS, BS), BS), :]
            sj = lax.dot_general(q_aug, kj, _NT, preferred_element_type=F32)
            m_new = jnp.maximum(m_prev, jnp.max(sj, axis=-1, keepdims=True))
            pj = jnp.exp(sj - m_new)
            acc = jnp.exp(m_prev - m_new) * acc + _dot(pj.astype(BF16), vj)
            return m_new, acc

        _, acc = lax.fori_loop(0, i, body, (m0, acc0))
        outs.append(acc / acc[:, VSUM_LANE:VSUM_LANE + 1])
    lane = lax.broadcasted_iota(I32, (BS, LANES), 1)
    o_ref[0] = jnp.where(lane < MOBA_HD, outs[0], pltpu.roll(outs[1], MOBA_HD, 1)).astype(BF16)


def _moba(mq, mk, mv, km, slopes):
    B, H, S, _ = mq.shape
    BS = MOBA_BLOCK
    nq = S // BS
    nb = km.shape[2]
    shift = (jnp.arange(nb)[:, None] + BIAS_LANE0 == jnp.arange(LANES)[None, :]).astype(BF16)
    grid_spec = pltpu.PrefetchScalarGridSpec(
        num_scalar_prefetch=1,
        grid=(B, MOBA_PAIRS, nq),
        in_specs=[
            pl.BlockSpec((1, 2, BS, LANES), lambda b, p, i, sl: (b, p, i, 0)),
            pl.BlockSpec((1, 2, S, LANES), lambda b, p, i, sl: (b, p, 0, 0)),
            pl.BlockSpec((1, 2, S, LANES), lambda b, p, i, sl: (b, p, 0, 0)),
            pl.BlockSpec((1, 2, nb, LANES), lambda b, p, i, sl: (b, p, 0, 0)),
            pl.BlockSpec((nb, LANES), lambda b, p, i, sl: (0, 0)),
        ],
        out_specs=pl.BlockSpec((1, BS, LANES), lambda b, p, i, sl: (b, i, p)),
    )
    return pl.pallas_call(
        _moba_kernel,
        grid_spec=grid_spec,
        out_shape=jax.ShapeDtypeStruct((B, S, MOBA_W), BF16),
        compiler_params=_cparams(("arbitrary", "arbitrary", "arbitrary"), 32),
        name="moba",
    )(slopes, mq, mk, mv, km, shift)


def _merge_kernel(x_ref, oret_ref, omoba_ref, cq_ref, kmem_ref, vmem_ref, gmix_ref, wg_ref, bg_ref,
                  wbr_ref, wbm_ref, wbc_ref, wout_ref, gffn_ref, wr_ref, br_ref, tri_ref,
                  x1_ref, xt_ref, eidx_ref, wts_ref, rank_ref, cnt_ref, carry_ref):
    D = x_ref.shape[1]
    tm = x_ref.shape[0]

    @pl.when(pl.program_id(0) == 0)
    def _():
        carry_ref[...] = jnp.zeros_like(carry_ref)

    x = x_ref[...]
    h = _rms(x, gmix_ref[...]).astype(BF16)
    cq = cq_ref[...]
    om = []
    for hh in range(MEM_HEADS):
        cols = slice(hh * MEM_HD, (hh + 1) * MEM_HD)
        sc = lax.dot_general(cq[:, cols], kmem_ref[0, :, cols], _NT, preferred_element_type=F32) * (MEM_HD ** -0.5)
        sc = sc - jnp.max(sc, axis=-1, keepdims=True)
        p = jnp.exp(sc)
        p = p / jnp.sum(p, axis=-1, keepdims=True)
        om.append(_dot(p.astype(BF16), vmem_ref[0, :, cols]))
    omem = jnp.concatenate(om, axis=-1).astype(BF16)
    y = None
    for br, (o, w_ref) in enumerate(((oret_ref[...], wbr_ref), (omoba_ref[...], wbm_ref), (omem, wbc_ref))):
        gl = _dot(h, wg_ref[:, br * D:(br + 1) * D]) + bg_ref[:, br * D:(br + 1) * D]
        term = jax.nn.sigmoid(gl) * _dot(o, w_ref[...])
        y = term if y is None else y + term
    x1 = x + _dot(y.astype(BF16), wout_ref[...])
    x1_ref[...] = x1
    xt = _rms(x1, gffn_ref[...])
    xt_ref[...] = xt
    logits = lax.dot_general(wr_ref[...], xt, _NT, precision=lax.Precision.HIGHEST,
                             preferred_element_type=F32) + br_ref[...]
    E = logits.shape[0]
    e_iota = lax.broadcasted_iota(I32, (E, tm), 0)
    l = logits
    vals, hots = [], []
    for k in range(TOP_K):
        m = jnp.max(l, axis=0, keepdims=True)
        idx = jnp.min(jnp.where(l == m, e_iota, E), axis=0, keepdims=True)
        hot = e_iota == idx
        l = jnp.where(hot, -jnp.inf, l)
        vals.append(m)
        hots.append(hot)
        eidx_ref[k:k + 1, :] = idx
    ex = [jnp.exp(v - vals[0]) for v in vals]
    den = ex[0]
    for k in range(1, TOP_K):
        den = den + ex[k]
    chosen = jnp.zeros((E, tm), F32)
    for k in range(TOP_K):
        wts_ref[k:k + 1, :] = ex[k] / den
        chosen = chosen + jnp.where(hots[k], 1.0, 0.0)
    carry = carry_ref[:, 0:1]
    prefix = _dot(chosen.astype(BF16), tri_ref[...]) + carry
    for k in range(TOP_K):
        rank_ref[k:k + 1, :] = jnp.sum(jnp.where(hots[k], prefix, 0.0), axis=0, keepdims=True).astype(I32)
    new_carry = carry + jnp.sum(chosen, axis=1, keepdims=True)
    carry_ref[...] = jnp.broadcast_to(new_carry, carry_ref.shape)
    cnt_ref[...] = jnp.broadcast_to(new_carry, cnt_ref.shape).astype(I32)


def _merge(x2, S, o_ret, o_moba, cq, kmem, vmem, g_mix, w_gate, b_gate, w_br_ret, w_br_moba, w_br_mem, w_out,
           g_ffn, w_router, b_router):
    N, D = x2.shape
    tm = TM_MERGE
    nS = S // tm
    M = kmem.shape[1]
    E = w_router.shape[1]
    tok = lambda w: pl.BlockSpec((tm, w), lambda i: (i, 0))
    const = lambda r, c: pl.BlockSpec((r, c), lambda i: (0, 0), pipeline_mode=pl.Buffered(1))
    mem_spec = pl.BlockSpec((1, M, MEM_W), lambda i: (i // nS, 0, 0))
    lanes_tok = lambda r, dt: (pl.BlockSpec((r, tm), lambda i: (0, i)), jax.ShapeDtypeStruct((r, N), dt))
    tri = (jnp.arange(tm)[:, None] < jnp.arange(tm)[None, :]).astype(BF16)
    e_spec, e_shape = lanes_tok(TOP_K, I32)
    w_spec, w_shape = lanes_tok(TOP_K, F32)
    r_spec, r_shape = lanes_tok(TOP_K, I32)
    return pl.pallas_call(
        _merge_kernel,
        grid=(N // tm,),
        in_specs=[tok(D), tok(RET_V), tok(MOBA_W), tok(MEM_W), mem_spec, mem_spec,
                  const(1, D), const(D, N_BRANCH * D), const(1, N_BRANCH * D),
                  const(RET_V, D), const(MOBA_W, D), const(MEM_W, D), const(D, D),
                  const(1, D), const(E, D), const(E, 1), const(tm, tm)],
        out_specs=[tok(D), tok(D), e_spec, w_spec, r_spec, pl.BlockSpec((E, LANES), lambda i: (0, 0))],
        out_shape=[jax.ShapeDtypeStruct((N, D), F32), jax.ShapeDtypeStruct((N, D), F32),
                   e_shape, w_shape, r_shape, jax.ShapeDtypeStruct((E, LANES), I32)],
        scratch_shapes=[pltpu.VMEM((E, LANES), F32)],
        compiler_params=_cparams(("arbitrary",), 56),
        name="merge",
    )(x2, o_ret, o_moba, cq, kmem, vmem, g_mix.reshape(1, D), w_gate, b_gate.reshape(1, N_BRANCH * D),
      w_br_ret.astype(BF16), w_br_moba.astype(BF16), w_br_mem.astype(BF16), w_out.astype(BF16),
      g_ffn.reshape(1, D), w_router.T, b_router.reshape(E, 1), tri)


def _dispatch_kernel(pad_end_ref, padded_ref, dest_ref, xt_ref, xb_ref, zeros_ref, zsem, sem):
    T = zeros_ref.shape[0]
    td = xt_ref.shape[0]

    @pl.when(pl.program_id(0) == 0)
    def _():
        zeros_ref[...] = jnp.zeros_like(zeros_ref)

        def fill(e):
            start = pl.multiple_of(pad_end_ref[e] - T, T)
            return pltpu.make_async_copy(zeros_ref, xb_ref.at[pl.ds(start, T)], zsem)

        for e in range(N_EXPERTS):
            @pl.when(padded_ref[e] > 0)
            def _():
                fill(e).start()
        for e in range(N_EXPERTS):
            @pl.when(padded_ref[e] > 0)
            def _():
                fill(e).wait()

        def tail(b):
            return pltpu.make_async_copy(zeros_ref, xb_ref.at[pl.ds(pl.multiple_of(b * T, T), T)], zsem)

        first_unused = pad_end_ref[N_EXPERTS - 1] // T
        n_blocks = xb_ref.shape[0] // T
        lax.fori_loop(first_unused, n_blocks, lambda b, c: (tail(b).start(), c)[1], 0)
        lax.fori_loop(first_unused, n_blocks, lambda b, c: (tail(b).wait(), c)[1], 0)

    def row(r, _):
        for k in range(TOP_K):
            pltpu.make_async_copy(xt_ref.at[pl.ds(r, 1)], xb_ref.at[pl.ds(dest_ref[k, r], 1)], sem).start()
        return 0

    lax.fori_loop(0, td, row, 0)
    for k in range(TOP_K):
        pltpu.make_async_copy(xt_ref, xb_ref.at[pl.ds(0, td)], sem).wait()


def _dispatch(xt, dest, pad_ends, padded, R):
    N, D = xt.shape
    td = T_DISPATCH
    grid_spec = pltpu.PrefetchScalarGridSpec(
        num_scalar_prefetch=2,
        grid=(N // td,),
        in_specs=[
            pl.BlockSpec((TOP_K, td), lambda i, pe, pd: (0, i), memory_space=pltpu.SMEM),
            pl.BlockSpec((td, D), lambda i, pe, pd: (i, 0)),
        ],
        out_specs=pl.BlockSpec(memory_space=pl.ANY),
        scratch_shapes=[pltpu.VMEM((MOE_T, D), F32), pltpu.SemaphoreType.DMA(()), pltpu.SemaphoreType.DMA(())],
    )
    return pl.pallas_call(
        _dispatch_kernel,
        grid_spec=grid_spec,
        out_shape=jax.ShapeDtypeStruct((R, D), F32),
        compiler_params=_cparams(("arbitrary",), 32),
        name="dispatch",
    )(pad_ends, padded, dest, xt)


def _expert_kernel(be_ref, nb_ref, xb_ref, w1_ref, b1_ref, w2_ref, b2_ref, perm_ref, yb_ref, w1p_ref, w2b_ref):
    j = pl.program_id(0)
    G = 2 * LANES

    @pl.when(j >= nb_ref[0])
    def _():
        yb_ref[...] = jnp.zeros_like(yb_ref)

    @pl.when(j < nb_ref[0])
    def _():
        e = be_ref[j]
        prev = be_ref[jnp.maximum(j - 1, 0)]

        @pl.when(jnp.logical_or(j == 0, e != prev))
        def _():
            for g in range(w1p_ref.shape[1] // G):
                w = w1_ref[0, :, g * G:(g + 1) * G].astype(BF16)
                w1p_ref[:, g * G:(g + 1) * G] = _dot(w, perm_ref[...]).astype(BF16)
            w2b_ref[...] = w2_ref[0].astype(BF16)

        x = xb_ref[...].astype(BF16)
        acts = []
        for g in range(w1p_ref.shape[1] // G):
            hg = _dot(x, w1p_ref[:, g * G:(g + 1) * G]) + b1_ref[0, :, g * G:(g + 1) * G]
            glu = jnp.minimum(hg[:, :LANES], SWIGLU_LIMIT)
            lin = jnp.clip(hg[:, LANES:], -SWIGLU_LIMIT, SWIGLU_LIMIT)
            acts.append((glu * jax.nn.sigmoid(SWIGLU_ALPHA * glu) * (lin + 1.0)).astype(BF16))
        act = jnp.concatenate(acts, axis=-1)
        yb_ref[...] = _dot(act, w2b_ref[...]) + b2_ref[0]


def _experts(xb, block_e, nblk, w1, b1p, w2, b2):
    R, D = xb.shape
    E, _, F2 = w1.shape
    F = F2 // 2
    T = MOE_T
    G = 2 * LANES
    c = np.arange(G)
    src = np.where(c < LANES, 2 * c, 2 * (c - LANES) + 1)
    perm = jnp.asarray(np.arange(G)[:, None] == src[None, :], dtype=BF16)
    blk = lambda j, be, nb: jnp.minimum(j, nb[0] - 1)
    grid_spec = pltpu.PrefetchScalarGridSpec(
        num_scalar_prefetch=2,
        grid=(R // T,),
        in_specs=[
            pl.BlockSpec((T, D), lambda j, be, nb: (blk(j, be, nb), 0)),
            pl.BlockSpec((1, D, F2), lambda j, be, nb: (be[blk(j, be, nb)], 0, 0)),
            pl.BlockSpec((1, 1, F2), lambda j, be, nb: (be[blk(j, be, nb)], 0, 0)),
            pl.BlockSpec((1, F, D), lambda j, be, nb: (be[blk(j, be, nb)], 0, 0)),
            pl.BlockSpec((1, 1, D), lambda j, be, nb: (be[blk(j, be, nb)], 0, 0)),
            pl.BlockSpec((G, G), lambda j, be, nb: (0, 0)),
        ],
        out_specs=pl.BlockSpec((T, D), lambda j, be, nb: (j, 0)),
        scratch_shapes=[pltpu.VMEM((D, F2), BF16), pltpu.VMEM((F, D), BF16)],
    )
    return pl.pallas_call(
        _expert_kernel,
        grid_spec=grid_spec,
        out_shape=jax.ShapeDtypeStruct((R, D), F32),
        compiler_params=_cparams(("arbitrary",), 56),
        name="experts",
    )(block_e, nblk, xb, w1, b1p, w2, b2.reshape(E, 1, D), perm)


def _combine_kernel(dest_ref, wts_ref, x1_ref, yb_ref, o_ref, buf_ref, sem):
    tc = x1_ref.shape[0]

    def row(r, _):
        for k in range(TOP_K):
            pltpu.make_async_copy(yb_ref.at[pl.ds(dest_ref[k, r], 1)], buf_ref.at[k, pl.ds(r, 1)], sem).start()
        return 0

    lax.fori_loop(0, tc, row, 0)
    for k in range(TOP_K):
        pltpu.make_async_copy(yb_ref.at[pl.ds(0, tc)], buf_ref.at[k], sem).wait()
    w = wts_ref[...]
    acc = x1_ref[...]
    for k in range(TOP_K):
        acc = acc + w[:, k:k + 1] * buf_ref[k]
    o_ref[...] = acc


def _combine(dest, wts_t, x1, yb):
    N, D = x1.shape
    tc = T_COMBINE
    return pl.pallas_call(
        _combine_kernel,
        grid=(N // tc,),
        in_specs=[
            pl.BlockSpec((TOP_K, tc), lambda i: (0, i), memory_space=pltpu.SMEM),
            pl.BlockSpec((tc, TOP_K), lambda i: (i, 0)),
            pl.BlockSpec((tc, D), lambda i: (i, 0)),
            pl.BlockSpec(memory_space=pl.ANY),
        ],
        out_specs=pl.BlockSpec((tc, D), lambda i: (i, 0)),
        out_shape=jax.ShapeDtypeStruct((N, D), F32),
        scratch_shapes=[pltpu.VMEM((TOP_K, tc, D), F32), pltpu.SemaphoreType.DMA(())],
        compiler_params=_cparams(("arbitrary",), 32),
        name="combine",
    )(dest, wts_t, x1, yb)


def _layer(x, mem, g_mix, w_in, b_gate, g_ret_out, g_moba_q, g_moba_k, g_mem, w_mem_kv, g_mem_q, g_mem_k,
           w_br_ret, w_br_moba, w_br_mem, w_out, g_ffn, w_router, b_router, w_mlp1, b_mlp1, w_mlp2, b_mlp2):
    B, S, D = x.shape
    N = B * S
    x2 = x.reshape(N, D)
    slopes_np = np.exp2(-8.0 * (np.arange(MOBA_HEADS, dtype=np.float64) + 1.0) / MOBA_HEADS)
    assert all(float(np.log2(s)).is_integer() for s in slopes_np)
    slopes = tuple(float(s) for s in slopes_np)

    w_in_bf = w_in.astype(BF16)
    kmem, vmem = _memkv(mem, g_mem, w_mem_kv, g_mem_k)
    rq, rk, rv, rg, mq, mk, mv, cq, km = _inproj(x2, B, S, g_mix, w_in_bf[:, :MIX_W], g_moba_q, g_moba_k, g_mem_q,
                                                 slopes)
    o_ret = _retention(rq, rk, rv, rg, g_ret_out)
    o_moba = _moba(mq, mk, mv, km, jnp.asarray(slopes, F32)).reshape(N, MOBA_W)
    x1, xt, eidx, wts, rank, cnt = _merge(x2, S, o_ret, o_moba, cq, kmem, vmem, g_mix, w_in_bf[:, MIX_W:], b_gate,
                                          w_br_ret, w_br_moba, w_br_mem, w_out, g_ffn, w_router, b_router)
    T = MOE_T
    counts = cnt[:, 0]
    padded = ((counts + T - 1) // T) * T
    pad_ends = jnp.cumsum(padded).astype(I32)
    pad_starts = pad_ends - padded
    dest = jnp.take(pad_starts, eidx, axis=0) + rank
    NB = -(-(N * TOP_K) // T) + N_EXPERTS
    ends_before = jnp.sum((pad_ends[None, :] <= (jnp.arange(NB, dtype=I32) * T)[:, None]).astype(I32), axis=1)
    block_e = jnp.minimum(ends_before, N_EXPERTS - 1).astype(I32)
    nblk = (pad_ends[-1:] // T).astype(I32)

    xb = _dispatch(xt, dest, pad_ends, padded.astype(I32), NB * T)
    F2 = w_mlp1.shape[-1]
    b1p = b_mlp1.reshape(N_EXPERTS, F2 // (2 * LANES), LANES, 2).transpose(0, 1, 3, 2).reshape(N_EXPERTS, 1, F2)
    yb = _experts(xb, block_e, nblk, w_mlp1, b1p, w_mlp2, b_mlp2)
    out = _combine(dest, wts.T, x1, yb)
    return out.reshape(B, S, D)


def kernel(x, mem, g_mix, w_in, b_gate, g_ret_out, g_moba_q, g_moba_k, g_mem, w_mem_kv, g_mem_q, g_mem_k, w_br_ret, w_br_moba, w_br_mem, w_out, g_ffn, w_router, b_router, w_mlp1, b_mlp1, w_mlp2, b_mlp2):
    for l in range(g_mix.shape[0]):
        x = _layer(x, mem, g_mix[l], w_in[l], b_gate[l], g_ret_out[l], g_moba_q[l], g_moba_k[l], g_mem[l],
                   w_mem_kv[l], g_mem_q[l], g_mem_k[l], w_br_ret[l], w_br_moba[l], w_br_mem[l], w_out[l],
                   g_ffn[l], w_router[l], b_router[l], w_mlp1[l], b_mlp1[l], w_mlp2[l], b_mlp2[l])
    return x
```

```python
import functools

import jax
import jax.numpy as jnp
import numpy as np
from jax import lax
from jax.experimental import pallas as pl
from jax.experimental.pallas import tpu as pltpu

F32 = jnp.float32
BF16 = jnp.bfloat16
I32 = jnp.int32

EPS = 1e-5
NEG = -1e30

RET_HEADS = 4
RET_DK = 64
RET_DV = 128
RET_CHUNK = 128
MOBA_HEADS = 8
MOBA_HD = 64
MOBA_BLOCK = 256
MOBA_TOPK = 3
MEM_HEADS = 4
MEM_HD = 128
N_BRANCH = 3
N_EXPERTS = 32
TOP_K = 4
SWIGLU_LIMIT = 7.0
SWIGLU_ALPHA = 1.702

RET_Q = RET_HEADS * RET_DK
RET_V = RET_HEADS * RET_DV
MOBA_W = MOBA_HEADS * MOBA_HD
MEM_W = MEM_HEADS * MEM_HD
MIX_W = 2 * RET_Q + 2 * RET_V + 3 * MOBA_W + MEM_W

LANES = 128
MOBA_PAIRS = MOBA_HEADS // 2
MOBA_HEADS_PER_STEP = 8
BIAS_LANE0 = MOBA_HD
ONE_LANE = 80
VSUM_LANE = MOBA_HD

MOE_T = 256
TM_PROJ = 512
TM_MERGE = 512
TQ_RET = 512
T_DISPATCH = 256
T_COMBINE = 256

_NT = (((1,), (1,)), ((), ()))
_TN = (((0,), (0,)), ((), ()))


def _rms(x, g):
    return x * lax.rsqrt(jnp.mean(x * x, axis=-1, keepdims=True) + EPS) * g


def _dot(a, b):
    return jnp.dot(a, b, preferred_element_type=F32)


def _cparams(sem, vmem_mb):
    return pltpu.CompilerParams(dimension_semantics=sem, vmem_limit_bytes=vmem_mb * 1024 * 1024)


def _memkv_kernel(mem_ref, g_ref, w_ref, gk_ref, k_ref, v_ref):
    m = _rms(mem_ref[0], g_ref[...]).astype(BF16)
    kv = _dot(m, w_ref[...])
    ks = [_rms(kv[:, h * MEM_HD:(h + 1) * MEM_HD], gk_ref[...]) for h in range(MEM_HEADS)]
    k_ref[0] = jnp.concatenate(ks, axis=-1).astype(BF16)
    v_ref[0] = kv[:, MEM_W:].astype(BF16)


def _memkv(mem, g_mem, w_mem_kv, g_mem_k):
    B, M, D = mem.shape
    return pl.pallas_call(
        _memkv_kernel,
        grid=(B,),
        in_specs=[
            pl.BlockSpec((1, M, D), lambda b: (b, 0, 0)),
            pl.BlockSpec((1, D), lambda b: (0, 0)),
            pl.BlockSpec((D, 2 * MEM_W), lambda b: (0, 0)),
            pl.BlockSpec((1, MEM_HD), lambda b: (0, 0)),
        ],
        out_specs=[
            pl.BlockSpec((1, M, MEM_W), lambda b: (b, 0, 0)),
            pl.BlockSpec((1, M, MEM_W), lambda b: (b, 0, 0)),
        ],
        out_shape=[jax.ShapeDtypeStruct((B, M, MEM_W), BF16)] * 2,
        compiler_params=_cparams(("arbitrary",), 32),
        name="memkv",
    )(mem, g_mem.reshape(1, D), w_mem_kv.astype(BF16), g_mem_k.reshape(1, MEM_HD))


def _head_pair_norm(a2, g2, lane):
    sq = a2 * a2
    lo = lane < MOBA_HD
    ss_lo = jnp.sum(jnp.where(lo, sq, 0.0), axis=-1, keepdims=True)
    ss_hi = jnp.sum(jnp.where(lo, 0.0, sq), axis=-1, keepdims=True)
    inv = jnp.where(lo, lax.rsqrt(ss_lo / MOBA_HD + EPS), lax.rsqrt(ss_hi / MOBA_HD + EPS))
    return a2 * inv * g2


def _inproj_kernel(slopes, seq_tiles, x_ref, gmix_ref, w_ref, gq_ref, gk_ref, gc_ref,
                   rq_ref, rk_ref, rv_ref, rg_ref, mq_ref, mk_ref, mv_ref, cq_ref, kmean_ref):
    tm = x_ref.shape[0]
    blocks_per_tile = tm // MOBA_BLOCK
    h = _rms(x_ref[...], gmix_ref[...]).astype(BF16)
    col = [0]

    def proj(width):
        a = _dot(h, w_ref[:, col[0]:col[0] + width])
        col[0] += width
        return a

    a = proj(2 * RET_Q)
    for hh in range(RET_HEADS):
        rq_ref[0, hh] = a[:, hh * RET_DK:(hh + 1) * RET_DK].astype(BF16)
        rk_ref[0, hh] = (a[:, RET_Q + hh * RET_DK:RET_Q + (hh + 1) * RET_DK] * (RET_DK ** -0.5)).astype(BF16)
    rv_ref[...] = proj(RET_V).astype(BF16)
    rg_ref[...] = proj(RET_V).astype(BF16)

    lane = lax.broadcasted_iota(I32, (tm, LANES), 1)
    row = lax.broadcasted_iota(I32, (tm, LANES), 0)
    lo = lane < MOBA_HD
    q_tail = jnp.where(lane == ONE_LANE, 1.0, 0.0)
    a = proj(MOBA_W)
    for p in range(MOBA_PAIRS):
        n2 = _head_pair_norm(a[:, p * LANES:(p + 1) * LANES], gq_ref[...], lane) * (MOBA_HD ** -0.5)
        mq_ref[0, 2 * p] = jnp.where(lo, n2, q_tail).astype(BF16)
        mq_ref[0, 2 * p + 1] = jnp.where(lo, pltpu.roll(n2, MOBA_HD, 1), q_tail).astype(BF16)
    blk = (pl.program_id(0) % seq_tiles) * blocks_per_tile + row // MOBA_BLOCK
    onehot_tail = jnp.where(lane == BIAS_LANE0 + blk, 1.0, 0.0)
    off = (row % MOBA_BLOCK).astype(F32)
    a = proj(MOBA_W)
    for p in range(MOBA_PAIRS):
        n2 = _head_pair_norm(a[:, p * LANES:(p + 1) * LANES], gk_ref[...], lane)
        for j in range(blocks_per_tile):
            kmean_ref[0, 0, p, j:j + 1, :] = jnp.mean(n2[j * MOBA_BLOCK:(j + 1) * MOBA_BLOCK], axis=0, keepdims=True)
        for s, src in ((0, n2), (1, pltpu.roll(n2, MOBA_HD, 1))):
            tail = jnp.where(lane == ONE_LANE, slopes[2 * p + s] * off, onehot_tail)
            mk_ref[0, 2 * p + s] = jnp.where(lo, src, tail).astype(BF16)
    v_tail = jnp.where(lane == VSUM_LANE, 1.0, 0.0)
    a = proj(MOBA_W)
    for p in range(MOBA_PAIRS):
        a2 = a[:, p * LANES:(p + 1) * LANES]
        mv_ref[0, 2 * p] = jnp.where(lo, a2, v_tail).astype(BF16)
        mv_ref[0, 2 * p + 1] = jnp.where(lo, pltpu.roll(a2, MOBA_HD, 1), v_tail).astype(BF16)
    a = proj(MEM_W)
    cq = [_rms(a[:, hh * MEM_HD:(hh + 1) * MEM_HD], gc_ref[...]) for hh in range(MEM_HEADS)]
    cq_ref[...] = jnp.concatenate(cq, axis=-1).astype(BF16)


def _inproj(x2, B, S, g_mix, w_mix, g_moba_q, g_moba_k, g_mem_q, slopes):
    N, D = x2.shape
    tm = TM_PROJ
    nS = S // tm
    bpt = tm // MOBA_BLOCK
    tok = lambda i: (i, 0)
    headmaj = lambda i: (i // nS, 0, i % nS, 0)
    g2 = lambda g: jnp.concatenate([g, g]).reshape(1, LANES)
    outs = pl.pallas_call(
        functools.partial(_inproj_kernel, slopes, nS),
        grid=(N // tm,),
        in_specs=[
            pl.BlockSpec((tm, D), tok),
            pl.BlockSpec((1, D), lambda i: (0, 0)),
            pl.BlockSpec((D, MIX_W), lambda i: (0, 0)),
            pl.BlockSpec((1, LANES), lambda i: (0, 0)),
            pl.BlockSpec((1, LANES), lambda i: (0, 0)),
            pl.BlockSpec((1, MEM_HD), lambda i: (0, 0)),
        ],
        out_specs=[
            pl.BlockSpec((1, RET_HEADS, tm, RET_DK), headmaj),
            pl.BlockSpec((1, RET_HEADS, tm, RET_DK), headmaj),
            pl.BlockSpec((tm, RET_V), tok),
            pl.BlockSpec((tm, RET_V), tok),
            pl.BlockSpec((1, MOBA_HEADS, tm, LANES), headmaj),
            pl.BlockSpec((1, MOBA_HEADS, tm, LANES), headmaj),
            pl.BlockSpec((1, MOBA_HEADS, tm, LANES), headmaj),
            pl.BlockSpec((tm, MEM_W), tok),
            pl.BlockSpec((1, 1, MOBA_PAIRS, bpt, LANES), lambda i: (i // nS, i % nS, 0, 0, 0)),
        ],
        out_shape=[
            jax.ShapeDtypeStruct((B, RET_HEADS, S, RET_DK), BF16),
            jax.ShapeDtypeStruct((B, RET_HEADS, S, RET_DK), BF16),
            jax.ShapeDtypeStruct((N, RET_V), BF16),
            jax.ShapeDtypeStruct((N, RET_V), BF16),
            jax.ShapeDtypeStruct((B, MOBA_HEADS, S, LANES), BF16),
            jax.ShapeDtypeStruct((B, MOBA_HEADS, S, LANES), BF16),
            jax.ShapeDtypeStruct((B, MOBA_HEADS, S, LANES), BF16),
            jax.ShapeDtypeStruct((N, MEM_W), BF16),
            jax.ShapeDtypeStruct((B, nS, MOBA_PAIRS, bpt, LANES), F32),
        ],
        compiler_params=_cparams(("arbitrary",), 56),
        name="inproj",
    )(x2, g_mix.reshape(1, D), w_mix, g2(g_moba_q), g2(g_moba_k), g_mem_q.reshape(1, MEM_HD))
    rq, rk, rv, rg, mq, mk, mv, cq, kmean_pairs = outs
    km = kmean_pairs.reshape(B, nS, MOBA_PAIRS, bpt, 2, MOBA_HD).transpose(0, 2, 4, 1, 3, 5)
    km = km.reshape(B, MOBA_HEADS, nS * bpt, MOBA_HD)
    km = jnp.pad(km, ((0, 0), (0, 0), (0, 0), (0, LANES - MOBA_HD)))
    return rq, rk, rv, rg, mq, mk, mv, cq, km


def _retention_kernel(q_ref, k_ref, v_ref, rg_ref, din_ref, dq_ref, dk_ref, dc_ref, g_ref, o_ref, state_ref):
    @pl.when(pl.program_id(2) == 0)
    def _():
        state_ref[...] = jnp.zeros_like(state_ref)

    C = RET_CHUNK
    for c in range(q_ref.shape[2] // C):
        rows = slice(c * C, (c + 1) * C)
        q = q_ref[0, 0, rows, :]
        k = k_ref[0, 0, rows, :]
        v = v_ref[rows, :]
        state = state_ref[...]
        scores = lax.dot_general(q, k, _NT, preferred_element_type=F32) * din_ref[0]
        intra = _dot(scores.astype(BF16), v)
        cross = _dot(q, state.astype(BF16)) * dq_ref[0]
        kd = (k.astype(F32) * dk_ref[0]).astype(BF16)
        state_ref[...] = dc_ref[0] * state + lax.dot_general(kd, v, _TN, preferred_element_type=F32)
        o = _rms(intra + cross, g_ref[...])
        o_ref[rows, :] = (o * jax.nn.silu(rg_ref[rows, :].astype(F32))).astype(BF16)


def _retention_decays():
    H, C = RET_HEADS, RET_CHUNK
    log_g = jnp.log1p(-jnp.exp2(-5.0 - jnp.arange(H, dtype=F32)))
    i = jnp.arange(C, dtype=F32)
    diff = i[:, None] - i[None, :]
    decay_in = jnp.where(diff >= 0, jnp.exp(jnp.maximum(diff, 0.0)[None] * log_g[:, None, None]), 0.0)
    decay_k = jnp.exp((C - 1 - i)[None, :] * log_g[:, None])
    decay_q = jnp.exp((i + 1)[None, :] * log_g[:, None])
    decay_chunk = jnp.exp(C * log_g)
    dq = jnp.broadcast_to(decay_q[:, :, None], (H, C, RET_DV))
    dk = jnp.broadcast_to(decay_k[:, :, None], (H, C, RET_DK))
    dc = jnp.broadcast_to(decay_chunk[:, None, None], (H, RET_DK, RET_DV))
    return decay_in, dq, dk, dc


def _retention(rq, rk, rv, rg, g_ret_out):
    B, H, S, dk = rq.shape
    tq = TQ_RET
    nT = S // tq
    C = RET_CHUNK
    din, dq, dk_, dc = _retention_decays()
    qk_spec = pl.BlockSpec((1, 1, tq, dk), lambda b, h, t: (b, h, t, 0))
    tok_spec = pl.BlockSpec((tq, RET_DV), lambda b, h, t: (b * nT + t, h))
    per_head = lambda r, c: pl.BlockSpec((1, r, c), lambda b, h, t: (h, 0, 0))
    return pl.pallas_call(
        _retention_kernel,
        grid=(B, H, nT),
        in_specs=[qk_spec, qk_spec, tok_spec, tok_spec,
                  per_head(C, C), per_head(C, RET_DV), per_head(C, RET_DK), per_head(RET_DK, RET_DV),
                  pl.BlockSpec((1, RET_DV), lambda b, h, t: (0, 0))],
        out_specs=tok_spec,
        out_shape=jax.ShapeDtypeStruct((B * S, RET_V), BF16),
        scratch_shapes=[pltpu.VMEM((RET_DK, RET_DV), F32)],
        compiler_params=_cparams(("arbitrary", "arbitrary", "arbitrary"), 32),
        name="retention",
    )(rq, rk, rv, rg, din, dq, dk_, dc, g_ret_out.reshape(1, RET_DV))


def _moba_kernel(slopes_ref, q_ref, k_ref, v_ref, km_ref, shift_ref, causal_ref, o_ref):
    NH = q_ref.shape[1]
    head0 = pl.program_id(1) * NH
    i = pl.program_id(2)
    BS = MOBA_BLOCK
    nb = km_ref.shape[2]
    heads = range(NH)
    n_iota = lax.broadcasted_iota(I32, (nb, BS), 0)
    past = n_iota < i
    qa = [q_ref[0, s] for s in heads]
    q_aug = []
    for s in heads:
        km = km_ref[0, s]
        hi = km.astype(BF16)
        mid = (km - hi.astype(F32)).astype(BF16)
        lo = (km - hi.astype(F32) - mid.astype(F32)).astype(BF16)
        g3 = lax.dot_general(jnp.concatenate([hi, mid, lo], axis=0), qa[s], _NT, preferred_element_type=F32)
        gate = (g3[:nb] + g3[nb:2 * nb]) + g3[2 * nb:]
        g = jnp.where(past, gate, -jnp.inf)
        rank = jnp.zeros((nb, BS), I32)
        for m in range(nb - 1):
            gm = g[m:m + 1, :]
            rank = rank + jnp.where(gm > g, 1, jnp.where(gm == g, jnp.where(m < n_iota, 1, 0), 0))
        sel = jnp.where(past, rank, MOBA_TOPK) < MOBA_TOPK
        bias = jnp.where(sel, (n_iota - i).astype(F32) * (slopes_ref[head0 + s] * BS), NEG).astype(BF16)
        placed = lax.dot_general(bias, shift_ref[...], _TN, preferred_element_type=F32)
        q_aug.append((qa[s].astype(F32) + placed).astype(BF16))

    own = pl.ds(pl.multiple_of(i * BS, BS), BS)
    m0, acc0 = [], []
    for s in heads:
        sc = lax.dot_general(qa[s], k_ref[0, s, own, :], _NT, preferred_element_type=F32) + causal_ref[...]
        m = jnp.max(sc, axis=-1, keepdims=True)
        m0.append(m)
        acc0.append(_dot(jnp.exp(sc - m).astype(BF16), v_ref[0, s, own, :]))

    def body(j, carry):
        ms, accs = carry
        blk = pl.ds(pl.multiple_of(j * BS, BS), BS)
        new_m, new_acc = [], []
        for s in heads:
            sj = lax.dot_general(q_aug[s], k_ref[0, s, blk, :], _NT, preferred_element_type=F32)
            m_new = jnp.maximum(ms[s], jnp.max(sj, axis=-1, keepdims=True))
            pj = jnp.exp(sj - m_new).astype(BF16)
            new_acc.append(jnp.exp(ms[s] - m_new) * accs[s] + _dot(pj, v_ref[0, s, blk, :]))
            new_m.append(m_new)
        return tuple(new_m), tuple(new_acc)

    _, accs = lax.fori_loop(0, i, body, (tuple(m0), tuple(acc0)))
    outs = [acc / acc[:, VSUM_LANE:VSUM_LANE + 1] for acc in accs]
    lane = lax.broadcasted_iota(I32, (BS, LANES), 1)
    pairs = [jnp.where(lane < MOBA_HD, outs[2 * p], pltpu.roll(outs[2 * p + 1], MOBA_HD, 1)) for p in range(NH // 2)]
    o_ref[0] = jnp.concatenate(pairs, axis=-1).astype(BF16)


def _moba(mq, mk, mv, km, slopes):
    B, H, S, _ = mq.shape
    BS = MOBA_BLOCK
    NH = MOBA_HEADS_PER_STEP
    nq = S // BS
    nb = km.shape[2]
    assert BIAS_LANE0 + nb <= ONE_LANE
    shift = (jnp.arange(nb)[:, None] + BIAS_LANE0 == jnp.arange(LANES)[None, :]).astype(BF16)
    causal = jnp.where(jnp.arange(BS)[:, None] >= jnp.arange(BS)[None, :], 0.0, NEG).astype(F32)
    per_head = lambda rows: pl.BlockSpec((1, NH, rows, LANES), lambda b, p, i, sl: (b, p, 0, 0))
    grid_spec = pltpu.PrefetchScalarGridSpec(
        num_scalar_prefetch=1,
        grid=(B, H // NH, nq),
        in_specs=[
            pl.BlockSpec((1, NH, BS, LANES), lambda b, p, i, sl: (b, p, i, 0)),
            per_head(S), per_head(S), per_head(nb),
            pl.BlockSpec((nb, LANES), lambda b, p, i, sl: (0, 0)),
            pl.BlockSpec((BS, BS), lambda b, p, i, sl: (0, 0)),
        ],
        out_specs=pl.BlockSpec((1, BS, NH * MOBA_HD), lambda b, p, i, sl: (b, i, p)),
    )
    return pl.pallas_call(
        _moba_kernel,
        grid_spec=grid_spec,
        out_shape=jax.ShapeDtypeStruct((B, S, MOBA_W), BF16),
        compiler_params=_cparams(("arbitrary", "arbitrary", "arbitrary"), 48),
        name="moba",
    )(slopes, mq, mk, mv, km, shift, causal)


def _merge_kernel(x_ref, oret_ref, omoba_ref, cq_ref, kmem_ref, vmem_ref, gmix_ref, wg_ref, bg_ref,
                  wbr_ref, wbm_ref, wbc_ref, wout_ref, gffn_ref, wr_ref, br_ref, tri_ref,
                  x1_ref, xt_ref, eidx_ref, wts_ref, rank_ref, cnt_ref, carry_ref):
    D = x_ref.shape[1]
    tm = x_ref.shape[0]

    @pl.when(pl.program_id(0) == 0)
    def _():
        carry_ref[...] = jnp.zeros_like(carry_ref)

    x = x_ref[...]
    h = _rms(x, gmix_ref[...]).astype(BF16)
    cq = cq_ref[...]
    om = []
    for hh in range(MEM_HEADS):
        cols = slice(hh * MEM_HD, (hh + 1) * MEM_HD)
        sc = lax.dot_general(cq[:, cols], kmem_ref[0, :, cols], _NT, preferred_element_type=F32) * (MEM_HD ** -0.5)
        sc = sc - jnp.max(sc, axis=-1, keepdims=True)
        p = jnp.exp(sc)
        p = p / jnp.sum(p, axis=-1, keepdims=True)
        om.append(_dot(p.astype(BF16), vmem_ref[0, :, cols]))
    omem = jnp.concatenate(om, axis=-1).astype(BF16)
    y = None
    for br, (o, w_ref) in enumerate(((oret_ref[...], wbr_ref), (omoba_ref[...], wbm_ref), (omem, wbc_ref))):
        gl = _dot(h, wg_ref[:, br * D:(br + 1) * D]) + bg_ref[:, br * D:(br + 1) * D]
        term = jax.nn.sigmoid(gl) * _dot(o, w_ref[...])
        y = term if y is None else y + term
    x1 = x + _dot(y.astype(BF16), wout_ref[...])
    x1_ref[...] = x1
    xt = _rms(x1, gffn_ref[...])
    xt_ref[...] = xt
    logits = lax.dot_general(wr_ref[...], xt, _NT, precision=lax.Precision.HIGHEST,
                             preferred_element_type=F32) + br_ref[...]
    E = logits.shape[0]
    e_iota = lax.broadcasted_iota(I32, (E, tm), 0)
    l = logits
    vals, hots = [], []
    for k in range(TOP_K):
        m = jnp.max(l, axis=0, keepdims=True)
        idx = jnp.min(jnp.where(l == m, e_iota, E), axis=0, keepdims=True)
        hot = e_iota == idx
        l = jnp.where(hot, -jnp.inf, l)
        vals.append(m)
        hots.append(hot)
        eidx_ref[k:k + 1, :] = idx
    ex = [jnp.exp(v - vals[0]) for v in vals]
    den = ex[0]
    for k in range(1, TOP_K):
        den = den + ex[k]
    chosen = jnp.zeros((E, tm), F32)
    for k in range(TOP_K):
        wts_ref[k:k + 1, :] = ex[k] / den
        chosen = chosen + jnp.where(hots[k], 1.0, 0.0)
    carry = carry_ref[:, 0:1]
    prefix = _dot(chosen.astype(BF16), tri_ref[...]) + carry
    for k in range(TOP_K):
        rank_ref[k:k + 1, :] = jnp.sum(jnp.where(hots[k], prefix, 0.0), axis=0, keepdims=True).astype(I32)
    new_carry = carry + jnp.sum(chosen, axis=1, keepdims=True)
    carry_ref[...] = jnp.broadcast_to(new_carry, carry_ref.shape)
    cnt_ref[...] = jnp.broadcast_to(new_carry, cnt_ref.shape).astype(I32)


def _merge(x2, S, o_ret, o_moba, cq, kmem, vmem, g_mix, w_gate, b_gate, w_br_ret, w_br_moba, w_br_mem, w_out,
           g_ffn, w_router, b_router):
    N, D = x2.shape
    tm = TM_MERGE
    nS = S // tm
    M = kmem.shape[1]
    E = w_router.shape[1]
    tok = lambda w: pl.BlockSpec((tm, w), lambda i: (i, 0))
    const = lambda r, c: pl.BlockSpec((r, c), lambda i: (0, 0), pipeline_mode=pl.Buffered(1))
    mem_spec = pl.BlockSpec((1, M, MEM_W), lambda i: (i // nS, 0, 0))
    lanes_tok = lambda r, dt: (pl.BlockSpec((r, tm), lambda i: (0, i)), jax.ShapeDtypeStruct((r, N), dt))
    tri = (jnp.arange(tm)[:, None] < jnp.arange(tm)[None, :]).astype(BF16)
    e_spec, e_shape = lanes_tok(TOP_K, I32)
    w_spec, w_shape = lanes_tok(TOP_K, F32)
    r_spec, r_shape = lanes_tok(TOP_K, I32)
    return pl.pallas_call(
        _merge_kernel,
        grid=(N // tm,),
        in_specs=[tok(D), tok(RET_V), tok(MOBA_W), tok(MEM_W), mem_spec, mem_spec,
                  const(1, D), const(D, N_BRANCH * D), const(1, N_BRANCH * D),
                  const(RET_V, D), const(MOBA_W, D), const(MEM_W, D), const(D, D),
                  const(1, D), const(E, D), const(E, 1), const(tm, tm)],
        out_specs=[tok(D), tok(D), e_spec, w_spec, r_spec, pl.BlockSpec((E, LANES), lambda i: (0, 0))],
        out_shape=[jax.ShapeDtypeStruct((N, D), F32), jax.ShapeDtypeStruct((N, D), F32),
                   e_shape, w_shape, r_shape, jax.ShapeDtypeStruct((E, LANES), I32)],
        scratch_shapes=[pltpu.VMEM((E, LANES), F32)],
        compiler_params=_cparams(("arbitrary",), 56),
        name="merge",
    )(x2, o_ret, o_moba, cq, kmem, vmem, g_mix.reshape(1, D), w_gate, b_gate.reshape(1, N_BRANCH * D),
      w_br_ret.astype(BF16), w_br_moba.astype(BF16), w_br_mem.astype(BF16), w_out.astype(BF16),
      g_ffn.reshape(1, D), w_router.T, b_router.reshape(E, 1), tri)


def _dispatch_kernel(pad_end_ref, padded_ref, dest_ref, xt_ref, xb_ref, zeros_ref, zsem, sem):
    T = zeros_ref.shape[0]
    td = xt_ref.shape[0]

    @pl.when(pl.program_id(0) == 0)
    def _():
        zeros_ref[...] = jnp.zeros_like(zeros_ref)

        def fill(e):
            start = pl.multiple_of(pad_end_ref[e] - T, T)
            return pltpu.make_async_copy(zeros_ref, xb_ref.at[pl.ds(start, T)], zsem)

        for e in range(N_EXPERTS):
            @pl.when(padded_ref[e] > 0)
            def _():
                fill(e).start()
        for e in range(N_EXPERTS):
            @pl.when(padded_ref[e] > 0)
            def _():
                fill(e).wait()

        def tail(b):
            return pltpu.make_async_copy(zeros_ref, xb_ref.at[pl.ds(pl.multiple_of(b * T, T), T)], zsem)

        first_unused = pad_end_ref[N_EXPERTS - 1] // T
        n_blocks = xb_ref.shape[0] // T
        lax.fori_loop(first_unused, n_blocks, lambda b, c: (tail(b).start(), c)[1], 0)
        lax.fori_loop(first_unused, n_blocks, lambda b, c: (tail(b).wait(), c)[1], 0)

    def row(r, _):
        for k in range(TOP_K):
            pltpu.make_async_copy(xt_ref.at[pl.ds(r, 1)], xb_ref.at[pl.ds(dest_ref[k, r], 1)], sem).start()
        return 0

    lax.fori_loop(0, td, row, 0)
    for k in range(TOP_K):
        pltpu.make_async_copy(xt_ref, xb_ref.at[pl.ds(0, td)], sem).wait()


def _dispatch(xt, dest, pad_ends, padded, R):
    N, D = xt.shape
    td = T_DISPATCH
    grid_spec = pltpu.PrefetchScalarGridSpec(
        num_scalar_prefetch=2,
        grid=(N // td,),
        in_specs=[
            pl.BlockSpec((TOP_K, td), lambda i, pe, pd: (0, i), memory_space=pltpu.SMEM),
            pl.BlockSpec((td, D), lambda i, pe, pd: (i, 0)),
        ],
        out_specs=pl.BlockSpec(memory_space=pl.ANY),
        scratch_shapes=[pltpu.VMEM((MOE_T, D), F32), pltpu.SemaphoreType.DMA(()), pltpu.SemaphoreType.DMA(())],
    )
    return pl.pallas_call(
        _dispatch_kernel,
        grid_spec=grid_spec,
        out_shape=jax.ShapeDtypeStruct((R, D), F32),
        compiler_params=_cparams(("arbitrary",), 32),
        name="dispatch",
    )(pad_ends, padded, dest, xt)


def _expert_kernel(be_ref, nb_ref, xb_ref, w1_ref, b1_ref, w2_ref, b2_ref, perm_ref, yb_ref, w1p_ref, w2b_ref):
    j = pl.program_id(0)
    G = 2 * LANES

    @pl.when(j >= nb_ref[0])
    def _():
        yb_ref[...] = jnp.zeros_like(yb_ref)

    @pl.when(j < nb_ref[0])
    def _():
        e = be_ref[j]
        prev = be_ref[jnp.maximum(j - 1, 0)]

        @pl.when(jnp.logical_or(j == 0, e != prev))
        def _():
            for g in range(w1p_ref.shape[1] // G):
                w = w1_ref[0, :, g * G:(g + 1) * G].astype(BF16)
                w1p_ref[:, g * G:(g + 1) * G] = _dot(w, perm_ref[...]).astype(BF16)
            w2b_ref[...] = w2_ref[0].astype(BF16)

        x = xb_ref[...].astype(BF16)
        acts = []
        for g in range(w1p_ref.shape[1] // G):
            hg = _dot(x, w1p_ref[:, g * G:(g + 1) * G]) + b1_ref[0, :, g * G:(g + 1) * G]
            glu = jnp.minimum(hg[:, :LANES], SWIGLU_LIMIT)
            lin = jnp.clip(hg[:, LANES:], -SWIGLU_LIMIT, SWIGLU_LIMIT)
            acts.append((glu * jax.nn.sigmoid(SWIGLU_ALPHA * glu) * (lin + 1.0)).astype(BF16))
        act = jnp.concatenate(acts, axis=-1)
        yb_ref[...] = _dot(act, w2b_ref[...]) + b2_ref[0]


def _experts(xb, block_e, nblk, w1, b1p, w2, b2):
    R, D = xb.shape
    E, _, F2 = w1.shape
    F = F2 // 2
    T = MOE_T
    G = 2 * LANES
    c = np.arange(G)
    src = np.where(c < LANES, 2 * c, 2 * (c - LANES) + 1)
    perm = jnp.asarray(np.arange(G)[:, None] == src[None, :], dtype=BF16)
    blk = lambda j, be, nb: jnp.minimum(j, nb[0] - 1)
    grid_spec = pltpu.PrefetchScalarGridSpec(
        num_scalar_prefetch=2,
        grid=(R // T,),
        in_specs=[
            pl.BlockSpec((T, D), lambda j, be, nb: (blk(j, be, nb), 0)),
            pl.BlockSpec((1, D, F2), lambda j, be, nb: (be[blk(j, be, nb)], 0, 0)),
            pl.BlockSpec((1, 1, F2), lambda j, be, nb: (be[blk(j, be, nb)], 0, 0)),
            pl.BlockSpec((1, F, D), lambda j, be, nb: (be[blk(j, be, nb)], 0, 0)),
            pl.BlockSpec((1, 1, D), lambda j, be, nb: (be[blk(j, be, nb)], 0, 0)),
            pl.BlockSpec((G, G), lambda j, be, nb: (0, 0)),
        ],
        out_specs=pl.BlockSpec((T, D), lambda j, be, nb: (j, 0)),
        scratch_shapes=[pltpu.VMEM((D, F2), BF16), pltpu.VMEM((F, D), BF16)],
    )
    return pl.pallas_call(
        _expert_kernel,
        grid_spec=grid_spec,
        out_shape=jax.ShapeDtypeStruct((R, D), F32),
        compiler_params=_cparams(("arbitrary",), 56),
        name="experts",
    )(block_e, nblk, xb, w1, b1p, w2, b2.reshape(E, 1, D), perm)


def _combine_kernel(dest_ref, wts_ref, x1_ref, yb_ref, o_ref, buf_ref, sem):
    tc = x1_ref.shape[0]

    def row(r, _):
        for k in range(TOP_K):
            pltpu.make_async_copy(yb_ref.at[pl.ds(dest_ref[k, r], 1)], buf_ref.at[k, pl.ds(r, 1)], sem).start()
        return 0

    lax.fori_loop(0, tc, row, 0)
    for k in range(TOP_K):
        pltpu.make_async_copy(yb_ref.at[pl.ds(0, tc)], buf_ref.at[k], sem).wait()
    w = wts_ref[...]
    acc = x1_ref[...]
    for k in range(TOP_K):
        acc = acc + w[:, k:k + 1] * buf_ref[k]
    o_ref[...] = acc


def _combine(dest, wts_t, x1, yb):
    N, D = x1.shape
    tc = T_COMBINE
    return pl.pallas_call(
        _combine_kernel,
        grid=(N // tc,),
        in_specs=[
            pl.BlockSpec((TOP_K, tc), lambda i: (0, i), memory_space=pltpu.SMEM),
            pl.BlockSpec((tc, TOP_K), lambda i: (i, 0)),
            pl.BlockSpec((tc, D), lambda i: (i, 0)),
            pl.BlockSpec(memory_space=pl.ANY),
        ],
        out_specs=pl.BlockSpec((tc, D), lambda i: (i, 0)),
        out_shape=jax.ShapeDtypeStruct((N, D), F32),
        scratch_shapes=[pltpu.VMEM((TOP_K, tc, D), F32), pltpu.SemaphoreType.DMA(())],
        compiler_params=_cparams(("arbitrary",), 32),
        name="combine",
    )(dest, wts_t, x1, yb)


def _layer(x, mem, g_mix, w_in, b_gate, g_ret_out, g_moba_q, g_moba_k, g_mem, w_mem_kv, g_mem_q, g_mem_k,
           w_br_ret, w_br_moba, w_br_mem, w_out, g_ffn, w_router, b_router, w_mlp1, b_mlp1, w_mlp2, b_mlp2):
    B, S, D = x.shape
    N = B * S
    x2 = x.reshape(N, D)
    slopes_np = np.exp2(-8.0 * (np.arange(MOBA_HEADS, dtype=np.float64) + 1.0) / MOBA_HEADS)
    assert all(float(np.log2(s)).is_integer() for s in slopes_np)
    slopes = tuple(float(s) for s in slopes_np)

    w_in_bf = w_in.astype(BF16)
    kmem, vmem = _memkv(mem, g_mem, w_mem_kv, g_mem_k)
    rq, rk, rv, rg, mq, mk, mv, cq, km = _inproj(x2, B, S, g_mix, w_in_bf[:, :MIX_W], g_moba_q, g_moba_k, g_mem_q,
                                                 slopes)
    o_ret = _retention(rq, rk, rv, rg, g_ret_out)
    o_moba = _moba(mq, mk, mv, km, jnp.asarray(slopes, F32)).reshape(N, MOBA_W)
    x1, xt, eidx, wts, rank, cnt = _merge(x2, S, o_ret, o_moba, cq, kmem, vmem, g_mix, w_in_bf[:, MIX_W:], b_gate,
                                          w_br_ret, w_br_moba, w_br_mem, w_out, g_ffn, w_router, b_router)
    T = MOE_T
    counts = cnt[:, 0]
    padded = ((counts + T - 1) // T) * T
    pad_ends = jnp.cumsum(padded).astype(I32)
    pad_starts = pad_ends - padded
    onehot = eidx[:, :, None] == jnp.arange(N_EXPERTS, dtype=I32)[None, None, :]
    dest = jnp.sum(jnp.where(onehot, pad_starts[None, None, :], 0), axis=-1) + rank
    NB = -(-(N * TOP_K) // T) + N_EXPERTS
    ends_before = jnp.sum((pad_ends[None, :] <= (jnp.arange(NB, dtype=I32) * T)[:, None]).astype(I32), axis=1)
    block_e = jnp.minimum(ends_before, N_EXPERTS - 1).astype(I32)
    nblk = (pad_ends[-1:] // T).astype(I32)

    xb = _dispatch(xt, dest, pad_ends, padded.astype(I32), NB * T)
    F2 = w_mlp1.shape[-1]
    b1p = b_mlp1.reshape(N_EXPERTS, F2 // (2 * LANES), LANES, 2).transpose(0, 1, 3, 2).reshape(N_EXPERTS, 1, F2)
    yb = _experts(xb, block_e, nblk, w_mlp1, b1p, w_mlp2, b_mlp2)
    out = _combine(dest, wts.T, x1, yb)
    return out.reshape(B, S, D)


def kernel(x, mem, g_mix, w_in, b_gate, g_ret_out, g_moba_q, g_moba_k, g_mem, w_mem_kv, g_mem_q, g_mem_k, w_br_ret, w_br_moba, w_br_mem, w_out, g_ffn, w_router, b_router, w_mlp1, b_mlp1, w_mlp2, b_mlp2):
    for l in range(g_mix.shape[0]):
        x = _layer(x, mem, g_mix[l], w_in[l], b_gate[l], g_ret_out[l], g_moba_q[l], g_moba_k[l], g_mem[l],
                   w_mem_kv[l], g_mem_q[l], g_mem_k[l], w_br_ret[l], w_br_moba[l], w_br_mem[l], w_out[l],
                   g_ffn[l], w_router[l], b_router[l], w_mlp1[l], b_mlp1[l], w_mlp2[l], b_mlp2[l])
    return x
```

```python
import functools

import jax
import jax.numpy as jnp
import numpy as np
from jax import lax
from jax.experimental import pallas as pl
from jax.experimental.pallas import tpu as pltpu

F32 = jnp.float32
BF16 = jnp.bfloat16
I32 = jnp.int32

EPS = 1e-5
NEG = -1e30

RET_HEADS = 4
RET_DK = 64
RET_DV = 128
RET_CHUNK = 128
MOBA_HEADS = 8
MOBA_HD = 64
MOBA_BLOCK = 256
MOBA_TOPK = 3
MEM_HEADS = 4
MEM_HD = 128
N_BRANCH = 3
N_EXPERTS = 32
TOP_K = 4
SWIGLU_LIMIT = 7.0
SWIGLU_ALPHA = 1.702

RET_Q = RET_HEADS * RET_DK
RET_V = RET_HEADS * RET_DV
MOBA_W = MOBA_HEADS * MOBA_HD
MEM_W = MEM_HEADS * MEM_HD
MIX_W = 2 * RET_Q + 2 * RET_V + 3 * MOBA_W + MEM_W

LANES = 128
MOBA_PAIRS = MOBA_HEADS // 2
MOBA_HEADS_PER_STEP = 8
BIAS_LANE0 = MOBA_HD
ONE_LANE = 80
VSUM_LANE = MOBA_HD

MOE_T = 256
TM_PROJ = 512
TM_MERGE = 512
TQ_RET = 512
ROUTE_T = 256
SEG_ALIGN = 8
LOCAL_ROWS = -(-(ROUTE_T * TOP_K + N_EXPERTS * (SEG_ALIGN - 1)) // MOE_T) * MOE_T
SEG_CHUNKS = tuple(2 ** p for p in range(ROUTE_T.bit_length() - 1, SEG_ALIGN.bit_length() - 2, -1))
TOTAL_CHUNKS = tuple(2 ** p for p in range(LOCAL_ROWS.bit_length() - 1, SEG_ALIGN.bit_length() - 2, -1))

_NT = (((1,), (1,)), ((), ()))
_TN = (((0,), (0,)), ((), ()))


def _rms(x, g):
    return x * lax.rsqrt(jnp.mean(x * x, axis=-1, keepdims=True) + EPS) * g


def _dot(a, b):
    return jnp.dot(a, b, preferred_element_type=F32)


def _cparams(sem, vmem_mb):
    return pltpu.CompilerParams(dimension_semantics=sem, vmem_limit_bytes=vmem_mb * 1024 * 1024)


def _memkv_kernel(mem_ref, g_ref, w_ref, gk_ref, k_ref, v_ref):
    m = _rms(mem_ref[0], g_ref[...]).astype(BF16)
    kv = _dot(m, w_ref[...])
    ks = [_rms(kv[:, h * MEM_HD:(h + 1) * MEM_HD], gk_ref[...]) for h in range(MEM_HEADS)]
    k_ref[0] = jnp.concatenate(ks, axis=-1).astype(BF16)
    v_ref[0] = kv[:, MEM_W:].astype(BF16)


def _memkv(mem, g_mem, w_mem_kv, g_mem_k):
    B, M, D = mem.shape
    return pl.pallas_call(
        _memkv_kernel,
        grid=(B,),
        in_specs=[
            pl.BlockSpec((1, M, D), lambda b: (b, 0, 0)),
            pl.BlockSpec((1, D), lambda b: (0, 0)),
            pl.BlockSpec((D, 2 * MEM_W), lambda b: (0, 0)),
            pl.BlockSpec((1, MEM_HD), lambda b: (0, 0)),
        ],
        out_specs=[
            pl.BlockSpec((1, M, MEM_W), lambda b: (b, 0, 0)),
            pl.BlockSpec((1, M, MEM_W), lambda b: (b, 0, 0)),
        ],
        out_shape=[jax.ShapeDtypeStruct((B, M, MEM_W), BF16)] * 2,
        compiler_params=_cparams(("arbitrary",), 32),
        name="memkv",
    )(mem, g_mem.reshape(1, D), w_mem_kv.astype(BF16), g_mem_k.reshape(1, MEM_HD))


def _head_pair_norm(a2, g2, lane):
    sq = a2 * a2
    lo = lane < MOBA_HD
    ss_lo = jnp.sum(jnp.where(lo, sq, 0.0), axis=-1, keepdims=True)
    ss_hi = jnp.sum(jnp.where(lo, 0.0, sq), axis=-1, keepdims=True)
    inv = jnp.where(lo, lax.rsqrt(ss_lo / MOBA_HD + EPS), lax.rsqrt(ss_hi / MOBA_HD + EPS))
    return a2 * inv * g2


def _inproj_kernel(slopes, seq_tiles, x_ref, gmix_ref, w_ref, gq_ref, gk_ref, gc_ref,
                   rq_ref, rk_ref, rv_ref, rg_ref, mq_ref, mk_ref, mv_ref, cq_ref, kmean_ref):
    tm = x_ref.shape[0]
    blocks_per_tile = tm // MOBA_BLOCK
    h = _rms(x_ref[...], gmix_ref[...]).astype(BF16)
    col = [0]

    def proj(width):
        a = _dot(h, w_ref[:, col[0]:col[0] + width])
        col[0] += width
        return a

    a = proj(2 * RET_Q)
    for hh in range(RET_HEADS):
        rq_ref[0, hh] = a[:, hh * RET_DK:(hh + 1) * RET_DK].astype(BF16)
        rk_ref[0, hh] = (a[:, RET_Q + hh * RET_DK:RET_Q + (hh + 1) * RET_DK] * (RET_DK ** -0.5)).astype(BF16)
    rv_ref[...] = proj(RET_V).astype(BF16)
    rg_ref[...] = proj(RET_V).astype(BF16)

    lane = lax.broadcasted_iota(I32, (tm, LANES), 1)
    row = lax.broadcasted_iota(I32, (tm, LANES), 0)
    lo = lane < MOBA_HD
    q_tail = jnp.where(lane == ONE_LANE, 1.0, 0.0)
    a = proj(MOBA_W)
    for p in range(MOBA_PAIRS):
        n2 = _head_pair_norm(a[:, p * LANES:(p + 1) * LANES], gq_ref[...], lane) * (MOBA_HD ** -0.5)
        mq_ref[0, 2 * p] = jnp.where(lo, n2, q_tail).astype(BF16)
        mq_ref[0, 2 * p + 1] = jnp.where(lo, pltpu.roll(n2, MOBA_HD, 1), q_tail).astype(BF16)
    blk = (pl.program_id(0) % seq_tiles) * blocks_per_tile + row // MOBA_BLOCK
    onehot_tail = jnp.where(lane == BIAS_LANE0 + blk, 1.0, 0.0)
    off = (row % MOBA_BLOCK).astype(F32)
    a = proj(MOBA_W)
    for p in range(MOBA_PAIRS):
        n2 = _head_pair_norm(a[:, p * LANES:(p + 1) * LANES], gk_ref[...], lane)
        for j in range(blocks_per_tile):
            kmean_ref[0, 0, p, j:j + 1, :] = jnp.mean(n2[j * MOBA_BLOCK:(j + 1) * MOBA_BLOCK], axis=0, keepdims=True)
        for s, src in ((0, n2), (1, pltpu.roll(n2, MOBA_HD, 1))):
            tail = jnp.where(lane == ONE_LANE, slopes[2 * p + s] * off, onehot_tail)
            mk_ref[0, 2 * p + s] = jnp.where(lo, src, tail).astype(BF16)
    v_tail = jnp.where(lane == VSUM_LANE, 1.0, 0.0)
    a = proj(MOBA_W)
    for p in range(MOBA_PAIRS):
        a2 = a[:, p * LANES:(p + 1) * LANES]
        mv_ref[0, 2 * p] = jnp.where(lo, a2, v_tail).astype(BF16)
        mv_ref[0, 2 * p + 1] = jnp.where(lo, pltpu.roll(a2, MOBA_HD, 1), v_tail).astype(BF16)
    a = proj(MEM_W)
    cq = [_rms(a[:, hh * MEM_HD:(hh + 1) * MEM_HD], gc_ref[...]) for hh in range(MEM_HEADS)]
    cq_ref[...] = jnp.concatenate(cq, axis=-1).astype(BF16)


def _inproj(x2, B, S, g_mix, w_mix, g_moba_q, g_moba_k, g_mem_q, slopes):
    N, D = x2.shape
    tm = TM_PROJ
    nS = S // tm
    bpt = tm // MOBA_BLOCK
    tok = lambda i: (i, 0)
    headmaj = lambda i: (i // nS, 0, i % nS, 0)
    g2 = lambda g: jnp.concatenate([g, g]).reshape(1, LANES)
    outs = pl.pallas_call(
        functools.partial(_inproj_kernel, slopes, nS),
        grid=(N // tm,),
        in_specs=[
            pl.BlockSpec((tm, D), tok),
            pl.BlockSpec((1, D), lambda i: (0, 0)),
            pl.BlockSpec((D, MIX_W), lambda i: (0, 0)),
            pl.BlockSpec((1, LANES), lambda i: (0, 0)),
            pl.BlockSpec((1, LANES), lambda i: (0, 0)),
            pl.BlockSpec((1, MEM_HD), lambda i: (0, 0)),
        ],
        out_specs=[
            pl.BlockSpec((1, RET_HEADS, tm, RET_DK), headmaj),
            pl.BlockSpec((1, RET_HEADS, tm, RET_DK), headmaj),
            pl.BlockSpec((tm, RET_V), tok),
            pl.BlockSpec((tm, RET_V), tok),
            pl.BlockSpec((1, MOBA_HEADS, tm, LANES), headmaj),
            pl.BlockSpec((1, MOBA_HEADS, tm, LANES), headmaj),
            pl.BlockSpec((1, MOBA_HEADS, tm, LANES), headmaj),
            pl.BlockSpec((tm, MEM_W), tok),
            pl.BlockSpec((1, 1, MOBA_PAIRS, bpt, LANES), lambda i: (i // nS, i % nS, 0, 0, 0)),
        ],
        out_shape=[
            jax.ShapeDtypeStruct((B, RET_HEADS, S, RET_DK), BF16),
            jax.ShapeDtypeStruct((B, RET_HEADS, S, RET_DK), BF16),
            jax.ShapeDtypeStruct((N, RET_V), BF16),
            jax.ShapeDtypeStruct((N, RET_V), BF16),
            jax.ShapeDtypeStruct((B, MOBA_HEADS, S, LANES), BF16),
            jax.ShapeDtypeStruct((B, MOBA_HEADS, S, LANES), BF16),
            jax.ShapeDtypeStruct((B, MOBA_HEADS, S, LANES), BF16),
            jax.ShapeDtypeStruct((N, MEM_W), BF16),
            jax.ShapeDtypeStruct((B, nS, MOBA_PAIRS, bpt, LANES), F32),
        ],
        compiler_params=_cparams(("arbitrary",), 56),
        name="inproj",
    )(x2, g_mix.reshape(1, D), w_mix, g2(g_moba_q), g2(g_moba_k), g_mem_q.reshape(1, MEM_HD))
    rq, rk, rv, rg, mq, mk, mv, cq, kmean_pairs = outs
    km = kmean_pairs.reshape(B, nS, MOBA_PAIRS, bpt, 2, MOBA_HD).transpose(0, 2, 4, 1, 3, 5)
    km = km.reshape(B, MOBA_HEADS, nS * bpt, MOBA_HD)
    km = jnp.pad(km, ((0, 0), (0, 0), (0, 0), (0, LANES - MOBA_HD)))
    return rq, rk, rv, rg, mq, mk, mv, cq, km


def _retention_kernel(q_ref, k_ref, v_ref, rg_ref, din_ref, dq_ref, dk_ref, dc_ref, g_ref, o_ref, state_ref):
    @pl.when(pl.program_id(2) == 0)
    def _():
        state_ref[...] = jnp.zeros_like(state_ref)

    C = RET_CHUNK
    for c in range(q_ref.shape[2] // C):
        rows = slice(c * C, (c + 1) * C)
        q = q_ref[0, 0, rows, :]
        k = k_ref[0, 0, rows, :]
        v = v_ref[rows, :]
        state = state_ref[...]
        scores = lax.dot_general(q, k, _NT, preferred_element_type=F32) * din_ref[0]
        intra = _dot(scores.astype(BF16), v)
        cross = _dot(q, state.astype(BF16)) * dq_ref[0]
        kd = (k.astype(F32) * dk_ref[0]).astype(BF16)
        state_ref[...] = dc_ref[0] * state + lax.dot_general(kd, v, _TN, preferred_element_type=F32)
        o = _rms(intra + cross, g_ref[...])
        o_ref[rows, :] = (o * jax.nn.silu(rg_ref[rows, :].astype(F32))).astype(BF16)


def _retention_decays():
    H, C = RET_HEADS, RET_CHUNK
    log_g = jnp.log1p(-jnp.exp2(-5.0 - jnp.arange(H, dtype=F32)))
    i = jnp.arange(C, dtype=F32)
    diff = i[:, None] - i[None, :]
    decay_in = jnp.where(diff >= 0, jnp.exp(jnp.maximum(diff, 0.0)[None] * log_g[:, None, None]), 0.0)
    decay_k = jnp.exp((C - 1 - i)[None, :] * log_g[:, None])
    decay_q = jnp.exp((i + 1)[None, :] * log_g[:, None])
    decay_chunk = jnp.exp(C * log_g)
    dq = jnp.broadcast_to(decay_q[:, :, None], (H, C, RET_DV))
    dk = jnp.broadcast_to(decay_k[:, :, None], (H, C, RET_DK))
    dc = jnp.broadcast_to(decay_chunk[:, None, None], (H, RET_DK, RET_DV))
    return decay_in, dq, dk, dc


def _retention(rq, rk, rv, rg, g_ret_out):
    B, H, S, dk = rq.shape
    tq = TQ_RET
    nT = S // tq
    C = RET_CHUNK
    din, dq, dk_, dc = _retention_decays()
    qk_spec = pl.BlockSpec((1, 1, tq, dk), lambda b, h, t: (b, h, t, 0))
    tok_spec = pl.BlockSpec((tq, RET_DV), lambda b, h, t: (b * nT + t, h))
    per_head = lambda r, c: pl.BlockSpec((1, r, c), lambda b, h, t: (h, 0, 0))
    return pl.pallas_call(
        _retention_kernel,
        grid=(B, H, nT),
        in_specs=[qk_spec, qk_spec, tok_spec, tok_spec,
                  per_head(C, C), per_head(C, RET_DV), per_head(C, RET_DK), per_head(RET_DK, RET_DV),
                  pl.BlockSpec((1, RET_DV), lambda b, h, t: (0, 0))],
        out_specs=tok_spec,
        out_shape=jax.ShapeDtypeStruct((B * S, RET_V), BF16),
        scratch_shapes=[pltpu.VMEM((RET_DK, RET_DV), F32)],
        compiler_params=_cparams(("arbitrary", "arbitrary", "arbitrary"), 32),
        name="retention",
    )(rq, rk, rv, rg, din, dq, dk_, dc, g_ret_out.reshape(1, RET_DV))


def _moba_kernel(slopes_ref, q_ref, k_ref, v_ref, km_ref, shift_ref, causal_ref, o_ref):
    NH = q_ref.shape[1]
    head0 = pl.program_id(1) * NH
    i = pl.program_id(2)
    BS = MOBA_BLOCK
    nb = km_ref.shape[2]
    heads = range(NH)
    n_iota = lax.broadcasted_iota(I32, (nb, BS), 0)
    past = n_iota < i
    qa = [q_ref[0, s] for s in heads]
    q_aug = []
    for s in heads:
        km = km_ref[0, s]
        hi = km.astype(BF16)
        mid = (km - hi.astype(F32)).astype(BF16)
        lo = (km - hi.astype(F32) - mid.astype(F32)).astype(BF16)
        g3 = lax.dot_general(jnp.concatenate([hi, mid, lo], axis=0), qa[s], _NT, preferred_element_type=F32)
        gate = (g3[:nb] + g3[nb:2 * nb]) + g3[2 * nb:]
        g = jnp.where(past, gate, -jnp.inf)
        rank = jnp.zeros((nb, BS), I32)
        for m in range(nb - 1):
            gm = g[m:m + 1, :]
            rank = rank + jnp.where(gm > g, 1, jnp.where(gm == g, jnp.where(m < n_iota, 1, 0), 0))
        sel = jnp.where(past, rank, MOBA_TOPK) < MOBA_TOPK
        bias = jnp.where(sel, (n_iota - i).astype(F32) * (slopes_ref[head0 + s] * BS), NEG).astype(BF16)
        placed = lax.dot_general(bias, shift_ref[...], _TN, preferred_element_type=F32)
        q_aug.append((qa[s].astype(F32) + placed).astype(BF16))

    own = pl.ds(pl.multiple_of(i * BS, BS), BS)
    m0, acc0 = [], []
    for s in heads:
        sc = lax.dot_general(qa[s], k_ref[0, s, own, :], _NT, preferred_element_type=F32) + causal_ref[...]
        m = jnp.max(sc, axis=-1, keepdims=True)
        m0.append(m)
        acc0.append(_dot(jnp.exp(sc - m).astype(BF16), v_ref[0, s, own, :]))

    def body(j, carry):
        ms, accs = carry
        blk = pl.ds(pl.multiple_of(j * BS, BS), BS)
        new_m, new_acc = [], []
        for s in heads:
            sj = lax.dot_general(q_aug[s], k_ref[0, s, blk, :], _NT, preferred_element_type=F32)
            m_new = jnp.maximum(ms[s], jnp.max(sj, axis=-1, keepdims=True))
            pj = jnp.exp(sj - m_new).astype(BF16)
            new_acc.append(jnp.exp(ms[s] - m_new) * accs[s] + _dot(pj, v_ref[0, s, blk, :]))
            new_m.append(m_new)
        return tuple(new_m), tuple(new_acc)

    _, accs = lax.fori_loop(0, i, body, (tuple(m0), tuple(acc0)))
    outs = [acc / acc[:, VSUM_LANE:VSUM_LANE + 1] for acc in accs]
    lane = lax.broadcasted_iota(I32, (BS, LANES), 1)
    pairs = [jnp.where(lane < MOBA_HD, outs[2 * p], pltpu.roll(outs[2 * p + 1], MOBA_HD, 1)) for p in range(NH // 2)]
    o_ref[0] = jnp.concatenate(pairs, axis=-1).astype(BF16)


def _moba(mq, mk, mv, km, slopes):
    B, H, S, _ = mq.shape
    BS = MOBA_BLOCK
    NH = MOBA_HEADS_PER_STEP
    nq = S // BS
    nb = km.shape[2]
    assert BIAS_LANE0 + nb <= ONE_LANE
    shift = (jnp.arange(nb)[:, None] + BIAS_LANE0 == jnp.arange(LANES)[None, :]).astype(BF16)
    causal = jnp.where(jnp.arange(BS)[:, None] >= jnp.arange(BS)[None, :], 0.0, NEG).astype(F32)
    per_head = lambda rows: pl.BlockSpec((1, NH, rows, LANES), lambda b, p, i, sl: (b, p, 0, 0))
    grid_spec = pltpu.PrefetchScalarGridSpec(
        num_scalar_prefetch=1,
        grid=(B, H // NH, nq),
        in_specs=[
            pl.BlockSpec((1, NH, BS, LANES), lambda b, p, i, sl: (b, p, i, 0)),
            per_head(S), per_head(S), per_head(nb),
            pl.BlockSpec((nb, LANES), lambda b, p, i, sl: (0, 0)),
            pl.BlockSpec((BS, BS), lambda b, p, i, sl: (0, 0)),
        ],
        out_specs=pl.BlockSpec((1, BS, NH * MOBA_HD), lambda b, p, i, sl: (b, i, p)),
    )
    return pl.pallas_call(
        _moba_kernel,
        grid_spec=grid_spec,
        out_shape=jax.ShapeDtypeStruct((B, S, MOBA_W), BF16),
        compiler_params=_cparams(("arbitrary", "arbitrary", "arbitrary"), 48),
        name="moba",
    )(slopes, mq, mk, mv, km, shift, causal)


def _merge_kernel(x_ref, oret_ref, omoba_ref, cq_ref, kmem_ref, vmem_ref, gmix_ref, wg_ref, bg_ref,
                  wbr_ref, wbm_ref, wbc_ref, wout_ref, gffn_ref, wr_ref, br_ref, tri_ref,
                  x1_ref, xt_ref, eidx_ref, wts_ref, rank_ref, cnt_ref):
    D = x_ref.shape[1]
    tm = x_ref.shape[0]

    x = x_ref[...]
    h = _rms(x, gmix_ref[...]).astype(BF16)
    cq = cq_ref[...]
    om = []
    for hh in range(MEM_HEADS):
        cols = slice(hh * MEM_HD, (hh + 1) * MEM_HD)
        sc = lax.dot_general(cq[:, cols], kmem_ref[0, :, cols], _NT, preferred_element_type=F32) * (MEM_HD ** -0.5)
        sc = sc - jnp.max(sc, axis=-1, keepdims=True)
        p = jnp.exp(sc)
        p = p / jnp.sum(p, axis=-1, keepdims=True)
        om.append(_dot(p.astype(BF16), vmem_ref[0, :, cols]))
    omem = jnp.concatenate(om, axis=-1).astype(BF16)
    y = None
    for br, (o, w_ref) in enumerate(((oret_ref[...], wbr_ref), (omoba_ref[...], wbm_ref), (omem, wbc_ref))):
        gl = _dot(h, wg_ref[:, br * D:(br + 1) * D]) + bg_ref[:, br * D:(br + 1) * D]
        term = jax.nn.sigmoid(gl) * _dot(o, w_ref[...])
        y = term if y is None else y + term
    x1 = x + _dot(y.astype(BF16), wout_ref[...])
    x1_ref[...] = x1
    xt = _rms(x1, gffn_ref[...])
    xt_ref[...] = xt.astype(BF16)
    logits = lax.dot_general(wr_ref[...], xt, _NT, precision=lax.Precision.HIGHEST,
                             preferred_element_type=F32) + br_ref[...]
    E = logits.shape[0]
    e_iota = lax.broadcasted_iota(I32, (E, tm), 0)
    l = logits
    vals, hots = [], []
    for k in range(TOP_K):
        m = jnp.max(l, axis=0, keepdims=True)
        idx = jnp.min(jnp.where(l == m, e_iota, E), axis=0, keepdims=True)
        hot = e_iota == idx
        l = jnp.where(hot, -jnp.inf, l)
        vals.append(m)
        hots.append(hot)
        eidx_ref[k:k + 1, :] = idx
    ex = [jnp.exp(v - vals[0]) for v in vals]
    den = ex[0]
    for k in range(1, TOP_K):
        den = den + ex[k]
    chosen = jnp.zeros((E, tm), F32)
    for k in range(TOP_K):
        wts_ref[k:k + 1, :] = ex[k] / den
        chosen = chosen + jnp.where(hots[k], 1.0, 0.0)
    RT = tri_ref.shape[0]
    for t in range(tm // RT):
        cols = slice(t * RT, (t + 1) * RT)
        prefix = _dot(chosen[:, cols].astype(BF16), tri_ref[...])
        for k in range(TOP_K):
            rank_ref[k:k + 1, cols] = jnp.sum(jnp.where(hots[k][:, cols], prefix, 0.0), axis=0, keepdims=True).astype(I32)
        counts = jnp.sum(chosen[:, cols], axis=1, keepdims=True)
        cnt_ref[t] = jnp.broadcast_to(counts, cnt_ref.shape[1:]).astype(I32)


def _merge(x2, S, o_ret, o_moba, cq, kmem, vmem, g_mix, w_gate, b_gate, w_br_ret, w_br_moba, w_br_mem, w_out,
           g_ffn, w_router, b_router):
    N, D = x2.shape
    tm = TM_MERGE
    nS = S // tm
    M = kmem.shape[1]
    E = w_router.shape[1]
    tok = lambda w: pl.BlockSpec((tm, w), lambda i: (i, 0))
    const = lambda r, c: pl.BlockSpec((r, c), lambda i: (0, 0), pipeline_mode=pl.Buffered(1))
    mem_spec = pl.BlockSpec((1, M, MEM_W), lambda i: (i // nS, 0, 0))
    lanes_tok = lambda r, dt: (pl.BlockSpec((r, tm), lambda i: (0, i)), jax.ShapeDtypeStruct((r, N), dt))
    RT = ROUTE_T
    tri = (jnp.arange(RT)[:, None] < jnp.arange(RT)[None, :]).astype(BF16)
    e_spec, e_shape = lanes_tok(TOP_K, I32)
    w_spec, w_shape = lanes_tok(TOP_K, F32)
    r_spec, r_shape = lanes_tok(TOP_K, I32)
    return pl.pallas_call(
        _merge_kernel,
        grid=(N // tm,),
        in_specs=[tok(D), tok(RET_V), tok(MOBA_W), tok(MEM_W), mem_spec, mem_spec,
                  const(1, D), const(D, N_BRANCH * D), const(1, N_BRANCH * D),
                  const(RET_V, D), const(MOBA_W, D), const(MEM_W, D), const(D, D),
                  const(1, D), const(E, D), const(E, 1), const(RT, RT)],
        out_specs=[tok(D), tok(D), e_spec, w_spec, r_spec, pl.BlockSpec((tm // RT, E, LANES), lambda i: (i, 0, 0))],
        out_shape=[jax.ShapeDtypeStruct((N, D), F32), jax.ShapeDtypeStruct((N, D), BF16),
                   e_shape, w_shape, r_shape, jax.ShapeDtypeStruct((N // RT, E, LANES), I32)],
        compiler_params=_cparams(("arbitrary",), 56),
        name="merge",
    )(x2, o_ret, o_moba, cq, kmem, vmem, g_mix.reshape(1, D), w_gate, b_gate.reshape(1, N_BRANCH * D),
      w_br_ret.astype(BF16), w_br_moba.astype(BF16), w_br_mem.astype(BF16), w_out.astype(BF16),
      g_ffn.reshape(1, D), w_router.T, b_router.reshape(E, 1), tri)


def _segment_copies(seg_ref, lstart_ref, gstart_ref, tile, local_ref, slots_ref, sem, to_slots, fn):
    def per_expert(e, carry):
        idx = tile * N_EXPERTS + e
        size = seg_ref[idx]
        off = jnp.int32(0)
        for chunk in SEG_CHUNKS:
            take = (size & chunk) != 0
            lo = pl.ds(pl.multiple_of(lstart_ref[idx] + off, SEG_ALIGN), chunk)
            gl = pl.ds(pl.multiple_of(gstart_ref[idx] + off, SEG_ALIGN), chunk)
            src, dst = (local_ref.at[lo], slots_ref.at[gl]) if to_slots else (slots_ref.at[gl], local_ref.at[lo])

            @pl.when(take)
            def _():
                fn(pltpu.make_async_copy(src, dst, sem))

            off = off + jnp.where(take, chunk, 0)
        return carry

    lax.fori_loop(0, N_EXPERTS, per_expert, 0)


def _wait_tile(total_ref, tile, local_ref, slots_ref, sem, to_slots):
    total = total_ref[tile]
    for chunk in TOTAL_CHUNKS:
        lo, gl = local_ref.at[pl.ds(0, chunk)], slots_ref.at[pl.ds(0, chunk)]
        src, dst = (lo, gl) if to_slots else (gl, lo)

        @pl.when((total & chunk) != 0)
        def _():
            pltpu.make_async_copy(src, dst, sem).wait()


def _dispatch_kernel(seg_ref, lstart_ref, gstart_ref, total_ref, pad_end_ref, padded_ref, ld_ref, xt_ref, xb_ref,
                     ybuf_ref, zeros_ref, zsem, sems):
    T = zeros_ref.shape[0]
    i = pl.program_id(0)
    n = pl.num_programs(0)
    slot = i % 2

    @pl.when(i == 0)
    def _():
        zeros_ref[...] = jnp.zeros_like(zeros_ref)

        def fill(start):
            return pltpu.make_async_copy(zeros_ref, xb_ref.at[pl.ds(pl.multiple_of(start, T), T)], zsem)

        for e in range(N_EXPERTS):
            @pl.when(padded_ref[e] > 0)
            def _():
                fill(pad_end_ref[e] - T).start()
        for e in range(N_EXPERTS):
            @pl.when(padded_ref[e] > 0)
            def _():
                fill(pad_end_ref[e] - T).wait()
        first_unused = pad_end_ref[N_EXPERTS - 1] // T
        n_blocks = xb_ref.shape[0] // T
        lax.fori_loop(first_unused, n_blocks, lambda b, c: (fill(b * T).start(), c)[1], 0)
        lax.fori_loop(first_unused, n_blocks, lambda b, c: (fill(b * T).wait(), c)[1], 0)

    @pl.when(i >= 2)
    def _():
        _wait_tile(total_ref, i - 2, ybuf_ref.at[slot], xb_ref, sems.at[slot], True)

    L = ybuf_ref.shape[1]
    RT = xt_ref.shape[0]
    r_iota = lax.broadcasted_iota(I32, (L, RT), 0)
    ld = ld_ref[...]
    onehot = jnp.zeros((L, RT), F32)
    for k in range(TOP_K):
        onehot = jnp.where(r_iota == ld[k:k + 1, :], 1.0, onehot)
    onehot = onehot.astype(BF16)
    ybuf_ref[slot] = _dot(onehot, xt_ref[...])
    _segment_copies(seg_ref, lstart_ref, gstart_ref, i, ybuf_ref.at[slot], xb_ref, sems.at[slot], True,
                    lambda c: c.start())

    @pl.when(i == n - 1)
    def _():
        @pl.when(n >= 2)
        def _():
            _wait_tile(total_ref, i - 1, ybuf_ref.at[1 - slot], xb_ref, sems.at[1 - slot], True)

        _wait_tile(total_ref, i, ybuf_ref.at[slot], xb_ref, sems.at[slot], True)


def _dispatch(xt, ld, seg, lstart, gstart, totals, pad_ends, padded, R):
    N, D = xt.shape
    RT = ROUTE_T
    grid_spec = pltpu.PrefetchScalarGridSpec(
        num_scalar_prefetch=6,
        grid=(N // RT,),
        in_specs=[
            pl.BlockSpec((TOP_K, RT), lambda i, *_: (0, i)),
            pl.BlockSpec((RT, D), lambda i, *_: (i, 0)),
        ],
        out_specs=pl.BlockSpec(memory_space=pl.ANY),
        scratch_shapes=[pltpu.VMEM((2, LOCAL_ROWS, D), F32), pltpu.VMEM((MOE_T, D), F32),
                        pltpu.SemaphoreType.DMA(()), pltpu.SemaphoreType.DMA((2,))],
    )
    return pl.pallas_call(
        _dispatch_kernel,
        grid_spec=grid_spec,
        out_shape=jax.ShapeDtypeStruct((R, D), F32),
        compiler_params=_cparams(("arbitrary",), 48),
        name="dispatch",
    )(seg, lstart, gstart, totals, pad_ends, padded, ld, xt)


def _expert_kernel(be_ref, nb_ref, xb_ref, w1_ref, b1_ref, w2_ref, b2_ref, perm_ref, yb_ref, w1p_ref, w2b_ref):
    j = pl.program_id(0)
    G = 2 * LANES

    @pl.when(j >= nb_ref[0])
    def _():
        yb_ref[...] = jnp.zeros_like(yb_ref)

    @pl.when(j < nb_ref[0])
    def _():
        e = be_ref[j]
        prev = be_ref[jnp.maximum(j - 1, 0)]

        @pl.when(jnp.logical_or(j == 0, e != prev))
        def _():
            for g in range(w1p_ref.shape[1] // G):
                w = w1_ref[0, :, g * G:(g + 1) * G].astype(BF16)
                w1p_ref[:, g * G:(g + 1) * G] = _dot(w, perm_ref[...]).astype(BF16)
            w2b_ref[...] = w2_ref[0].astype(BF16)

        x = xb_ref[...].astype(BF16)
        acts = []
        for g in range(w1p_ref.shape[1] // G):
            hg = _dot(x, w1p_ref[:, g * G:(g + 1) * G]) + b1_ref[0, :, g * G:(g + 1) * G]
            glu = jnp.minimum(hg[:, :LANES], SWIGLU_LIMIT)
            lin = jnp.clip(hg[:, LANES:], -SWIGLU_LIMIT, SWIGLU_LIMIT)
            acts.append((glu * jax.nn.sigmoid(SWIGLU_ALPHA * glu) * (lin + 1.0)).astype(BF16))
        act = jnp.concatenate(acts, axis=-1)
        yb_ref[...] = _dot(act, w2b_ref[...]) + b2_ref[0]


def _experts(xb, block_e, nblk, w1, b1p, w2, b2):
    R, D = xb.shape
    E, _, F2 = w1.shape
    F = F2 // 2
    T = MOE_T
    G = 2 * LANES
    c = np.arange(G)
    src = np.where(c < LANES, 2 * c, 2 * (c - LANES) + 1)
    perm = jnp.asarray(np.arange(G)[:, None] == src[None, :], dtype=BF16)
    blk = lambda j, be, nb: jnp.minimum(j, nb[0] - 1)
    grid_spec = pltpu.PrefetchScalarGridSpec(
        num_scalar_prefetch=2,
        grid=(R // T,),
        in_specs=[
            pl.BlockSpec((T, D), lambda j, be, nb: (blk(j, be, nb), 0)),
            pl.BlockSpec((1, D, F2), lambda j, be, nb: (be[blk(j, be, nb)], 0, 0)),
            pl.BlockSpec((1, 1, F2), lambda j, be, nb: (be[blk(j, be, nb)], 0, 0)),
            pl.BlockSpec((1, F, D), lambda j, be, nb: (be[blk(j, be, nb)], 0, 0)),
            pl.BlockSpec((1, 1, D), lambda j, be, nb: (be[blk(j, be, nb)], 0, 0)),
            pl.BlockSpec((G, G), lambda j, be, nb: (0, 0)),
        ],
        out_specs=pl.BlockSpec((T, D), lambda j, be, nb: (j, 0)),
        scratch_shapes=[pltpu.VMEM((D, F2), BF16), pltpu.VMEM((F, D), BF16)],
    )
    return pl.pallas_call(
        _expert_kernel,
        grid_spec=grid_spec,
        out_shape=jax.ShapeDtypeStruct((R, D), F32),
        compiler_params=_cparams(("arbitrary",), 56),
        name="experts",
    )(block_e, nblk, xb, w1, b1p, w2, b2.reshape(E, 1, D), perm)


def _combine_kernel(seg_ref, lstart_ref, gstart_ref, total_ref, ldt_ref, wt_ref, x1_ref, yb_ref, o_ref, ybuf_ref,
                    sems):
    i = pl.program_id(0)
    n = pl.num_programs(0)
    slot = i % 2

    def fetch(tile, s):
        _segment_copies(seg_ref, lstart_ref, gstart_ref, tile, ybuf_ref.at[s], yb_ref, sems.at[s], False,
                        lambda c: c.start())

    @pl.when(i == 0)
    def _():
        ybuf_ref[...] = jnp.zeros_like(ybuf_ref)
        fetch(i, slot)

    @pl.when(i + 1 < n)
    def _():
        fetch(i + 1, 1 - slot)

    _wait_tile(total_ref, i, ybuf_ref.at[slot], yb_ref, sems.at[slot], False)

    L = ybuf_ref.shape[1]
    RT = x1_ref.shape[0]
    c_iota = lax.broadcasted_iota(I32, (RT, L), 1)
    ldt = ldt_ref[...]
    w = wt_ref[...]
    w_hi = w.astype(BF16).astype(F32)
    w_lo = w - w_hi
    g_hi = jnp.zeros((RT, L), F32)
    g_lo = jnp.zeros((RT, L), F32)
    for k in range(TOP_K):
        hit = c_iota == ldt[:, k:k + 1]
        g_hi = jnp.where(hit, w_hi[:, k:k + 1], g_hi)
        g_lo = jnp.where(hit, w_lo[:, k:k + 1], g_lo)
    y = ybuf_ref[slot].astype(BF16)
    o_ref[...] = x1_ref[...] + (_dot(g_hi.astype(BF16), y) + _dot(g_lo.astype(BF16), y))


def _combine(ldt, wts_t, x1, yb, seg, lstart, gstart, totals):
    N, D = x1.shape
    RT = ROUTE_T
    grid_spec = pltpu.PrefetchScalarGridSpec(
        num_scalar_prefetch=4,
        grid=(N // RT,),
        in_specs=[
            pl.BlockSpec((RT, TOP_K), lambda i, *_: (i, 0)),
            pl.BlockSpec((RT, TOP_K), lambda i, *_: (i, 0)),
            pl.BlockSpec((RT, D), lambda i, *_: (i, 0)),
            pl.BlockSpec(memory_space=pl.ANY),
        ],
        out_specs=pl.BlockSpec((RT, D), lambda i, *_: (i, 0)),
        scratch_shapes=[pltpu.VMEM((2, LOCAL_ROWS, D), F32), pltpu.SemaphoreType.DMA((2,))],
    )
    return pl.pallas_call(
        _combine_kernel,
        grid_spec=grid_spec,
        out_shape=jax.ShapeDtypeStruct((N, D), F32),
        compiler_params=_cparams(("arbitrary",), 48),
        name="combine",
    )(seg, lstart, gstart, totals, ldt, wts_t, x1, yb)


def _layer(x, mem, g_mix, w_in, b_gate, g_ret_out, g_moba_q, g_moba_k, g_mem, w_mem_kv, g_mem_q, g_mem_k,
           w_br_ret, w_br_moba, w_br_mem, w_out, g_ffn, w_router, b_router, w_mlp1, b_mlp1, w_mlp2, b_mlp2):
    B, S, D = x.shape
    N = B * S
    x2 = x.reshape(N, D)
    slopes_np = np.exp2(-8.0 * (np.arange(MOBA_HEADS, dtype=np.float64) + 1.0) / MOBA_HEADS)
    assert all(float(np.log2(s)).is_integer() for s in slopes_np)
    slopes = tuple(float(s) for s in slopes_np)

    w_in_bf = w_in.astype(BF16)
    kmem, vmem = _memkv(mem, g_mem, w_mem_kv, g_mem_k)
    rq, rk, rv, rg, mq, mk, mv, cq, km = _inproj(x2, B, S, g_mix, w_in_bf[:, :MIX_W], g_moba_q, g_moba_k, g_mem_q,
                                                 slopes)
    o_ret = _retention(rq, rk, rv, rg, g_ret_out)
    o_moba = _moba(mq, mk, mv, km, jnp.asarray(slopes, F32)).reshape(N, MOBA_W)
    x1, xt, eidx, wts, rank, cnt = _merge(x2, S, o_ret, o_moba, cq, kmem, vmem, g_mix, w_in_bf[:, MIX_W:], b_gate,
                                          w_br_ret, w_br_moba, w_br_mem, w_out, g_ffn, w_router, b_router)
    T = MOE_T
    tcnt = cnt[:, :, 0]
    seg = ((tcnt + SEG_ALIGN - 1) // SEG_ALIGN) * SEG_ALIGN
    region_rows = jnp.sum(seg, axis=0)
    padded = ((region_rows + T - 1) // T) * T
    pad_ends = jnp.cumsum(padded).astype(I32)
    pad_starts = pad_ends - padded
    gstart = pad_starts[None, :] + jnp.cumsum(seg, axis=0) - seg
    lstart = jnp.cumsum(seg, axis=1) - seg
    lstart_tok = jnp.broadcast_to(lstart[:, None, :], (N // ROUTE_T, ROUTE_T, N_EXPERTS)).reshape(N, N_EXPERTS)
    onehot = eidx[:, :, None] == jnp.arange(N_EXPERTS, dtype=I32)[None, None, :]
    ld = jnp.sum(jnp.where(onehot, lstart_tok[None], 0), axis=-1) + rank
    NB = -(-(N // ROUTE_T * LOCAL_ROWS) // T) + N_EXPERTS
    ends_before = jnp.sum((pad_ends[None, :] <= (jnp.arange(NB, dtype=I32) * T)[:, None]).astype(I32), axis=1)
    block_e = jnp.minimum(ends_before, N_EXPERTS - 1).astype(I32)
    nblk = (pad_ends[-1:] // T).astype(I32)
    seg_f, lstart_f, gstart_f = (a.reshape(-1).astype(I32) for a in (seg, lstart, gstart))
    totals = jnp.sum(seg, axis=1).astype(I32)

    xb = _dispatch(xt, ld, seg_f, lstart_f, gstart_f, totals, pad_ends, padded.astype(I32), NB * T)
    F2 = w_mlp1.shape[-1]
    b1p = b_mlp1.reshape(N_EXPERTS, F2 // (2 * LANES), LANES, 2).transpose(0, 1, 3, 2).reshape(N_EXPERTS, 1, F2)
    yb = _experts(xb, block_e, nblk, w_mlp1, b1p, w_mlp2, b_mlp2)
    out = _combine(ld.T, wts.T, x1, yb, seg_f, lstart_f, gstart_f, totals)
    return out.reshape(B, S, D)


def kernel(x, mem, g_mix, w_in, b_gate, g_ret_out, g_moba_q, g_moba_k, g_mem, w_mem_kv, g_mem_q, g_mem_k, w_br_ret, w_br_moba, w_br_mem, w_out, g_ffn, w_router, b_router, w_mlp1, b_mlp1, w_mlp2, b_mlp2):
    for l in range(g_mix.shape[0]):
        x = _layer(x, mem, g_mix[l], w_in[l], b_gate[l], g_ret_out[l], g_moba_q[l], g_moba_k[l], g_mem[l],
                   w_mem_kv[l], g_mem_q[l], g_mem_k[l], w_br_ret[l], w_br_moba[l], w_br_mem[l], w_out[l],
                   g_ffn[l], w_router[l], b_router[l], w_mlp1[l], b_mlp1[l], w_mlp2[l], b_mlp2[l])
    return x
```

```python
import functools

import jax
import jax.numpy as jnp
import numpy as np
from jax import lax
from jax.experimental import pallas as pl
from jax.experimental.pallas import tpu as pltpu

F32 = jnp.float32
BF16 = jnp.bfloat16
I32 = jnp.int32

EPS = 1e-5
NEG = -1e30

RET_HEADS = 4
RET_DK = 64
RET_DV = 128
RET_CHUNK = 128
MOBA_HEADS = 8
MOBA_HD = 64
MOBA_BLOCK = 256
MOBA_TOPK = 3
MEM_HEADS = 4
MEM_HD = 128
N_BRANCH = 3
N_EXPERTS = 32
TOP_K = 4
SWIGLU_LIMIT = 7.0
SWIGLU_ALPHA = 1.702

RET_Q = RET_HEADS * RET_DK
RET_V = RET_HEADS * RET_DV
MOBA_W = MOBA_HEADS * MOBA_HD
MEM_W = MEM_HEADS * MEM_HD
MIX_W = 2 * RET_Q + 2 * RET_V + 3 * MOBA_W + MEM_W

LANES = 128
MOBA_PAIRS = MOBA_HEADS // 2
MOBA_HEADS_PER_STEP = 8
BIAS_LANE0 = MOBA_HD
ONE_LANE = 80
VSUM_LANE = MOBA_HD

MOE_T = 256
TM_PROJ = 512
TM_MERGE = 512
TQ_RET = 512
ROUTE_T = 256
SEG_ALIGN = 8
LOCAL_ROWS = -(-(ROUTE_T * TOP_K + N_EXPERTS * (SEG_ALIGN - 1)) // MOE_T) * MOE_T
SEG_CHUNKS = tuple(2 ** p for p in range(ROUTE_T.bit_length() - 1, SEG_ALIGN.bit_length() - 2, -1))
TOTAL_CHUNKS = tuple(2 ** p for p in range(LOCAL_ROWS.bit_length() - 1, SEG_ALIGN.bit_length() - 2, -1))

_NT = (((1,), (1,)), ((), ()))
_TN = (((0,), (0,)), ((), ()))


def _rms(x, g):
    return x * lax.rsqrt(jnp.mean(x * x, axis=-1, keepdims=True) + EPS) * g


def _dot(a, b):
    return jnp.dot(a, b, preferred_element_type=F32)


def _cparams(sem, vmem_mb):
    return pltpu.CompilerParams(dimension_semantics=sem, vmem_limit_bytes=vmem_mb * 1024 * 1024)


def _memkv_kernel(mem_ref, g_ref, w_ref, gk_ref, k_ref, v_ref):
    m = _rms(mem_ref[0], g_ref[...]).astype(BF16)
    kv = _dot(m, w_ref[...])
    ks = [_rms(kv[:, h * MEM_HD:(h + 1) * MEM_HD], gk_ref[...]) for h in range(MEM_HEADS)]
    k_ref[0] = jnp.concatenate(ks, axis=-1).astype(BF16)
    v_ref[0] = kv[:, MEM_W:].astype(BF16)


def _memkv(mem, g_mem, w_mem_kv, g_mem_k):
    B, M, D = mem.shape
    return pl.pallas_call(
        _memkv_kernel,
        grid=(B,),
        in_specs=[
            pl.BlockSpec((1, M, D), lambda b: (b, 0, 0)),
            pl.BlockSpec((1, D), lambda b: (0, 0)),
            pl.BlockSpec((D, 2 * MEM_W), lambda b: (0, 0)),
            pl.BlockSpec((1, MEM_HD), lambda b: (0, 0)),
        ],
        out_specs=[
            pl.BlockSpec((1, M, MEM_W), lambda b: (b, 0, 0)),
            pl.BlockSpec((1, M, MEM_W), lambda b: (b, 0, 0)),
        ],
        out_shape=[jax.ShapeDtypeStruct((B, M, MEM_W), BF16)] * 2,
        compiler_params=_cparams(("arbitrary",), 32),
        name="memkv",
    )(mem, g_mem.reshape(1, D), w_mem_kv.astype(BF16), g_mem_k.reshape(1, MEM_HD))


def _head_pair_norm(a2, g2, lane):
    sq = a2 * a2
    lo = lane < MOBA_HD
    ss_lo = jnp.sum(jnp.where(lo, sq, 0.0), axis=-1, keepdims=True)
    ss_hi = jnp.sum(jnp.where(lo, 0.0, sq), axis=-1, keepdims=True)
    inv = jnp.where(lo, lax.rsqrt(ss_lo / MOBA_HD + EPS), lax.rsqrt(ss_hi / MOBA_HD + EPS))
    return a2 * inv * g2


def _inproj_kernel(slopes, seq_tiles, x_ref, gmix_ref, w_ref, gq_ref, gk_ref, gc_ref,
                   rq_ref, rk_ref, rv_ref, rg_ref, mq_ref, mk_ref, mv_ref, cq_ref, kmean_ref):
    tm = x_ref.shape[0]
    blocks_per_tile = tm // MOBA_BLOCK
    h = _rms(x_ref[...], gmix_ref[...]).astype(BF16)
    col = [0]

    def proj(width):
        a = _dot(h, w_ref[:, col[0]:col[0] + width])
        col[0] += width
        return a

    a = proj(2 * RET_Q)
    for hh in range(RET_HEADS):
        rq_ref[0, hh] = a[:, hh * RET_DK:(hh + 1) * RET_DK].astype(BF16)
        rk_ref[0, hh] = (a[:, RET_Q + hh * RET_DK:RET_Q + (hh + 1) * RET_DK] * (RET_DK ** -0.5)).astype(BF16)
    rv_ref[...] = proj(RET_V).astype(BF16)
    rg_ref[...] = proj(RET_V).astype(BF16)

    lane = lax.broadcasted_iota(I32, (tm, LANES), 1)
    row = lax.broadcasted_iota(I32, (tm, LANES), 0)
    lo = lane < MOBA_HD
    q_tail = jnp.where(lane == ONE_LANE, 1.0, 0.0)
    a = proj(MOBA_W)
    for p in range(MOBA_PAIRS):
        n2 = _head_pair_norm(a[:, p * LANES:(p + 1) * LANES], gq_ref[...], lane) * (MOBA_HD ** -0.5)
        mq_ref[0, 2 * p] = jnp.where(lo, n2, q_tail).astype(BF16)
        mq_ref[0, 2 * p + 1] = jnp.where(lo, pltpu.roll(n2, MOBA_HD, 1), q_tail).astype(BF16)
    blk = (pl.program_id(0) % seq_tiles) * blocks_per_tile + row // MOBA_BLOCK
    onehot_tail = jnp.where(lane == BIAS_LANE0 + blk, 1.0, 0.0)
    off = (row % MOBA_BLOCK).astype(F32)
    a = proj(MOBA_W)
    for p in range(MOBA_PAIRS):
        n2 = _head_pair_norm(a[:, p * LANES:(p + 1) * LANES], gk_ref[...], lane)
        for j in range(blocks_per_tile):
            kmean_ref[0, 0, p, j:j + 1, :] = jnp.mean(n2[j * MOBA_BLOCK:(j + 1) * MOBA_BLOCK], axis=0, keepdims=True)
        for s, src in ((0, n2), (1, pltpu.roll(n2, MOBA_HD, 1))):
            tail = jnp.where(lane == ONE_LANE, slopes[2 * p + s] * off, onehot_tail)
            mk_ref[0, 2 * p + s] = jnp.where(lo, src, tail).astype(BF16)
    v_tail = jnp.where(lane == VSUM_LANE, 1.0, 0.0)
    a = proj(MOBA_W)
    for p in range(MOBA_PAIRS):
        a2 = a[:, p * LANES:(p + 1) * LANES]
        mv_ref[0, 2 * p] = jnp.where(lo, a2, v_tail).astype(BF16)
        mv_ref[0, 2 * p + 1] = jnp.where(lo, pltpu.roll(a2, MOBA_HD, 1), v_tail).astype(BF16)
    a = proj(MEM_W)
    cq = [_rms(a[:, hh * MEM_HD:(hh + 1) * MEM_HD], gc_ref[...]) for hh in range(MEM_HEADS)]
    cq_ref[...] = jnp.concatenate(cq, axis=-1).astype(BF16)


def _inproj(x2, B, S, g_mix, w_mix, g_moba_q, g_moba_k, g_mem_q, slopes):
    N, D = x2.shape
    tm = TM_PROJ
    nS = S // tm
    bpt = tm // MOBA_BLOCK
    tok = lambda i: (i, 0)
    headmaj = lambda i: (i // nS, 0, i % nS, 0)
    g2 = lambda g: jnp.concatenate([g, g]).reshape(1, LANES)
    outs = pl.pallas_call(
        functools.partial(_inproj_kernel, slopes, nS),
        grid=(N // tm,),
        in_specs=[
            pl.BlockSpec((tm, D), tok),
            pl.BlockSpec((1, D), lambda i: (0, 0)),
            pl.BlockSpec((D, MIX_W), lambda i: (0, 0)),
            pl.BlockSpec((1, LANES), lambda i: (0, 0)),
            pl.BlockSpec((1, LANES), lambda i: (0, 0)),
            pl.BlockSpec((1, MEM_HD), lambda i: (0, 0)),
        ],
        out_specs=[
            pl.BlockSpec((1, RET_HEADS, tm, RET_DK), headmaj),
            pl.BlockSpec((1, RET_HEADS, tm, RET_DK), headmaj),
            pl.BlockSpec((tm, RET_V), tok),
            pl.BlockSpec((tm, RET_V), tok),
            pl.BlockSpec((1, MOBA_HEADS, tm, LANES), headmaj),
            pl.BlockSpec((1, MOBA_HEADS, tm, LANES), headmaj),
            pl.BlockSpec((1, MOBA_HEADS, tm, LANES), headmaj),
            pl.BlockSpec((tm, MEM_W), tok),
            pl.BlockSpec((1, 1, MOBA_PAIRS, bpt, LANES), lambda i: (i // nS, i % nS, 0, 0, 0)),
        ],
        out_shape=[
            jax.ShapeDtypeStruct((B, RET_HEADS, S, RET_DK), BF16),
            jax.ShapeDtypeStruct((B, RET_HEADS, S, RET_DK), BF16),
            jax.ShapeDtypeStruct((N, RET_V), BF16),
            jax.ShapeDtypeStruct((N, RET_V), BF16),
            jax.ShapeDtypeStruct((B, MOBA_HEADS, S, LANES), BF16),
            jax.ShapeDtypeStruct((B, MOBA_HEADS, S, LANES), BF16),
            jax.ShapeDtypeStruct((B, MOBA_HEADS, S, LANES), BF16),
            jax.ShapeDtypeStruct((N, MEM_W), BF16),
            jax.ShapeDtypeStruct((B, nS, MOBA_PAIRS, bpt, LANES), F32),
        ],
        compiler_params=_cparams(("arbitrary",), 56),
        name="inproj",
    )(x2, g_mix.reshape(1, D), w_mix, g2(g_moba_q), g2(g_moba_k), g_mem_q.reshape(1, MEM_HD))
    rq, rk, rv, rg, mq, mk, mv, cq, kmean_pairs = outs
    km = kmean_pairs.reshape(B, nS, MOBA_PAIRS, bpt, 2, MOBA_HD).transpose(0, 2, 4, 1, 3, 5)
    km = km.reshape(B, MOBA_HEADS, nS * bpt, MOBA_HD)
    km = jnp.pad(km, ((0, 0), (0, 0), (0, 0), (0, LANES - MOBA_HD)))
    return rq, rk, rv, rg, mq, mk, mv, cq, km


def _retention_kernel(q_ref, k_ref, v_ref, rg_ref, din_ref, dq_ref, dk_ref, dc_ref, g_ref, o_ref, state_ref):
    @pl.when(pl.program_id(2) == 0)
    def _():
        state_ref[...] = jnp.zeros_like(state_ref)

    C = RET_CHUNK
    for c in range(q_ref.shape[2] // C):
        rows = slice(c * C, (c + 1) * C)
        q = q_ref[0, 0, rows, :]
        k = k_ref[0, 0, rows, :]
        v = v_ref[rows, :]
        state = state_ref[...]
        scores = lax.dot_general(q, k, _NT, preferred_element_type=F32) * din_ref[0]
        intra = _dot(scores.astype(BF16), v)
        cross = _dot(q, state.astype(BF16)) * dq_ref[0]
        kd = (k.astype(F32) * dk_ref[0]).astype(BF16)
        state_ref[...] = dc_ref[0] * state + lax.dot_general(kd, v, _TN, preferred_element_type=F32)
        o = _rms(intra + cross, g_ref[...])
        o_ref[rows, :] = (o * jax.nn.silu(rg_ref[rows, :].astype(F32))).astype(BF16)


def _retention_decays():
    H, C = RET_HEADS, RET_CHUNK
    log_g = jnp.log1p(-jnp.exp2(-5.0 - jnp.arange(H, dtype=F32)))
    i = jnp.arange(C, dtype=F32)
    diff = i[:, None] - i[None, :]
    decay_in = jnp.where(diff >= 0, jnp.exp(jnp.maximum(diff, 0.0)[None] * log_g[:, None, None]), 0.0)
    decay_k = jnp.exp((C - 1 - i)[None, :] * log_g[:, None])
    decay_q = jnp.exp((i + 1)[None, :] * log_g[:, None])
    decay_chunk = jnp.exp(C * log_g)
    dq = jnp.broadcast_to(decay_q[:, :, None], (H, C, RET_DV))
    dk = jnp.broadcast_to(decay_k[:, :, None], (H, C, RET_DK))
    dc = jnp.broadcast_to(decay_chunk[:, None, None], (H, RET_DK, RET_DV))
    return decay_in, dq, dk, dc


def _retention(rq, rk, rv, rg, g_ret_out):
    B, H, S, dk = rq.shape
    tq = TQ_RET
    nT = S // tq
    C = RET_CHUNK
    din, dq, dk_, dc = _retention_decays()
    qk_spec = pl.BlockSpec((1, 1, tq, dk), lambda b, h, t: (b, h, t, 0))
    tok_spec = pl.BlockSpec((tq, RET_DV), lambda b, h, t: (b * nT + t, h))
    per_head = lambda r, c: pl.BlockSpec((1, r, c), lambda b, h, t: (h, 0, 0))
    return pl.pallas_call(
        _retention_kernel,
        grid=(B, H, nT),
        in_specs=[qk_spec, qk_spec, tok_spec, tok_spec,
                  per_head(C, C), per_head(C, RET_DV), per_head(C, RET_DK), per_head(RET_DK, RET_DV),
                  pl.BlockSpec((1, RET_DV), lambda b, h, t: (0, 0))],
        out_specs=tok_spec,
        out_shape=jax.ShapeDtypeStruct((B * S, RET_V), BF16),
        scratch_shapes=[pltpu.VMEM((RET_DK, RET_DV), F32)],
        compiler_params=_cparams(("arbitrary", "arbitrary", "arbitrary"), 32),
        name="retention",
    )(rq, rk, rv, rg, din, dq, dk_, dc, g_ret_out.reshape(1, RET_DV))


def _moba_kernel(slopes_ref, q_ref, k_ref, v_ref, km_ref, shift_ref, causal_ref, o_ref):
    NH = q_ref.shape[1]
    head0 = pl.program_id(1) * NH
    i = pl.program_id(2)
    BS = MOBA_BLOCK
    nb = km_ref.shape[2]
    heads = range(NH)
    n_iota = lax.broadcasted_iota(I32, (nb, BS), 0)
    past = n_iota < i
    qa = [q_ref[0, s] for s in heads]
    q_aug = []
    for s in heads:
        km = km_ref[0, s]
        hi = km.astype(BF16)
        mid = (km - hi.astype(F32)).astype(BF16)
        lo = (km - hi.astype(F32) - mid.astype(F32)).astype(BF16)
        g3 = lax.dot_general(jnp.concatenate([hi, mid, lo], axis=0), qa[s], _NT, preferred_element_type=F32)
        gate = (g3[:nb] + g3[nb:2 * nb]) + g3[2 * nb:]
        g = jnp.where(past, gate, -jnp.inf)
        rank = jnp.zeros((nb, BS), I32)
        for m in range(nb - 1):
            gm = g[m:m + 1, :]
            rank = rank + jnp.where(gm > g, 1, jnp.where(gm == g, jnp.where(m < n_iota, 1, 0), 0))
        sel = jnp.where(past, rank, MOBA_TOPK) < MOBA_TOPK
        bias = jnp.where(sel, (n_iota - i).astype(F32) * (slopes_ref[head0 + s] * BS), NEG).astype(BF16)
        placed = lax.dot_general(bias, shift_ref[...], _TN, preferred_element_type=F32)
        q_aug.append((qa[s].astype(F32) + placed).astype(BF16))

    own = pl.ds(pl.multiple_of(i * BS, BS), BS)
    m0, acc0 = [], []
    for s in heads:
        sc = lax.dot_general(qa[s], k_ref[0, s, own, :], _NT, preferred_element_type=F32) + causal_ref[...]
        m = jnp.max(sc, axis=-1, keepdims=True)
        m0.append(m)
        acc0.append(_dot(jnp.exp(sc - m).astype(BF16), v_ref[0, s, own, :]))

    def body(j, carry):
        ms, accs = carry
        blk = pl.ds(pl.multiple_of(j * BS, BS), BS)
        new_m, new_acc = [], []
        for s in heads:
            sj = lax.dot_general(q_aug[s], k_ref[0, s, blk, :], _NT, preferred_element_type=F32)
            m_new = jnp.maximum(ms[s], jnp.max(sj, axis=-1, keepdims=True))
            pj = jnp.exp(sj - m_new).astype(BF16)
            new_acc.append(jnp.exp(ms[s] - m_new) * accs[s] + _dot(pj, v_ref[0, s, blk, :]))
            new_m.append(m_new)
        return tuple(new_m), tuple(new_acc)

    _, accs = lax.fori_loop(0, i, body, (tuple(m0), tuple(acc0)))
    outs = [acc / acc[:, VSUM_LANE:VSUM_LANE + 1] for acc in accs]
    lane = lax.broadcasted_iota(I32, (BS, LANES), 1)
    pairs = [jnp.where(lane < MOBA_HD, outs[2 * p], pltpu.roll(outs[2 * p + 1], MOBA_HD, 1)) for p in range(NH // 2)]
    o_ref[0] = jnp.concatenate(pairs, axis=-1).astype(BF16)


def _moba(mq, mk, mv, km, slopes):
    B, H, S, _ = mq.shape
    BS = MOBA_BLOCK
    NH = MOBA_HEADS_PER_STEP
    nq = S // BS
    nb = km.shape[2]
    assert BIAS_LANE0 + nb <= ONE_LANE
    shift = (jnp.arange(nb)[:, None] + BIAS_LANE0 == jnp.arange(LANES)[None, :]).astype(BF16)
    causal = jnp.where(jnp.arange(BS)[:, None] >= jnp.arange(BS)[None, :], 0.0, NEG).astype(F32)
    per_head = lambda rows: pl.BlockSpec((1, NH, rows, LANES), lambda b, p, i, sl: (b, p, 0, 0))
    grid_spec = pltpu.PrefetchScalarGridSpec(
        num_scalar_prefetch=1,
        grid=(B, H // NH, nq),
        in_specs=[
            pl.BlockSpec((1, NH, BS, LANES), lambda b, p, i, sl: (b, p, i, 0)),
            per_head(S), per_head(S), per_head(nb),
            pl.BlockSpec((nb, LANES), lambda b, p, i, sl: (0, 0)),
            pl.BlockSpec((BS, BS), lambda b, p, i, sl: (0, 0)),
        ],
        out_specs=pl.BlockSpec((1, BS, NH * MOBA_HD), lambda b, p, i, sl: (b, i, p)),
    )
    return pl.pallas_call(
        _moba_kernel,
        grid_spec=grid_spec,
        out_shape=jax.ShapeDtypeStruct((B, S, MOBA_W), BF16),
        compiler_params=_cparams(("arbitrary", "arbitrary", "arbitrary"), 48),
        name="moba",
    )(slopes, mq, mk, mv, km, shift, causal)


def _merge_kernel(x_ref, oret_ref, omoba_ref, cq_ref, kmem_ref, vmem_ref, gmix_ref, wg_ref, bg_ref,
                  wbr_ref, wbm_ref, wbc_ref, wout_ref, gffn_ref, wr_ref, br_ref, tri_ref,
                  x1_ref, xt_ref, eidx_ref, wts_ref, rank_ref, cnt_ref):
    D = x_ref.shape[1]
    tm = x_ref.shape[0]

    x = x_ref[...]
    h = _rms(x, gmix_ref[...]).astype(BF16)
    cq = cq_ref[...]
    om = []
    for hh in range(MEM_HEADS):
        cols = slice(hh * MEM_HD, (hh + 1) * MEM_HD)
        sc = lax.dot_general(cq[:, cols], kmem_ref[0, :, cols], _NT, preferred_element_type=F32) * (MEM_HD ** -0.5)
        sc = sc - jnp.max(sc, axis=-1, keepdims=True)
        p = jnp.exp(sc)
        p = p / jnp.sum(p, axis=-1, keepdims=True)
        om.append(_dot(p.astype(BF16), vmem_ref[0, :, cols]))
    omem = jnp.concatenate(om, axis=-1).astype(BF16)
    y = None
    for br, (o, w_ref) in enumerate(((oret_ref[...], wbr_ref), (omoba_ref[...], wbm_ref), (omem, wbc_ref))):
        gl = _dot(h, wg_ref[:, br * D:(br + 1) * D]) + bg_ref[:, br * D:(br + 1) * D]
        term = jax.nn.sigmoid(gl) * _dot(o, w_ref[...])
        y = term if y is None else y + term
    x1 = x + _dot(y.astype(BF16), wout_ref[...])
    x1_ref[...] = x1
    xt = _rms(x1, gffn_ref[...])
    wr = wr_ref[...]
    E = wr.shape[0]
    w_hi = wr.astype(BF16)
    w_mid = (wr - w_hi.astype(F32)).astype(BF16)
    w_lo = (wr - w_hi.astype(F32) - w_mid.astype(F32)).astype(BF16)
    xt_hi = xt.astype(BF16)
    xt_ref[...] = xt_hi
    xt_lo = (xt - xt_hi.astype(F32)).astype(BF16)
    a = lax.dot_general(jnp.concatenate([w_hi, w_mid, w_lo], axis=0), xt_hi, _NT, preferred_element_type=F32)
    b = lax.dot_general(jnp.concatenate([w_hi, w_mid], axis=0), xt_lo, _NT, preferred_element_type=F32)
    logits = (a[:E] + (a[E:2 * E] + b[:E])) + (a[2 * E:] + b[E:]) + br_ref[...]
    e_iota = lax.broadcasted_iota(I32, (E, tm), 0)
    l = logits
    vals, hots = [], []
    for k in range(TOP_K):
        m = jnp.max(l, axis=0, keepdims=True)
        idx = jnp.min(jnp.where(l == m, e_iota, E), axis=0, keepdims=True)
        hot = e_iota == idx
        l = jnp.where(hot, -jnp.inf, l)
        vals.append(m)
        hots.append(hot)
        eidx_ref[k:k + 1, :] = idx
    ex = [jnp.exp(v - vals[0]) for v in vals]
    den = ex[0]
    for k in range(1, TOP_K):
        den = den + ex[k]
    chosen = jnp.zeros((E, tm), F32)
    for k in range(TOP_K):
        wts_ref[k:k + 1, :] = ex[k] / den
        chosen = chosen + jnp.where(hots[k], 1.0, 0.0)
    RT = tri_ref.shape[0]
    for t in range(tm // RT):
        cols = slice(t * RT, (t + 1) * RT)
        prefix = _dot(chosen[:, cols].astype(BF16), tri_ref[...])
        for k in range(TOP_K):
            rank_ref[k:k + 1, cols] = jnp.sum(jnp.where(hots[k][:, cols], prefix, 0.0), axis=0, keepdims=True).astype(I32)
        counts = jnp.sum(chosen[:, cols], axis=1, keepdims=True)
        cnt_ref[t] = jnp.broadcast_to(counts, cnt_ref.shape[1:]).astype(I32)


def _merge(x2, S, o_ret, o_moba, cq, kmem, vmem, g_mix, w_gate, b_gate, w_br_ret, w_br_moba, w_br_mem, w_out,
           g_ffn, w_router, b_router):
    N, D = x2.shape
    tm = TM_MERGE
    nS = S // tm
    M = kmem.shape[1]
    E = w_router.shape[1]
    tok = lambda w: pl.BlockSpec((tm, w), lambda i: (i, 0))
    const = lambda r, c: pl.BlockSpec((r, c), lambda i: (0, 0), pipeline_mode=pl.Buffered(1))
    mem_spec = pl.BlockSpec((1, M, MEM_W), lambda i: (i // nS, 0, 0))
    lanes_tok = lambda r, dt: (pl.BlockSpec((r, tm), lambda i: (0, i)), jax.ShapeDtypeStruct((r, N), dt))
    RT = ROUTE_T
    tri = (jnp.arange(RT)[:, None] < jnp.arange(RT)[None, :]).astype(BF16)
    e_spec, e_shape = lanes_tok(TOP_K, I32)
    w_spec, w_shape = lanes_tok(TOP_K, F32)
    r_spec, r_shape = lanes_tok(TOP_K, I32)
    return pl.pallas_call(
        _merge_kernel,
        grid=(N // tm,),
        in_specs=[tok(D), tok(RET_V), tok(MOBA_W), tok(MEM_W), mem_spec, mem_spec,
                  const(1, D), const(D, N_BRANCH * D), const(1, N_BRANCH * D),
                  const(RET_V, D), const(MOBA_W, D), const(MEM_W, D), const(D, D),
                  const(1, D), const(E, D), const(E, 1), const(RT, RT)],
        out_specs=[tok(D), tok(D), e_spec, w_spec, r_spec, pl.BlockSpec((tm // RT, E, LANES), lambda i: (i, 0, 0))],
        out_shape=[jax.ShapeDtypeStruct((N, D), F32), jax.ShapeDtypeStruct((N, D), BF16),
                   e_shape, w_shape, r_shape, jax.ShapeDtypeStruct((N // RT, E, LANES), I32)],
        compiler_params=_cparams(("arbitrary",), 56),
        name="merge",
    )(x2, o_ret, o_moba, cq, kmem, vmem, g_mix.reshape(1, D), w_gate, b_gate.reshape(1, N_BRANCH * D),
      w_br_ret.astype(BF16), w_br_moba.astype(BF16), w_br_mem.astype(BF16), w_out.astype(BF16),
      g_ffn.reshape(1, D), w_router.T, b_router.reshape(E, 1), tri)


def _segment_copies(seg_ref, lstart_ref, gstart_ref, tile, local_ref, slots_ref, sem, to_slots, fn):
    def per_expert(e, carry):
        idx = tile * N_EXPERTS + e
        size = seg_ref[idx]
        off = jnp.int32(0)
        for chunk in SEG_CHUNKS:
            take = (size & chunk) != 0
            lo = pl.ds(pl.multiple_of(lstart_ref[idx] + off, SEG_ALIGN), chunk)
            gl = pl.ds(pl.multiple_of(gstart_ref[idx] + off, SEG_ALIGN), chunk)
            src, dst = (local_ref.at[lo], slots_ref.at[gl]) if to_slots else (slots_ref.at[gl], local_ref.at[lo])

            @pl.when(take)
            def _():
                fn(pltpu.make_async_copy(src, dst, sem))

            off = off + jnp.where(take, chunk, 0)
        return carry

    lax.fori_loop(0, N_EXPERTS, per_expert, 0)


def _wait_tile(total_ref, tile, local_ref, slots_ref, sem, to_slots):
    total = total_ref[tile]
    for chunk in TOTAL_CHUNKS:
        lo, gl = local_ref.at[pl.ds(0, chunk)], slots_ref.at[pl.ds(0, chunk)]
        src, dst = (lo, gl) if to_slots else (gl, lo)

        @pl.when((total & chunk) != 0)
        def _():
            pltpu.make_async_copy(src, dst, sem).wait()


def _dispatch_kernel(seg_ref, lstart_ref, gstart_ref, total_ref, pad_end_ref, padded_ref, ld_ref, xt_ref, xb_ref,
                     ybuf_ref, zeros_ref, zsem, sems):
    T = zeros_ref.shape[0]
    i = pl.program_id(0)
    n = pl.num_programs(0)
    slot = i % 2

    @pl.when(i == 0)
    def _():
        zeros_ref[...] = jnp.zeros_like(zeros_ref)

        def fill(start):
            return pltpu.make_async_copy(zeros_ref, xb_ref.at[pl.ds(pl.multiple_of(start, T), T)], zsem)

        for e in range(N_EXPERTS):
            @pl.when(padded_ref[e] > 0)
            def _():
                fill(pad_end_ref[e] - T).start()
        for e in range(N_EXPERTS):
            @pl.when(padded_ref[e] > 0)
            def _():
                fill(pad_end_ref[e] - T).wait()
        first_unused = pad_end_ref[N_EXPERTS - 1] // T
        n_blocks = xb_ref.shape[0] // T
        lax.fori_loop(first_unused, n_blocks, lambda b, c: (fill(b * T).start(), c)[1], 0)
        lax.fori_loop(first_unused, n_blocks, lambda b, c: (fill(b * T).wait(), c)[1], 0)

    @pl.when(i >= 2)
    def _():
        _wait_tile(total_ref, i - 2, ybuf_ref.at[slot], xb_ref, sems.at[slot], True)

    L = ybuf_ref.shape[1]
    RT = xt_ref.shape[0]
    r_iota = lax.broadcasted_iota(I32, (L, RT), 0)
    ld = ld_ref[...]
    onehot = jnp.zeros((L, RT), F32)
    for k in range(TOP_K):
        onehot = jnp.where(r_iota == ld[k:k + 1, :], 1.0, onehot)
    onehot = onehot.astype(BF16)
    ybuf_ref[slot] = _dot(onehot, xt_ref[...])
    _segment_copies(seg_ref, lstart_ref, gstart_ref, i, ybuf_ref.at[slot], xb_ref, sems.at[slot], True,
                    lambda c: c.start())

    @pl.when(i == n - 1)
    def _():
        @pl.when(n >= 2)
        def _():
            _wait_tile(total_ref, i - 1, ybuf_ref.at[1 - slot], xb_ref, sems.at[1 - slot], True)

        _wait_tile(total_ref, i, ybuf_ref.at[slot], xb_ref, sems.at[slot], True)


def _dispatch(xt, ld, seg, lstart, gstart, totals, pad_ends, padded, R):
    N, D = xt.shape
    RT = ROUTE_T
    grid_spec = pltpu.PrefetchScalarGridSpec(
        num_scalar_prefetch=6,
        grid=(N // RT,),
        in_specs=[
            pl.BlockSpec((TOP_K, RT), lambda i, *_: (0, i)),
            pl.BlockSpec((RT, D), lambda i, *_: (i, 0)),
        ],
        out_specs=pl.BlockSpec(memory_space=pl.ANY),
        scratch_shapes=[pltpu.VMEM((2, LOCAL_ROWS, D), F32), pltpu.VMEM((MOE_T, D), F32),
                        pltpu.SemaphoreType.DMA(()), pltpu.SemaphoreType.DMA((2,))],
    )
    return pl.pallas_call(
        _dispatch_kernel,
        grid_spec=grid_spec,
        out_shape=jax.ShapeDtypeStruct((R, D), F32),
        compiler_params=_cparams(("arbitrary",), 48),
        name="dispatch",
    )(seg, lstart, gstart, totals, pad_ends, padded, ld, xt)


def _expert_kernel(start_ref, nblk_ref, w1_ref, b1_ref, w2_ref, b2_ref, perm_ref, xb_ref, yb_ref,
                   w1p_ref, w2b_ref, xbuf_ref, ybuf_ref, in_sems, out_sems):
    e = pl.program_id(0)
    T = xbuf_ref.shape[1]
    G = 2 * LANES
    nb = nblk_ref[e]
    base = start_ref[e]

    def rows(b):
        return pl.ds(pl.multiple_of(base + b * T, T), T)

    def in_copy(b, slot):
        return pltpu.make_async_copy(xb_ref.at[rows(b)], xbuf_ref.at[slot], in_sems.at[slot])

    def out_copy(b, slot):
        return pltpu.make_async_copy(ybuf_ref.at[slot], yb_ref.at[rows(b)], out_sems.at[slot])

    @pl.when(nb > 0)
    def _():
        in_copy(0, 0).start()
        for g in range(w1p_ref.shape[1] // G):
            w = w1_ref[0, :, g * G:(g + 1) * G].astype(BF16)
            w1p_ref[:, g * G:(g + 1) * G] = _dot(w, perm_ref[...]).astype(BF16)
        w2b_ref[...] = w2_ref[0].astype(BF16)

        def block(b, carry):
            slot = b % 2
            in_copy(b, slot).wait()

            @pl.when(b + 1 < nb)
            def _():
                in_copy(b + 1, 1 - slot).start()

            x = xbuf_ref[slot].astype(BF16)
            acts = []
            for g in range(w1p_ref.shape[1] // G):
                hg = _dot(x, w1p_ref[:, g * G:(g + 1) * G]) + b1_ref[0, :, g * G:(g + 1) * G]
                glu = jnp.minimum(hg[:, :LANES], SWIGLU_LIMIT)
                lin = jnp.clip(hg[:, LANES:], -SWIGLU_LIMIT, SWIGLU_LIMIT)
                acts.append((glu * jax.nn.sigmoid(SWIGLU_ALPHA * glu) * (lin + 1.0)).astype(BF16))
            y = _dot(jnp.concatenate(acts, axis=-1), w2b_ref[...]) + b2_ref[0]

            @pl.when(b >= 2)
            def _():
                out_copy(b - 2, slot).wait()

            ybuf_ref[slot] = y
            out_copy(b, slot).start()
            return carry

        lax.fori_loop(0, nb, block, 0)

        @pl.when(nb >= 2)
        def _():
            out_copy(nb - 2, nb % 2).wait()

        out_copy(nb - 1, (nb - 1) % 2).wait()

    @pl.when(e == pl.num_programs(0) - 1)
    def _():
        ybuf_ref[0] = jnp.zeros(ybuf_ref.shape[1:], ybuf_ref.dtype)
        first_unused = base // T + nb
        n_blocks = yb_ref.shape[0] // T

        def tail(b):
            return pltpu.make_async_copy(ybuf_ref.at[0], yb_ref.at[pl.ds(pl.multiple_of(b * T, T), T)], out_sems.at[0])

        lax.fori_loop(first_unused, n_blocks, lambda b, c: (tail(b).start(), c)[1], 0)
        lax.fori_loop(first_unused, n_blocks, lambda b, c: (tail(b).wait(), c)[1], 0)


def _experts(xb, region_start, region_blocks, w1, b1p, w2, b2):
    R, D = xb.shape
    E, _, F2 = w1.shape
    F = F2 // 2
    T = MOE_T
    G = 2 * LANES
    c = np.arange(G)
    src = np.where(c < LANES, 2 * c, 2 * (c - LANES) + 1)
    perm = jnp.asarray(np.arange(G)[:, None] == src[None, :], dtype=BF16)
    per_expert = lambda r, w: pl.BlockSpec((1, r, w), lambda e, *_: (e, 0, 0))
    grid_spec = pltpu.PrefetchScalarGridSpec(
        num_scalar_prefetch=2,
        grid=(E,),
        in_specs=[
            per_expert(D, F2), per_expert(1, F2), per_expert(F, D), per_expert(1, D),
            pl.BlockSpec((G, G), lambda e, *_: (0, 0)),
            pl.BlockSpec(memory_space=pl.ANY),
        ],
        out_specs=pl.BlockSpec(memory_space=pl.ANY),
        scratch_shapes=[pltpu.VMEM((D, F2), BF16), pltpu.VMEM((F, D), BF16),
                        pltpu.VMEM((2, T, D), F32), pltpu.VMEM((2, T, D), F32),
                        pltpu.SemaphoreType.DMA((2,)), pltpu.SemaphoreType.DMA((2,))],
    )
    return pl.pallas_call(
        _expert_kernel,
        grid_spec=grid_spec,
        out_shape=jax.ShapeDtypeStruct((R, D), F32),
        compiler_params=_cparams(("arbitrary",), 56),
        name="experts",
    )(region_start, region_blocks, w1, b1p, w2, b2.reshape(E, 1, D), perm, xb)


def _combine_kernel(seg_ref, lstart_ref, gstart_ref, total_ref, ldt_ref, wt_ref, x1_ref, yb_ref, o_ref, ybuf_ref,
                    sems):
    i = pl.program_id(0)
    n = pl.num_programs(0)
    slot = i % 2

    def fetch(tile, s):
        _segment_copies(seg_ref, lstart_ref, gstart_ref, tile, ybuf_ref.at[s], yb_ref, sems.at[s], False,
                        lambda c: c.start())

    @pl.when(i == 0)
    def _():
        ybuf_ref[...] = jnp.zeros_like(ybuf_ref)
        fetch(i, slot)

    @pl.when(i + 1 < n)
    def _():
        fetch(i + 1, 1 - slot)

    _wait_tile(total_ref, i, ybuf_ref.at[slot], yb_ref, sems.at[slot], False)

    L = ybuf_ref.shape[1]
    RT = x1_ref.shape[0]
    c_iota = lax.broadcasted_iota(I32, (RT, L), 1)
    ldt = ldt_ref[...]
    w = wt_ref[...]
    w_hi = w.astype(BF16).astype(F32)
    w_lo = w - w_hi
    g_hi = jnp.zeros((RT, L), F32)
    g_lo = jnp.zeros((RT, L), F32)
    for k in range(TOP_K):
        hit = c_iota == ldt[:, k:k + 1]
        g_hi = jnp.where(hit, w_hi[:, k:k + 1], g_hi)
        g_lo = jnp.where(hit, w_lo[:, k:k + 1], g_lo)
    y = ybuf_ref[slot].astype(BF16)
    o_ref[...] = x1_ref[...] + (_dot(g_hi.astype(BF16), y) + _dot(g_lo.astype(BF16), y))


def _combine(ldt, wts_t, x1, yb, seg, lstart, gstart, totals):
    N, D = x1.shape
    RT = ROUTE_T
    grid_spec = pltpu.PrefetchScalarGridSpec(
        num_scalar_prefetch=4,
        grid=(N // RT,),
        in_specs=[
            pl.BlockSpec((RT, TOP_K), lambda i, *_: (i, 0)),
            pl.BlockSpec((RT, TOP_K), lambda i, *_: (i, 0)),
            pl.BlockSpec((RT, D), lambda i, *_: (i, 0)),
            pl.BlockSpec(memory_space=pl.ANY),
        ],
        out_specs=pl.BlockSpec((RT, D), lambda i, *_: (i, 0)),
        scratch_shapes=[pltpu.VMEM((2, LOCAL_ROWS, D), F32), pltpu.SemaphoreType.DMA((2,))],
    )
    return pl.pallas_call(
        _combine_kernel,
        grid_spec=grid_spec,
        out_shape=jax.ShapeDtypeStruct((N, D), F32),
        compiler_params=_cparams(("arbitrary",), 48),
        name="combine",
    )(seg, lstart, gstart, totals, ldt, wts_t, x1, yb)


def _layer(x, mem, g_mix, w_in, b_gate, g_ret_out, g_moba_q, g_moba_k, g_mem, w_mem_kv, g_mem_q, g_mem_k,
           w_br_ret, w_br_moba, w_br_mem, w_out, g_ffn, w_router, b_router, w_mlp1, b_mlp1, w_mlp2, b_mlp2):
    B, S, D = x.shape
    N = B * S
    x2 = x.reshape(N, D)
    slopes_np = np.exp2(-8.0 * (np.arange(MOBA_HEADS, dtype=np.float64) + 1.0) / MOBA_HEADS)
    assert all(float(np.log2(s)).is_integer() for s in slopes_np)
    slopes = tuple(float(s) for s in slopes_np)

    w_in_bf = w_in.astype(BF16)
    kmem, vmem = _memkv(mem, g_mem, w_mem_kv, g_mem_k)
    rq, rk, rv, rg, mq, mk, mv, cq, km = _inproj(x2, B, S, g_mix, w_in_bf[:, :MIX_W], g_moba_q, g_moba_k, g_mem_q,
                                                 slopes)
    o_ret = _retention(rq, rk, rv, rg, g_ret_out)
    o_moba = _moba(mq, mk, mv, km, jnp.asarray(slopes, F32)).reshape(N, MOBA_W)
    x1, xt, eidx, wts, rank, cnt = _merge(x2, S, o_ret, o_moba, cq, kmem, vmem, g_mix, w_in_bf[:, MIX_W:], b_gate,
                                          w_br_ret, w_br_moba, w_br_mem, w_out, g_ffn, w_router, b_router)
    T = MOE_T
    tcnt = cnt[:, :, 0]
    seg = ((tcnt + SEG_ALIGN - 1) // SEG_ALIGN) * SEG_ALIGN
    region_rows = jnp.sum(seg, axis=0)
    padded = ((region_rows + T - 1) // T) * T
    pad_ends = jnp.cumsum(padded).astype(I32)
    pad_starts = pad_ends - padded
    gstart = pad_starts[None, :] + jnp.cumsum(seg, axis=0) - seg
    lstart = jnp.cumsum(seg, axis=1) - seg
    lstart_tok = jnp.broadcast_to(lstart[:, None, :], (N // ROUTE_T, ROUTE_T, N_EXPERTS)).reshape(N, N_EXPERTS)
    onehot = eidx[:, :, None] == jnp.arange(N_EXPERTS, dtype=I32)[None, None, :]
    ld = jnp.sum(jnp.where(onehot, lstart_tok[None], 0), axis=-1) + rank
    NB = -(-(N // ROUTE_T * LOCAL_ROWS) // T) + N_EXPERTS
    seg_f, lstart_f, gstart_f = (a.reshape(-1).astype(I32) for a in (seg, lstart, gstart))
    totals = jnp.sum(seg, axis=1).astype(I32)

    xb = _dispatch(xt, ld, seg_f, lstart_f, gstart_f, totals, pad_ends, padded.astype(I32), NB * T)
    F2 = w_mlp1.shape[-1]
    b1p = b_mlp1.reshape(N_EXPERTS, F2 // (2 * LANES), LANES, 2).transpose(0, 1, 3, 2).reshape(N_EXPERTS, 1, F2)
    yb = _experts(xb, pad_starts.astype(I32), (padded // T).astype(I32), w_mlp1, b1p, w_mlp2, b_mlp2)
    out = _combine(ld.T, wts.T, x1, yb, seg_f, lstart_f, gstart_f, totals)
    return out.reshape(B, S, D)


def kernel(x, mem, g_mix, w_in, b_gate, g_ret_out, g_moba_q, g_moba_k, g_mem, w_mem_kv, g_mem_q, g_mem_k, w_br_ret, w_br_moba, w_br_mem, w_out, g_ffn, w_router, b_router, w_mlp1, b_mlp1, w_mlp2, b_mlp2):
    for l in range(g_mix.shape[0]):
        x = _layer(x, mem, g_mix[l], w_in[l], b_gate[l], g_ret_out[l], g_moba_q[l], g_moba_k[l], g_mem[l],
                   w_mem_kv[l], g_mem_q[l], g_mem_k[l], w_br_ret[l], w_br_moba[l], w_br_mem[l], w_out[l],
                   g_ffn[l], w_router[l], b_router[l], w_mlp1[l], b_mlp1[l], w_mlp2[l], b_mlp2[l])
    return x
```

```python
import functools

import jax
import jax.numpy as jnp
import numpy as np
from jax import lax
from jax.experimental import pallas as pl
from jax.experimental.pallas import tpu as pltpu

F32 = jnp.float32
BF16 = jnp.bfloat16
I32 = jnp.int32

EPS = 1e-5
NEG = -1e30

RET_HEADS = 4
RET_DK = 64
RET_DV = 128
RET_CHUNK = 128
MOBA_HEADS = 8
MOBA_HD = 64
MOBA_BLOCK = 256
MOBA_TOPK = 3
MEM_HEADS = 4
MEM_HD = 128
N_BRANCH = 3
N_EXPERTS = 32
TOP_K = 4
SWIGLU_LIMIT = 7.0
SWIGLU_ALPHA = 1.702

RET_Q = RET_HEADS * RET_DK
RET_V = RET_HEADS * RET_DV
MOBA_W = MOBA_HEADS * MOBA_HD
MEM_W = MEM_HEADS * MEM_HD
MIX_W = 2 * RET_Q + 2 * RET_V + 3 * MOBA_W + MEM_W

LANES = 128
MOBA_PAIRS = MOBA_HEADS // 2
MOBA_HEADS_PER_STEP = 8
BIAS_LANE0 = MOBA_HD
ONE_LANE = 80
VSUM_LANE = MOBA_HD

MOE_T = 512
TM_PROJ = 512
TM_MERGE = 512
TQ_RET = 512
ROUTE_T = 256
SEG_ALIGN = 8
LOCAL_ROWS = -(-(ROUTE_T * TOP_K + N_EXPERTS * (SEG_ALIGN - 1)) // LANES) * LANES
SEG_CHUNKS = tuple(2 ** p for p in range(ROUTE_T.bit_length() - 1, SEG_ALIGN.bit_length() - 2, -1))
TOTAL_CHUNKS = tuple(2 ** p for p in range(LOCAL_ROWS.bit_length() - 1, SEG_ALIGN.bit_length() - 2, -1))

_NT = (((1,), (1,)), ((), ()))
_TN = (((0,), (0,)), ((), ()))


def _rms(x, g):
    return x * lax.rsqrt(jnp.mean(x * x, axis=-1, keepdims=True) + EPS) * g


def _dot(a, b):
    return jnp.dot(a, b, preferred_element_type=F32)


def _cparams(sem, vmem_mb):
    return pltpu.CompilerParams(dimension_semantics=sem, vmem_limit_bytes=vmem_mb * 1024 * 1024)


def _memkv_kernel(mem_ref, g_ref, w_ref, gk_ref, k_ref, v_ref):
    m = _rms(mem_ref[0], g_ref[...]).astype(BF16)
    kv = _dot(m, w_ref[...])
    ks = [_rms(kv[:, h * MEM_HD:(h + 1) * MEM_HD], gk_ref[...]) for h in range(MEM_HEADS)]
    k_ref[0] = jnp.concatenate(ks, axis=-1).astype(BF16)
    v_ref[0] = kv[:, MEM_W:].astype(BF16)


def _memkv(mem, g_mem, w_mem_kv, g_mem_k):
    B, M, D = mem.shape
    return pl.pallas_call(
        _memkv_kernel,
        grid=(B,),
        in_specs=[
            pl.BlockSpec((1, M, D), lambda b: (b, 0, 0)),
            pl.BlockSpec((1, D), lambda b: (0, 0)),
            pl.BlockSpec((D, 2 * MEM_W), lambda b: (0, 0)),
            pl.BlockSpec((1, MEM_HD), lambda b: (0, 0)),
        ],
        out_specs=[
            pl.BlockSpec((1, M, MEM_W), lambda b: (b, 0, 0)),
            pl.BlockSpec((1, M, MEM_W), lambda b: (b, 0, 0)),
        ],
        out_shape=[jax.ShapeDtypeStruct((B, M, MEM_W), BF16)] * 2,
        compiler_params=_cparams(("arbitrary",), 32),
        name="memkv",
    )(mem, g_mem.reshape(1, D), w_mem_kv.astype(BF16), g_mem_k.reshape(1, MEM_HD))


def _head_pair_norm(a2, g2, lane):
    sq = a2 * a2
    lo = lane < MOBA_HD
    ss_lo = jnp.sum(jnp.where(lo, sq, 0.0), axis=-1, keepdims=True)
    ss_hi = jnp.sum(jnp.where(lo, 0.0, sq), axis=-1, keepdims=True)
    inv = jnp.where(lo, lax.rsqrt(ss_lo / MOBA_HD + EPS), lax.rsqrt(ss_hi / MOBA_HD + EPS))
    return a2 * inv * g2


def _inproj_kernel(slopes, seq_tiles, x_ref, gmix_ref, w_ref, gq_ref, gk_ref, gc_ref,
                   rq_ref, rk_ref, rv_ref, rg_ref, mq_ref, mk_ref, mv_ref, cq_ref, kmean_ref):
    tm = x_ref.shape[0]
    blocks_per_tile = tm // MOBA_BLOCK
    h = _rms(x_ref[...], gmix_ref[...]).astype(BF16)
    col = [0]

    def proj(width):
        a = _dot(h, w_ref[:, col[0]:col[0] + width])
        col[0] += width
        return a

    a = proj(2 * RET_Q)
    for hh in range(RET_HEADS):
        rq_ref[0, hh] = a[:, hh * RET_DK:(hh + 1) * RET_DK].astype(BF16)
        rk_ref[0, hh] = (a[:, RET_Q + hh * RET_DK:RET_Q + (hh + 1) * RET_DK] * (RET_DK ** -0.5)).astype(BF16)
    rv_ref[...] = proj(RET_V).astype(BF16)
    rg_ref[...] = proj(RET_V).astype(BF16)

    lane = lax.broadcasted_iota(I32, (tm, LANES), 1)
    row = lax.broadcasted_iota(I32, (tm, LANES), 0)
    lo = lane < MOBA_HD
    q_tail = jnp.where(lane == ONE_LANE, 1.0, 0.0)
    a = proj(MOBA_W)
    for p in range(MOBA_PAIRS):
        n2 = _head_pair_norm(a[:, p * LANES:(p + 1) * LANES], gq_ref[...], lane) * (MOBA_HD ** -0.5)
        mq_ref[0, 2 * p] = jnp.where(lo, n2, q_tail).astype(BF16)
        mq_ref[0, 2 * p + 1] = jnp.where(lo, pltpu.roll(n2, MOBA_HD, 1), q_tail).astype(BF16)
    blk = (pl.program_id(0) % seq_tiles) * blocks_per_tile + row // MOBA_BLOCK
    onehot_tail = jnp.where(lane == BIAS_LANE0 + blk, 1.0, 0.0)
    off = (row % MOBA_BLOCK).astype(F32)
    a = proj(MOBA_W)
    for p in range(MOBA_PAIRS):
        n2 = _head_pair_norm(a[:, p * LANES:(p + 1) * LANES], gk_ref[...], lane)
        for j in range(blocks_per_tile):
            kmean_ref[0, 0, p, j:j + 1, :] = jnp.mean(n2[j * MOBA_BLOCK:(j + 1) * MOBA_BLOCK], axis=0, keepdims=True)
        for s, src in ((0, n2), (1, pltpu.roll(n2, MOBA_HD, 1))):
            tail = jnp.where(lane == ONE_LANE, slopes[2 * p + s] * off, onehot_tail)
            mk_ref[0, 2 * p + s] = jnp.where(lo, src, tail).astype(BF16)
    v_tail = jnp.where(lane == VSUM_LANE, 1.0, 0.0)
    a = proj(MOBA_W)
    for p in range(MOBA_PAIRS):
        a2 = a[:, p * LANES:(p + 1) * LANES]
        mv_ref[0, 2 * p] = jnp.where(lo, a2, v_tail).astype(BF16)
        mv_ref[0, 2 * p + 1] = jnp.where(lo, pltpu.roll(a2, MOBA_HD, 1), v_tail).astype(BF16)
    a = proj(MEM_W)
    cq = [_rms(a[:, hh * MEM_HD:(hh + 1) * MEM_HD], gc_ref[...]) for hh in range(MEM_HEADS)]
    cq_ref[...] = jnp.concatenate(cq, axis=-1).astype(BF16)


def _inproj(x2, B, S, g_mix, w_mix, g_moba_q, g_moba_k, g_mem_q, slopes):
    N, D = x2.shape
    tm = TM_PROJ
    nS = S // tm
    bpt = tm // MOBA_BLOCK
    tok = lambda i: (i, 0)
    headmaj = lambda i: (i // nS, 0, i % nS, 0)
    g2 = lambda g: jnp.concatenate([g, g]).reshape(1, LANES)
    outs = pl.pallas_call(
        functools.partial(_inproj_kernel, slopes, nS),
        grid=(N // tm,),
        in_specs=[
            pl.BlockSpec((tm, D), tok),
            pl.BlockSpec((1, D), lambda i: (0, 0)),
            pl.BlockSpec((D, MIX_W), lambda i: (0, 0)),
            pl.BlockSpec((1, LANES), lambda i: (0, 0)),
            pl.BlockSpec((1, LANES), lambda i: (0, 0)),
            pl.BlockSpec((1, MEM_HD), lambda i: (0, 0)),
        ],
        out_specs=[
            pl.BlockSpec((1, RET_HEADS, tm, RET_DK), headmaj),
            pl.BlockSpec((1, RET_HEADS, tm, RET_DK), headmaj),
            pl.BlockSpec((tm, RET_V), tok),
            pl.BlockSpec((tm, RET_V), tok),
            pl.BlockSpec((1, MOBA_HEADS, tm, LANES), headmaj),
            pl.BlockSpec((1, MOBA_HEADS, tm, LANES), headmaj),
            pl.BlockSpec((1, MOBA_HEADS, tm, LANES), headmaj),
            pl.BlockSpec((tm, MEM_W), tok),
            pl.BlockSpec((1, 1, MOBA_PAIRS, bpt, LANES), lambda i: (i // nS, i % nS, 0, 0, 0)),
        ],
        out_shape=[
            jax.ShapeDtypeStruct((B, RET_HEADS, S, RET_DK), BF16),
            jax.ShapeDtypeStruct((B, RET_HEADS, S, RET_DK), BF16),
            jax.ShapeDtypeStruct((N, RET_V), BF16),
            jax.ShapeDtypeStruct((N, RET_V), BF16),
            jax.ShapeDtypeStruct((B, MOBA_HEADS, S, LANES), BF16),
            jax.ShapeDtypeStruct((B, MOBA_HEADS, S, LANES), BF16),
            jax.ShapeDtypeStruct((B, MOBA_HEADS, S, LANES), BF16),
            jax.ShapeDtypeStruct((N, MEM_W), BF16),
            jax.ShapeDtypeStruct((B, nS, MOBA_PAIRS, bpt, LANES), F32),
        ],
        compiler_params=_cparams(("arbitrary",), 56),
        name="inproj",
    )(x2, g_mix.reshape(1, D), w_mix, g2(g_moba_q), g2(g_moba_k), g_mem_q.reshape(1, MEM_HD))
    rq, rk, rv, rg, mq, mk, mv, cq, kmean_pairs = outs
    km = kmean_pairs.reshape(B, nS, MOBA_PAIRS, bpt, 2, MOBA_HD).transpose(0, 2, 4, 1, 3, 5)
    km = km.reshape(B, MOBA_HEADS, nS * bpt, MOBA_HD)
    km = jnp.pad(km, ((0, 0), (0, 0), (0, 0), (0, LANES - MOBA_HD)))
    return rq, rk, rv, rg, mq, mk, mv, cq, km


def _retention_kernel(q_ref, k_ref, v_ref, rg_ref, din_ref, dq_ref, dk_ref, dc_ref, g_ref, o_ref, state_ref):
    @pl.when(pl.program_id(2) == 0)
    def _():
        state_ref[...] = jnp.zeros_like(state_ref)

    C = RET_CHUNK
    for c in range(q_ref.shape[2] // C):
        rows = slice(c * C, (c + 1) * C)
        q = q_ref[0, 0, rows, :]
        k = k_ref[0, 0, rows, :]
        v = v_ref[rows, :]
        state = state_ref[...]
        scores = lax.dot_general(q, k, _NT, preferred_element_type=F32) * din_ref[0]
        intra = _dot(scores.astype(BF16), v)
        cross = _dot(q, state.astype(BF16)) * dq_ref[0]
        kd = (k.astype(F32) * dk_ref[0]).astype(BF16)
        state_ref[...] = dc_ref[0] * state + lax.dot_general(kd, v, _TN, preferred_element_type=F32)
        o = _rms(intra + cross, g_ref[...])
        o_ref[rows, :] = (o * jax.nn.silu(rg_ref[rows, :].astype(F32))).astype(BF16)


def _retention_decays():
    H, C = RET_HEADS, RET_CHUNK
    log_g = jnp.log1p(-jnp.exp2(-5.0 - jnp.arange(H, dtype=F32)))
    i = jnp.arange(C, dtype=F32)
    diff = i[:, None] - i[None, :]
    decay_in = jnp.where(diff >= 0, jnp.exp(jnp.maximum(diff, 0.0)[None] * log_g[:, None, None]), 0.0)
    decay_k = jnp.exp((C - 1 - i)[None, :] * log_g[:, None])
    decay_q = jnp.exp((i + 1)[None, :] * log_g[:, None])
    decay_chunk = jnp.exp(C * log_g)
    dq = jnp.broadcast_to(decay_q[:, :, None], (H, C, RET_DV))
    dk = jnp.broadcast_to(decay_k[:, :, None], (H, C, RET_DK))
    dc = jnp.broadcast_to(decay_chunk[:, None, None], (H, RET_DK, RET_DV))
    return decay_in, dq, dk, dc


def _retention(rq, rk, rv, rg, g_ret_out):
    B, H, S, dk = rq.shape
    tq = TQ_RET
    nT = S // tq
    C = RET_CHUNK
    din, dq, dk_, dc = _retention_decays()
    qk_spec = pl.BlockSpec((1, 1, tq, dk), lambda b, h, t: (b, h, t, 0))
    tok_spec = pl.BlockSpec((tq, RET_DV), lambda b, h, t: (b * nT + t, h))
    per_head = lambda r, c: pl.BlockSpec((1, r, c), lambda b, h, t: (h, 0, 0))
    return pl.pallas_call(
        _retention_kernel,
        grid=(B, H, nT),
        in_specs=[qk_spec, qk_spec, tok_spec, tok_spec,
                  per_head(C, C), per_head(C, RET_DV), per_head(C, RET_DK), per_head(RET_DK, RET_DV),
                  pl.BlockSpec((1, RET_DV), lambda b, h, t: (0, 0))],
        out_specs=tok_spec,
        out_shape=jax.ShapeDtypeStruct((B * S, RET_V), BF16),
        scratch_shapes=[pltpu.VMEM((RET_DK, RET_DV), F32)],
        compiler_params=_cparams(("arbitrary", "arbitrary", "arbitrary"), 32),
        name="retention",
    )(rq, rk, rv, rg, din, dq, dk_, dc, g_ret_out.reshape(1, RET_DV))


def _moba_kernel(slopes_ref, q_ref, k_ref, v_ref, km_ref, shift_ref, causal_ref, o_ref):
    NH = q_ref.shape[1]
    head0 = pl.program_id(1) * NH
    i = pl.program_id(2)
    BS = MOBA_BLOCK
    nb = km_ref.shape[2]
    heads = range(NH)
    n_iota = lax.broadcasted_iota(I32, (nb, BS), 0)
    past = n_iota < i
    qa = [q_ref[0, s] for s in heads]
    q_aug = []
    for s in heads:
        km = km_ref[0, s]
        hi = km.astype(BF16)
        mid = (km - hi.astype(F32)).astype(BF16)
        lo = (km - hi.astype(F32) - mid.astype(F32)).astype(BF16)
        g3 = lax.dot_general(jnp.concatenate([hi, mid, lo], axis=0), qa[s], _NT, preferred_element_type=F32)
        gate = (g3[:nb] + g3[nb:2 * nb]) + g3[2 * nb:]
        g = jnp.where(past, gate, -jnp.inf)
        rank = jnp.zeros((nb, BS), I32)
        for m in range(nb - 1):
            gm = g[m:m + 1, :]
            rank = rank + jnp.where(gm > g, 1, jnp.where(gm == g, jnp.where(m < n_iota, 1, 0), 0))
        sel = jnp.where(past, rank, MOBA_TOPK) < MOBA_TOPK
        bias = jnp.where(sel, (n_iota - i).astype(F32) * (slopes_ref[head0 + s] * BS), NEG).astype(BF16)
        placed = lax.dot_general(bias, shift_ref[...], _TN, preferred_element_type=F32)
        q_aug.append((qa[s].astype(F32) + placed).astype(BF16))

    own = pl.ds(pl.multiple_of(i * BS, BS), BS)
    m0, acc0 = [], []
    for s in heads:
        sc = lax.dot_general(qa[s], k_ref[0, s, own, :], _NT, preferred_element_type=F32) + causal_ref[...]
        m = jnp.max(sc, axis=-1, keepdims=True)
        m0.append(m)
        acc0.append(_dot(jnp.exp(sc - m).astype(BF16), v_ref[0, s, own, :]))

    def body(j, carry):
        ms, accs = carry
        blk = pl.ds(pl.multiple_of(j * BS, BS), BS)
        new_m, new_acc = [], []
        for s in heads:
            sj = lax.dot_general(q_aug[s], k_ref[0, s, blk, :], _NT, preferred_element_type=F32)
            m_new = jnp.maximum(ms[s], jnp.max(sj, axis=-1, keepdims=True))
            pj = jnp.exp(sj - m_new).astype(BF16)
            new_acc.append(jnp.exp(ms[s] - m_new) * accs[s] + _dot(pj, v_ref[0, s, blk, :]))
            new_m.append(m_new)
        return tuple(new_m), tuple(new_acc)

    _, accs = lax.fori_loop(0, i, body, (tuple(m0), tuple(acc0)))
    outs = [acc / acc[:, VSUM_LANE:VSUM_LANE + 1] for acc in accs]
    lane = lax.broadcasted_iota(I32, (BS, LANES), 1)
    pairs = [jnp.where(lane < MOBA_HD, outs[2 * p], pltpu.roll(outs[2 * p + 1], MOBA_HD, 1)) for p in range(NH // 2)]
    o_ref[0] = jnp.concatenate(pairs, axis=-1).astype(BF16)


def _moba(mq, mk, mv, km, slopes):
    B, H, S, _ = mq.shape
    BS = MOBA_BLOCK
    NH = MOBA_HEADS_PER_STEP
    nq = S // BS
    nb = km.shape[2]
    assert BIAS_LANE0 + nb <= ONE_LANE
    shift = (jnp.arange(nb)[:, None] + BIAS_LANE0 == jnp.arange(LANES)[None, :]).astype(BF16)
    causal = jnp.where(jnp.arange(BS)[:, None] >= jnp.arange(BS)[None, :], 0.0, NEG).astype(F32)
    per_head = lambda rows: pl.BlockSpec((1, NH, rows, LANES), lambda b, p, i, sl: (b, p, 0, 0))
    grid_spec = pltpu.PrefetchScalarGridSpec(
        num_scalar_prefetch=1,
        grid=(B, H // NH, nq),
        in_specs=[
            pl.BlockSpec((1, NH, BS, LANES), lambda b, p, i, sl: (b, p, i, 0)),
            per_head(S), per_head(S), per_head(nb),
            pl.BlockSpec((nb, LANES), lambda b, p, i, sl: (0, 0)),
            pl.BlockSpec((BS, BS), lambda b, p, i, sl: (0, 0)),
        ],
        out_specs=pl.BlockSpec((1, BS, NH * MOBA_HD), lambda b, p, i, sl: (b, i, p)),
    )
    return pl.pallas_call(
        _moba_kernel,
        grid_spec=grid_spec,
        out_shape=jax.ShapeDtypeStruct((B, S, MOBA_W), BF16),
        compiler_params=_cparams(("arbitrary", "arbitrary", "arbitrary"), 48),
        name="moba",
    )(slopes, mq, mk, mv, km, shift, causal)


def _merge_kernel(x_ref, oret_ref, omoba_ref, cq_ref, kmem_ref, vmem_ref, gmix_ref, wg_ref, bg_ref,
                  wbr_ref, wbm_ref, wbc_ref, wout_ref, gffn_ref, wr_ref, br_ref, tri_ref,
                  x1_ref, xt_ref, eidx_ref, wts_ref, rank_ref, cnt_ref):
    D = x_ref.shape[1]
    tm = x_ref.shape[0]

    x = x_ref[...]
    h = _rms(x, gmix_ref[...]).astype(BF16)
    cq = cq_ref[...]
    om = []
    for hh in range(MEM_HEADS):
        cols = slice(hh * MEM_HD, (hh + 1) * MEM_HD)
        sc = lax.dot_general(cq[:, cols], kmem_ref[0, :, cols], _NT, preferred_element_type=F32) * (MEM_HD ** -0.5)
        sc = sc - jnp.max(sc, axis=-1, keepdims=True)
        p = jnp.exp(sc)
        p = p / jnp.sum(p, axis=-1, keepdims=True)
        om.append(_dot(p.astype(BF16), vmem_ref[0, :, cols]))
    omem = jnp.concatenate(om, axis=-1).astype(BF16)
    y = None
    for br, (o, w_ref) in enumerate(((oret_ref[...], wbr_ref), (omoba_ref[...], wbm_ref), (omem, wbc_ref))):
        gl = _dot(h, wg_ref[:, br * D:(br + 1) * D]) + bg_ref[:, br * D:(br + 1) * D]
        term = jax.nn.sigmoid(gl) * _dot(o, w_ref[...])
        y = term if y is None else y + term
    x1 = x + _dot(y.astype(BF16), wout_ref[...])
    x1_ref[...] = x1
    xt = _rms(x1, gffn_ref[...])
    wr = wr_ref[...]
    E = wr.shape[0]
    w_hi = wr.astype(BF16)
    w_mid = (wr - w_hi.astype(F32)).astype(BF16)
    w_lo = (wr - w_hi.astype(F32) - w_mid.astype(F32)).astype(BF16)
    xt_hi = xt.astype(BF16)
    xt_ref[...] = xt_hi
    xt_lo = (xt - xt_hi.astype(F32)).astype(BF16)
    a = lax.dot_general(jnp.concatenate([w_hi, w_mid, w_lo], axis=0), xt_hi, _NT, preferred_element_type=F32)
    b = lax.dot_general(jnp.concatenate([w_hi, w_mid], axis=0), xt_lo, _NT, preferred_element_type=F32)
    logits = (a[:E] + (a[E:2 * E] + b[:E])) + (a[2 * E:] + b[E:]) + br_ref[...]
    e_iota = lax.broadcasted_iota(I32, (E, tm), 0)
    l = logits
    vals, hots = [], []
    for k in range(TOP_K):
        m = jnp.max(l, axis=0, keepdims=True)
        idx = jnp.min(jnp.where(l == m, e_iota, E), axis=0, keepdims=True)
        hot = e_iota == idx
        l = jnp.where(hot, -jnp.inf, l)
        vals.append(m)
        hots.append(hot)
        eidx_ref[k:k + 1, :] = idx
    ex = [jnp.exp(v - vals[0]) for v in vals]
    den = ex[0]
    for k in range(1, TOP_K):
        den = den + ex[k]
    chosen = jnp.zeros((E, tm), F32)
    for k in range(TOP_K):
        wts_ref[k:k + 1, :] = ex[k] / den
        chosen = chosen + jnp.where(hots[k], 1.0, 0.0)
    RT = tri_ref.shape[0]
    for t in range(tm // RT):
        cols = slice(t * RT, (t + 1) * RT)
        prefix = _dot(chosen[:, cols].astype(BF16), tri_ref[...])
        for k in range(TOP_K):
            rank_ref[k:k + 1, cols] = jnp.sum(jnp.where(hots[k][:, cols], prefix, 0.0), axis=0, keepdims=True).astype(I32)
        counts = jnp.sum(chosen[:, cols], axis=1, keepdims=True)
        cnt_ref[t] = jnp.broadcast_to(counts, cnt_ref.shape[1:]).astype(I32)


def _merge(x2, S, o_ret, o_moba, cq, kmem, vmem, g_mix, w_gate, b_gate, w_br_ret, w_br_moba, w_br_mem, w_out,
           g_ffn, w_router, b_router):
    N, D = x2.shape
    tm = TM_MERGE
    nS = S // tm
    M = kmem.shape[1]
    E = w_router.shape[1]
    tok = lambda w: pl.BlockSpec((tm, w), lambda i: (i, 0))
    const = lambda r, c: pl.BlockSpec((r, c), lambda i: (0, 0), pipeline_mode=pl.Buffered(1))
    mem_spec = pl.BlockSpec((1, M, MEM_W), lambda i: (i // nS, 0, 0))
    lanes_tok = lambda r, dt: (pl.BlockSpec((r, tm), lambda i: (0, i)), jax.ShapeDtypeStruct((r, N), dt))
    RT = ROUTE_T
    tri = (jnp.arange(RT)[:, None] < jnp.arange(RT)[None, :]).astype(BF16)
    e_spec, e_shape = lanes_tok(TOP_K, I32)
    w_spec, w_shape = lanes_tok(TOP_K, F32)
    r_spec, r_shape = lanes_tok(TOP_K, I32)
    return pl.pallas_call(
        _merge_kernel,
        grid=(N // tm,),
        in_specs=[tok(D), tok(RET_V), tok(MOBA_W), tok(MEM_W), mem_spec, mem_spec,
                  const(1, D), const(D, N_BRANCH * D), const(1, N_BRANCH * D),
                  const(RET_V, D), const(MOBA_W, D), const(MEM_W, D), const(D, D),
                  const(1, D), const(E, D), const(E, 1), const(RT, RT)],
        out_specs=[tok(D), tok(D), e_spec, w_spec, r_spec, pl.BlockSpec((tm // RT, E, LANES), lambda i: (i, 0, 0))],
        out_shape=[jax.ShapeDtypeStruct((N, D), F32), jax.ShapeDtypeStruct((N, D), BF16),
                   e_shape, w_shape, r_shape, jax.ShapeDtypeStruct((N // RT, E, LANES), I32)],
        compiler_params=_cparams(("arbitrary",), 56),
        name="merge",
    )(x2, o_ret, o_moba, cq, kmem, vmem, g_mix.reshape(1, D), w_gate, b_gate.reshape(1, N_BRANCH * D),
      w_br_ret.astype(BF16), w_br_moba.astype(BF16), w_br_mem.astype(BF16), w_out.astype(BF16),
      g_ffn.reshape(1, D), w_router.T, b_router.reshape(E, 1), tri)


def _segment_copies(seg_ref, lstart_ref, gstart_ref, tile, local_ref, slots_ref, sem, to_slots, fn):
    def per_expert(e, carry):
        idx = tile * N_EXPERTS + e
        size = seg_ref[idx]
        off = jnp.int32(0)
        for chunk in SEG_CHUNKS:
            take = (size & chunk) != 0
            lo = pl.ds(pl.multiple_of(lstart_ref[idx] + off, SEG_ALIGN), chunk)
            gl = pl.ds(pl.multiple_of(gstart_ref[idx] + off, SEG_ALIGN), chunk)
            src, dst = (local_ref.at[lo], slots_ref.at[gl]) if to_slots else (slots_ref.at[gl], local_ref.at[lo])

            @pl.when(take)
            def _():
                fn(pltpu.make_async_copy(src, dst, sem))

            off = off + jnp.where(take, chunk, 0)
        return carry

    lax.fori_loop(0, N_EXPERTS, per_expert, 0)


def _wait_tile(total_ref, tile, local_ref, slots_ref, sem, to_slots):
    total = total_ref[tile]
    for chunk in TOTAL_CHUNKS:
        lo, gl = local_ref.at[pl.ds(0, chunk)], slots_ref.at[pl.ds(0, chunk)]
        src, dst = (lo, gl) if to_slots else (gl, lo)

        @pl.when((total & chunk) != 0)
        def _():
            pltpu.make_async_copy(src, dst, sem).wait()


def _dispatch_kernel(seg_ref, lstart_ref, gstart_ref, total_ref, pad_end_ref, padded_ref, ld_ref, xt_ref, xb_ref,
                     ybuf_ref, zeros_ref, zsem, sems):
    T = zeros_ref.shape[0]
    i = pl.program_id(0)
    n = pl.num_programs(0)
    slot = i % 2

    @pl.when(i == 0)
    def _():
        zeros_ref[...] = jnp.zeros_like(zeros_ref)

        def fill(start):
            return pltpu.make_async_copy(zeros_ref, xb_ref.at[pl.ds(pl.multiple_of(start, T), T)], zsem)

        for e in range(N_EXPERTS):
            @pl.when(padded_ref[e] > 0)
            def _():
                fill(pad_end_ref[e] - T).start()
        for e in range(N_EXPERTS):
            @pl.when(padded_ref[e] > 0)
            def _():
                fill(pad_end_ref[e] - T).wait()
        first_unused = pad_end_ref[N_EXPERTS - 1] // T
        n_blocks = xb_ref.shape[0] // T
        lax.fori_loop(first_unused, n_blocks, lambda b, c: (fill(b * T).start(), c)[1], 0)
        lax.fori_loop(first_unused, n_blocks, lambda b, c: (fill(b * T).wait(), c)[1], 0)

    @pl.when(i >= 2)
    def _():
        _wait_tile(total_ref, i - 2, ybuf_ref.at[slot], xb_ref, sems.at[slot], True)

    L = ybuf_ref.shape[1]
    RT = xt_ref.shape[0]
    r_iota = lax.broadcasted_iota(I32, (L, RT), 0)
    ld = ld_ref[...]
    onehot = jnp.zeros((L, RT), F32)
    for k in range(TOP_K):
        onehot = jnp.where(r_iota == ld[k:k + 1, :], 1.0, onehot)
    onehot = onehot.astype(BF16)
    ybuf_ref[slot] = _dot(onehot, xt_ref[...])
    _segment_copies(seg_ref, lstart_ref, gstart_ref, i, ybuf_ref.at[slot], xb_ref, sems.at[slot], True,
                    lambda c: c.start())

    @pl.when(i == n - 1)
    def _():
        @pl.when(n >= 2)
        def _():
            _wait_tile(total_ref, i - 1, ybuf_ref.at[1 - slot], xb_ref, sems.at[1 - slot], True)

        _wait_tile(total_ref, i, ybuf_ref.at[slot], xb_ref, sems.at[slot], True)


def _dispatch(xt, ld, seg, lstart, gstart, totals, pad_ends, padded, R):
    N, D = xt.shape
    RT = ROUTE_T
    grid_spec = pltpu.PrefetchScalarGridSpec(
        num_scalar_prefetch=6,
        grid=(N // RT,),
        in_specs=[
            pl.BlockSpec((TOP_K, RT), lambda i, *_: (0, i)),
            pl.BlockSpec((RT, D), lambda i, *_: (i, 0)),
        ],
        out_specs=pl.BlockSpec(memory_space=pl.ANY),
        scratch_shapes=[pltpu.VMEM((2, LOCAL_ROWS, D), F32), pltpu.VMEM((MOE_T, D), F32),
                        pltpu.SemaphoreType.DMA(()), pltpu.SemaphoreType.DMA((2,))],
    )
    return pl.pallas_call(
        _dispatch_kernel,
        grid_spec=grid_spec,
        out_shape=jax.ShapeDtypeStruct((R, D), F32),
        compiler_params=_cparams(("arbitrary",), 48),
        name="dispatch",
    )(seg, lstart, gstart, totals, pad_ends, padded, ld, xt)


def _expert_kernel(start_ref, nblk_ref, w1_ref, b1_ref, w2_ref, b2_ref, perm_ref, xb_ref, yb_ref,
                   w1p_ref, w2b_ref, xbuf_ref, ybuf_ref, in_sems, out_sems):
    e = pl.program_id(0)
    T = xbuf_ref.shape[1]
    G = 2 * LANES
    nb = nblk_ref[e]
    base = start_ref[e]

    def rows(b):
        return pl.ds(pl.multiple_of(base + b * T, T), T)

    def in_copy(b, slot):
        return pltpu.make_async_copy(xb_ref.at[rows(b)], xbuf_ref.at[slot], in_sems.at[slot])

    def out_copy(b, slot):
        return pltpu.make_async_copy(ybuf_ref.at[slot], yb_ref.at[rows(b)], out_sems.at[slot])

    @pl.when(nb > 0)
    def _():
        in_copy(0, 0).start()
        for g in range(w1p_ref.shape[1] // G):
            w = w1_ref[0, :, g * G:(g + 1) * G].astype(BF16)
            w1p_ref[:, g * G:(g + 1) * G] = _dot(w, perm_ref[...]).astype(BF16)
        w2b_ref[...] = w2_ref[0].astype(BF16)

        def block(b, carry):
            slot = b % 2
            in_copy(b, slot).wait()

            @pl.when(b + 1 < nb)
            def _():
                in_copy(b + 1, 1 - slot).start()

            x = xbuf_ref[slot].astype(BF16)
            acts = []
            for g in range(w1p_ref.shape[1] // G):
                hg = _dot(x, w1p_ref[:, g * G:(g + 1) * G]) + b1_ref[0, :, g * G:(g + 1) * G]
                glu = jnp.minimum(hg[:, :LANES], SWIGLU_LIMIT)
                lin = jnp.clip(hg[:, LANES:], -SWIGLU_LIMIT, SWIGLU_LIMIT)
                acts.append((glu * jax.nn.sigmoid(SWIGLU_ALPHA * glu) * (lin + 1.0)).astype(BF16))
            y = _dot(jnp.concatenate(acts, axis=-1), w2b_ref[...]) + b2_ref[0]

            @pl.when(b >= 2)
            def _():
                out_copy(b - 2, slot).wait()

            ybuf_ref[slot] = y
            out_copy(b, slot).start()
            return carry

        lax.fori_loop(0, nb, block, 0)

        @pl.when(nb >= 2)
        def _():
            out_copy(nb - 2, nb % 2).wait()

        out_copy(nb - 1, (nb - 1) % 2).wait()

    @pl.when(e == pl.num_programs(0) - 1)
    def _():
        ybuf_ref[0] = jnp.zeros(ybuf_ref.shape[1:], ybuf_ref.dtype)
        first_unused = base // T + nb
        n_blocks = yb_ref.shape[0] // T

        def tail(b):
            return pltpu.make_async_copy(ybuf_ref.at[0], yb_ref.at[pl.ds(pl.multiple_of(b * T, T), T)], out_sems.at[0])

        lax.fori_loop(first_unused, n_blocks, lambda b, c: (tail(b).start(), c)[1], 0)
        lax.fori_loop(first_unused, n_blocks, lambda b, c: (tail(b).wait(), c)[1], 0)


def _experts(xb, region_start, region_blocks, w1, b1p, w2, b2):
    R, D = xb.shape
    E, _, F2 = w1.shape
    F = F2 // 2
    T = MOE_T
    G = 2 * LANES
    c = np.arange(G)
    src = np.where(c < LANES, 2 * c, 2 * (c - LANES) + 1)
    perm = jnp.asarray(np.arange(G)[:, None] == src[None, :], dtype=BF16)
    per_expert = lambda r, w: pl.BlockSpec((1, r, w), lambda e, *_: (e, 0, 0))
    grid_spec = pltpu.PrefetchScalarGridSpec(
        num_scalar_prefetch=2,
        grid=(E,),
        in_specs=[
            per_expert(D, F2), per_expert(1, F2), per_expert(F, D), per_expert(1, D),
            pl.BlockSpec((G, G), lambda e, *_: (0, 0)),
            pl.BlockSpec(memory_space=pl.ANY),
        ],
        out_specs=pl.BlockSpec(memory_space=pl.ANY),
        scratch_shapes=[pltpu.VMEM((D, F2), BF16), pltpu.VMEM((F, D), BF16),
                        pltpu.VMEM((2, T, D), F32), pltpu.VMEM((2, T, D), F32),
                        pltpu.SemaphoreType.DMA((2,)), pltpu.SemaphoreType.DMA((2,))],
    )
    return pl.pallas_call(
        _expert_kernel,
        grid_spec=grid_spec,
        out_shape=jax.ShapeDtypeStruct((R, D), F32),
        compiler_params=_cparams(("arbitrary",), 56),
        name="experts",
    )(region_start, region_blocks, w1, b1p, w2, b2.reshape(E, 1, D), perm, xb)


def _combine_kernel(seg_ref, lstart_ref, gstart_ref, total_ref, ldt_ref, wt_ref, x1_ref, yb_ref, o_ref, ybuf_ref,
                    sems):
    i = pl.program_id(0)
    n = pl.num_programs(0)
    slot = i % 2

    def fetch(tile, s):
        _segment_copies(seg_ref, lstart_ref, gstart_ref, tile, ybuf_ref.at[s], yb_ref, sems.at[s], False,
                        lambda c: c.start())

    @pl.when(i == 0)
    def _():
        ybuf_ref[...] = jnp.zeros_like(ybuf_ref)
        fetch(i, slot)

    @pl.when(i + 1 < n)
    def _():
        fetch(i + 1, 1 - slot)

    _wait_tile(total_ref, i, ybuf_ref.at[slot], yb_ref, sems.at[slot], False)

    L = ybuf_ref.shape[1]
    RT = x1_ref.shape[0]
    c_iota = lax.broadcasted_iota(I32, (RT, L), 1)
    ldt = ldt_ref[...]
    w = wt_ref[...]
    w_hi = w.astype(BF16).astype(F32)
    w_lo = w - w_hi
    g_hi = jnp.zeros((RT, L), F32)
    g_lo = jnp.zeros((RT, L), F32)
    for k in range(TOP_K):
        hit = c_iota == ldt[:, k:k + 1]
        g_hi = jnp.where(hit, w_hi[:, k:k + 1], g_hi)
        g_lo = jnp.where(hit, w_lo[:, k:k + 1], g_lo)
    y = ybuf_ref[slot].astype(BF16)
    o_ref[...] = x1_ref[...] + (_dot(g_hi.astype(BF16), y) + _dot(g_lo.astype(BF16), y))


def _combine(ldt, wts_t, x1, yb, seg, lstart, gstart, totals):
    N, D = x1.shape
    RT = ROUTE_T
    grid_spec = pltpu.PrefetchScalarGridSpec(
        num_scalar_prefetch=4,
        grid=(N // RT,),
        in_specs=[
            pl.BlockSpec((RT, TOP_K), lambda i, *_: (i, 0)),
            pl.BlockSpec((RT, TOP_K), lambda i, *_: (i, 0)),
            pl.BlockSpec((RT, D), lambda i, *_: (i, 0)),
            pl.BlockSpec(memory_space=pl.ANY),
        ],
        out_specs=pl.BlockSpec((RT, D), lambda i, *_: (i, 0)),
        scratch_shapes=[pltpu.VMEM((2, LOCAL_ROWS, D), F32), pltpu.SemaphoreType.DMA((2,))],
    )
    return pl.pallas_call(
        _combine_kernel,
        grid_spec=grid_spec,
        out_shape=jax.ShapeDtypeStruct((N, D), F32),
        compiler_params=_cparams(("arbitrary",), 48),
        name="combine",
    )(seg, lstart, gstart, totals, ldt, wts_t, x1, yb)


def _layer(x, mem, g_mix, w_in, b_gate, g_ret_out, g_moba_q, g_moba_k, g_mem, w_mem_kv, g_mem_q, g_mem_k,
           w_br_ret, w_br_moba, w_br_mem, w_out, g_ffn, w_router, b_router, w_mlp1, b_mlp1, w_mlp2, b_mlp2):
    B, S, D = x.shape
    N = B * S
    x2 = x.reshape(N, D)
    slopes_np = np.exp2(-8.0 * (np.arange(MOBA_HEADS, dtype=np.float64) + 1.0) / MOBA_HEADS)
    assert all(float(np.log2(s)).is_integer() for s in slopes_np)
    slopes = tuple(float(s) for s in slopes_np)

    w_in_bf = w_in.astype(BF16)
    kmem, vmem = _memkv(mem, g_mem, w_mem_kv, g_mem_k)
    rq, rk, rv, rg, mq, mk, mv, cq, km = _inproj(x2, B, S, g_mix, w_in_bf[:, :MIX_W], g_moba_q, g_moba_k, g_mem_q,
                                                 slopes)
    o_ret = _retention(rq, rk, rv, rg, g_ret_out)
    o_moba = _moba(mq, mk, mv, km, jnp.asarray(slopes, F32)).reshape(N, MOBA_W)
    x1, xt, eidx, wts, rank, cnt = _merge(x2, S, o_ret, o_moba, cq, kmem, vmem, g_mix, w_in_bf[:, MIX_W:], b_gate,
                                          w_br_ret, w_br_moba, w_br_mem, w_out, g_ffn, w_router, b_router)
    T = MOE_T
    tcnt = cnt[:, :, 0]
    seg = ((tcnt + SEG_ALIGN - 1) // SEG_ALIGN) * SEG_ALIGN
    region_rows = jnp.sum(seg, axis=0)
    padded = ((region_rows + T - 1) // T) * T
    pad_ends = jnp.cumsum(padded).astype(I32)
    pad_starts = pad_ends - padded
    gstart = pad_starts[None, :] + jnp.cumsum(seg, axis=0) - seg
    lstart = jnp.cumsum(seg, axis=1) - seg
    lstart_tok = jnp.broadcast_to(lstart[:, None, :], (N // ROUTE_T, ROUTE_T, N_EXPERTS)).reshape(N, N_EXPERTS)
    onehot = eidx[:, :, None] == jnp.arange(N_EXPERTS, dtype=I32)[None, None, :]
    ld = jnp.sum(jnp.where(onehot, lstart_tok[None], 0), axis=-1) + rank
    NB = -(-(N // ROUTE_T * LOCAL_ROWS) // T) + N_EXPERTS
    seg_f, lstart_f, gstart_f = (a.reshape(-1).astype(I32) for a in (seg, lstart, gstart))
    totals = jnp.sum(seg, axis=1).astype(I32)

    xb = _dispatch(xt, ld, seg_f, lstart_f, gstart_f, totals, pad_ends, padded.astype(I32), NB * T)
    F2 = w_mlp1.shape[-1]
    b1p = b_mlp1.reshape(N_EXPERTS, F2 // (2 * LANES), LANES, 2).transpose(0, 1, 3, 2).reshape(N_EXPERTS, 1, F2)
    yb = _experts(xb, pad_starts.astype(I32), (padded // T).astype(I32), w_mlp1, b1p, w_mlp2, b_mlp2)
    out = _combine(ld.T, wts.T, x1, yb, seg_f, lstart_f, gstart_f, totals)
    return out.reshape(B, S, D)


def kernel(x, mem, g_mix, w_in, b_gate, g_ret_out, g_moba_q, g_moba_k, g_mem, w_mem_kv, g_mem_q, g_mem_k, w_br_ret, w_br_moba, w_br_mem, w_out, g_ffn, w_router, b_router, w_mlp1, b_mlp1, w_mlp2, b_mlp2):
    for l in range(g_mix.shape[0]):
        x = _layer(x, mem, g_mix[l], w_in[l], b_gate[l], g_ret_out[l], g_moba_q[l], g_moba_k[l], g_mem[l],
                   w_mem_kv[l], g_mem_q[l], g_mem_k[l], w_br_ret[l], w_br_moba[l], w_br_mem[l], w_out[l],
                   g_ffn[l], w_router[l], b_router[l], w_mlp1[l], b_mlp1[l], w_mlp2[l], b_mlp2[l])
    return x
```

```python
import functools

import jax
import jax.numpy as jnp
import numpy as np
from jax import lax
from jax.experimental import pallas as pl
from jax.experimental.pallas import tpu as pltpu

F32 = jnp.float32
BF16 = jnp.bfloat16
I32 = jnp.int32

EPS = 1e-5
NEG = -1e30

RET_HEADS = 4
RET_DK = 64
RET_DV = 128
RET_CHUNK = 128
MOBA_HEADS = 8
MOBA_HD = 64
MOBA_BLOCK = 256
MOBA_TOPK = 3
MEM_HEADS = 4
MEM_HD = 128
N_BRANCH = 3
N_EXPERTS = 32
TOP_K = 4
SWIGLU_LIMIT = 7.0
SWIGLU_ALPHA = 1.702

RET_Q = RET_HEADS * RET_DK
RET_V = RET_HEADS * RET_DV
MOBA_W = MOBA_HEADS * MOBA_HD
MEM_W = MEM_HEADS * MEM_HD
MIX_W = 2 * RET_Q + 2 * RET_V + 3 * MOBA_W + MEM_W

LANES = 128
MOBA_PAIRS = MOBA_HEADS // 2
MOBA_HEADS_PER_STEP = 8
BIAS_LANE0 = MOBA_HD
ONE_LANE = 80
VSUM_LANE = MOBA_HD

MOE_T = 256
TM_PROJ = 512
TM_MERGE = 512
TQ_RET = 512
ROUTE_T = 256
SEG_ALIGN = 8
LOCAL_ROWS = -(-(ROUTE_T * TOP_K + N_EXPERTS * (SEG_ALIGN - 1)) // LANES) * LANES
SEG_CHUNKS = tuple(2 ** p for p in range(ROUTE_T.bit_length() - 1, SEG_ALIGN.bit_length() - 2, -1))
TOTAL_CHUNKS = tuple(2 ** p for p in range(LOCAL_ROWS.bit_length() - 1, SEG_ALIGN.bit_length() - 2, -1))

_NT = (((1,), (1,)), ((), ()))
_TN = (((0,), (0,)), ((), ()))


def _rms(x, g):
    return x * lax.rsqrt(jnp.mean(x * x, axis=-1, keepdims=True) + EPS) * g


def _dot(a, b):
    return jnp.dot(a, b, preferred_element_type=F32)


def _cparams(sem, vmem_mb):
    return pltpu.CompilerParams(dimension_semantics=sem, vmem_limit_bytes=vmem_mb * 1024 * 1024)


def _memkv_kernel(mem_ref, g_ref, w_ref, gk_ref, k_ref, v_ref):
    m = _rms(mem_ref[0], g_ref[...]).astype(BF16)
    kv = _dot(m, w_ref[...])
    ks = [_rms(kv[:, h * MEM_HD:(h + 1) * MEM_HD], gk_ref[...]) for h in range(MEM_HEADS)]
    k_ref[0] = jnp.concatenate(ks, axis=-1).astype(BF16)
    v_ref[0] = kv[:, MEM_W:].astype(BF16)


def _memkv(mem, g_mem, w_mem_kv, g_mem_k):
    B, M, D = mem.shape
    return pl.pallas_call(
        _memkv_kernel,
        grid=(B,),
        in_specs=[
            pl.BlockSpec((1, M, D), lambda b: (b, 0, 0)),
            pl.BlockSpec((1, D), lambda b: (0, 0)),
            pl.BlockSpec((D, 2 * MEM_W), lambda b: (0, 0)),
            pl.BlockSpec((1, MEM_HD), lambda b: (0, 0)),
        ],
        out_specs=[
            pl.BlockSpec((1, M, MEM_W), lambda b: (b, 0, 0)),
            pl.BlockSpec((1, M, MEM_W), lambda b: (b, 0, 0)),
        ],
        out_shape=[jax.ShapeDtypeStruct((B, M, MEM_W), BF16)] * 2,
        compiler_params=_cparams(("arbitrary",), 32),
        name="memkv",
    )(mem, g_mem.reshape(1, D), w_mem_kv.astype(BF16), g_mem_k.reshape(1, MEM_HD))


def _head_pair_norm(a2, g2, lane):
    sq = a2 * a2
    lo = lane < MOBA_HD
    ss_lo = jnp.sum(jnp.where(lo, sq, 0.0), axis=-1, keepdims=True)
    ss_hi = jnp.sum(jnp.where(lo, 0.0, sq), axis=-1, keepdims=True)
    inv = jnp.where(lo, lax.rsqrt(ss_lo / MOBA_HD + EPS), lax.rsqrt(ss_hi / MOBA_HD + EPS))
    return a2 * inv * g2


def _inproj_kernel(slopes, seq_tiles, x_ref, gmix_ref, w_ref, gq_ref, gk_ref, gc_ref,
                   rq_ref, rk_ref, rv_ref, rg_ref, mq_ref, mk_ref, mv_ref, cq_ref, kmean_ref):
    tm = x_ref.shape[0]
    blocks_per_tile = tm // MOBA_BLOCK
    h = _rms(x_ref[...], gmix_ref[...]).astype(BF16)
    col = [0]

    def proj(width):
        a = _dot(h, w_ref[:, col[0]:col[0] + width])
        col[0] += width
        return a

    a = proj(2 * RET_Q)
    for hh in range(RET_HEADS):
        rq_ref[0, hh] = a[:, hh * RET_DK:(hh + 1) * RET_DK].astype(BF16)
        rk_ref[0, hh] = (a[:, RET_Q + hh * RET_DK:RET_Q + (hh + 1) * RET_DK] * (RET_DK ** -0.5)).astype(BF16)
    rv_ref[...] = proj(RET_V).astype(BF16)
    rg_ref[...] = proj(RET_V).astype(BF16)

    lane = lax.broadcasted_iota(I32, (tm, LANES), 1)
    row = lax.broadcasted_iota(I32, (tm, LANES), 0)
    lo = lane < MOBA_HD
    q_tail = jnp.where(lane == ONE_LANE, 1.0, 0.0)
    a = proj(MOBA_W)
    for p in range(MOBA_PAIRS):
        n2 = _head_pair_norm(a[:, p * LANES:(p + 1) * LANES], gq_ref[...], lane) * (MOBA_HD ** -0.5)
        mq_ref[0, 2 * p] = jnp.where(lo, n2, q_tail).astype(BF16)
        mq_ref[0, 2 * p + 1] = jnp.where(lo, pltpu.roll(n2, MOBA_HD, 1), q_tail).astype(BF16)
    blk = (pl.program_id(0) % seq_tiles) * blocks_per_tile + row // MOBA_BLOCK
    onehot_tail = jnp.where(lane == BIAS_LANE0 + blk, 1.0, 0.0)
    off = (row % MOBA_BLOCK).astype(F32)
    a = proj(MOBA_W)
    for p in range(MOBA_PAIRS):
        n2 = _head_pair_norm(a[:, p * LANES:(p + 1) * LANES], gk_ref[...], lane)
        for j in range(blocks_per_tile):
            kmean_ref[0, 0, p, j:j + 1, :] = jnp.mean(n2[j * MOBA_BLOCK:(j + 1) * MOBA_BLOCK], axis=0, keepdims=True)
        for s, src in ((0, n2), (1, pltpu.roll(n2, MOBA_HD, 1))):
            tail = jnp.where(lane == ONE_LANE, slopes[2 * p + s] * off, onehot_tail)
            mk_ref[0, 2 * p + s] = jnp.where(lo, src, tail).astype(BF16)
    v_tail = jnp.where(lane == VSUM_LANE, 1.0, 0.0)
    a = proj(MOBA_W)
    for p in range(MOBA_PAIRS):
        a2 = a[:, p * LANES:(p + 1) * LANES]
        mv_ref[0, 2 * p] = jnp.where(lo, a2, v_tail).astype(BF16)
        mv_ref[0, 2 * p + 1] = jnp.where(lo, pltpu.roll(a2, MOBA_HD, 1), v_tail).astype(BF16)
    a = proj(MEM_W)
    cq = [_rms(a[:, hh * MEM_HD:(hh + 1) * MEM_HD], gc_ref[...]) for hh in range(MEM_HEADS)]
    cq_ref[...] = jnp.concatenate(cq, axis=-1).astype(BF16)


def _inproj(x2, B, S, g_mix, w_mix, g_moba_q, g_moba_k, g_mem_q, slopes):
    N, D = x2.shape
    tm = TM_PROJ
    nS = S // tm
    bpt = tm // MOBA_BLOCK
    tok = lambda i: (i, 0)
    headmaj = lambda i: (i // nS, 0, i % nS, 0)
    g2 = lambda g: jnp.concatenate([g, g]).reshape(1, LANES)
    outs = pl.pallas_call(
        functools.partial(_inproj_kernel, slopes, nS),
        grid=(N // tm,),
        in_specs=[
            pl.BlockSpec((tm, D), tok),
            pl.BlockSpec((1, D), lambda i: (0, 0)),
            pl.BlockSpec((D, MIX_W), lambda i: (0, 0)),
            pl.BlockSpec((1, LANES), lambda i: (0, 0)),
            pl.BlockSpec((1, LANES), lambda i: (0, 0)),
            pl.BlockSpec((1, MEM_HD), lambda i: (0, 0)),
        ],
        out_specs=[
            pl.BlockSpec((1, RET_HEADS, tm, RET_DK), headmaj),
            pl.BlockSpec((1, RET_HEADS, tm, RET_DK), headmaj),
            pl.BlockSpec((tm, RET_V), tok),
            pl.BlockSpec((tm, RET_V), tok),
            pl.BlockSpec((1, MOBA_HEADS, tm, LANES), headmaj),
            pl.BlockSpec((1, MOBA_HEADS, tm, LANES), headmaj),
            pl.BlockSpec((1, MOBA_HEADS, tm, LANES), headmaj),
            pl.BlockSpec((tm, MEM_W), tok),
            pl.BlockSpec((1, 1, MOBA_PAIRS, bpt, LANES), lambda i: (i // nS, i % nS, 0, 0, 0)),
        ],
        out_shape=[
            jax.ShapeDtypeStruct((B, RET_HEADS, S, RET_DK), BF16),
            jax.ShapeDtypeStruct((B, RET_HEADS, S, RET_DK), BF16),
            jax.ShapeDtypeStruct((N, RET_V), BF16),
            jax.ShapeDtypeStruct((N, RET_V), BF16),
            jax.ShapeDtypeStruct((B, MOBA_HEADS, S, LANES), BF16),
            jax.ShapeDtypeStruct((B, MOBA_HEADS, S, LANES), BF16),
            jax.ShapeDtypeStruct((B, MOBA_HEADS, S, LANES), BF16),
            jax.ShapeDtypeStruct((N, MEM_W), BF16),
            jax.ShapeDtypeStruct((B, nS, MOBA_PAIRS, bpt, LANES), F32),
        ],
        compiler_params=_cparams(("arbitrary",), 56),
        name="inproj",
    )(x2, g_mix.reshape(1, D), w_mix, g2(g_moba_q), g2(g_moba_k), g_mem_q.reshape(1, MEM_HD))
    rq, rk, rv, rg, mq, mk, mv, cq, kmean_pairs = outs
    km = kmean_pairs.reshape(B, nS, MOBA_PAIRS, bpt, 2, MOBA_HD).transpose(0, 2, 4, 1, 3, 5)
    km = km.reshape(B, MOBA_HEADS, nS * bpt, MOBA_HD)
    km = jnp.pad(km, ((0, 0), (0, 0), (0, 0), (0, LANES - MOBA_HD)))
    return rq, rk, rv, rg, mq, mk, mv, cq, km


def _retention_kernel(q_ref, k_ref, v_ref, rg_ref, din_ref, dq_ref, dk_ref, dc_ref, g_ref, o_ref, state_ref):
    @pl.when(pl.program_id(2) == 0)
    def _():
        state_ref[...] = jnp.zeros_like(state_ref)

    C = RET_CHUNK
    for c in range(q_ref.shape[2] // C):
        rows = slice(c * C, (c + 1) * C)
        q = q_ref[0, 0, rows, :]
        k = k_ref[0, 0, rows, :]
        v = v_ref[rows, :]
        state = state_ref[...]
        scores = lax.dot_general(q, k, _NT, preferred_element_type=F32) * din_ref[0]
        intra = _dot(scores.astype(BF16), v)
        cross = _dot(q, state.astype(BF16)) * dq_ref[0]
        kd = (k.astype(F32) * dk_ref[0]).astype(BF16)
        state_ref[...] = dc_ref[0] * state + lax.dot_general(kd, v, _TN, preferred_element_type=F32)
        o = _rms(intra + cross, g_ref[...])
        o_ref[rows, :] = (o * jax.nn.silu(rg_ref[rows, :].astype(F32))).astype(BF16)


def _retention_decays():
    H, C = RET_HEADS, RET_CHUNK
    log_g = jnp.log1p(-jnp.exp2(-5.0 - jnp.arange(H, dtype=F32)))
    i = jnp.arange(C, dtype=F32)
    diff = i[:, None] - i[None, :]
    decay_in = jnp.where(diff >= 0, jnp.exp(jnp.maximum(diff, 0.0)[None] * log_g[:, None, None]), 0.0)
    decay_k = jnp.exp((C - 1 - i)[None, :] * log_g[:, None])
    decay_q = jnp.exp((i + 1)[None, :] * log_g[:, None])
    decay_chunk = jnp.exp(C * log_g)
    dq = jnp.broadcast_to(decay_q[:, :, None], (H, C, RET_DV))
    dk = jnp.broadcast_to(decay_k[:, :, None], (H, C, RET_DK))
    dc = jnp.broadcast_to(decay_chunk[:, None, None], (H, RET_DK, RET_DV))
    return decay_in, dq, dk, dc


def _retention(rq, rk, rv, rg, g_ret_out):
    B, H, S, dk = rq.shape
    tq = TQ_RET
    nT = S // tq
    C = RET_CHUNK
    din, dq, dk_, dc = _retention_decays()
    qk_spec = pl.BlockSpec((1, 1, tq, dk), lambda b, h, t: (b, h, t, 0))
    tok_spec = pl.BlockSpec((tq, RET_DV), lambda b, h, t: (b * nT + t, h))
    per_head = lambda r, c: pl.BlockSpec((1, r, c), lambda b, h, t: (h, 0, 0))
    return pl.pallas_call(
        _retention_kernel,
        grid=(B, H, nT),
        in_specs=[qk_spec, qk_spec, tok_spec, tok_spec,
                  per_head(C, C), per_head(C, RET_DV), per_head(C, RET_DK), per_head(RET_DK, RET_DV),
                  pl.BlockSpec((1, RET_DV), lambda b, h, t: (0, 0))],
        out_specs=tok_spec,
        out_shape=jax.ShapeDtypeStruct((B * S, RET_V), BF16),
        scratch_shapes=[pltpu.VMEM((RET_DK, RET_DV), F32)],
        compiler_params=_cparams(("arbitrary", "arbitrary", "arbitrary"), 32),
        name="retention",
    )(rq, rk, rv, rg, din, dq, dk_, dc, g_ret_out.reshape(1, RET_DV))


def _moba_kernel(slopes_ref, q_ref, k_ref, v_ref, km_ref, shift_ref, causal_ref, o_ref):
    NH = q_ref.shape[1]
    head0 = pl.program_id(1) * NH
    i = pl.program_id(2)
    BS = MOBA_BLOCK
    nb = km_ref.shape[2]
    heads = range(NH)
    n_iota = lax.broadcasted_iota(I32, (nb, BS), 0)
    past = n_iota < i
    qa = [q_ref[0, s] for s in heads]
    q_aug = []
    for s in heads:
        km = km_ref[0, s]
        hi = km.astype(BF16)
        mid = (km - hi.astype(F32)).astype(BF16)
        lo = (km - hi.astype(F32) - mid.astype(F32)).astype(BF16)
        g3 = lax.dot_general(jnp.concatenate([hi, mid, lo], axis=0), qa[s], _NT, preferred_element_type=F32)
        gate = (g3[:nb] + g3[nb:2 * nb]) + g3[2 * nb:]
        g = jnp.where(past, gate, -jnp.inf)
        rank = jnp.zeros((nb, BS), I32)
        for m in range(nb - 1):
            gm = g[m:m + 1, :]
            rank = rank + jnp.where(gm > g, 1, jnp.where(gm == g, jnp.where(m < n_iota, 1, 0), 0))
        sel = jnp.where(past, rank, MOBA_TOPK) < MOBA_TOPK
        bias = jnp.where(sel, (n_iota - i).astype(F32) * (slopes_ref[head0 + s] * BS), NEG).astype(BF16)
        placed = lax.dot_general(bias, shift_ref[...], _TN, preferred_element_type=F32)
        q_aug.append((qa[s].astype(F32) + placed).astype(BF16))

    own = pl.ds(pl.multiple_of(i * BS, BS), BS)
    m0, acc0 = [], []
    for s in heads:
        sc = lax.dot_general(qa[s], k_ref[0, s, own, :], _NT, preferred_element_type=F32) + causal_ref[...]
        m = jnp.max(sc, axis=-1, keepdims=True)
        m0.append(m)
        acc0.append(_dot(jnp.exp(sc - m).astype(BF16), v_ref[0, s, own, :]))

    def body(j, carry):
        ms, accs = carry
        blk = pl.ds(pl.multiple_of(j * BS, BS), BS)
        new_m, new_acc = [], []
        for s in heads:
            sj = lax.dot_general(q_aug[s], k_ref[0, s, blk, :], _NT, preferred_element_type=F32)
            m_new = jnp.maximum(ms[s], jnp.max(sj, axis=-1, keepdims=True))
            pj = jnp.exp(sj - m_new).astype(BF16)
            new_acc.append(jnp.exp(ms[s] - m_new) * accs[s] + _dot(pj, v_ref[0, s, blk, :]))
            new_m.append(m_new)
        return tuple(new_m), tuple(new_acc)

    _, accs = lax.fori_loop(0, i, body, (tuple(m0), tuple(acc0)))
    outs = [acc / acc[:, VSUM_LANE:VSUM_LANE + 1] for acc in accs]
    lane = lax.broadcasted_iota(I32, (BS, LANES), 1)
    pairs = [jnp.where(lane < MOBA_HD, outs[2 * p], pltpu.roll(outs[2 * p + 1], MOBA_HD, 1)) for p in range(NH // 2)]
    o_ref[0] = jnp.concatenate(pairs, axis=-1).astype(BF16)


def _moba(mq, mk, mv, km, slopes):
    B, H, S, _ = mq.shape
    BS = MOBA_BLOCK
    NH = MOBA_HEADS_PER_STEP
    nq = S // BS
    nb = km.shape[2]
    assert BIAS_LANE0 + nb <= ONE_LANE
    shift = (jnp.arange(nb)[:, None] + BIAS_LANE0 == jnp.arange(LANES)[None, :]).astype(BF16)
    causal = jnp.where(jnp.arange(BS)[:, None] >= jnp.arange(BS)[None, :], 0.0, NEG).astype(F32)
    per_head = lambda rows: pl.BlockSpec((1, NH, rows, LANES), lambda b, p, i, sl: (b, p, 0, 0))
    grid_spec = pltpu.PrefetchScalarGridSpec(
        num_scalar_prefetch=1,
        grid=(B, H // NH, nq),
        in_specs=[
            pl.BlockSpec((1, NH, BS, LANES), lambda b, p, i, sl: (b, p, i, 0)),
            per_head(S), per_head(S), per_head(nb),
            pl.BlockSpec((nb, LANES), lambda b, p, i, sl: (0, 0)),
            pl.BlockSpec((BS, BS), lambda b, p, i, sl: (0, 0)),
        ],
        out_specs=pl.BlockSpec((1, BS, NH * MOBA_HD), lambda b, p, i, sl: (b, i, p)),
    )
    return pl.pallas_call(
        _moba_kernel,
        grid_spec=grid_spec,
        out_shape=jax.ShapeDtypeStruct((B, S, MOBA_W), BF16),
        compiler_params=_cparams(("arbitrary", "arbitrary", "arbitrary"), 48),
        name="moba",
    )(slopes, mq, mk, mv, km, shift, causal)


def _merge_kernel(x_ref, oret_ref, omoba_ref, cq_ref, kmem_ref, vmem_ref, gmix_ref, wg_ref, bg_ref,
                  wbr_ref, wbm_ref, wbc_ref, wout_ref, gffn_ref, wr_ref, br_ref, tri_ref,
                  x1_ref, xt_ref, eidx_ref, wts_ref, rank_ref, cnt_ref):
    D = x_ref.shape[1]
    tm = x_ref.shape[0]

    x = x_ref[...]
    h = _rms(x, gmix_ref[...]).astype(BF16)
    cq = cq_ref[...]
    om = []
    for hh in range(MEM_HEADS):
        cols = slice(hh * MEM_HD, (hh + 1) * MEM_HD)
        sc = lax.dot_general(cq[:, cols], kmem_ref[0, :, cols], _NT, preferred_element_type=F32) * (MEM_HD ** -0.5)
        sc = sc - jnp.max(sc, axis=-1, keepdims=True)
        p = jnp.exp(sc)
        p = p / jnp.sum(p, axis=-1, keepdims=True)
        om.append(_dot(p.astype(BF16), vmem_ref[0, :, cols]))
    omem = jnp.concatenate(om, axis=-1).astype(BF16)
    y = None
    for br, (o, w_ref) in enumerate(((oret_ref[...], wbr_ref), (omoba_ref[...], wbm_ref), (omem, wbc_ref))):
        gl = _dot(h, wg_ref[:, br * D:(br + 1) * D]) + bg_ref[:, br * D:(br + 1) * D]
        term = jax.nn.sigmoid(gl) * _dot(o, w_ref[...])
        y = term if y is None else y + term
    x1 = x + _dot(y.astype(BF16), wout_ref[...])
    x1_ref[...] = x1
    xt = _rms(x1, gffn_ref[...])
    wr = wr_ref[...]
    E = wr.shape[0]
    w_hi = wr.astype(BF16)
    w_mid = (wr - w_hi.astype(F32)).astype(BF16)
    w_lo = (wr - w_hi.astype(F32) - w_mid.astype(F32)).astype(BF16)
    xt_hi = xt.astype(BF16)
    xt_ref[...] = xt_hi
    xt_lo = (xt - xt_hi.astype(F32)).astype(BF16)
    a = lax.dot_general(jnp.concatenate([w_hi, w_mid, w_lo], axis=0), xt_hi, _NT, preferred_element_type=F32)
    b = lax.dot_general(jnp.concatenate([w_hi, w_mid], axis=0), xt_lo, _NT, preferred_element_type=F32)
    logits = (a[:E] + (a[E:2 * E] + b[:E])) + (a[2 * E:] + b[E:]) + br_ref[...]
    e_iota = lax.broadcasted_iota(I32, (E, tm), 0)
    l = logits
    vals, hots = [], []
    for k in range(TOP_K):
        m = jnp.max(l, axis=0, keepdims=True)
        idx = jnp.min(jnp.where(l == m, e_iota, E), axis=0, keepdims=True)
        hot = e_iota == idx
        l = jnp.where(hot, -jnp.inf, l)
        vals.append(m)
        hots.append(hot)
        eidx_ref[k:k + 1, :] = idx
    ex = [jnp.exp(v - vals[0]) for v in vals]
    den = ex[0]
    for k in range(1, TOP_K):
        den = den + ex[k]
    chosen = jnp.zeros((E, tm), F32)
    for k in range(TOP_K):
        wts_ref[k:k + 1, :] = ex[k] / den
        chosen = chosen + jnp.where(hots[k], 1.0, 0.0)
    RT = tri_ref.shape[0]
    for t in range(tm // RT):
        cols = slice(t * RT, (t + 1) * RT)
        prefix = _dot(chosen[:, cols].astype(BF16), tri_ref[...])
        for k in range(TOP_K):
            rank_ref[k:k + 1, cols] = jnp.sum(jnp.where(hots[k][:, cols], prefix, 0.0), axis=0, keepdims=True).astype(I32)
        counts = jnp.sum(chosen[:, cols], axis=1, keepdims=True)
        cnt_ref[t] = jnp.broadcast_to(counts, cnt_ref.shape[1:]).astype(I32)


def _merge(x2, S, o_ret, o_moba, cq, kmem, vmem, g_mix, w_gate, b_gate, w_br_ret, w_br_moba, w_br_mem, w_out,
           g_ffn, w_router, b_router):
    N, D = x2.shape
    tm = TM_MERGE
    nS = S // tm
    M = kmem.shape[1]
    E = w_router.shape[1]
    tok = lambda w: pl.BlockSpec((tm, w), lambda i: (i, 0))
    const = lambda r, c: pl.BlockSpec((r, c), lambda i: (0, 0), pipeline_mode=pl.Buffered(1))
    mem_spec = pl.BlockSpec((1, M, MEM_W), lambda i: (i // nS, 0, 0))
    lanes_tok = lambda r, dt: (pl.BlockSpec((r, tm), lambda i: (0, i)), jax.ShapeDtypeStruct((r, N), dt))
    RT = ROUTE_T
    tri = (jnp.arange(RT)[:, None] < jnp.arange(RT)[None, :]).astype(BF16)
    e_spec, e_shape = lanes_tok(TOP_K, I32)
    w_spec, w_shape = lanes_tok(TOP_K, F32)
    r_spec, r_shape = lanes_tok(TOP_K, I32)
    return pl.pallas_call(
        _merge_kernel,
        grid=(N // tm,),
        in_specs=[tok(D), tok(RET_V), tok(MOBA_W), tok(MEM_W), mem_spec, mem_spec,
                  const(1, D), const(D, N_BRANCH * D), const(1, N_BRANCH * D),
                  const(RET_V, D), const(MOBA_W, D), const(MEM_W, D), const(D, D),
                  const(1, D), const(E, D), const(E, 1), const(RT, RT)],
        out_specs=[tok(D), tok(D), e_spec, w_spec, r_spec, pl.BlockSpec((tm // RT, E, LANES), lambda i: (i, 0, 0))],
        out_shape=[jax.ShapeDtypeStruct((N, D), F32), jax.ShapeDtypeStruct((N, D), BF16),
                   e_shape, w_shape, r_shape, jax.ShapeDtypeStruct((N // RT, E, LANES), I32)],
        compiler_params=_cparams(("arbitrary",), 56),
        name="merge",
    )(x2, o_ret, o_moba, cq, kmem, vmem, g_mix.reshape(1, D), w_gate, b_gate.reshape(1, N_BRANCH * D),
      w_br_ret.astype(BF16), w_br_moba.astype(BF16), w_br_mem.astype(BF16), w_out.astype(BF16),
      g_ffn.reshape(1, D), w_router.T, b_router.reshape(E, 1), tri)


def _segment_copies(seg_ref, lstart_ref, gstart_ref, tile, local_ref, slots_ref, sem, to_slots, fn):
    def per_expert(e, carry):
        idx = tile * N_EXPERTS + e
        size = seg_ref[idx]
        off = jnp.int32(0)
        for chunk in SEG_CHUNKS:
            take = (size & chunk) != 0
            lo = pl.ds(pl.multiple_of(lstart_ref[idx] + off, SEG_ALIGN), chunk)
            gl = pl.ds(pl.multiple_of(gstart_ref[idx] + off, SEG_ALIGN), chunk)
            src, dst = (local_ref.at[lo], slots_ref.at[gl]) if to_slots else (slots_ref.at[gl], local_ref.at[lo])

            @pl.when(take)
            def _():
                fn(pltpu.make_async_copy(src, dst, sem))

            off = off + jnp.where(take, chunk, 0)
        return carry

    lax.fori_loop(0, N_EXPERTS, per_expert, 0)


def _wait_tile(total_ref, tile, local_ref, slots_ref, sem, to_slots):
    total = total_ref[tile]
    for chunk in TOTAL_CHUNKS:
        lo, gl = local_ref.at[pl.ds(0, chunk)], slots_ref.at[pl.ds(0, chunk)]
        src, dst = (lo, gl) if to_slots else (gl, lo)

        @pl.when((total & chunk) != 0)
        def _():
            pltpu.make_async_copy(src, dst, sem).wait()


def _dispatch_kernel(seg_ref, lstart_ref, gstart_ref, total_ref, pad_end_ref, padded_ref, ld_ref, xt_ref, xb_ref,
                     ybuf_ref, zeros_ref, zsem, sems):
    T = zeros_ref.shape[0]
    i = pl.program_id(0)
    n = pl.num_programs(0)
    slot = i % 2

    @pl.when(i == 0)
    def _():
        zeros_ref[...] = jnp.zeros_like(zeros_ref)

        def fill(start):
            return pltpu.make_async_copy(zeros_ref, xb_ref.at[pl.ds(pl.multiple_of(start, T), T)], zsem)

        for e in range(N_EXPERTS):
            @pl.when(padded_ref[e] > 0)
            def _():
                fill(pad_end_ref[e] - T).start()
        for e in range(N_EXPERTS):
            @pl.when(padded_ref[e] > 0)
            def _():
                fill(pad_end_ref[e] - T).wait()
        first_unused = pad_end_ref[N_EXPERTS - 1] // T
        n_blocks = xb_ref.shape[0] // T
        lax.fori_loop(first_unused, n_blocks, lambda b, c: (fill(b * T).start(), c)[1], 0)
        lax.fori_loop(first_unused, n_blocks, lambda b, c: (fill(b * T).wait(), c)[1], 0)

    @pl.when(i >= 2)
    def _():
        _wait_tile(total_ref, i - 2, ybuf_ref.at[slot], xb_ref, sems.at[slot], True)

    L = ybuf_ref.shape[1]
    RT = xt_ref.shape[0]
    r_iota = lax.broadcasted_iota(I32, (L, RT), 0)
    ld = ld_ref[...]
    onehot = jnp.zeros((L, RT), F32)
    for k in range(TOP_K):
        onehot = jnp.where(r_iota == ld[k:k + 1, :], 1.0, onehot)
    onehot = onehot.astype(BF16)
    ybuf_ref[slot] = _dot(onehot, xt_ref[...])
    _segment_copies(seg_ref, lstart_ref, gstart_ref, i, ybuf_ref.at[slot], xb_ref, sems.at[slot], True,
                    lambda c: c.start())

    @pl.when(i == n - 1)
    def _():
        @pl.when(n >= 2)
        def _():
            _wait_tile(total_ref, i - 1, ybuf_ref.at[1 - slot], xb_ref, sems.at[1 - slot], True)

        _wait_tile(total_ref, i, ybuf_ref.at[slot], xb_ref, sems.at[slot], True)


def _dispatch(xt, ld, seg, lstart, gstart, totals, pad_ends, padded, R):
    N, D = xt.shape
    RT = ROUTE_T
    grid_spec = pltpu.PrefetchScalarGridSpec(
        num_scalar_prefetch=6,
        grid=(N // RT,),
        in_specs=[
            pl.BlockSpec((TOP_K, RT), lambda i, *_: (0, i)),
            pl.BlockSpec((RT, D), lambda i, *_: (i, 0)),
        ],
        out_specs=pl.BlockSpec(memory_space=pl.ANY),
        scratch_shapes=[pltpu.VMEM((2, LOCAL_ROWS, D), F32), pltpu.VMEM((MOE_T, D), F32),
                        pltpu.SemaphoreType.DMA(()), pltpu.SemaphoreType.DMA((2,))],
    )
    return pl.pallas_call(
        _dispatch_kernel,
        grid_spec=grid_spec,
        out_shape=jax.ShapeDtypeStruct((R, D), F32),
        compiler_params=_cparams(("arbitrary",), 48),
        name="dispatch",
    )(seg, lstart, gstart, totals, pad_ends, padded, ld, xt)


def _expert_kernel(start_ref, nblk_ref, b1_ref, b2_ref, perm_ref, w1_hbm, w2_hbm, xb_ref, yb_ref,
                   w1f_ref, w2f_ref, w1p_ref, w2b_ref, xbuf_ref, ybuf_ref, w_sems, in_sems, out_sems):
    e = pl.program_id(0)
    n_experts = pl.num_programs(0)
    ws = e % 2

    def w_copies(ex, slot):
        return (pltpu.make_async_copy(w1_hbm.at[ex], w1f_ref.at[slot], w_sems.at[slot, 0]),
                pltpu.make_async_copy(w2_hbm.at[ex], w2f_ref.at[slot], w_sems.at[slot, 1]))
    T = xbuf_ref.shape[1]
    G = 2 * LANES
    nb = nblk_ref[e]
    base = start_ref[e]

    def rows(b):
        return pl.ds(pl.multiple_of(base + b * T, T), T)

    def in_copy(b, slot):
        return pltpu.make_async_copy(xb_ref.at[rows(b)], xbuf_ref.at[slot], in_sems.at[slot])

    def out_copy(b, slot):
        return pltpu.make_async_copy(ybuf_ref.at[slot], yb_ref.at[rows(b)], out_sems.at[slot])

    @pl.when(nb > 0)
    def _():
        in_copy(0, 0).start()

    @pl.when(e == 0)
    def _():
        for c in w_copies(0, 0):
            c.start()

    for c in w_copies(e, ws):
        c.wait()

    @pl.when(e + 1 < n_experts)
    def _():
        for c in w_copies(e + 1, 1 - ws):
            c.start(priority=1)

    @pl.when(nb > 0)
    def _():
        for g in range(w1p_ref.shape[1] // G):
            w = w1f_ref[ws, :, g * G:(g + 1) * G].astype(BF16)
            w1p_ref[:, g * G:(g + 1) * G] = _dot(w, perm_ref[...]).astype(BF16)
        w2b_ref[...] = w2f_ref[ws].astype(BF16)

        def block(b, carry):
            slot = b % 2
            in_copy(b, slot).wait()

            @pl.when(b + 1 < nb)
            def _():
                in_copy(b + 1, 1 - slot).start()

            x = xbuf_ref[slot].astype(BF16)
            acts = []
            for g in range(w1p_ref.shape[1] // G):
                hg = _dot(x, w1p_ref[:, g * G:(g + 1) * G]) + b1_ref[0, :, g * G:(g + 1) * G]
                glu = jnp.minimum(hg[:, :LANES], SWIGLU_LIMIT)
                lin = jnp.clip(hg[:, LANES:], -SWIGLU_LIMIT, SWIGLU_LIMIT)
                acts.append((glu * jax.nn.sigmoid(SWIGLU_ALPHA * glu) * (lin + 1.0)).astype(BF16))
            y = _dot(jnp.concatenate(acts, axis=-1), w2b_ref[...]) + b2_ref[0]

            @pl.when(b >= 2)
            def _():
                out_copy(b - 2, slot).wait()

            ybuf_ref[slot] = y
            out_copy(b, slot).start()
            return carry

        lax.fori_loop(0, nb, block, 0)

        @pl.when(nb >= 2)
        def _():
            out_copy(nb - 2, nb % 2).wait()

        out_copy(nb - 1, (nb - 1) % 2).wait()

    @pl.when(e == pl.num_programs(0) - 1)
    def _():
        ybuf_ref[0] = jnp.zeros(ybuf_ref.shape[1:], ybuf_ref.dtype)
        first_unused = base // T + nb
        n_blocks = yb_ref.shape[0] // T

        def tail(b):
            return pltpu.make_async_copy(ybuf_ref.at[0], yb_ref.at[pl.ds(pl.multiple_of(b * T, T), T)], out_sems.at[0])

        lax.fori_loop(first_unused, n_blocks, lambda b, c: (tail(b).start(), c)[1], 0)
        lax.fori_loop(first_unused, n_blocks, lambda b, c: (tail(b).wait(), c)[1], 0)


def _experts(xb, region_start, region_blocks, w1, b1p, w2, b2):
    R, D = xb.shape
    E, _, F2 = w1.shape
    F = F2 // 2
    T = MOE_T
    G = 2 * LANES
    c = np.arange(G)
    src = np.where(c < LANES, 2 * c, 2 * (c - LANES) + 1)
    perm = jnp.asarray(np.arange(G)[:, None] == src[None, :], dtype=BF16)
    per_expert = lambda r, w: pl.BlockSpec((1, r, w), lambda e, *_: (e, 0, 0))
    grid_spec = pltpu.PrefetchScalarGridSpec(
        num_scalar_prefetch=2,
        grid=(E,),
        in_specs=[
            per_expert(1, F2), per_expert(1, D),
            pl.BlockSpec((G, G), lambda e, *_: (0, 0)),
            pl.BlockSpec(memory_space=pl.ANY), pl.BlockSpec(memory_space=pl.ANY), pl.BlockSpec(memory_space=pl.ANY),
        ],
        out_specs=pl.BlockSpec(memory_space=pl.ANY),
        scratch_shapes=[pltpu.VMEM((2, D, F2), F32), pltpu.VMEM((2, F, D), F32),
                        pltpu.VMEM((D, F2), BF16), pltpu.VMEM((F, D), BF16),
                        pltpu.VMEM((2, T, D), F32), pltpu.VMEM((2, T, D), F32),
                        pltpu.SemaphoreType.DMA((2, 2)), pltpu.SemaphoreType.DMA((2,)), pltpu.SemaphoreType.DMA((2,))],
    )
    return pl.pallas_call(
        _expert_kernel,
        grid_spec=grid_spec,
        out_shape=jax.ShapeDtypeStruct((R, D), F32),
        compiler_params=_cparams(("arbitrary",), 56),
        name="experts",
    )(region_start, region_blocks, b1p, b2.reshape(E, 1, D), perm, w1, w2, xb)


def _combine_kernel(seg_ref, lstart_ref, gstart_ref, total_ref, ldt_ref, wt_ref, x1_ref, yb_ref, o_ref, ybuf_ref,
                    sems):
    i = pl.program_id(0)
    n = pl.num_programs(0)
    slot = i % 2

    def fetch(tile, s):
        _segment_copies(seg_ref, lstart_ref, gstart_ref, tile, ybuf_ref.at[s], yb_ref, sems.at[s], False,
                        lambda c: c.start())

    @pl.when(i == 0)
    def _():
        ybuf_ref[...] = jnp.zeros_like(ybuf_ref)
        fetch(i, slot)

    @pl.when(i + 1 < n)
    def _():
        fetch(i + 1, 1 - slot)

    _wait_tile(total_ref, i, ybuf_ref.at[slot], yb_ref, sems.at[slot], False)

    L = ybuf_ref.shape[1]
    RT = x1_ref.shape[0]
    c_iota = lax.broadcasted_iota(I32, (RT, L), 1)
    ldt = ldt_ref[...]
    w = wt_ref[...]
    w_hi = w.astype(BF16).astype(F32)
    w_lo = w - w_hi
    g_hi = jnp.zeros((RT, L), F32)
    g_lo = jnp.zeros((RT, L), F32)
    for k in range(TOP_K):
        hit = c_iota == ldt[:, k:k + 1]
        g_hi = jnp.where(hit, w_hi[:, k:k + 1], g_hi)
        g_lo = jnp.where(hit, w_lo[:, k:k + 1], g_lo)
    y = ybuf_ref[slot].astype(BF16)
    o_ref[...] = x1_ref[...] + (_dot(g_hi.astype(BF16), y) + _dot(g_lo.astype(BF16), y))


def _combine(ldt, wts_t, x1, yb, seg, lstart, gstart, totals):
    N, D = x1.shape
    RT = ROUTE_T
    grid_spec = pltpu.PrefetchScalarGridSpec(
        num_scalar_prefetch=4,
        grid=(N // RT,),
        in_specs=[
            pl.BlockSpec((RT, TOP_K), lambda i, *_: (i, 0)),
            pl.BlockSpec((RT, TOP_K), lambda i, *_: (i, 0)),
            pl.BlockSpec((RT, D), lambda i, *_: (i, 0)),
            pl.BlockSpec(memory_space=pl.ANY),
        ],
        out_specs=pl.BlockSpec((RT, D), lambda i, *_: (i, 0)),
        scratch_shapes=[pltpu.VMEM((2, LOCAL_ROWS, D), F32), pltpu.SemaphoreType.DMA((2,))],
    )
    return pl.pallas_call(
        _combine_kernel,
        grid_spec=grid_spec,
        out_shape=jax.ShapeDtypeStruct((N, D), F32),
        compiler_params=_cparams(("arbitrary",), 48),
        name="combine",
    )(seg, lstart, gstart, totals, ldt, wts_t, x1, yb)


def _layer(x, mem, g_mix, w_in, b_gate, g_ret_out, g_moba_q, g_moba_k, g_mem, w_mem_kv, g_mem_q, g_mem_k,
           w_br_ret, w_br_moba, w_br_mem, w_out, g_ffn, w_router, b_router, w_mlp1, b_mlp1, w_mlp2, b_mlp2):
    B, S, D = x.shape
    N = B * S
    x2 = x.reshape(N, D)
    slopes_np = np.exp2(-8.0 * (np.arange(MOBA_HEADS, dtype=np.float64) + 1.0) / MOBA_HEADS)
    assert all(float(np.log2(s)).is_integer() for s in slopes_np)
    slopes = tuple(float(s) for s in slopes_np)

    w_in_bf = w_in.astype(BF16)
    kmem, vmem = _memkv(mem, g_mem, w_mem_kv, g_mem_k)
    rq, rk, rv, rg, mq, mk, mv, cq, km = _inproj(x2, B, S, g_mix, w_in_bf[:, :MIX_W], g_moba_q, g_moba_k, g_mem_q,
                                                 slopes)
    o_ret = _retention(rq, rk, rv, rg, g_ret_out)
    o_moba = _moba(mq, mk, mv, km, jnp.asarray(slopes, F32)).reshape(N, MOBA_W)
    x1, xt, eidx, wts, rank, cnt = _merge(x2, S, o_ret, o_moba, cq, kmem, vmem, g_mix, w_in_bf[:, MIX_W:], b_gate,
                                          w_br_ret, w_br_moba, w_br_mem, w_out, g_ffn, w_router, b_router)
    T = MOE_T
    tcnt = cnt[:, :, 0]
    seg = ((tcnt + SEG_ALIGN - 1) // SEG_ALIGN) * SEG_ALIGN
    region_rows = jnp.sum(seg, axis=0)
    padded = ((region_rows + T - 1) // T) * T
    pad_ends = jnp.cumsum(padded).astype(I32)
    pad_starts = pad_ends - padded
    gstart = pad_starts[None, :] + jnp.cumsum(seg, axis=0) - seg
    lstart = jnp.cumsum(seg, axis=1) - seg
    lstart_tok = jnp.broadcast_to(lstart[:, None, :], (N // ROUTE_T, ROUTE_T, N_EXPERTS)).reshape(N, N_EXPERTS)
    onehot = eidx[:, :, None] == jnp.arange(N_EXPERTS, dtype=I32)[None, None, :]
    ld = jnp.sum(jnp.where(onehot, lstart_tok[None], 0), axis=-1) + rank
    NB = -(-(N // ROUTE_T * LOCAL_ROWS) // T) + N_EXPERTS
    seg_f, lstart_f, gstart_f = (a.reshape(-1).astype(I32) for a in (seg, lstart, gstart))
    totals = jnp.sum(seg, axis=1).astype(I32)

    xb = _dispatch(xt, ld, seg_f, lstart_f, gstart_f, totals, pad_ends, padded.astype(I32), NB * T)
    F2 = w_mlp1.shape[-1]
    b1p = b_mlp1.reshape(N_EXPERTS, F2 // (2 * LANES), LANES, 2).transpose(0, 1, 3, 2).reshape(N_EXPERTS, 1, F2)
    yb = _experts(xb, pad_starts.astype(I32), (padded // T).astype(I32), w_mlp1, b1p, w_mlp2, b_mlp2)
    out = _combine(ld.T, wts.T, x1, yb, seg_f, lstart_f, gstart_f, totals)
    return out.reshape(B, S, D)


def kernel(x, mem, g_mix, w_in, b_gate, g_ret_out, g_moba_q, g_moba_k, g_mem, w_mem_kv, g_mem_q, g_mem_k, w_br_ret, w_br_moba, w_br_mem, w_out, g_ffn, w_router, b_router, w_mlp1, b_mlp1, w_mlp2, b_mlp2):
    for l in range(g_mix.shape[0]):
        x = _layer(x, mem, g_mix[l], w_in[l], b_gate[l], g_ret_out[l], g_moba_q[l], g_moba_k[l], g_mem[l],
                   w_mem_kv[l], g_mem_q[l], g_mem_k[l], w_br_ret[l], w_br_moba[l], w_br_mem[l], w_out[l],
                   g_ffn[l], w_router[l], b_router[l], w_mlp1[l], b_mlp1[l], w_mlp2[l], b_mlp2[l])
    return x
```

```python
import functools

import jax
import jax.numpy as jnp
import numpy as np
from jax import lax
from jax.experimental import pallas as pl
from jax.experimental.pallas import tpu as pltpu

F32 = jnp.float32
BF16 = jnp.bfloat16
I32 = jnp.int32

EPS = 1e-5
NEG = -1e30

RET_HEADS = 4
RET_DK = 64
RET_DV = 128
RET_CHUNK = 128
MOBA_HEADS = 8
MOBA_HD = 64
MOBA_BLOCK = 256
MOBA_TOPK = 3
MEM_HEADS = 4
MEM_HD = 128
N_BRANCH = 3
N_EXPERTS = 32
TOP_K = 4
SWIGLU_LIMIT = 7.0
SWIGLU_ALPHA = 1.702

RET_Q = RET_HEADS * RET_DK
RET_V = RET_HEADS * RET_DV
MOBA_W = MOBA_HEADS * MOBA_HD
MEM_W = MEM_HEADS * MEM_HD
MIX_W = 2 * RET_Q + 2 * RET_V + 3 * MOBA_W + MEM_W

LANES = 128
MOBA_PAIRS = MOBA_HEADS // 2
MOBA_HEADS_PER_STEP = 8
BIAS_LANE0 = MOBA_HD
ONE_LANE = 80
VSUM_LANE = MOBA_HD

MOE_T = 256
IN_DEPTH = 3
TM_PROJ = 512
TM_MERGE = 512
TQ_RET = 512
ROUTE_T = 256
SEG_ALIGN = 8
LOCAL_ROWS = -(-(ROUTE_T * TOP_K + N_EXPERTS * (SEG_ALIGN - 1)) // LANES) * LANES
SEG_CHUNKS = tuple(2 ** p for p in range(ROUTE_T.bit_length() - 1, SEG_ALIGN.bit_length() - 2, -1))
TOTAL_CHUNKS = tuple(2 ** p for p in range(LOCAL_ROWS.bit_length() - 1, SEG_ALIGN.bit_length() - 2, -1))

_NT = (((1,), (1,)), ((), ()))
_TN = (((0,), (0,)), ((), ()))


def _rms(x, g):
    return x * lax.rsqrt(jnp.mean(x * x, axis=-1, keepdims=True) + EPS) * g


def _dot(a, b):
    return jnp.dot(a, b, preferred_element_type=F32)


def _cparams(sem, vmem_mb):
    return pltpu.CompilerParams(dimension_semantics=sem, vmem_limit_bytes=vmem_mb * 1024 * 1024)


def _memkv_kernel(mem_ref, g_ref, w_ref, gk_ref, k_ref, v_ref):
    m = _rms(mem_ref[0], g_ref[...]).astype(BF16)
    kv = _dot(m, w_ref[...])
    ks = [_rms(kv[:, h * MEM_HD:(h + 1) * MEM_HD], gk_ref[...]) for h in range(MEM_HEADS)]
    k_ref[0] = jnp.concatenate(ks, axis=-1).astype(BF16)
    v_ref[0] = kv[:, MEM_W:].astype(BF16)


def _memkv(mem, g_mem, w_mem_kv, g_mem_k):
    B, M, D = mem.shape
    return pl.pallas_call(
        _memkv_kernel,
        grid=(B,),
        in_specs=[
            pl.BlockSpec((1, M, D), lambda b: (b, 0, 0)),
            pl.BlockSpec((1, D), lambda b: (0, 0)),
            pl.BlockSpec((D, 2 * MEM_W), lambda b: (0, 0)),
            pl.BlockSpec((1, MEM_HD), lambda b: (0, 0)),
        ],
        out_specs=[
            pl.BlockSpec((1, M, MEM_W), lambda b: (b, 0, 0)),
            pl.BlockSpec((1, M, MEM_W), lambda b: (b, 0, 0)),
        ],
        out_shape=[jax.ShapeDtypeStruct((B, M, MEM_W), BF16)] * 2,
        compiler_params=_cparams(("arbitrary",), 32),
        name="memkv",
    )(mem, g_mem.reshape(1, D), w_mem_kv.astype(BF16), g_mem_k.reshape(1, MEM_HD))


def _head_pair_norm(a2, g2, lane):
    sq = a2 * a2
    lo = lane < MOBA_HD
    ss_lo = jnp.sum(jnp.where(lo, sq, 0.0), axis=-1, keepdims=True)
    ss_hi = jnp.sum(jnp.where(lo, 0.0, sq), axis=-1, keepdims=True)
    inv = jnp.where(lo, lax.rsqrt(ss_lo / MOBA_HD + EPS), lax.rsqrt(ss_hi / MOBA_HD + EPS))
    return a2 * inv * g2


def _inproj_kernel(slopes, seq_tiles, x_ref, gmix_ref, w_ref, gq_ref, gk_ref, gc_ref,
                   rq_ref, rk_ref, rv_ref, rg_ref, mq_ref, mk_ref, mv_ref, cq_ref, kmean_ref):
    tm = x_ref.shape[0]
    blocks_per_tile = tm // MOBA_BLOCK
    h = _rms(x_ref[...], gmix_ref[...]).astype(BF16)
    col = [0]

    def proj(width):
        a = _dot(h, w_ref[:, col[0]:col[0] + width])
        col[0] += width
        return a

    a = proj(2 * RET_Q)
    for hh in range(RET_HEADS):
        rq_ref[0, hh] = a[:, hh * RET_DK:(hh + 1) * RET_DK].astype(BF16)
        rk_ref[0, hh] = (a[:, RET_Q + hh * RET_DK:RET_Q + (hh + 1) * RET_DK] * (RET_DK ** -0.5)).astype(BF16)
    rv_ref[...] = proj(RET_V).astype(BF16)
    rg_ref[...] = proj(RET_V).astype(BF16)

    lane = lax.broadcasted_iota(I32, (tm, LANES), 1)
    row = lax.broadcasted_iota(I32, (tm, LANES), 0)
    lo = lane < MOBA_HD
    q_tail = jnp.where(lane == ONE_LANE, 1.0, 0.0)
    a = proj(MOBA_W)
    for p in range(MOBA_PAIRS):
        n2 = _head_pair_norm(a[:, p * LANES:(p + 1) * LANES], gq_ref[...], lane) * (MOBA_HD ** -0.5)
        mq_ref[0, 2 * p] = jnp.where(lo, n2, q_tail).astype(BF16)
        mq_ref[0, 2 * p + 1] = jnp.where(lo, pltpu.roll(n2, MOBA_HD, 1), q_tail).astype(BF16)
    blk = (pl.program_id(0) % seq_tiles) * blocks_per_tile + row // MOBA_BLOCK
    onehot_tail = jnp.where(lane == BIAS_LANE0 + blk, 1.0, 0.0)
    off = (row % MOBA_BLOCK).astype(F32)
    a = proj(MOBA_W)
    for p in range(MOBA_PAIRS):
        n2 = _head_pair_norm(a[:, p * LANES:(p + 1) * LANES], gk_ref[...], lane)
        for j in range(blocks_per_tile):
            kmean_ref[0, 0, p, j:j + 1, :] = jnp.mean(n2[j * MOBA_BLOCK:(j + 1) * MOBA_BLOCK], axis=0, keepdims=True)
        for s, src in ((0, n2), (1, pltpu.roll(n2, MOBA_HD, 1))):
            tail = jnp.where(lane == ONE_LANE, slopes[2 * p + s] * off, onehot_tail)
            mk_ref[0, 2 * p + s] = jnp.where(lo, src, tail).astype(BF16)
    v_tail = jnp.where(lane == VSUM_LANE, 1.0, 0.0)
    a = proj(MOBA_W)
    for p in range(MOBA_PAIRS):
        a2 = a[:, p * LANES:(p + 1) * LANES]
        mv_ref[0, 2 * p] = jnp.where(lo, a2, v_tail).astype(BF16)
        mv_ref[0, 2 * p + 1] = jnp.where(lo, pltpu.roll(a2, MOBA_HD, 1), v_tail).astype(BF16)
    a = proj(MEM_W)
    cq = [_rms(a[:, hh * MEM_HD:(hh + 1) * MEM_HD], gc_ref[...]) for hh in range(MEM_HEADS)]
    cq_ref[...] = jnp.concatenate(cq, axis=-1).astype(BF16)


def _inproj(x2, B, S, g_mix, w_mix, g_moba_q, g_moba_k, g_mem_q, slopes):
    N, D = x2.shape
    tm = TM_PROJ
    nS = S // tm
    bpt = tm // MOBA_BLOCK
    tok = lambda i: (i, 0)
    headmaj = lambda i: (i // nS, 0, i % nS, 0)
    g2 = lambda g: jnp.concatenate([g, g]).reshape(1, LANES)
    outs = pl.pallas_call(
        functools.partial(_inproj_kernel, slopes, nS),
        grid=(N // tm,),
        in_specs=[
            pl.BlockSpec((tm, D), tok),
            pl.BlockSpec((1, D), lambda i: (0, 0)),
            pl.BlockSpec((D, MIX_W), lambda i: (0, 0)),
            pl.BlockSpec((1, LANES), lambda i: (0, 0)),
            pl.BlockSpec((1, LANES), lambda i: (0, 0)),
            pl.BlockSpec((1, MEM_HD), lambda i: (0, 0)),
        ],
        out_specs=[
            pl.BlockSpec((1, RET_HEADS, tm, RET_DK), headmaj),
            pl.BlockSpec((1, RET_HEADS, tm, RET_DK), headmaj),
            pl.BlockSpec((tm, RET_V), tok),
            pl.BlockSpec((tm, RET_V), tok),
            pl.BlockSpec((1, MOBA_HEADS, tm, LANES), headmaj),
            pl.BlockSpec((1, MOBA_HEADS, tm, LANES), headmaj),
            pl.BlockSpec((1, MOBA_HEADS, tm, LANES), headmaj),
            pl.BlockSpec((tm, MEM_W), tok),
            pl.BlockSpec((1, 1, MOBA_PAIRS, bpt, LANES), lambda i: (i // nS, i % nS, 0, 0, 0)),
        ],
        out_shape=[
            jax.ShapeDtypeStruct((B, RET_HEADS, S, RET_DK), BF16),
            jax.ShapeDtypeStruct((B, RET_HEADS, S, RET_DK), BF16),
            jax.ShapeDtypeStruct((N, RET_V), BF16),
            jax.ShapeDtypeStruct((N, RET_V), BF16),
            jax.ShapeDtypeStruct((B, MOBA_HEADS, S, LANES), BF16),
            jax.ShapeDtypeStruct((B, MOBA_HEADS, S, LANES), BF16),
            jax.ShapeDtypeStruct((B, MOBA_HEADS, S, LANES), BF16),
            jax.ShapeDtypeStruct((N, MEM_W), BF16),
            jax.ShapeDtypeStruct((B, nS, MOBA_PAIRS, bpt, LANES), F32),
        ],
        compiler_params=_cparams(("arbitrary",), 56),
        name="inproj",
    )(x2, g_mix.reshape(1, D), w_mix, g2(g_moba_q), g2(g_moba_k), g_mem_q.reshape(1, MEM_HD))
    rq, rk, rv, rg, mq, mk, mv, cq, kmean_pairs = outs
    km = kmean_pairs.reshape(B, nS, MOBA_PAIRS, bpt, 2, MOBA_HD).transpose(0, 2, 4, 1, 3, 5)
    km = km.reshape(B, MOBA_HEADS, nS * bpt, MOBA_HD)
    km = jnp.pad(km, ((0, 0), (0, 0), (0, 0), (0, LANES - MOBA_HD)))
    return rq, rk, rv, rg, mq, mk, mv, cq, km


def _retention_kernel(q_ref, k_ref, v_ref, rg_ref, din_ref, dq_ref, dk_ref, dc_ref, g_ref, o_ref, state_ref):
    @pl.when(pl.program_id(1) == 0)
    def _():
        state_ref[...] = jnp.zeros_like(state_ref)

    C = RET_CHUNK
    for h in range(q_ref.shape[1]):
        cols = slice(h * RET_DV, (h + 1) * RET_DV)
        state = state_ref[h]
        for c in range(q_ref.shape[2] // C):
            rows = slice(c * C, (c + 1) * C)
            q = q_ref[0, h, rows, :]
            k = k_ref[0, h, rows, :]
            v = v_ref[rows, cols]
            scores = lax.dot_general(q, k, _NT, preferred_element_type=F32) * din_ref[h]
            intra = _dot(scores.astype(BF16), v)
            cross = _dot(q, state.astype(BF16)) * dq_ref[h]
            kd = (k.astype(F32) * dk_ref[h]).astype(BF16)
            state = dc_ref[h] * state + lax.dot_general(kd, v, _TN, preferred_element_type=F32)
            o = _rms(intra + cross, g_ref[...])
            o_ref[rows, cols] = (o * jax.nn.silu(rg_ref[rows, cols].astype(F32))).astype(BF16)
        state_ref[h] = state


def _retention_decays():
    H, C = RET_HEADS, RET_CHUNK
    log_g = jnp.log1p(-jnp.exp2(-5.0 - jnp.arange(H, dtype=F32)))
    i = jnp.arange(C, dtype=F32)
    diff = i[:, None] - i[None, :]
    decay_in = jnp.where(diff >= 0, jnp.exp(jnp.maximum(diff, 0.0)[None] * log_g[:, None, None]), 0.0)
    decay_k = jnp.exp((C - 1 - i)[None, :] * log_g[:, None])
    decay_q = jnp.exp((i + 1)[None, :] * log_g[:, None])
    decay_chunk = jnp.exp(C * log_g)
    dq = jnp.broadcast_to(decay_q[:, :, None], (H, C, RET_DV))
    dk = jnp.broadcast_to(decay_k[:, :, None], (H, C, RET_DK))
    dc = jnp.broadcast_to(decay_chunk[:, None, None], (H, RET_DK, RET_DV))
    return decay_in, dq, dk, dc


def _retention(rq, rk, rv, rg, g_ret_out):
    B, H, S, dk = rq.shape
    tq = TQ_RET
    nT = S // tq
    C = RET_CHUNK
    din, dq, dk_, dc = _retention_decays()
    qk_spec = pl.BlockSpec((1, H, tq, dk), lambda b, t: (b, 0, t, 0))
    tok_spec = pl.BlockSpec((tq, RET_V), lambda b, t: (b * nT + t, 0))
    const = lambda r, c: pl.BlockSpec((H, r, c), lambda b, t: (0, 0, 0))
    return pl.pallas_call(
        _retention_kernel,
        grid=(B, nT),
        in_specs=[qk_spec, qk_spec, tok_spec, tok_spec,
                  const(C, C), const(C, RET_DV), const(C, RET_DK), const(RET_DK, RET_DV),
                  pl.BlockSpec((1, RET_DV), lambda b, t: (0, 0))],
        out_specs=tok_spec,
        out_shape=jax.ShapeDtypeStruct((B * S, RET_V), BF16),
        scratch_shapes=[pltpu.VMEM((H, RET_DK, RET_DV), F32)],
        compiler_params=_cparams(("arbitrary", "arbitrary"), 32),
        name="retention",
    )(rq, rk, rv, rg, din, dq, dk_, dc, g_ret_out.reshape(1, RET_DV))


def _moba_kernel(slopes_ref, q_ref, k_ref, v_ref, km_ref, shift_ref, mask_ref, o_ref):
    NH = q_ref.shape[1]
    head0 = pl.program_id(1) * NH
    i0 = pl.program_id(2) * 2
    BS = MOBA_BLOCK
    nb = km_ref.shape[2]
    heads = range(NH)
    n_iota = lax.broadcasted_iota(I32, (nb, 2 * BS), 0)
    q_blk = i0 + lax.broadcasted_iota(I32, (nb, 2 * BS), 1) // BS
    past = n_iota < q_blk
    qa = [q_ref[0, s] for s in heads]
    q_aug = []
    for s in heads:
        km = km_ref[0, s]
        hi = km.astype(BF16)
        mid = (km - hi.astype(F32)).astype(BF16)
        lo = (km - hi.astype(F32) - mid.astype(F32)).astype(BF16)
        g3 = lax.dot_general(jnp.concatenate([hi, mid, lo], axis=0), qa[s], _NT, preferred_element_type=F32)
        gate = (g3[:nb] + g3[nb:2 * nb]) + g3[2 * nb:]
        g = jnp.where(past, gate, -jnp.inf)
        rank = jnp.zeros((nb, 2 * BS), I32)
        for m in range(nb - 1):
            gm = g[m:m + 1, :]
            rank = rank + jnp.where(gm > g, 1, jnp.where(gm == g, jnp.where(m < n_iota, 1, 0), 0))
        sel = jnp.where(past, rank, MOBA_TOPK) < MOBA_TOPK
        bias = jnp.where(sel, (n_iota - q_blk).astype(F32) * (slopes_ref[head0 + s] * BS), NEG).astype(BF16)
        placed = lax.dot_general(bias, shift_ref[...], _TN, preferred_element_type=F32)
        q_aug.append((qa[s].astype(F32) + placed).astype(BF16))

    def rows_of(j):
        return pl.ds(pl.multiple_of(j * BS, BS), BS)

    m_lo, m_hi, acc_lo, acc_hi = [], [], [], []
    for s in heads:
        q_mix = jnp.concatenate([qa[s][:BS], q_aug[s][BS:]], axis=0)
        sc = lax.dot_general(q_mix, k_ref[0, s, rows_of(i0), :], _NT, preferred_element_type=F32) + mask_ref[...]
        m = jnp.max(sc, axis=-1, keepdims=True)
        acc = _dot(jnp.exp(sc - m).astype(BF16), v_ref[0, s, rows_of(i0), :])
        m_lo.append(m[:BS])
        acc_lo.append(acc[:BS])
        sc1 = lax.dot_general(qa[s][BS:], k_ref[0, s, rows_of(i0 + 1), :], _NT, preferred_element_type=F32)
        sc1 = sc1 + mask_ref[:BS, :]
        m1 = jnp.maximum(m[BS:], jnp.max(sc1, axis=-1, keepdims=True))
        p1 = jnp.exp(sc1 - m1).astype(BF16)
        m_hi.append(m1)
        acc_hi.append(jnp.exp(m[BS:] - m1) * acc[BS:] + _dot(p1, v_ref[0, s, rows_of(i0 + 1), :]))

    def body(j, carry):
        ms_lo, ms_hi, accs_lo, accs_hi = carry
        out = ([], [], [], [])
        for s in heads:
            sj = lax.dot_general(q_aug[s], k_ref[0, s, rows_of(j), :], _NT, preferred_element_type=F32)
            ps, alphas, ms = [], [], []
            for half, m_prev in ((slice(0, BS), ms_lo[s]), (slice(BS, 2 * BS), ms_hi[s])):
                m_new = jnp.maximum(m_prev, jnp.max(sj[half], axis=-1, keepdims=True))
                ps.append(jnp.exp(sj[half] - m_new).astype(BF16))
                alphas.append(jnp.exp(m_prev - m_new))
                ms.append(m_new)
            pv = _dot(jnp.concatenate(ps, axis=0), v_ref[0, s, rows_of(j), :])
            out[0].append(ms[0])
            out[1].append(ms[1])
            out[2].append(alphas[0] * accs_lo[s] + pv[:BS])
            out[3].append(alphas[1] * accs_hi[s] + pv[BS:])
        return tuple(tuple(o) for o in out)

    _, _, accs_lo, accs_hi = lax.fori_loop(0, i0, body, (tuple(m_lo), tuple(m_hi), tuple(acc_lo), tuple(acc_hi)))
    lane = lax.broadcasted_iota(I32, (BS, LANES), 1)
    for t, accs in enumerate((accs_lo, accs_hi)):
        outs = [acc / acc[:, VSUM_LANE:VSUM_LANE + 1] for acc in accs]
        pairs = [jnp.where(lane < MOBA_HD, outs[2 * p], pltpu.roll(outs[2 * p + 1], MOBA_HD, 1))
                 for p in range(NH // 2)]
        o_ref[0, t * BS:(t + 1) * BS, :] = jnp.concatenate(pairs, axis=-1).astype(BF16)


def _moba(mq, mk, mv, km, slopes):
    B, H, S, _ = mq.shape
    BS = MOBA_BLOCK
    NH = MOBA_HEADS_PER_STEP
    nq = S // BS
    nb = km.shape[2]
    assert BIAS_LANE0 + nb <= ONE_LANE and nq % 2 == 0
    shift = (jnp.arange(nb)[:, None] + BIAS_LANE0 == jnp.arange(LANES)[None, :]).astype(BF16)
    causal = jnp.where(jnp.arange(BS)[:, None] >= jnp.arange(BS)[None, :], 0.0, NEG).astype(F32)
    mask = jnp.concatenate([causal, jnp.zeros((BS, BS), F32)], axis=0)
    per_head = lambda rows: pl.BlockSpec((1, NH, rows, LANES), lambda b, p, i, sl: (b, p, 0, 0))
    grid_spec = pltpu.PrefetchScalarGridSpec(
        num_scalar_prefetch=1,
        grid=(B, H // NH, nq // 2),
        in_specs=[
            pl.BlockSpec((1, NH, 2 * BS, LANES), lambda b, p, i, sl: (b, p, i, 0)),
            per_head(S), per_head(S), per_head(nb),
            pl.BlockSpec((nb, LANES), lambda b, p, i, sl: (0, 0)),
            pl.BlockSpec((2 * BS, BS), lambda b, p, i, sl: (0, 0)),
        ],
        out_specs=pl.BlockSpec((1, 2 * BS, NH * MOBA_HD), lambda b, p, i, sl: (b, i, p)),
    )
    return pl.pallas_call(
        _moba_kernel,
        grid_spec=grid_spec,
        out_shape=jax.ShapeDtypeStruct((B, S, MOBA_W), BF16),
        compiler_params=_cparams(("arbitrary", "arbitrary", "arbitrary"), 56),
        name="moba",
    )(slopes, mq, mk, mv, km, shift, mask)


def _merge_kernel(x_ref, oret_ref, omoba_ref, cq_ref, kmem_ref, vmem_ref, gmix_ref, wg_ref, bg_ref,
                  wbr_ref, wbm_ref, wbc_ref, wout_ref, gffn_ref, wr_ref, br_ref, tri_ref,
                  x1_ref, xt_ref, eidx_ref, wts_ref, rank_ref, cnt_ref):
    D = x_ref.shape[1]
    tm = x_ref.shape[0]

    x = x_ref[...]
    h = _rms(x, gmix_ref[...]).astype(BF16)
    cq = cq_ref[...]
    om = []
    for hh in range(MEM_HEADS):
        cols = slice(hh * MEM_HD, (hh + 1) * MEM_HD)
        sc = lax.dot_general(cq[:, cols], kmem_ref[0, :, cols], _NT, preferred_element_type=F32) * (MEM_HD ** -0.5)
        sc = sc - jnp.max(sc, axis=-1, keepdims=True)
        p = jnp.exp(sc)
        p = p / jnp.sum(p, axis=-1, keepdims=True)
        om.append(_dot(p.astype(BF16), vmem_ref[0, :, cols]))
    omem = jnp.concatenate(om, axis=-1).astype(BF16)
    y = None
    for br, (o, w_ref) in enumerate(((oret_ref[...], wbr_ref), (omoba_ref[...], wbm_ref), (omem, wbc_ref))):
        gl = _dot(h, wg_ref[:, br * D:(br + 1) * D]) + bg_ref[:, br * D:(br + 1) * D]
        term = jax.nn.sigmoid(gl) * _dot(o, w_ref[...])
        y = term if y is None else y + term
    x1 = x + _dot(y.astype(BF16), wout_ref[...])
    x1_ref[...] = x1
    xt = _rms(x1, gffn_ref[...])
    wr = wr_ref[...]
    E = wr.shape[0]
    w_hi = wr.astype(BF16)
    w_mid = (wr - w_hi.astype(F32)).astype(BF16)
    w_lo = (wr - w_hi.astype(F32) - w_mid.astype(F32)).astype(BF16)
    xt_hi = xt.astype(BF16)
    xt_ref[...] = xt_hi
    xt_lo = (xt - xt_hi.astype(F32)).astype(BF16)
    a = lax.dot_general(jnp.concatenate([w_hi, w_mid, w_lo], axis=0), xt_hi, _NT, preferred_element_type=F32)
    b = lax.dot_general(jnp.concatenate([w_hi, w_mid], axis=0), xt_lo, _NT, preferred_element_type=F32)
    logits = (a[:E] + (a[E:2 * E] + b[:E])) + (a[2 * E:] + b[E:]) + br_ref[...]
    e_iota = lax.broadcasted_iota(I32, (E, tm), 0)
    l = logits
    vals, hots = [], []
    for k in range(TOP_K):
        m = jnp.max(l, axis=0, keepdims=True)
        idx = jnp.min(jnp.where(l == m, e_iota, E), axis=0, keepdims=True)
        hot = e_iota == idx
        l = jnp.where(hot, -jnp.inf, l)
        vals.append(m)
        hots.append(hot)
        eidx_ref[k:k + 1, :] = idx
    ex = [jnp.exp(v - vals[0]) for v in vals]
    den = ex[0]
    for k in range(1, TOP_K):
        den = den + ex[k]
    chosen = jnp.zeros((E, tm), F32)
    for k in range(TOP_K):
        wts_ref[k:k + 1, :] = ex[k] / den
        chosen = chosen + jnp.where(hots[k], 1.0, 0.0)
    RT = tri_ref.shape[0]
    for t in range(tm // RT):
        cols = slice(t * RT, (t + 1) * RT)
        prefix = _dot(chosen[:, cols].astype(BF16), tri_ref[...])
        for k in range(TOP_K):
            rank_ref[k:k + 1, cols] = jnp.sum(jnp.where(hots[k][:, cols], prefix, 0.0), axis=0, keepdims=True).astype(I32)
        counts = jnp.sum(chosen[:, cols], axis=1, keepdims=True)
        cnt_ref[t] = jnp.broadcast_to(counts, cnt_ref.shape[1:]).astype(I32)


def _merge(x2, S, o_ret, o_moba, cq, kmem, vmem, g_mix, w_gate, b_gate, w_br_ret, w_br_moba, w_br_mem, w_out,
           g_ffn, w_router, b_router):
    N, D = x2.shape
    tm = TM_MERGE
    nS = S // tm
    M = kmem.shape[1]
    E = w_router.shape[1]
    tok = lambda w: pl.BlockSpec((tm, w), lambda i: (i, 0))
    const = lambda r, c: pl.BlockSpec((r, c), lambda i: (0, 0), pipeline_mode=pl.Buffered(1))
    mem_spec = pl.BlockSpec((1, M, MEM_W), lambda i: (i // nS, 0, 0))
    lanes_tok = lambda r, dt: (pl.BlockSpec((r, tm), lambda i: (0, i)), jax.ShapeDtypeStruct((r, N), dt))
    RT = ROUTE_T
    tri = (jnp.arange(RT)[:, None] < jnp.arange(RT)[None, :]).astype(BF16)
    e_spec, e_shape = lanes_tok(TOP_K, I32)
    w_spec, w_shape = lanes_tok(TOP_K, F32)
    r_spec, r_shape = lanes_tok(TOP_K, I32)
    return pl.pallas_call(
        _merge_kernel,
        grid=(N // tm,),
        in_specs=[tok(D), tok(RET_V), tok(MOBA_W), tok(MEM_W), mem_spec, mem_spec,
                  const(1, D), const(D, N_BRANCH * D), const(1, N_BRANCH * D),
                  const(RET_V, D), const(MOBA_W, D), const(MEM_W, D), const(D, D),
                  const(1, D), const(E, D), const(E, 1), const(RT, RT)],
        out_specs=[tok(D), tok(D), e_spec, w_spec, r_spec, pl.BlockSpec((tm // RT, E, LANES), lambda i: (i, 0, 0))],
        out_shape=[jax.ShapeDtypeStruct((N, D), F32), jax.ShapeDtypeStruct((N, D), BF16),
                   e_shape, w_shape, r_shape, jax.ShapeDtypeStruct((N // RT, E, LANES), I32)],
        compiler_params=_cparams(("arbitrary",), 56),
        name="merge",
    )(x2, o_ret, o_moba, cq, kmem, vmem, g_mix.reshape(1, D), w_gate, b_gate.reshape(1, N_BRANCH * D),
      w_br_ret.astype(BF16), w_br_moba.astype(BF16), w_br_mem.astype(BF16), w_out.astype(BF16),
      g_ffn.reshape(1, D), w_router.T, b_router.reshape(E, 1), tri)


def _segment_copies(seg_ref, lstart_ref, gstart_ref, tile, local_ref, slots_ref, sem, to_slots, fn):
    def per_expert(e, carry):
        idx = tile * N_EXPERTS + e
        size = seg_ref[idx]
        off = jnp.int32(0)
        for chunk in SEG_CHUNKS:
            take = (size & chunk) != 0
            lo = pl.ds(pl.multiple_of(lstart_ref[idx] + off, SEG_ALIGN), chunk)
            gl = pl.ds(pl.multiple_of(gstart_ref[idx] + off, SEG_ALIGN), chunk)
            src, dst = (local_ref.at[lo], slots_ref.at[gl]) if to_slots else (slots_ref.at[gl], local_ref.at[lo])

            @pl.when(take)
            def _():
                fn(pltpu.make_async_copy(src, dst, sem))

            off = off + jnp.where(take, chunk, 0)
        return carry

    lax.fori_loop(0, N_EXPERTS, per_expert, 0)


def _wait_tile(total_ref, tile, local_ref, slots_ref, sem, to_slots):
    total = total_ref[tile]
    for chunk in TOTAL_CHUNKS:
        lo, gl = local_ref.at[pl.ds(0, chunk)], slots_ref.at[pl.ds(0, chunk)]
        src, dst = (lo, gl) if to_slots else (gl, lo)

        @pl.when((total & chunk) != 0)
        def _():
            pltpu.make_async_copy(src, dst, sem).wait()


def _dispatch_kernel(seg_ref, lstart_ref, gstart_ref, total_ref, pad_end_ref, padded_ref, ld_ref, xt_ref, xb_ref,
                     ybuf_ref, zeros_ref, zsem, sems):
    T = zeros_ref.shape[0]
    i = pl.program_id(0)
    n = pl.num_programs(0)
    slot = i % 2

    @pl.when(i == 0)
    def _():
        zeros_ref[...] = jnp.zeros_like(zeros_ref)

        def fill(start):
            return pltpu.make_async_copy(zeros_ref, xb_ref.at[pl.ds(pl.multiple_of(start, T), T)], zsem)

        for e in range(N_EXPERTS):
            @pl.when(padded_ref[e] > 0)
            def _():
                fill(pad_end_ref[e] - T).start()
        for e in range(N_EXPERTS):
            @pl.when(padded_ref[e] > 0)
            def _():
                fill(pad_end_ref[e] - T).wait()
        first_unused = pad_end_ref[N_EXPERTS - 1] // T
        n_blocks = xb_ref.shape[0] // T
        lax.fori_loop(first_unused, n_blocks, lambda b, c: (fill(b * T).start(), c)[1], 0)
        lax.fori_loop(first_unused, n_blocks, lambda b, c: (fill(b * T).wait(), c)[1], 0)

    @pl.when(i >= 2)
    def _():
        _wait_tile(total_ref, i - 2, ybuf_ref.at[slot], xb_ref, sems.at[slot], True)

    L = ybuf_ref.shape[1]
    RT = xt_ref.shape[0]
    r_iota = lax.broadcasted_iota(I32, (L, RT), 0)
    ld = ld_ref[...]
    onehot = jnp.zeros((L, RT), F32)
    for k in range(TOP_K):
        onehot = jnp.where(r_iota == ld[k:k + 1, :], 1.0, onehot)
    onehot = onehot.astype(BF16)
    ybuf_ref[slot] = _dot(onehot, xt_ref[...])
    _segment_copies(seg_ref, lstart_ref, gstart_ref, i, ybuf_ref.at[slot], xb_ref, sems.at[slot], True,
                    lambda c: c.start())

    @pl.when(i == n - 1)
    def _():
        @pl.when(n >= 2)
        def _():
            _wait_tile(total_ref, i - 1, ybuf_ref.at[1 - slot], xb_ref, sems.at[1 - slot], True)

        _wait_tile(total_ref, i, ybuf_ref.at[slot], xb_ref, sems.at[slot], True)


def _dispatch(xt, ld, seg, lstart, gstart, totals, pad_ends, padded, R):
    N, D = xt.shape
    RT = ROUTE_T
    grid_spec = pltpu.PrefetchScalarGridSpec(
        num_scalar_prefetch=6,
        grid=(N // RT,),
        in_specs=[
            pl.BlockSpec((TOP_K, RT), lambda i, *_: (0, i)),
            pl.BlockSpec((RT, D), lambda i, *_: (i, 0)),
        ],
        out_specs=pl.BlockSpec(memory_space=pl.ANY),
        scratch_shapes=[pltpu.VMEM((2, LOCAL_ROWS, D), F32), pltpu.VMEM((MOE_T, D), F32),
                        pltpu.SemaphoreType.DMA(()), pltpu.SemaphoreType.DMA((2,))],
    )
    return pl.pallas_call(
        _dispatch_kernel,
        grid_spec=grid_spec,
        out_shape=jax.ShapeDtypeStruct((R, D), F32),
        compiler_params=_cparams(("arbitrary",), 48),
        name="dispatch",
    )(seg, lstart, gstart, totals, pad_ends, padded, ld, xt)


def _expert_kernel(start_ref, nblk_ref, b1_ref, b2_ref, perm_ref, w1_hbm, w2_hbm, xb_ref, yb_ref,
                   w1f_ref, w2f_ref, w1p_ref, w2b_ref, xbuf_ref, ybuf_ref, w_sems, in_sems, out_sems):
    e = pl.program_id(0)
    n_experts = pl.num_programs(0)
    ws = e % 2

    def w_copies(ex, slot):
        return (pltpu.make_async_copy(w1_hbm.at[ex], w1f_ref.at[slot], w_sems.at[slot, 0]),
                pltpu.make_async_copy(w2_hbm.at[ex], w2f_ref.at[slot], w_sems.at[slot, 1]))
    T = xbuf_ref.shape[1]
    G = 2 * LANES
    nb = nblk_ref[e]
    base = start_ref[e]

    def rows(b):
        return pl.ds(pl.multiple_of(base + b * T, T), T)

    def in_copy(b, slot):
        return pltpu.make_async_copy(xb_ref.at[rows(b)], xbuf_ref.at[slot], in_sems.at[slot])

    def out_copy(b, slot):
        return pltpu.make_async_copy(ybuf_ref.at[slot], yb_ref.at[rows(b)], out_sems.at[slot])

    for d in range(IN_DEPTH - 1):
        @pl.when(nb > d)
        def _():
            in_copy(d, d).start()

    @pl.when(e == 0)
    def _():
        for c in w_copies(0, 0):
            c.start()

    for c in w_copies(e, ws):
        c.wait()

    @pl.when(e + 1 < n_experts)
    def _():
        for c in w_copies(e + 1, 1 - ws):
            c.start(priority=1)

    @pl.when(nb > 0)
    def _():
        for g in range(w1p_ref.shape[1] // G):
            w = w1f_ref[ws, :, g * G:(g + 1) * G].astype(BF16)
            w1p_ref[:, g * G:(g + 1) * G] = _dot(w, perm_ref[...]).astype(BF16)
        w2b_ref[...] = w2f_ref[ws].astype(BF16)

        def block(b, carry):
            slot = b % 2
            islot = b % IN_DEPTH
            in_copy(b, islot).wait()

            @pl.when(b + IN_DEPTH - 1 < nb)
            def _():
                in_copy(b + IN_DEPTH - 1, (b + IN_DEPTH - 1) % IN_DEPTH).start()

            x = xbuf_ref[islot].astype(BF16)
            acts = []
            for g in range(w1p_ref.shape[1] // G):
                hg = _dot(x, w1p_ref[:, g * G:(g + 1) * G]) + b1_ref[0, :, g * G:(g + 1) * G]
                glu = jnp.minimum(hg[:, :LANES], SWIGLU_LIMIT)
                lin = jnp.clip(hg[:, LANES:], -SWIGLU_LIMIT, SWIGLU_LIMIT)
                acts.append((glu * jax.nn.sigmoid(SWIGLU_ALPHA * glu) * (lin + 1.0)).astype(BF16))
            y = _dot(jnp.concatenate(acts, axis=-1), w2b_ref[...]) + b2_ref[0]

            @pl.when(b >= 2)
            def _():
                out_copy(b - 2, slot).wait()

            ybuf_ref[slot] = y
            out_copy(b, slot).start()
            return carry

        lax.fori_loop(0, nb, block, 0)

        @pl.when(nb >= 2)
        def _():
            out_copy(nb - 2, nb % 2).wait()

        out_copy(nb - 1, (nb - 1) % 2).wait()

    @pl.when(e == pl.num_programs(0) - 1)
    def _():
        ybuf_ref[0] = jnp.zeros(ybuf_ref.shape[1:], ybuf_ref.dtype)
        first_unused = base // T + nb
        n_blocks = yb_ref.shape[0] // T

        def tail(b):
            return pltpu.make_async_copy(ybuf_ref.at[0], yb_ref.at[pl.ds(pl.multiple_of(b * T, T), T)], out_sems.at[0])

        lax.fori_loop(first_unused, n_blocks, lambda b, c: (tail(b).start(), c)[1], 0)
        lax.fori_loop(first_unused, n_blocks, lambda b, c: (tail(b).wait(), c)[1], 0)


def _experts(xb, region_start, region_blocks, w1, b1p, w2, b2):
    R, D = xb.shape
    E, _, F2 = w1.shape
    F = F2 // 2
    T = MOE_T
    G = 2 * LANES
    c = np.arange(G)
    src = np.where(c < LANES, 2 * c, 2 * (c - LANES) + 1)
    perm = jnp.asarray(np.arange(G)[:, None] == src[None, :], dtype=BF16)
    per_expert = lambda r, w: pl.BlockSpec((1, r, w), lambda e, *_: (e, 0, 0))
    grid_spec = pltpu.PrefetchScalarGridSpec(
        num_scalar_prefetch=2,
        grid=(E,),
        in_specs=[
            per_expert(1, F2), per_expert(1, D),
            pl.BlockSpec((G, G), lambda e, *_: (0, 0)),
            pl.BlockSpec(memory_space=pl.ANY), pl.BlockSpec(memory_space=pl.ANY), pl.BlockSpec(memory_space=pl.ANY),
        ],
        out_specs=pl.BlockSpec(memory_space=pl.ANY),
        scratch_shapes=[pltpu.VMEM((2, D, F2), F32), pltpu.VMEM((2, F, D), F32),
                        pltpu.VMEM((D, F2), BF16), pltpu.VMEM((F, D), BF16),
                        pltpu.VMEM((IN_DEPTH, T, D), F32), pltpu.VMEM((2, T, D), F32),
                        pltpu.SemaphoreType.DMA((2, 2)), pltpu.SemaphoreType.DMA((IN_DEPTH,)),
                        pltpu.SemaphoreType.DMA((2,))],
    )
    return pl.pallas_call(
        _expert_kernel,
        grid_spec=grid_spec,
        out_shape=jax.ShapeDtypeStruct((R, D), F32),
        compiler_params=_cparams(("arbitrary",), 56),
        name="experts",
    )(region_start, region_blocks, b1p, b2.reshape(E, 1, D), perm, w1, w2, xb)


def _combine_kernel(seg_ref, lstart_ref, gstart_ref, total_ref, ldt_ref, wt_ref, x1_ref, yb_ref, o_ref, ybuf_ref,
                    sems):
    i = pl.program_id(0)
    n = pl.num_programs(0)
    slot = i % 2

    def fetch(tile, s):
        _segment_copies(seg_ref, lstart_ref, gstart_ref, tile, ybuf_ref.at[s], yb_ref, sems.at[s], False,
                        lambda c: c.start())

    @pl.when(i == 0)
    def _():
        ybuf_ref[...] = jnp.zeros_like(ybuf_ref)
        fetch(i, slot)

    @pl.when(i + 1 < n)
    def _():
        fetch(i + 1, 1 - slot)

    _wait_tile(total_ref, i, ybuf_ref.at[slot], yb_ref, sems.at[slot], False)

    L = ybuf_ref.shape[1]
    RT = x1_ref.shape[0]
    c_iota = lax.broadcasted_iota(I32, (RT, L), 1)
    ldt = ldt_ref[...]
    w = wt_ref[...]
    w_hi = w.astype(BF16).astype(F32)
    w_lo = w - w_hi
    g_hi = jnp.zeros((RT, L), F32)
    g_lo = jnp.zeros((RT, L), F32)
    for k in range(TOP_K):
        hit = c_iota == ldt[:, k:k + 1]
        g_hi = jnp.where(hit, w_hi[:, k:k + 1], g_hi)
        g_lo = jnp.where(hit, w_lo[:, k:k + 1], g_lo)
    y = ybuf_ref[slot].astype(BF16)
    o_ref[...] = x1_ref[...] + (_dot(g_hi.astype(BF16), y) + _dot(g_lo.astype(BF16), y))


def _combine(ldt, wts_t, x1, yb, seg, lstart, gstart, totals):
    N, D = x1.shape
    RT = ROUTE_T
    grid_spec = pltpu.PrefetchScalarGridSpec(
        num_scalar_prefetch=4,
        grid=(N // RT,),
        in_specs=[
            pl.BlockSpec((RT, TOP_K), lambda i, *_: (i, 0)),
            pl.BlockSpec((RT, TOP_K), lambda i, *_: (i, 0)),
            pl.BlockSpec((RT, D), lambda i, *_: (i, 0)),
            pl.BlockSpec(memory_space=pl.ANY),
        ],
        out_specs=pl.BlockSpec((RT, D), lambda i, *_: (i, 0)),
        scratch_shapes=[pltpu.VMEM((2, LOCAL_ROWS, D), F32), pltpu.SemaphoreType.DMA((2,))],
    )
    return pl.pallas_call(
        _combine_kernel,
        grid_spec=grid_spec,
        out_shape=jax.ShapeDtypeStruct((N, D), F32),
        compiler_params=_cparams(("arbitrary",), 48),
        name="combine",
    )(seg, lstart, gstart, totals, ldt, wts_t, x1, yb)


def _layer(x, mem, g_mix, w_in, b_gate, g_ret_out, g_moba_q, g_moba_k, g_mem, w_mem_kv, g_mem_q, g_mem_k,
           w_br_ret, w_br_moba, w_br_mem, w_out, g_ffn, w_router, b_router, w_mlp1, b_mlp1, w_mlp2, b_mlp2):
    B, S, D = x.shape
    N = B * S
    x2 = x.reshape(N, D)
    slopes_np = np.exp2(-8.0 * (np.arange(MOBA_HEADS, dtype=np.float64) + 1.0) / MOBA_HEADS)
    assert all(float(np.log2(s)).is_integer() for s in slopes_np)
    slopes = tuple(float(s) for s in slopes_np)

    w_in_bf = w_in.astype(BF16)
    kmem, vmem = _memkv(mem, g_mem, w_mem_kv, g_mem_k)
    rq, rk, rv, rg, mq, mk, mv, cq, km = _inproj(x2, B, S, g_mix, w_in_bf[:, :MIX_W], g_moba_q, g_moba_k, g_mem_q,
                                                 slopes)
    o_ret = _retention(rq, rk, rv, rg, g_ret_out)
    o_moba = _moba(mq, mk, mv, km, jnp.asarray(slopes, F32)).reshape(N, MOBA_W)
    x1, xt, eidx, wts, rank, cnt = _merge(x2, S, o_ret, o_moba, cq, kmem, vmem, g_mix, w_in_bf[:, MIX_W:], b_gate,
                                          w_br_ret, w_br_moba, w_br_mem, w_out, g_ffn, w_router, b_router)
    T = MOE_T
    tcnt = cnt[:, :, 0]
    seg = ((tcnt + SEG_ALIGN - 1) // SEG_ALIGN) * SEG_ALIGN
    region_rows = jnp.sum(seg, axis=0)
    padded = ((region_rows + T - 1) // T) * T
    pad_ends = jnp.cumsum(padded).astype(I32)
    pad_starts = pad_ends - padded
    gstart = pad_starts[None, :] + jnp.cumsum(seg, axis=0) - seg
    lstart = jnp.cumsum(seg, axis=1) - seg
    lstart_tok = jnp.broadcast_to(lstart[:, None, :], (N // ROUTE_T, ROUTE_T, N_EXPERTS)).reshape(N, N_EXPERTS)
    onehot = eidx[:, :, None] == jnp.arange(N_EXPERTS, dtype=I32)[None, None, :]
    ld = jnp.sum(jnp.where(onehot, lstart_tok[None], 0), axis=-1) + rank
    NB = -(-(N // ROUTE_T * LOCAL_ROWS) // T) + N_EXPERTS
    seg_f, lstart_f, gstart_f = (a.reshape(-1).astype(I32) for a in (seg, lstart, gstart))
    totals = jnp.sum(seg, axis=1).astype(I32)

    xb = _dispatch(xt, ld, seg_f, lstart_f, gstart_f, totals, pad_ends, padded.astype(I32), NB * T)
    F2 = w_mlp1.shape[-1]
    b1p = b_mlp1.reshape(N_EXPERTS, F2 // (2 * LANES), LANES, 2).transpose(0, 1, 3, 2).reshape(N_EXPERTS, 1, F2)
    yb = _experts(xb, pad_starts.astype(I32), (padded // T).astype(I32), w_mlp1, b1p, w_mlp2, b_mlp2)
    out = _combine(ld.T, wts.T, x1, yb, seg_f, lstart_f, gstart_f, totals)
    return out.reshape(B, S, D)


def kernel(x, mem, g_mix, w_in, b_gate, g_ret_out, g_moba_q, g_moba_k, g_mem, w_mem_kv, g_mem_q, g_mem_k, w_br_ret, w_br_moba, w_br_mem, w_out, g_ffn, w_router, b_router, w_mlp1, b_mlp1, w_mlp2, b_mlp2):
    for l in range(g_mix.shape[0]):
        x = _layer(x, mem, g_mix[l], w_in[l], b_gate[l], g_ret_out[l], g_moba_q[l], g_moba_k[l], g_mem[l],
                   w_mem_kv[l], g_mem_q[l], g_mem_k[l], w_br_ret[l], w_br_moba[l], w_br_mem[l], w_out[l],
                   g_ffn[l], w_router[l], b_router[l], w_mlp1[l], b_mlp1[l], w_mlp2[l], b_mlp2[l])
    return x
```

```python
import functools

import jax
import jax.numpy as jnp
import numpy as np
from jax import lax
from jax.experimental import pallas as pl
from jax.experimental.pallas import tpu as pltpu

F32 = jnp.float32
BF16 = jnp.bfloat16
I32 = jnp.int32

EPS = 1e-5
NEG = -1e30

RET_HEADS = 4
RET_DK = 64
RET_DV = 128
RET_CHUNK = 128
MOBA_HEADS = 8
MOBA_HD = 64
MOBA_BLOCK = 256
MOBA_TOPK = 3
MEM_HEADS = 4
MEM_HD = 128
N_BRANCH = 3
N_EXPERTS = 32
TOP_K = 4
SWIGLU_LIMIT = 7.0
SWIGLU_ALPHA = 1.702

RET_Q = RET_HEADS * RET_DK
RET_V = RET_HEADS * RET_DV
MOBA_W = MOBA_HEADS * MOBA_HD
MEM_W = MEM_HEADS * MEM_HD
MIX_W = 2 * RET_Q + 2 * RET_V + 3 * MOBA_W + MEM_W

LANES = 128
MOBA_PAIRS = MOBA_HEADS // 2
MOBA_HEADS_PER_STEP = 8
BIAS_LANE0 = MOBA_HD
ONE_LANE = 80
VSUM_LANE = MOBA_HD

MOE_T = 256
IN_DEPTH = 3
TM_PROJ = 512
TM_MERGE = 512
TQ_RET = 512
ROUTE_T = 256
SEG_ALIGN = 8
LOCAL_ROWS = -(-(ROUTE_T * TOP_K + N_EXPERTS * (SEG_ALIGN - 1)) // LANES) * LANES
SEG_CHUNKS = tuple(2 ** p for p in range(ROUTE_T.bit_length() - 1, SEG_ALIGN.bit_length() - 2, -1))
SEG_SMALL = 64
TOTAL_CHUNKS = tuple(2 ** p for p in range(LOCAL_ROWS.bit_length() - 1, SEG_ALIGN.bit_length() - 2, -1))

_NT = (((1,), (1,)), ((), ()))
_TN = (((0,), (0,)), ((), ()))


def _rms(x, g):
    return x * lax.rsqrt(jnp.mean(x * x, axis=-1, keepdims=True) + EPS) * g


def _dot(a, b):
    return jnp.dot(a, b, preferred_element_type=F32)


def _cparams(sem, vmem_mb):
    return pltpu.CompilerParams(dimension_semantics=sem, vmem_limit_bytes=vmem_mb * 1024 * 1024)


def _memkv_kernel(mem_ref, g_ref, w_ref, gk_ref, k_ref, v_ref):
    m = _rms(mem_ref[0], g_ref[...]).astype(BF16)
    kv = _dot(m, w_ref[...])
    ks = [_rms(kv[:, h * MEM_HD:(h + 1) * MEM_HD], gk_ref[...]) for h in range(MEM_HEADS)]
    k_ref[0] = jnp.concatenate(ks, axis=-1).astype(BF16)
    v_ref[0] = kv[:, MEM_W:].astype(BF16)


def _memkv(mem, g_mem, w_mem_kv, g_mem_k):
    B, M, D = mem.shape
    return pl.pallas_call(
        _memkv_kernel,
        grid=(B,),
        in_specs=[
            pl.BlockSpec((1, M, D), lambda b: (b, 0, 0)),
            pl.BlockSpec((1, D), lambda b: (0, 0)),
            pl.BlockSpec((D, 2 * MEM_W), lambda b: (0, 0)),
            pl.BlockSpec((1, MEM_HD), lambda b: (0, 0)),
        ],
        out_specs=[
            pl.BlockSpec((1, M, MEM_W), lambda b: (b, 0, 0)),
            pl.BlockSpec((1, M, MEM_W), lambda b: (b, 0, 0)),
        ],
        out_shape=[jax.ShapeDtypeStruct((B, M, MEM_W), BF16)] * 2,
        compiler_params=_cparams(("arbitrary",), 32),
        name="memkv",
    )(mem, g_mem.reshape(1, D), w_mem_kv.astype(BF16), g_mem_k.reshape(1, MEM_HD))


def _head_pair_norm(a2, g2, lane):
    sq = a2 * a2
    lo = lane < MOBA_HD
    ss_lo = jnp.sum(jnp.where(lo, sq, 0.0), axis=-1, keepdims=True)
    ss_hi = jnp.sum(jnp.where(lo, 0.0, sq), axis=-1, keepdims=True)
    inv = jnp.where(lo, lax.rsqrt(ss_lo / MOBA_HD + EPS), lax.rsqrt(ss_hi / MOBA_HD + EPS))
    return a2 * inv * g2


def _inproj_kernel(slopes, seq_tiles, x_ref, gmix_ref, w_ref, gq_ref, gk_ref, gc_ref,
                   rq_ref, rk_ref, rv_ref, rg_ref, mq_ref, mk_ref, mv_ref, cq_ref, kmean_ref):
    tm = x_ref.shape[0]
    blocks_per_tile = tm // MOBA_BLOCK
    h = _rms(x_ref[...], gmix_ref[...]).astype(BF16)
    col = [0]

    def proj(width):
        a = _dot(h, w_ref[:, col[0]:col[0] + width])
        col[0] += width
        return a

    a = proj(2 * RET_Q)
    for hh in range(RET_HEADS):
        rq_ref[0, hh] = a[:, hh * RET_DK:(hh + 1) * RET_DK].astype(BF16)
        rk_ref[0, hh] = (a[:, RET_Q + hh * RET_DK:RET_Q + (hh + 1) * RET_DK] * (RET_DK ** -0.5)).astype(BF16)
    rv_ref[...] = proj(RET_V).astype(BF16)
    rg_ref[...] = proj(RET_V).astype(BF16)

    lane = lax.broadcasted_iota(I32, (tm, LANES), 1)
    row = lax.broadcasted_iota(I32, (tm, LANES), 0)
    lo = lane < MOBA_HD
    q_tail = jnp.where(lane == ONE_LANE, 1.0, 0.0)
    a = proj(MOBA_W)
    for p in range(MOBA_PAIRS):
        n2 = _head_pair_norm(a[:, p * LANES:(p + 1) * LANES], gq_ref[...], lane) * (MOBA_HD ** -0.5)
        mq_ref[0, 2 * p] = jnp.where(lo, n2, q_tail).astype(BF16)
        mq_ref[0, 2 * p + 1] = jnp.where(lo, pltpu.roll(n2, MOBA_HD, 1), q_tail).astype(BF16)
    blk = (pl.program_id(0) % seq_tiles) * blocks_per_tile + row // MOBA_BLOCK
    onehot_tail = jnp.where(lane == BIAS_LANE0 + blk, 1.0, 0.0)
    off = (row % MOBA_BLOCK).astype(F32)
    a = proj(MOBA_W)
    for p in range(MOBA_PAIRS):
        n2 = _head_pair_norm(a[:, p * LANES:(p + 1) * LANES], gk_ref[...], lane)
        for j in range(blocks_per_tile):
            kmean_ref[0, 0, p, j:j + 1, :] = jnp.mean(n2[j * MOBA_BLOCK:(j + 1) * MOBA_BLOCK], axis=0, keepdims=True)
        for s, src in ((0, n2), (1, pltpu.roll(n2, MOBA_HD, 1))):
            tail = jnp.where(lane == ONE_LANE, slopes[2 * p + s] * off, onehot_tail)
            mk_ref[0, 2 * p + s] = jnp.where(lo, src, tail).astype(BF16)
    v_tail = jnp.where(lane == VSUM_LANE, 1.0, 0.0)
    a = proj(MOBA_W)
    for p in range(MOBA_PAIRS):
        a2 = a[:, p * LANES:(p + 1) * LANES]
        mv_ref[0, 2 * p] = jnp.where(lo, a2, v_tail).astype(BF16)
        mv_ref[0, 2 * p + 1] = jnp.where(lo, pltpu.roll(a2, MOBA_HD, 1), v_tail).astype(BF16)
    a = proj(MEM_W)
    cq = [_rms(a[:, hh * MEM_HD:(hh + 1) * MEM_HD], gc_ref[...]) for hh in range(MEM_HEADS)]
    cq_ref[...] = jnp.concatenate(cq, axis=-1).astype(BF16)


def _inproj(x2, B, S, g_mix, w_mix, g_moba_q, g_moba_k, g_mem_q, slopes):
    N, D = x2.shape
    tm = TM_PROJ
    nS = S // tm
    bpt = tm // MOBA_BLOCK
    tok = lambda i: (i, 0)
    headmaj = lambda i: (i // nS, 0, i % nS, 0)
    g2 = lambda g: jnp.concatenate([g, g]).reshape(1, LANES)
    outs = pl.pallas_call(
        functools.partial(_inproj_kernel, slopes, nS),
        grid=(N // tm,),
        in_specs=[
            pl.BlockSpec((tm, D), tok),
            pl.BlockSpec((1, D), lambda i: (0, 0)),
            pl.BlockSpec((D, MIX_W), lambda i: (0, 0)),
            pl.BlockSpec((1, LANES), lambda i: (0, 0)),
            pl.BlockSpec((1, LANES), lambda i: (0, 0)),
            pl.BlockSpec((1, MEM_HD), lambda i: (0, 0)),
        ],
        out_specs=[
            pl.BlockSpec((1, RET_HEADS, tm, RET_DK), headmaj),
            pl.BlockSpec((1, RET_HEADS, tm, RET_DK), headmaj),
            pl.BlockSpec((tm, RET_V), tok),
            pl.BlockSpec((tm, RET_V), tok),
            pl.BlockSpec((1, MOBA_HEADS, tm, LANES), headmaj),
            pl.BlockSpec((1, MOBA_HEADS, tm, LANES), headmaj),
            pl.BlockSpec((1, MOBA_HEADS, tm, LANES), headmaj),
            pl.BlockSpec((tm, MEM_W), tok),
            pl.BlockSpec((1, 1, MOBA_PAIRS, bpt, LANES), lambda i: (i // nS, i % nS, 0, 0, 0)),
        ],
        out_shape=[
            jax.ShapeDtypeStruct((B, RET_HEADS, S, RET_DK), BF16),
            jax.ShapeDtypeStruct((B, RET_HEADS, S, RET_DK), BF16),
            jax.ShapeDtypeStruct((N, RET_V), BF16),
            jax.ShapeDtypeStruct((N, RET_V), BF16),
            jax.ShapeDtypeStruct((B, MOBA_HEADS, S, LANES), BF16),
            jax.ShapeDtypeStruct((B, MOBA_HEADS, S, LANES), BF16),
            jax.ShapeDtypeStruct((B, MOBA_HEADS, S, LANES), BF16),
            jax.ShapeDtypeStruct((N, MEM_W), BF16),
            jax.ShapeDtypeStruct((B, nS, MOBA_PAIRS, bpt, LANES), F32),
        ],
        compiler_params=_cparams(("arbitrary",), 56),
        name="inproj",
    )(x2, g_mix.reshape(1, D), w_mix, g2(g_moba_q), g2(g_moba_k), g_mem_q.reshape(1, MEM_HD))
    rq, rk, rv, rg, mq, mk, mv, cq, kmean_pairs = outs
    km = kmean_pairs.reshape(B, nS, MOBA_PAIRS, bpt, 2, MOBA_HD).transpose(0, 2, 4, 1, 3, 5)
    km = km.reshape(B, MOBA_HEADS, nS * bpt, MOBA_HD)
    km = jnp.pad(km, ((0, 0), (0, 0), (0, 0), (0, LANES - MOBA_HD)))
    return rq, rk, rv, rg, mq, mk, mv, cq, km


def _retention_kernel(q_ref, k_ref, v_ref, rg_ref, din_ref, dq_ref, dk_ref, dc_ref, g_ref, o_ref, state_ref):
    @pl.when(pl.program_id(1) == 0)
    def _():
        state_ref[...] = jnp.zeros_like(state_ref)

    C = RET_CHUNK
    for h in range(q_ref.shape[1]):
        cols = slice(h * RET_DV, (h + 1) * RET_DV)
        state = state_ref[h]
        for c in range(q_ref.shape[2] // C):
            rows = slice(c * C, (c + 1) * C)
            q = q_ref[0, h, rows, :]
            k = k_ref[0, h, rows, :]
            v = v_ref[rows, cols]
            scores = lax.dot_general(q, k, _NT, preferred_element_type=F32) * din_ref[h]
            intra = _dot(scores.astype(BF16), v)
            cross = _dot(q, state.astype(BF16)) * dq_ref[h]
            kd = (k.astype(F32) * dk_ref[h]).astype(BF16)
            state = dc_ref[h] * state + lax.dot_general(kd, v, _TN, preferred_element_type=F32)
            o = _rms(intra + cross, g_ref[...])
            o_ref[rows, cols] = (o * jax.nn.silu(rg_ref[rows, cols].astype(F32))).astype(BF16)
        state_ref[h] = state


def _retention_decays():
    H, C = RET_HEADS, RET_CHUNK
    log_g = jnp.log1p(-jnp.exp2(-5.0 - jnp.arange(H, dtype=F32)))
    i = jnp.arange(C, dtype=F32)
    diff = i[:, None] - i[None, :]
    decay_in = jnp.where(diff >= 0, jnp.exp(jnp.maximum(diff, 0.0)[None] * log_g[:, None, None]), 0.0)
    decay_k = jnp.exp((C - 1 - i)[None, :] * log_g[:, None])
    decay_q = jnp.exp((i + 1)[None, :] * log_g[:, None])
    decay_chunk = jnp.exp(C * log_g)
    dq = jnp.broadcast_to(decay_q[:, :, None], (H, C, RET_DV))
    dk = jnp.broadcast_to(decay_k[:, :, None], (H, C, RET_DK))
    dc = jnp.broadcast_to(decay_chunk[:, None, None], (H, RET_DK, RET_DV))
    return decay_in, dq, dk, dc


def _retention(rq, rk, rv, rg, g_ret_out):
    B, H, S, dk = rq.shape
    tq = TQ_RET
    nT = S // tq
    C = RET_CHUNK
    din, dq, dk_, dc = _retention_decays()
    qk_spec = pl.BlockSpec((1, H, tq, dk), lambda b, t: (b, 0, t, 0))
    tok_spec = pl.BlockSpec((tq, RET_V), lambda b, t: (b * nT + t, 0))
    const = lambda r, c: pl.BlockSpec((H, r, c), lambda b, t: (0, 0, 0))
    return pl.pallas_call(
        _retention_kernel,
        grid=(B, nT),
        in_specs=[qk_spec, qk_spec, tok_spec, tok_spec,
                  const(C, C), const(C, RET_DV), const(C, RET_DK), const(RET_DK, RET_DV),
                  pl.BlockSpec((1, RET_DV), lambda b, t: (0, 0))],
        out_specs=tok_spec,
        out_shape=jax.ShapeDtypeStruct((B * S, RET_V), BF16),
        scratch_shapes=[pltpu.VMEM((H, RET_DK, RET_DV), F32)],
        compiler_params=_cparams(("arbitrary", "arbitrary"), 32),
        name="retention",
    )(rq, rk, rv, rg, din, dq, dk_, dc, g_ret_out.reshape(1, RET_DV))


def _moba_kernel(slopes_ref, q_ref, k_ref, v_ref, km_ref, shift_ref, mask_ref, o_ref):
    NH = q_ref.shape[1]
    head0 = pl.program_id(1) * NH
    i0 = pl.program_id(2) * 2
    BS = MOBA_BLOCK
    nb = km_ref.shape[2]
    heads = range(NH)
    n_iota = lax.broadcasted_iota(I32, (nb, 2 * BS), 0)
    q_blk = i0 + lax.broadcasted_iota(I32, (nb, 2 * BS), 1) // BS
    past = n_iota < q_blk
    qa = [q_ref[0, s] for s in heads]
    q_aug = []
    for s in heads:
        km = km_ref[0, s]
        hi = km.astype(BF16)
        mid = (km - hi.astype(F32)).astype(BF16)
        lo = (km - hi.astype(F32) - mid.astype(F32)).astype(BF16)
        g3 = lax.dot_general(jnp.concatenate([hi, mid, lo], axis=0), qa[s], _NT, preferred_element_type=F32)
        gate = (g3[:nb] + g3[nb:2 * nb]) + g3[2 * nb:]
        g = jnp.where(past, gate, -jnp.inf)
        rank = jnp.zeros((nb, 2 * BS), I32)
        for m in range(nb - 1):
            gm = g[m:m + 1, :]
            rank = rank + jnp.where(gm > g, 1, jnp.where(gm == g, jnp.where(m < n_iota, 1, 0), 0))
        sel = jnp.where(past, rank, MOBA_TOPK) < MOBA_TOPK
        bias = jnp.where(sel, (n_iota - q_blk).astype(F32) * (slopes_ref[head0 + s] * BS), NEG).astype(BF16)
        placed = lax.dot_general(bias, shift_ref[...], _TN, preferred_element_type=F32)
        q_aug.append((qa[s].astype(F32) + placed).astype(BF16))

    def rows_of(j):
        return pl.ds(pl.multiple_of(j * BS, BS), BS)

    m_lo, m_hi, acc_lo, acc_hi = [], [], [], []
    for s in heads:
        q_mix = jnp.concatenate([qa[s][:BS], q_aug[s][BS:]], axis=0)
        sc = lax.dot_general(q_mix, k_ref[0, s, rows_of(i0), :], _NT, preferred_element_type=F32) + mask_ref[...]
        m = jnp.max(sc, axis=-1, keepdims=True)
        acc = _dot(jnp.exp(sc - m).astype(BF16), v_ref[0, s, rows_of(i0), :])
        m_lo.append(m[:BS])
        acc_lo.append(acc[:BS])
        sc1 = lax.dot_general(qa[s][BS:], k_ref[0, s, rows_of(i0 + 1), :], _NT, preferred_element_type=F32)
        sc1 = sc1 + mask_ref[:BS, :]
        m1 = jnp.maximum(m[BS:], jnp.max(sc1, axis=-1, keepdims=True))
        p1 = jnp.exp(sc1 - m1).astype(BF16)
        m_hi.append(m1)
        acc_hi.append(jnp.exp(m[BS:] - m1) * acc[BS:] + _dot(p1, v_ref[0, s, rows_of(i0 + 1), :]))

    def body(j, carry):
        ms_lo, ms_hi, accs_lo, accs_hi = carry
        out = ([], [], [], [])
        for s in heads:
            sj = lax.dot_general(q_aug[s], k_ref[0, s, rows_of(j), :], _NT, preferred_element_type=F32)
            ps, alphas, ms = [], [], []
            for half, m_prev in ((slice(0, BS), ms_lo[s]), (slice(BS, 2 * BS), ms_hi[s])):
                m_new = jnp.maximum(m_prev, jnp.max(sj[half], axis=-1, keepdims=True))
                ps.append(jnp.exp(sj[half] - m_new).astype(BF16))
                alphas.append(jnp.exp(m_prev - m_new))
                ms.append(m_new)
            pv = _dot(jnp.concatenate(ps, axis=0), v_ref[0, s, rows_of(j), :])
            out[0].append(ms[0])
            out[1].append(ms[1])
            out[2].append(alphas[0] * accs_lo[s] + pv[:BS])
            out[3].append(alphas[1] * accs_hi[s] + pv[BS:])
        return tuple(tuple(o) for o in out)

    _, _, accs_lo, accs_hi = lax.fori_loop(0, i0, body, (tuple(m_lo), tuple(m_hi), tuple(acc_lo), tuple(acc_hi)))
    lane = lax.broadcasted_iota(I32, (BS, LANES), 1)
    for t, accs in enumerate((accs_lo, accs_hi)):
        outs = [acc / acc[:, VSUM_LANE:VSUM_LANE + 1] for acc in accs]
        pairs = [jnp.where(lane < MOBA_HD, outs[2 * p], pltpu.roll(outs[2 * p + 1], MOBA_HD, 1))
                 for p in range(NH // 2)]
        o_ref[0, t * BS:(t + 1) * BS, :] = jnp.concatenate(pairs, axis=-1).astype(BF16)


def _moba(mq, mk, mv, km, slopes):
    B, H, S, _ = mq.shape
    BS = MOBA_BLOCK
    NH = MOBA_HEADS_PER_STEP
    nq = S // BS
    nb = km.shape[2]
    assert BIAS_LANE0 + nb <= ONE_LANE and nq % 2 == 0
    shift = (jnp.arange(nb)[:, None] + BIAS_LANE0 == jnp.arange(LANES)[None, :]).astype(BF16)
    causal = jnp.where(jnp.arange(BS)[:, None] >= jnp.arange(BS)[None, :], 0.0, NEG).astype(F32)
    mask = jnp.concatenate([causal, jnp.zeros((BS, BS), F32)], axis=0)
    per_head = lambda rows: pl.BlockSpec((1, NH, rows, LANES), lambda b, p, i, sl: (b, p, 0, 0))
    grid_spec = pltpu.PrefetchScalarGridSpec(
        num_scalar_prefetch=1,
        grid=(B, H // NH, nq // 2),
        in_specs=[
            pl.BlockSpec((1, NH, 2 * BS, LANES), lambda b, p, i, sl: (b, p, i, 0)),
            per_head(S), per_head(S), per_head(nb),
            pl.BlockSpec((nb, LANES), lambda b, p, i, sl: (0, 0)),
            pl.BlockSpec((2 * BS, BS), lambda b, p, i, sl: (0, 0)),
        ],
        out_specs=pl.BlockSpec((1, 2 * BS, NH * MOBA_HD), lambda b, p, i, sl: (b, i, p)),
    )
    return pl.pallas_call(
        _moba_kernel,
        grid_spec=grid_spec,
        out_shape=jax.ShapeDtypeStruct((B, S, MOBA_W), BF16),
        compiler_params=_cparams(("arbitrary", "arbitrary", "arbitrary"), 56),
        name="moba",
    )(slopes, mq, mk, mv, km, shift, mask)


def _merge_kernel(x_ref, oret_ref, omoba_ref, cq_ref, kmem_ref, vmem_ref, gmix_ref, wg_ref, bg_ref,
                  wbr_ref, wbm_ref, wbc_ref, wout_ref, gffn_ref, wr_ref, br_ref, tri_ref,
                  x1_ref, xt_ref, eidx_ref, wts_ref, rank_ref, cnt_ref):
    D = x_ref.shape[1]
    tm = x_ref.shape[0]

    x = x_ref[...]
    h = _rms(x, gmix_ref[...]).astype(BF16)
    cq = cq_ref[...]
    om = []
    for hh in range(MEM_HEADS):
        cols = slice(hh * MEM_HD, (hh + 1) * MEM_HD)
        sc = lax.dot_general(cq[:, cols], kmem_ref[0, :, cols], _NT, preferred_element_type=F32) * (MEM_HD ** -0.5)
        sc = sc - jnp.max(sc, axis=-1, keepdims=True)
        p = jnp.exp(sc)
        p = p / jnp.sum(p, axis=-1, keepdims=True)
        om.append(_dot(p.astype(BF16), vmem_ref[0, :, cols]))
    omem = jnp.concatenate(om, axis=-1).astype(BF16)
    y = None
    for br, (o, w_ref) in enumerate(((oret_ref[...], wbr_ref), (omoba_ref[...], wbm_ref), (omem, wbc_ref))):
        gl = _dot(h, wg_ref[:, br * D:(br + 1) * D]) + bg_ref[:, br * D:(br + 1) * D]
        term = jax.nn.sigmoid(gl) * _dot(o, w_ref[...])
        y = term if y is None else y + term
    x1 = x + _dot(y.astype(BF16), wout_ref[...])
    x1_ref[...] = x1
    xt = _rms(x1, gffn_ref[...])
    wr = wr_ref[...]
    E = wr.shape[0]
    w_hi = wr.astype(BF16)
    w_mid = (wr - w_hi.astype(F32)).astype(BF16)
    w_lo = (wr - w_hi.astype(F32) - w_mid.astype(F32)).astype(BF16)
    xt_hi = xt.astype(BF16)
    xt_ref[...] = xt_hi
    xt_lo = (xt - xt_hi.astype(F32)).astype(BF16)
    a = lax.dot_general(jnp.concatenate([w_hi, w_mid, w_lo], axis=0), xt_hi, _NT, preferred_element_type=F32)
    b = lax.dot_general(jnp.concatenate([w_hi, w_mid], axis=0), xt_lo, _NT, preferred_element_type=F32)
    logits = (a[:E] + (a[E:2 * E] + b[:E])) + (a[2 * E:] + b[E:]) + br_ref[...]
    e_iota = lax.broadcasted_iota(I32, (E, tm), 0)
    l = logits
    vals, hots = [], []
    for k in range(TOP_K):
        m = jnp.max(l, axis=0, keepdims=True)
        idx = jnp.min(jnp.where(l == m, e_iota, E), axis=0, keepdims=True)
        hot = e_iota == idx
        l = jnp.where(hot, -jnp.inf, l)
        vals.append(m)
        hots.append(hot)
        eidx_ref[k:k + 1, :] = idx
    ex = [jnp.exp(v - vals[0]) for v in vals]
    den = ex[0]
    for k in range(1, TOP_K):
        den = den + ex[k]
    chosen = jnp.zeros((E, tm), F32)
    for k in range(TOP_K):
        wts_ref[k:k + 1, :] = ex[k] / den
        chosen = chosen + jnp.where(hots[k], 1.0, 0.0)
    RT = tri_ref.shape[0]
    for t in range(tm // RT):
        cols = slice(t * RT, (t + 1) * RT)
        prefix = _dot(chosen[:, cols].astype(BF16), tri_ref[...])
        for k in range(TOP_K):
            rank_ref[k:k + 1, cols] = jnp.sum(jnp.where(hots[k][:, cols], prefix, 0.0), axis=0, keepdims=True).astype(I32)
        counts = jnp.sum(chosen[:, cols], axis=1, keepdims=True)
        cnt_ref[t] = jnp.broadcast_to(counts, cnt_ref.shape[1:]).astype(I32)


def _merge(x2, S, o_ret, o_moba, cq, kmem, vmem, g_mix, w_gate, b_gate, w_br_ret, w_br_moba, w_br_mem, w_out,
           g_ffn, w_router, b_router):
    N, D = x2.shape
    tm = TM_MERGE
    nS = S // tm
    M = kmem.shape[1]
    E = w_router.shape[1]
    tok = lambda w: pl.BlockSpec((tm, w), lambda i: (i, 0))
    const = lambda r, c: pl.BlockSpec((r, c), lambda i: (0, 0), pipeline_mode=pl.Buffered(1))
    mem_spec = pl.BlockSpec((1, M, MEM_W), lambda i: (i // nS, 0, 0))
    lanes_tok = lambda r, dt: (pl.BlockSpec((r, tm), lambda i: (0, i)), jax.ShapeDtypeStruct((r, N), dt))
    RT = ROUTE_T
    tri = (jnp.arange(RT)[:, None] < jnp.arange(RT)[None, :]).astype(BF16)
    e_spec, e_shape = lanes_tok(TOP_K, I32)
    w_spec, w_shape = lanes_tok(TOP_K, F32)
    r_spec, r_shape = lanes_tok(TOP_K, I32)
    return pl.pallas_call(
        _merge_kernel,
        grid=(N // tm,),
        in_specs=[tok(D), tok(RET_V), tok(MOBA_W), tok(MEM_W), mem_spec, mem_spec,
                  const(1, D), const(D, N_BRANCH * D), const(1, N_BRANCH * D),
                  const(RET_V, D), const(MOBA_W, D), const(MEM_W, D), const(D, D),
                  const(1, D), const(E, D), const(E, 1), const(RT, RT)],
        out_specs=[tok(D), tok(D), e_spec, w_spec, r_spec, pl.BlockSpec((tm // RT, E, LANES), lambda i: (i, 0, 0))],
        out_shape=[jax.ShapeDtypeStruct((N, D), F32), jax.ShapeDtypeStruct((N, D), BF16),
                   e_shape, w_shape, r_shape, jax.ShapeDtypeStruct((N // RT, E, LANES), I32)],
        compiler_params=_cparams(("arbitrary",), 56),
        name="merge",
    )(x2, o_ret, o_moba, cq, kmem, vmem, g_mix.reshape(1, D), w_gate, b_gate.reshape(1, N_BRANCH * D),
      w_br_ret.astype(BF16), w_br_moba.astype(BF16), w_br_mem.astype(BF16), w_out.astype(BF16),
      g_ffn.reshape(1, D), w_router.T, b_router.reshape(E, 1), tri)


def _segment_copies(seg_ref, lstart_ref, gstart_ref, tile, local_ref, slots_ref, sem, to_slots, fn):
    def per_expert(e, carry):
        idx = tile * N_EXPERTS + e
        size = seg_ref[idx]
        lstart = lstart_ref[idx]
        gstart = gstart_ref[idx]

        def copy_chunks(chunks, off):
            for chunk in chunks:
                take = (size & chunk) != 0
                lo = pl.ds(pl.multiple_of(lstart + off, SEG_ALIGN), chunk)
                gl = pl.ds(pl.multiple_of(gstart + off, SEG_ALIGN), chunk)
                src, dst = (local_ref.at[lo], slots_ref.at[gl]) if to_slots else (slots_ref.at[gl], local_ref.at[lo])

                @pl.when(take)
                def _():
                    fn(pltpu.make_async_copy(src, dst, sem))

                off = off + jnp.where(take, chunk, 0)

        large = size & -SEG_SMALL

        @pl.when(large != 0)
        def _():
            copy_chunks([c for c in SEG_CHUNKS if c >= SEG_SMALL], jnp.int32(0))

        copy_chunks([c for c in SEG_CHUNKS if c < SEG_SMALL], large)
        return carry

    lax.fori_loop(0, N_EXPERTS, per_expert, 0)


def _wait_tile(total_ref, tile, local_ref, slots_ref, sem, to_slots):
    total = total_ref[tile]
    for chunk in TOTAL_CHUNKS:
        lo, gl = local_ref.at[pl.ds(0, chunk)], slots_ref.at[pl.ds(0, chunk)]
        src, dst = (lo, gl) if to_slots else (gl, lo)

        @pl.when((total & chunk) != 0)
        def _():
            pltpu.make_async_copy(src, dst, sem).wait()


def _dispatch_kernel(seg_ref, lstart_ref, gstart_ref, total_ref, pad_end_ref, padded_ref, ld_ref, xt_ref, xb_ref,
                     ybuf_ref, zeros_ref, zsem, sems):
    T = zeros_ref.shape[0]
    i = pl.program_id(0)
    n = pl.num_programs(0)
    slot = i % 2

    @pl.when(i == 0)
    def _():
        zeros_ref[...] = jnp.zeros_like(zeros_ref)

        def fill(start):
            return pltpu.make_async_copy(zeros_ref, xb_ref.at[pl.ds(pl.multiple_of(start, T), T)], zsem)

        for e in range(N_EXPERTS):
            @pl.when(padded_ref[e] > 0)
            def _():
                fill(pad_end_ref[e] - T).start()
        for e in range(N_EXPERTS):
            @pl.when(padded_ref[e] > 0)
            def _():
                fill(pad_end_ref[e] - T).wait()
        first_unused = pad_end_ref[N_EXPERTS - 1] // T
        n_blocks = xb_ref.shape[0] // T
        lax.fori_loop(first_unused, n_blocks, lambda b, c: (fill(b * T).start(), c)[1], 0)
        lax.fori_loop(first_unused, n_blocks, lambda b, c: (fill(b * T).wait(), c)[1], 0)

    @pl.when(i >= 2)
    def _():
        _wait_tile(total_ref, i - 2, ybuf_ref.at[slot], xb_ref, sems.at[slot], True)

    L = ybuf_ref.shape[1]
    RT = xt_ref.shape[0]
    r_iota = lax.broadcasted_iota(I32, (L, RT), 0)
    ld = ld_ref[...]
    onehot = jnp.zeros((L, RT), F32)
    for k in range(TOP_K):
        onehot = jnp.where(r_iota == ld[k:k + 1, :], 1.0, onehot)
    onehot = onehot.astype(BF16)
    ybuf_ref[slot] = _dot(onehot, xt_ref[...])
    _segment_copies(seg_ref, lstart_ref, gstart_ref, i, ybuf_ref.at[slot], xb_ref, sems.at[slot], True,
                    lambda c: c.start())

    @pl.when(i == n - 1)
    def _():
        @pl.when(n >= 2)
        def _():
            _wait_tile(total_ref, i - 1, ybuf_ref.at[1 - slot], xb_ref, sems.at[1 - slot], True)

        _wait_tile(total_ref, i, ybuf_ref.at[slot], xb_ref, sems.at[slot], True)


def _dispatch(xt, ld, seg, lstart, gstart, totals, pad_ends, padded, R):
    N, D = xt.shape
    RT = ROUTE_T
    grid_spec = pltpu.PrefetchScalarGridSpec(
        num_scalar_prefetch=6,
        grid=(N // RT,),
        in_specs=[
            pl.BlockSpec((TOP_K, RT), lambda i, *_: (0, i)),
            pl.BlockSpec((RT, D), lambda i, *_: (i, 0)),
        ],
        out_specs=pl.BlockSpec(memory_space=pl.ANY),
        scratch_shapes=[pltpu.VMEM((2, LOCAL_ROWS, D), F32), pltpu.VMEM((MOE_T, D), F32),
                        pltpu.SemaphoreType.DMA(()), pltpu.SemaphoreType.DMA((2,))],
    )
    return pl.pallas_call(
        _dispatch_kernel,
        grid_spec=grid_spec,
        out_shape=jax.ShapeDtypeStruct((R, D), F32),
        compiler_params=_cparams(("arbitrary",), 48),
        name="dispatch",
    )(seg, lstart, gstart, totals, pad_ends, padded, ld, xt)


def _expert_kernel(start_ref, nblk_ref, b1_ref, b2_ref, perm_ref, w1_hbm, w2_hbm, xb_ref, yb_ref,
                   w1f_ref, w2f_ref, w1p_ref, w2b_ref, xbuf_ref, ybuf_ref, w_sems, in_sems, out_sems):
    e = pl.program_id(0)
    n_experts = pl.num_programs(0)
    ws = e % 2

    def w_copies(ex, slot):
        return (pltpu.make_async_copy(w1_hbm.at[ex], w1f_ref.at[slot], w_sems.at[slot, 0]),
                pltpu.make_async_copy(w2_hbm.at[ex], w2f_ref.at[slot], w_sems.at[slot, 1]))
    T = xbuf_ref.shape[1]
    G = 2 * LANES
    nb = nblk_ref[e]
    base = start_ref[e]

    def rows(b):
        return pl.ds(pl.multiple_of(base + b * T, T), T)

    def in_copy(b, slot):
        return pltpu.make_async_copy(xb_ref.at[rows(b)], xbuf_ref.at[slot], in_sems.at[slot])

    def out_copy(b, slot):
        return pltpu.make_async_copy(ybuf_ref.at[slot], yb_ref.at[rows(b)], out_sems.at[slot])

    for d in range(IN_DEPTH - 1):
        @pl.when(nb > d)
        def _():
            in_copy(d, d).start()

    @pl.when(e == 0)
    def _():
        for c in w_copies(0, 0):
            c.start()

    for c in w_copies(e, ws):
        c.wait()

    @pl.when(e + 1 < n_experts)
    def _():
        for c in w_copies(e + 1, 1 - ws):
            c.start(priority=1)

    @pl.when(nb > 0)
    def _():
        for g in range(w1p_ref.shape[1] // G):
            w = w1f_ref[ws, :, g * G:(g + 1) * G].astype(BF16)
            w1p_ref[:, g * G:(g + 1) * G] = _dot(w, perm_ref[...]).astype(BF16)
        w2b_ref[...] = w2f_ref[ws].astype(BF16)

        def block(b, carry):
            slot = b % 2
            islot = b % IN_DEPTH
            in_copy(b, islot).wait()

            @pl.when(b + IN_DEPTH - 1 < nb)
            def _():
                in_copy(b + IN_DEPTH - 1, (b + IN_DEPTH - 1) % IN_DEPTH).start()

            x = xbuf_ref[islot].astype(BF16)
            acts = []
            for g in range(w1p_ref.shape[1] // G):
                hg = _dot(x, w1p_ref[:, g * G:(g + 1) * G]) + b1_ref[0, :, g * G:(g + 1) * G]
                glu = jnp.minimum(hg[:, :LANES], SWIGLU_LIMIT)
                lin = jnp.clip(hg[:, LANES:], -SWIGLU_LIMIT, SWIGLU_LIMIT)
                acts.append((glu * jax.nn.sigmoid(SWIGLU_ALPHA * glu) * (lin + 1.0)).astype(BF16))
            y = _dot(jnp.concatenate(acts, axis=-1), w2b_ref[...]) + b2_ref[0]

            @pl.when(b >= 2)
            def _():
                out_copy(b - 2, slot).wait()

            ybuf_ref[slot] = y
            out_copy(b, slot).start()
            return carry

        lax.fori_loop(0, nb, block, 0)

        @pl.when(nb >= 2)
        def _():
            out_copy(nb - 2, nb % 2).wait()

        out_copy(nb - 1, (nb - 1) % 2).wait()

    @pl.when(e == pl.num_programs(0) - 1)
    def _():
        ybuf_ref[0] = jnp.zeros(ybuf_ref.shape[1:], ybuf_ref.dtype)
        first_unused = base // T + nb
        n_blocks = yb_ref.shape[0] // T

        def tail(b):
            return pltpu.make_async_copy(ybuf_ref.at[0], yb_ref.at[pl.ds(pl.multiple_of(b * T, T), T)], out_sems.at[0])

        lax.fori_loop(first_unused, n_blocks, lambda b, c: (tail(b).start(), c)[1], 0)
        lax.fori_loop(first_unused, n_blocks, lambda b, c: (tail(b).wait(), c)[1], 0)


def _experts(xb, region_start, region_blocks, w1, b1p, w2, b2):
    R, D = xb.shape
    E, _, F2 = w1.shape
    F = F2 // 2
    T = MOE_T
    G = 2 * LANES
    c = np.arange(G)
    src = np.where(c < LANES, 2 * c, 2 * (c - LANES) + 1)
    perm = jnp.asarray(np.arange(G)[:, None] == src[None, :], dtype=BF16)
    per_expert = lambda r, w: pl.BlockSpec((1, r, w), lambda e, *_: (e, 0, 0))
    grid_spec = pltpu.PrefetchScalarGridSpec(
        num_scalar_prefetch=2,
        grid=(E,),
        in_specs=[
            per_expert(1, F2), per_expert(1, D),
            pl.BlockSpec((G, G), lambda e, *_: (0, 0)),
            pl.BlockSpec(memory_space=pl.ANY), pl.BlockSpec(memory_space=pl.ANY), pl.BlockSpec(memory_space=pl.ANY),
        ],
        out_specs=pl.BlockSpec(memory_space=pl.ANY),
        scratch_shapes=[pltpu.VMEM((2, D, F2), F32), pltpu.VMEM((2, F, D), F32),
                        pltpu.VMEM((D, F2), BF16), pltpu.VMEM((F, D), BF16),
                        pltpu.VMEM((IN_DEPTH, T, D), F32), pltpu.VMEM((2, T, D), F32),
                        pltpu.SemaphoreType.DMA((2, 2)), pltpu.SemaphoreType.DMA((IN_DEPTH,)),
                        pltpu.SemaphoreType.DMA((2,))],
    )
    return pl.pallas_call(
        _expert_kernel,
        grid_spec=grid_spec,
        out_shape=jax.ShapeDtypeStruct((R, D), F32),
        compiler_params=_cparams(("arbitrary",), 56),
        name="experts",
    )(region_start, region_blocks, b1p, b2.reshape(E, 1, D), perm, w1, w2, xb)


def _combine_kernel(seg_ref, lstart_ref, gstart_ref, total_ref, ldt_ref, wt_ref, x1_ref, yb_ref, o_ref, ybuf_ref,
                    sems):
    i = pl.program_id(0)
    n = pl.num_programs(0)
    slot = i % 2

    def fetch(tile, s):
        _segment_copies(seg_ref, lstart_ref, gstart_ref, tile, ybuf_ref.at[s], yb_ref, sems.at[s], False,
                        lambda c: c.start())

    @pl.when(i == 0)
    def _():
        ybuf_ref[...] = jnp.zeros_like(ybuf_ref)
        fetch(i, slot)

    @pl.when(i + 1 < n)
    def _():
        fetch(i + 1, 1 - slot)

    _wait_tile(total_ref, i, ybuf_ref.at[slot], yb_ref, sems.at[slot], False)

    L = ybuf_ref.shape[1]
    RT = x1_ref.shape[0]
    c_iota = lax.broadcasted_iota(I32, (RT, L), 1)
    ldt = ldt_ref[...]
    w = wt_ref[...]
    w_hi = w.astype(BF16).astype(F32)
    w_lo = w - w_hi
    g_hi = jnp.zeros((RT, L), F32)
    g_lo = jnp.zeros((RT, L), F32)
    for k in range(TOP_K):
        hit = c_iota == ldt[:, k:k + 1]
        g_hi = jnp.where(hit, w_hi[:, k:k + 1], g_hi)
        g_lo = jnp.where(hit, w_lo[:, k:k + 1], g_lo)
    y = ybuf_ref[slot].astype(BF16)
    o_ref[...] = x1_ref[...] + (_dot(g_hi.astype(BF16), y) + _dot(g_lo.astype(BF16), y))


def _combine(ldt, wts_t, x1, yb, seg, lstart, gstart, totals):
    N, D = x1.shape
    RT = ROUTE_T
    grid_spec = pltpu.PrefetchScalarGridSpec(
        num_scalar_prefetch=4,
        grid=(N // RT,),
        in_specs=[
            pl.BlockSpec((RT, TOP_K), lambda i, *_: (i, 0)),
            pl.BlockSpec((RT, TOP_K), lambda i, *_: (i, 0)),
            pl.BlockSpec((RT, D), lambda i, *_: (i, 0)),
            pl.BlockSpec(memory_space=pl.ANY),
        ],
        out_specs=pl.BlockSpec((RT, D), lambda i, *_: (i, 0)),
        scratch_shapes=[pltpu.VMEM((2, LOCAL_ROWS, D), F32), pltpu.SemaphoreType.DMA((2,))],
    )
    return pl.pallas_call(
        _combine_kernel,
        grid_spec=grid_spec,
        out_shape=jax.ShapeDtypeStruct((N, D), F32),
        compiler_params=_cparams(("arbitrary",), 48),
        name="combine",
    )(seg, lstart, gstart, totals, ldt, wts_t, x1, yb)


def _layer(x, mem, g_mix, w_in, b_gate, g_ret_out, g_moba_q, g_moba_k, g_mem, w_mem_kv, g_mem_q, g_mem_k,
           w_br_ret, w_br_moba, w_br_mem, w_out, g_ffn, w_router, b_router, w_mlp1, b_mlp1, w_mlp2, b_mlp2):
    B, S, D = x.shape
    N = B * S
    x2 = x.reshape(N, D)
    slopes_np = np.exp2(-8.0 * (np.arange(MOBA_HEADS, dtype=np.float64) + 1.0) / MOBA_HEADS)
    assert all(float(np.log2(s)).is_integer() for s in slopes_np)
    slopes = tuple(float(s) for s in slopes_np)

    w_mix = w_in[:, :MIX_W].astype(BF16)
    w_gate = w_in[:, MIX_W:].astype(BF16)
    kmem, vmem = _memkv(mem, g_mem, w_mem_kv, g_mem_k)
    rq, rk, rv, rg, mq, mk, mv, cq, km = _inproj(x2, B, S, g_mix, w_mix, g_moba_q, g_moba_k, g_mem_q, slopes)
    o_ret = _retention(rq, rk, rv, rg, g_ret_out)
    o_moba = _moba(mq, mk, mv, km, jnp.asarray(slopes, F32)).reshape(N, MOBA_W)
    x1, xt, eidx, wts, rank, cnt = _merge(x2, S, o_ret, o_moba, cq, kmem, vmem, g_mix, w_gate, b_gate,
                                          w_br_ret, w_br_moba, w_br_mem, w_out, g_ffn, w_router, b_router)
    T = MOE_T
    tcnt = cnt[:, :, 0]
    seg = ((tcnt + SEG_ALIGN - 1) // SEG_ALIGN) * SEG_ALIGN
    region_rows = jnp.sum(seg, axis=0)
    padded = ((region_rows + T - 1) // T) * T
    pad_ends = jnp.cumsum(padded).astype(I32)
    pad_starts = pad_ends - padded
    gstart = pad_starts[None, :] + jnp.cumsum(seg, axis=0) - seg
    lstart = jnp.cumsum(seg, axis=1) - seg
    lstart_tok = jnp.broadcast_to(lstart[:, None, :], (N // ROUTE_T, ROUTE_T, N_EXPERTS)).reshape(N, N_EXPERTS)
    onehot = eidx[:, :, None] == jnp.arange(N_EXPERTS, dtype=I32)[None, None, :]
    ld = jnp.sum(jnp.where(onehot, lstart_tok[None], 0), axis=-1) + rank
    NB = -(-(N // ROUTE_T * LOCAL_ROWS) // T) + N_EXPERTS
    seg_f, lstart_f, gstart_f = (a.reshape(-1).astype(I32) for a in (seg, lstart, gstart))
    totals = jnp.sum(seg, axis=1).astype(I32)

    xb = _dispatch(xt, ld, seg_f, lstart_f, gstart_f, totals, pad_ends, padded.astype(I32), NB * T)
    F2 = w_mlp1.shape[-1]
    b1p = b_mlp1.reshape(N_EXPERTS, F2 // (2 * LANES), LANES, 2).transpose(0, 1, 3, 2).reshape(N_EXPERTS, 1, F2)
    yb = _experts(xb, pad_starts.astype(I32), (padded // T).astype(I32), w_mlp1, b1p, w_mlp2, b_mlp2)
    out = _combine(ld.T, wts.T, x1, yb, seg_f, lstart_f, gstart_f, totals)
    return out.reshape(B, S, D)


def kernel(x, mem, g_mix, w_in, b_gate, g_ret_out, g_moba_q, g_moba_k, g_mem, w_mem_kv, g_mem_q, g_mem_k, w_br_ret, w_br_moba, w_br_mem, w_out, g_ffn, w_router, b_router, w_mlp1, b_mlp1, w_mlp2, b_mlp2):
    for l in range(g_mix.shape[0]):
        x = _layer(x, mem, g_mix[l], w_in[l], b_gate[l], g_ret_out[l], g_moba_q[l], g_moba_k[l], g_mem[l],
                   w_mem_kv[l], g_mem_q[l], g_mem_k[l], w_br_ret[l], w_br_moba[l], w_br_mem[l], w_out[l],
                   g_ffn[l], w_router[l], b_router[l], w_mlp1[l], b_mlp1[l], w_mlp2[l], b_mlp2[l])
    return x
```

```python
import functools

import jax
import jax.numpy as jnp
import numpy as np
from jax import lax
from jax.experimental import pallas as pl
from jax.experimental.pallas import tpu as pltpu

F32 = jnp.float32
BF16 = jnp.bfloat16
I32 = jnp.int32

EPS = 1e-5
NEG = -1e30

RET_HEADS = 4
RET_DK = 64
RET_DV = 128
RET_CHUNK = 128
MOBA_HEADS = 8
MOBA_HD = 64
MOBA_BLOCK = 256
MOBA_TOPK = 3
MEM_HEADS = 4
MEM_HD = 128
N_BRANCH = 3
N_EXPERTS = 32
TOP_K = 4
SWIGLU_LIMIT = 7.0
SWIGLU_ALPHA = 1.702

RET_Q = RET_HEADS * RET_DK
RET_V = RET_HEADS * RET_DV
MOBA_W = MOBA_HEADS * MOBA_HD
MEM_W = MEM_HEADS * MEM_HD
MIX_W = 2 * RET_Q + 2 * RET_V + 3 * MOBA_W + MEM_W

LANES = 128
V7X_VMEM_MIB = 64
VMEM_LARGE_MIB = V7X_VMEM_MIB - 8
VMEM_MEDIUM_MIB = V7X_VMEM_MIB * 3 // 4
VMEM_SMALL_MIB = V7X_VMEM_MIB // 2
MOBA_PAIRS = MOBA_HEADS // 2
MOBA_HEADS_PER_STEP = 8
BIAS_LANE0 = MOBA_HD
ONE_LANE = 80
VSUM_LANE = MOBA_HD

MOE_T = 256
IN_DEPTH = 2
TM_PROJ = 512
TM_MERGE = 512
TQ_RET = 512
ROUTE_T = 256
SEG_ALIGN = 8
LOCAL_ROWS = -(-(ROUTE_T * TOP_K + N_EXPERTS * (SEG_ALIGN - 1)) // LANES) * LANES
SEG_CHUNKS = tuple(2 ** p for p in range(ROUTE_T.bit_length() - 1, SEG_ALIGN.bit_length() - 2, -1))
SEG_SMALL = 64
TOTAL_CHUNKS = tuple(2 ** p for p in range(LOCAL_ROWS.bit_length() - 1, SEG_ALIGN.bit_length() - 2, -1))

_NT = (((1,), (1,)), ((), ()))
_TN = (((0,), (0,)), ((), ()))


def _rms(x, g):
    return x * lax.rsqrt(jnp.mean(x * x, axis=-1, keepdims=True) + EPS) * g


def _dot(a, b):
    return jnp.dot(a, b, preferred_element_type=F32)


def _cparams(sem, vmem_mb):
    return pltpu.CompilerParams(dimension_semantics=sem, vmem_limit_bytes=vmem_mb * 1024 * 1024)


def _memkv_kernel(mem_ref, g_ref, w_ref, gk_ref, k_ref, v_ref):
    m = _rms(mem_ref[0], g_ref[...]).astype(BF16)
    kv = _dot(m, w_ref[...])
    ks = [_rms(kv[:, h * MEM_HD:(h + 1) * MEM_HD], gk_ref[...]) for h in range(MEM_HEADS)]
    k_ref[0] = jnp.concatenate(ks, axis=-1).astype(BF16)
    v_ref[0] = kv[:, MEM_W:].astype(BF16)


def _memkv(mem, g_mem, w_mem_kv, g_mem_k):
    B, M, D = mem.shape
    return pl.pallas_call(
        _memkv_kernel,
        grid=(B,),
        in_specs=[
            pl.BlockSpec((1, M, D), lambda b: (b, 0, 0)),
            pl.BlockSpec((1, D), lambda b: (0, 0)),
            pl.BlockSpec((D, 2 * MEM_W), lambda b: (0, 0)),
            pl.BlockSpec((1, MEM_HD), lambda b: (0, 0)),
        ],
        out_specs=[
            pl.BlockSpec((1, M, MEM_W), lambda b: (b, 0, 0)),
            pl.BlockSpec((1, M, MEM_W), lambda b: (b, 0, 0)),
        ],
        out_shape=[jax.ShapeDtypeStruct((B, M, MEM_W), BF16)] * 2,
        compiler_params=_cparams(("arbitrary",), VMEM_SMALL_MIB),
        name="memkv",
    )(mem, g_mem.reshape(1, D), w_mem_kv.astype(BF16), g_mem_k.reshape(1, MEM_HD))


def _head_pair_norm(a2, g2, lane):
    sq = a2 * a2
    lo = lane < MOBA_HD
    ss_lo = jnp.sum(jnp.where(lo, sq, 0.0), axis=-1, keepdims=True)
    ss_hi = jnp.sum(jnp.where(lo, 0.0, sq), axis=-1, keepdims=True)
    inv = jnp.where(lo, lax.rsqrt(ss_lo / MOBA_HD + EPS), lax.rsqrt(ss_hi / MOBA_HD + EPS))
    return a2 * inv * g2


def _inproj_kernel(slopes, seq_tiles, x_ref, gmix_ref, w_ref, gq_ref, gk_ref, gc_ref,
                   rq_ref, rk_ref, rv_ref, rg_ref, mq_ref, mk_ref, mv_ref, cq_ref, kmean_ref):
    tm = x_ref.shape[0]
    blocks_per_tile = tm // MOBA_BLOCK
    h = _rms(x_ref[...], gmix_ref[...]).astype(BF16)
    col = [0]

    def proj(width):
        a = _dot(h, w_ref[:, col[0]:col[0] + width])
        col[0] += width
        return a

    a = proj(2 * RET_Q)
    for hh in range(RET_HEADS):
        rq_ref[0, hh] = a[:, hh * RET_DK:(hh + 1) * RET_DK].astype(BF16)
        rk_ref[0, hh] = (a[:, RET_Q + hh * RET_DK:RET_Q + (hh + 1) * RET_DK] * (RET_DK ** -0.5)).astype(BF16)
    rv_ref[...] = proj(RET_V).astype(BF16)
    rg_ref[...] = proj(RET_V).astype(BF16)

    lane = lax.broadcasted_iota(I32, (tm, LANES), 1)
    row = lax.broadcasted_iota(I32, (tm, LANES), 0)
    lo = lane < MOBA_HD
    q_tail = jnp.where(lane == ONE_LANE, 1.0, 0.0)
    a = proj(MOBA_W)
    for p in range(MOBA_PAIRS):
        n2 = _head_pair_norm(a[:, p * LANES:(p + 1) * LANES], gq_ref[...], lane) * (MOBA_HD ** -0.5)
        mq_ref[0, 2 * p] = jnp.where(lo, n2, q_tail).astype(BF16)
        mq_ref[0, 2 * p + 1] = jnp.where(lo, pltpu.roll(n2, MOBA_HD, 1), q_tail).astype(BF16)
    blk = (pl.program_id(0) % seq_tiles) * blocks_per_tile + row // MOBA_BLOCK
    onehot_tail = jnp.where(lane == BIAS_LANE0 + blk, 1.0, 0.0)
    off = (row % MOBA_BLOCK).astype(F32)
    a = proj(MOBA_W)
    for p in range(MOBA_PAIRS):
        n2 = _head_pair_norm(a[:, p * LANES:(p + 1) * LANES], gk_ref[...], lane)
        for j in range(blocks_per_tile):
            kmean_ref[0, 0, p, j:j + 1, :] = jnp.mean(n2[j * MOBA_BLOCK:(j + 1) * MOBA_BLOCK], axis=0, keepdims=True)
        for s, src in ((0, n2), (1, pltpu.roll(n2, MOBA_HD, 1))):
            tail = jnp.where(lane == ONE_LANE, slopes[2 * p + s] * off, onehot_tail)
            mk_ref[0, 2 * p + s] = jnp.where(lo, src, tail).astype(BF16)
    v_tail = jnp.where(lane == VSUM_LANE, 1.0, 0.0)
    a = proj(MOBA_W)
    for p in range(MOBA_PAIRS):
        a2 = a[:, p * LANES:(p + 1) * LANES]
        mv_ref[0, 2 * p] = jnp.where(lo, a2, v_tail).astype(BF16)
        mv_ref[0, 2 * p + 1] = jnp.where(lo, pltpu.roll(a2, MOBA_HD, 1), v_tail).astype(BF16)
    a = proj(MEM_W)
    cq = [_rms(a[:, hh * MEM_HD:(hh + 1) * MEM_HD], gc_ref[...]) for hh in range(MEM_HEADS)]
    cq_ref[...] = jnp.concatenate(cq, axis=-1).astype(BF16)


def _inproj(x2, B, S, g_mix, w_mix, g_moba_q, g_moba_k, g_mem_q, slopes):
    N, D = x2.shape
    tm = TM_PROJ
    nS = S // tm
    bpt = tm // MOBA_BLOCK
    tok = lambda i: (i, 0)
    headmaj = lambda i: (i // nS, 0, i % nS, 0)
    g2 = lambda g: jnp.concatenate([g, g]).reshape(1, LANES)
    outs = pl.pallas_call(
        functools.partial(_inproj_kernel, slopes, nS),
        grid=(N // tm,),
        in_specs=[
            pl.BlockSpec((tm, D), tok),
            pl.BlockSpec((1, D), lambda i: (0, 0)),
            pl.BlockSpec((D, MIX_W), lambda i: (0, 0)),
            pl.BlockSpec((1, LANES), lambda i: (0, 0)),
            pl.BlockSpec((1, LANES), lambda i: (0, 0)),
            pl.BlockSpec((1, MEM_HD), lambda i: (0, 0)),
        ],
        out_specs=[
            pl.BlockSpec((1, RET_HEADS, tm, RET_DK), headmaj),
            pl.BlockSpec((1, RET_HEADS, tm, RET_DK), headmaj),
            pl.BlockSpec((tm, RET_V), tok),
            pl.BlockSpec((tm, RET_V), tok),
            pl.BlockSpec((1, MOBA_HEADS, tm, LANES), headmaj),
            pl.BlockSpec((1, MOBA_HEADS, tm, LANES), headmaj),
            pl.BlockSpec((1, MOBA_HEADS, tm, LANES), headmaj),
            pl.BlockSpec((tm, MEM_W), tok),
            pl.BlockSpec((1, 1, MOBA_PAIRS, bpt, LANES), lambda i: (i // nS, i % nS, 0, 0, 0)),
        ],
        out_shape=[
            jax.ShapeDtypeStruct((B, RET_HEADS, S, RET_DK), BF16),
            jax.ShapeDtypeStruct((B, RET_HEADS, S, RET_DK), BF16),
            jax.ShapeDtypeStruct((N, RET_V), BF16),
            jax.ShapeDtypeStruct((N, RET_V), BF16),
            jax.ShapeDtypeStruct((B, MOBA_HEADS, S, LANES), BF16),
            jax.ShapeDtypeStruct((B, MOBA_HEADS, S, LANES), BF16),
            jax.ShapeDtypeStruct((B, MOBA_HEADS, S, LANES), BF16),
            jax.ShapeDtypeStruct((N, MEM_W), BF16),
            jax.ShapeDtypeStruct((B, nS, MOBA_PAIRS, bpt, LANES), F32),
        ],
        compiler_params=_cparams(("arbitrary",), VMEM_LARGE_MIB),
        name="inproj",
    )(x2, g_mix.reshape(1, D), w_mix, g2(g_moba_q), g2(g_moba_k), g_mem_q.reshape(1, MEM_HD))
    rq, rk, rv, rg, mq, mk, mv, cq, kmean_pairs = outs
    km = kmean_pairs.reshape(B, nS, MOBA_PAIRS, bpt, 2, MOBA_HD).transpose(0, 2, 4, 1, 3, 5)
    km = km.reshape(B, MOBA_HEADS, nS * bpt, MOBA_HD)
    km = jnp.pad(km, ((0, 0), (0, 0), (0, 0), (0, LANES - MOBA_HD)))
    return rq, rk, rv, rg, mq, mk, mv, cq, km


def _retention_kernel(q_ref, k_ref, v_ref, rg_ref, din_ref, dq_ref, dk_ref, dc_ref, g_ref, o_ref, state_ref):
    @pl.when(pl.program_id(1) == 0)
    def _():
        state_ref[...] = jnp.zeros_like(state_ref)

    C = RET_CHUNK
    for h in range(q_ref.shape[1]):
        cols = slice(h * RET_DV, (h + 1) * RET_DV)
        state = state_ref[h]
        for c in range(q_ref.shape[2] // C):
            rows = slice(c * C, (c + 1) * C)
            q = q_ref[0, h, rows, :]
            k = k_ref[0, h, rows, :]
            v = v_ref[rows, cols]
            scores = lax.dot_general(q, k, _NT, preferred_element_type=F32) * din_ref[h]
            intra = _dot(scores.astype(BF16), v)
            cross = _dot(q, state.astype(BF16)) * dq_ref[h]
            kd = (k.astype(F32) * dk_ref[h]).astype(BF16)
            state = dc_ref[h] * state + lax.dot_general(kd, v, _TN, preferred_element_type=F32)
            o = _rms(intra + cross, g_ref[...])
            o_ref[rows, cols] = (o * jax.nn.silu(rg_ref[rows, cols].astype(F32))).astype(BF16)
        state_ref[h] = state


def _retention_decays():
    H, C = RET_HEADS, RET_CHUNK
    log_g = jnp.log1p(-jnp.exp2(-5.0 - jnp.arange(H, dtype=F32)))
    i = jnp.arange(C, dtype=F32)
    diff = i[:, None] - i[None, :]
    decay_in = jnp.where(diff >= 0, jnp.exp(jnp.maximum(diff, 0.0)[None] * log_g[:, None, None]), 0.0)
    decay_k = jnp.exp((C - 1 - i)[None, :] * log_g[:, None])
    decay_q = jnp.exp((i + 1)[None, :] * log_g[:, None])
    decay_chunk = jnp.exp(C * log_g)
    dq = jnp.broadcast_to(decay_q[:, :, None], (H, C, RET_DV))
    dk = jnp.broadcast_to(decay_k[:, :, None], (H, C, RET_DK))
    dc = jnp.broadcast_to(decay_chunk[:, None, None], (H, RET_DK, RET_DV))
    return decay_in, dq, dk, dc


def _retention(rq, rk, rv, rg, g_ret_out):
    B, H, S, dk = rq.shape
    tq = TQ_RET
    nT = S // tq
    C = RET_CHUNK
    din, dq, dk_, dc = _retention_decays()
    qk_spec = pl.BlockSpec((1, H, tq, dk), lambda b, t: (b, 0, t, 0))
    tok_spec = pl.BlockSpec((tq, RET_V), lambda b, t: (b * nT + t, 0))
    const = lambda r, c: pl.BlockSpec((H, r, c), lambda b, t: (0, 0, 0))
    return pl.pallas_call(
        _retention_kernel,
        grid=(B, nT),
        in_specs=[qk_spec, qk_spec, tok_spec, tok_spec,
                  const(C, C), const(C, RET_DV), const(C, RET_DK), const(RET_DK, RET_DV),
                  pl.BlockSpec((1, RET_DV), lambda b, t: (0, 0))],
        out_specs=tok_spec,
        out_shape=jax.ShapeDtypeStruct((B * S, RET_V), BF16),
        scratch_shapes=[pltpu.VMEM((H, RET_DK, RET_DV), F32)],
        compiler_params=_cparams(("arbitrary", "arbitrary"), VMEM_SMALL_MIB),
        name="retention",
    )(rq, rk, rv, rg, din, dq, dk_, dc, g_ret_out.reshape(1, RET_DV))


def _moba_kernel(slopes_ref, q_ref, k_ref, v_ref, km_ref, shift_ref, mask_ref, o_ref):
    NH = q_ref.shape[1]
    head0 = pl.program_id(1) * NH
    i0 = pl.program_id(2) * 2
    BS = MOBA_BLOCK
    nb = km_ref.shape[2]
    heads = range(NH)
    n_iota = lax.broadcasted_iota(I32, (nb, 2 * BS), 0)
    q_blk = i0 + lax.broadcasted_iota(I32, (nb, 2 * BS), 1) // BS
    past = n_iota < q_blk
    qa = [q_ref[0, s] for s in heads]
    q_aug = []
    for s in heads:
        km = km_ref[0, s]
        hi = km.astype(BF16)
        mid = (km - hi.astype(F32)).astype(BF16)
        lo = (km - hi.astype(F32) - mid.astype(F32)).astype(BF16)
        g3 = lax.dot_general(jnp.concatenate([hi, mid, lo], axis=0), qa[s], _NT, preferred_element_type=F32)
        gate = (g3[:nb] + g3[nb:2 * nb]) + g3[2 * nb:]
        g = jnp.where(past, gate, -jnp.inf)
        rank = jnp.zeros((nb, 2 * BS), I32)
        for m in range(nb - 1):
            gm = g[m:m + 1, :]
            rank = rank + jnp.where(gm > g, 1, jnp.where(gm == g, jnp.where(m < n_iota, 1, 0), 0))
        sel = jnp.where(past, rank, MOBA_TOPK) < MOBA_TOPK
        bias = jnp.where(sel, (n_iota - q_blk).astype(F32) * (slopes_ref[head0 + s] * BS), NEG).astype(BF16)
        placed = lax.dot_general(bias, shift_ref[...], _TN, preferred_element_type=F32)
        q_aug.append((qa[s].astype(F32) + placed).astype(BF16))

    def rows_of(j):
        return pl.ds(pl.multiple_of(j * BS, BS), BS)

    m_lo, m_hi, acc_lo, acc_hi = [], [], [], []
    for s in heads:
        q_mix = jnp.concatenate([qa[s][:BS], q_aug[s][BS:]], axis=0)
        sc = lax.dot_general(q_mix, k_ref[0, s, rows_of(i0), :], _NT, preferred_element_type=F32) + mask_ref[...]
        m = jnp.max(sc, axis=-1, keepdims=True)
        acc = _dot(jnp.exp(sc - m).astype(BF16), v_ref[0, s, rows_of(i0), :])
        m_lo.append(m[:BS])
        acc_lo.append(acc[:BS])
        sc1 = lax.dot_general(qa[s][BS:], k_ref[0, s, rows_of(i0 + 1), :], _NT, preferred_element_type=F32)
        sc1 = sc1 + mask_ref[:BS, :]
        m1 = jnp.maximum(m[BS:], jnp.max(sc1, axis=-1, keepdims=True))
        p1 = jnp.exp(sc1 - m1).astype(BF16)
        m_hi.append(m1)
        acc_hi.append(jnp.exp(m[BS:] - m1) * acc[BS:] + _dot(p1, v_ref[0, s, rows_of(i0 + 1), :]))

    def body(j, carry):
        ms_lo, ms_hi, accs_lo, accs_hi = carry
        out = ([], [], [], [])
        for s in heads:
            sj = lax.dot_general(q_aug[s], k_ref[0, s, rows_of(j), :], _NT, preferred_element_type=F32)
            ps, alphas, ms = [], [], []
            for half, m_prev in ((slice(0, BS), ms_lo[s]), (slice(BS, 2 * BS), ms_hi[s])):
                m_new = jnp.maximum(m_prev, jnp.max(sj[half], axis=-1, keepdims=True))
                ps.append(jnp.exp(sj[half] - m_new).astype(BF16))
                alphas.append(jnp.exp(m_prev - m_new))
                ms.append(m_new)
            pv = _dot(jnp.concatenate(ps, axis=0), v_ref[0, s, rows_of(j), :])
            out[0].append(ms[0])
            out[1].append(ms[1])
            out[2].append(alphas[0] * accs_lo[s] + pv[:BS])
            out[3].append(alphas[1] * accs_hi[s] + pv[BS:])
        return tuple(tuple(o) for o in out)

    _, _, accs_lo, accs_hi = lax.fori_loop(0, i0 // 2, lambda j, c: body(2 * j + 1, body(2 * j, c)),
                                           (tuple(m_lo), tuple(m_hi), tuple(acc_lo), tuple(acc_hi)))
    lane = lax.broadcasted_iota(I32, (BS, LANES), 1)
    for t, accs in enumerate((accs_lo, accs_hi)):
        outs = [acc / acc[:, VSUM_LANE:VSUM_LANE + 1] for acc in accs]
        pairs = [jnp.where(lane < MOBA_HD, outs[2 * p], pltpu.roll(outs[2 * p + 1], MOBA_HD, 1))
                 for p in range(NH // 2)]
        o_ref[0, t * BS:(t + 1) * BS, :] = jnp.concatenate(pairs, axis=-1).astype(BF16)


def _moba(mq, mk, mv, km, slopes):
    B, H, S, _ = mq.shape
    BS = MOBA_BLOCK
    NH = MOBA_HEADS_PER_STEP
    nq = S // BS
    nb = km.shape[2]
    assert BIAS_LANE0 + nb <= ONE_LANE and nq % 2 == 0
    shift = (jnp.arange(nb)[:, None] + BIAS_LANE0 == jnp.arange(LANES)[None, :]).astype(BF16)
    causal = jnp.where(jnp.arange(BS)[:, None] >= jnp.arange(BS)[None, :], 0.0, NEG).astype(F32)
    mask = jnp.concatenate([causal, jnp.zeros((BS, BS), F32)], axis=0)
    per_head = lambda rows: pl.BlockSpec((1, NH, rows, LANES), lambda b, p, i, sl: (b, p, 0, 0))
    grid_spec = pltpu.PrefetchScalarGridSpec(
        num_scalar_prefetch=1,
        grid=(B, H // NH, nq // 2),
        in_specs=[
            pl.BlockSpec((1, NH, 2 * BS, LANES), lambda b, p, i, sl: (b, p, i, 0)),
            per_head(S), per_head(S), per_head(nb),
            pl.BlockSpec((nb, LANES), lambda b, p, i, sl: (0, 0)),
            pl.BlockSpec((2 * BS, BS), lambda b, p, i, sl: (0, 0)),
        ],
        out_specs=pl.BlockSpec((1, 2 * BS, NH * MOBA_HD), lambda b, p, i, sl: (b, i, p)),
    )
    return pl.pallas_call(
        _moba_kernel,
        grid_spec=grid_spec,
        out_shape=jax.ShapeDtypeStruct((B, S, MOBA_W), BF16),
        compiler_params=_cparams(("arbitrary", "arbitrary", "arbitrary"), VMEM_LARGE_MIB),
        name="moba",
    )(slopes, mq, mk, mv, km, shift, mask)


def _merge_kernel(x_ref, oret_ref, omoba_ref, cq_ref, kmem_ref, vmem_ref, gmix_ref, wg_ref, bg_ref,
                  wbr_ref, wbm_ref, wbc_ref, wout_ref, gffn_ref, wr_ref, br_ref, tri_ref,
                  x1_ref, xt_ref, eidx_ref, wts_ref, rank_ref, cnt_ref):
    D = x_ref.shape[1]
    tm = x_ref.shape[0]

    x = x_ref[...]
    h = _rms(x, gmix_ref[...]).astype(BF16)
    cq = cq_ref[...]
    om = []
    for hh in range(MEM_HEADS):
        cols = slice(hh * MEM_HD, (hh + 1) * MEM_HD)
        sc = lax.dot_general(cq[:, cols], kmem_ref[0, :, cols], _NT, preferred_element_type=F32) * (MEM_HD ** -0.5)
        sc = sc - jnp.max(sc, axis=-1, keepdims=True)
        p = jnp.exp(sc)
        p = p / jnp.sum(p, axis=-1, keepdims=True)
        om.append(_dot(p.astype(BF16), vmem_ref[0, :, cols]))
    omem = jnp.concatenate(om, axis=-1).astype(BF16)
    y = None
    for br, (o, w_ref) in enumerate(((oret_ref[...], wbr_ref), (omoba_ref[...], wbm_ref), (omem, wbc_ref))):
        gl = _dot(h, wg_ref[:, br * D:(br + 1) * D]) + bg_ref[:, br * D:(br + 1) * D]
        term = jax.nn.sigmoid(gl) * _dot(o, w_ref[...])
        y = term if y is None else y + term
    x1 = x + _dot(y.astype(BF16), wout_ref[...])
    x1_ref[...] = x1
    xt = _rms(x1, gffn_ref[...])
    wr = wr_ref[...]
    E = wr.shape[0]
    w_hi = wr.astype(BF16)
    w_mid = (wr - w_hi.astype(F32)).astype(BF16)
    w_lo = (wr - w_hi.astype(F32) - w_mid.astype(F32)).astype(BF16)
    xt_hi = xt.astype(BF16)
    xt_ref[...] = xt_hi
    xt_lo = (xt - xt_hi.astype(F32)).astype(BF16)
    a = lax.dot_general(jnp.concatenate([w_hi, w_mid, w_lo], axis=0), xt_hi, _NT, preferred_element_type=F32)
    b = lax.dot_general(jnp.concatenate([w_hi, w_mid], axis=0), xt_lo, _NT, preferred_element_type=F32)
    logits = (a[:E] + (a[E:2 * E] + b[:E])) + (a[2 * E:] + b[E:]) + br_ref[...]
    e_iota = lax.broadcasted_iota(I32, (E, tm), 0)
    l = logits
    vals, hots = [], []
    for k in range(TOP_K):
        m = jnp.max(l, axis=0, keepdims=True)
        idx = jnp.min(jnp.where(l == m, e_iota, E), axis=0, keepdims=True)
        hot = e_iota == idx
        l = jnp.where(hot, -jnp.inf, l)
        vals.append(m)
        hots.append(hot)
        eidx_ref[k:k + 1, :] = idx
    ex = [jnp.exp(v - vals[0]) for v in vals]
    den = ex[0]
    for k in range(1, TOP_K):
        den = den + ex[k]
    chosen = jnp.zeros((E, tm), F32)
    for k in range(TOP_K):
        wts_ref[k:k + 1, :] = ex[k] / den
        chosen = chosen + jnp.where(hots[k], 1.0, 0.0)
    RT = tri_ref.shape[0]
    for t in range(tm // RT):
        cols = slice(t * RT, (t + 1) * RT)
        prefix = _dot(chosen[:, cols].astype(BF16), tri_ref[...])
        for k in range(TOP_K):
            rank_ref[k:k + 1, cols] = jnp.sum(jnp.where(hots[k][:, cols], prefix, 0.0), axis=0, keepdims=True).astype(I32)
        counts = jnp.sum(chosen[:, cols], axis=1, keepdims=True)
        cnt_ref[t] = jnp.broadcast_to(counts, cnt_ref.shape[1:]).astype(I32)


def _merge(x2, S, o_ret, o_moba, cq, kmem, vmem, g_mix, w_gate, b_gate, w_br_ret, w_br_moba, w_br_mem, w_out,
           g_ffn, w_router, b_router):
    N, D = x2.shape
    tm = TM_MERGE
    nS = S // tm
    M = kmem.shape[1]
    E = w_router.shape[1]
    tok = lambda w: pl.BlockSpec((tm, w), lambda i: (i, 0))
    const = lambda r, c: pl.BlockSpec((r, c), lambda i: (0, 0), pipeline_mode=pl.Buffered(1))
    mem_spec = pl.BlockSpec((1, M, MEM_W), lambda i: (i // nS, 0, 0))
    lanes_tok = lambda r, dt: (pl.BlockSpec((r, tm), lambda i: (0, i)), jax.ShapeDtypeStruct((r, N), dt))
    RT = ROUTE_T
    tri = (jnp.arange(RT)[:, None] < jnp.arange(RT)[None, :]).astype(BF16)
    e_spec, e_shape = lanes_tok(TOP_K, I32)
    w_spec, w_shape = lanes_tok(TOP_K, F32)
    r_spec, r_shape = lanes_tok(TOP_K, I32)
    return pl.pallas_call(
        _merge_kernel,
        grid=(N // tm,),
        in_specs=[tok(D), tok(RET_V), tok(MOBA_W), tok(MEM_W), mem_spec, mem_spec,
                  const(1, D), const(D, N_BRANCH * D), const(1, N_BRANCH * D),
                  const(RET_V, D), const(MOBA_W, D), const(MEM_W, D), const(D, D),
                  const(1, D), const(E, D), const(E, 1), const(RT, RT)],
        out_specs=[tok(D), tok(D), e_spec, w_spec, r_spec, pl.BlockSpec((tm // RT, E, LANES), lambda i: (i, 0, 0))],
        out_shape=[jax.ShapeDtypeStruct((N, D), F32), jax.ShapeDtypeStruct((N, D), BF16),
                   e_shape, w_shape, r_shape, jax.ShapeDtypeStruct((N // RT, E, LANES), I32)],
        compiler_params=_cparams(("arbitrary",), VMEM_LARGE_MIB),
        name="merge",
    )(x2, o_ret, o_moba, cq, kmem, vmem, g_mix.reshape(1, D), w_gate, b_gate.reshape(1, N_BRANCH * D),
      w_br_ret.astype(BF16), w_br_moba.astype(BF16), w_br_mem.astype(BF16), w_out.astype(BF16),
      g_ffn.reshape(1, D), w_router.T, b_router.reshape(E, 1), tri)


def _segment_copies(seg_ref, lstart_ref, gstart_ref, tile, local_ref, slots_ref, sem, to_slots, fn):
    def per_expert(e, carry):
        idx = tile * N_EXPERTS + e
        size = seg_ref[idx]
        lstart = lstart_ref[idx]
        gstart = gstart_ref[idx]

        def copy_chunks(chunks, off):
            for chunk in chunks:
                take = (size & chunk) != 0
                lo = pl.ds(pl.multiple_of(lstart + off, SEG_ALIGN), chunk)
                gl = pl.ds(pl.multiple_of(gstart + off, SEG_ALIGN), chunk)
                src, dst = (local_ref.at[lo], slots_ref.at[gl]) if to_slots else (slots_ref.at[gl], local_ref.at[lo])

                @pl.when(take)
                def _():
                    fn(pltpu.make_async_copy(src, dst, sem))

                off = off + jnp.where(take, chunk, 0)

        large = size & -SEG_SMALL

        @pl.when(large != 0)
        def _():
            copy_chunks([c for c in SEG_CHUNKS if c >= SEG_SMALL], jnp.int32(0))

        copy_chunks([c for c in SEG_CHUNKS if c < SEG_SMALL], large)
        return carry

    lax.fori_loop(0, N_EXPERTS, per_expert, 0)


def _wait_tile(total_ref, tile, local_ref, slots_ref, sem, to_slots):
    total = total_ref[tile]
    for chunk in TOTAL_CHUNKS:
        lo, gl = local_ref.at[pl.ds(0, chunk)], slots_ref.at[pl.ds(0, chunk)]
        src, dst = (lo, gl) if to_slots else (gl, lo)

        @pl.when((total & chunk) != 0)
        def _():
            pltpu.make_async_copy(src, dst, sem).wait()


def _dispatch_kernel(seg_ref, lstart_ref, gstart_ref, total_ref, pad_end_ref, padded_ref, ld_ref, xt_ref, xb_ref,
                     ybuf_ref, zeros_ref, zsem, sems):
    T = zeros_ref.shape[0]
    i = pl.program_id(0)
    n = pl.num_programs(0)
    slot = i % 2

    @pl.when(i == 0)
    def _():
        zeros_ref[...] = jnp.zeros_like(zeros_ref)

        def fill(start):
            return pltpu.make_async_copy(zeros_ref, xb_ref.at[pl.ds(pl.multiple_of(start, T), T)], zsem)

        for e in range(N_EXPERTS):
            @pl.when(padded_ref[e] > 0)
            def _():
                fill(pad_end_ref[e] - T).start()
        for e in range(N_EXPERTS):
            @pl.when(padded_ref[e] > 0)
            def _():
                fill(pad_end_ref[e] - T).wait()
        first_unused = pad_end_ref[N_EXPERTS - 1] // T
        n_blocks = xb_ref.shape[0] // T
        lax.fori_loop(first_unused, n_blocks, lambda b, c: (fill(b * T).start(), c)[1], 0)
        lax.fori_loop(first_unused, n_blocks, lambda b, c: (fill(b * T).wait(), c)[1], 0)

    @pl.when(i >= 2)
    def _():
        _wait_tile(total_ref, i - 2, ybuf_ref.at[slot], xb_ref, sems.at[slot], True)

    L = ybuf_ref.shape[1]
    RT = xt_ref.shape[0]
    r_iota = lax.broadcasted_iota(I32, (L, RT), 0)
    ld = ld_ref[...]
    onehot = jnp.zeros((L, RT), F32)
    for k in range(TOP_K):
        onehot = jnp.where(r_iota == ld[k:k + 1, :], 1.0, onehot)
    onehot = onehot.astype(BF16)
    ybuf_ref[slot] = _dot(onehot, xt_ref[...])
    _segment_copies(seg_ref, lstart_ref, gstart_ref, i, ybuf_ref.at[slot], xb_ref, sems.at[slot], True,
                    lambda c: c.start())

    @pl.when(i == n - 1)
    def _():
        @pl.when(n >= 2)
        def _():
            _wait_tile(total_ref, i - 1, ybuf_ref.at[1 - slot], xb_ref, sems.at[1 - slot], True)

        _wait_tile(total_ref, i, ybuf_ref.at[slot], xb_ref, sems.at[slot], True)


def _dispatch(xt, ld, seg, lstart, gstart, totals, pad_ends, padded, R):
    N, D = xt.shape
    RT = ROUTE_T
    grid_spec = pltpu.PrefetchScalarGridSpec(
        num_scalar_prefetch=6,
        grid=(N // RT,),
        in_specs=[
            pl.BlockSpec((TOP_K, RT), lambda i, *_: (0, i)),
            pl.BlockSpec((RT, D), lambda i, *_: (i, 0)),
        ],
        out_specs=pl.BlockSpec(memory_space=pl.ANY),
        scratch_shapes=[pltpu.VMEM((2, LOCAL_ROWS, D), F32), pltpu.VMEM((MOE_T, D), F32),
                        pltpu.SemaphoreType.DMA(()), pltpu.SemaphoreType.DMA((2,))],
    )
    return pl.pallas_call(
        _dispatch_kernel,
        grid_spec=grid_spec,
        out_shape=jax.ShapeDtypeStruct((R, D), F32),
        compiler_params=_cparams(("arbitrary",), VMEM_MEDIUM_MIB),
        name="dispatch",
    )(seg, lstart, gstart, totals, pad_ends, padded, ld, xt)


def _expert_kernel(start_ref, nblk_ref, b1_ref, b2_ref, perm_ref, w1_hbm, w2_hbm, xb_ref, yb_ref,
                   w1f_ref, w2f_ref, w1p_ref, w2b_ref, xbuf_ref, ybuf_ref, w_sems, in_sems, out_sems):
    e = pl.program_id(0)
    n_experts = pl.num_programs(0)
    ws = e % 2

    def w_copies(ex, slot):
        return (pltpu.make_async_copy(w1_hbm.at[ex], w1f_ref.at[slot], w_sems.at[slot, 0]),
                pltpu.make_async_copy(w2_hbm.at[ex], w2f_ref.at[slot], w_sems.at[slot, 1]))
    T = xbuf_ref.shape[1]
    G = 2 * LANES
    nb = nblk_ref[e]
    base = start_ref[e]

    def rows(b):
        return pl.ds(pl.multiple_of(base + b * T, T), T)

    def in_copy(b, slot):
        return pltpu.make_async_copy(xb_ref.at[rows(b)], xbuf_ref.at[slot], in_sems.at[slot])

    def out_copy(b, slot):
        return pltpu.make_async_copy(ybuf_ref.at[slot], yb_ref.at[rows(b)], out_sems.at[slot])

    for d in range(IN_DEPTH - 1):
        @pl.when(nb > d)
        def _():
            in_copy(d, d).start()

    @pl.when(e == 0)
    def _():
        for c in w_copies(0, 0):
            c.start()

    for c in w_copies(e, ws):
        c.wait()

    @pl.when(e + 1 < n_experts)
    def _():
        for c in w_copies(e + 1, 1 - ws):
            c.start(priority=1)

    @pl.when(nb > 0)
    def _():
        for g in range(w1p_ref.shape[1] // G):
            w = w1f_ref[ws, :, g * G:(g + 1) * G].astype(BF16)
            w1p_ref[:, g * G:(g + 1) * G] = _dot(w, perm_ref[...]).astype(BF16)
        w2b_ref[...] = w2f_ref[ws].astype(BF16)

        def block(b, carry):
            slot = b % 2
            islot = b % IN_DEPTH
            in_copy(b, islot).wait()

            @pl.when(b + IN_DEPTH - 1 < nb)
            def _():
                in_copy(b + IN_DEPTH - 1, (b + IN_DEPTH - 1) % IN_DEPTH).start()

            x = xbuf_ref[islot].astype(BF16)
            acts = []
            for g in range(w1p_ref.shape[1] // G):
                hg = _dot(x, w1p_ref[:, g * G:(g + 1) * G]) + b1_ref[0, :, g * G:(g + 1) * G]
                glu = jnp.minimum(hg[:, :LANES], SWIGLU_LIMIT)
                lin = jnp.clip(hg[:, LANES:], -SWIGLU_LIMIT, SWIGLU_LIMIT)
                acts.append((glu * jax.nn.sigmoid(SWIGLU_ALPHA * glu) * (lin + 1.0)).astype(BF16))
            y = _dot(jnp.concatenate(acts, axis=-1), w2b_ref[...]) + b2_ref[0]

            @pl.when(b >= 2)
            def _():
                out_copy(b - 2, slot).wait()

            ybuf_ref[slot] = y
            out_copy(b, slot).start()
            return carry

        lax.fori_loop(0, nb, block, 0)

        @pl.when(nb >= 2)
        def _():
            out_copy(nb - 2, nb % 2).wait()

        out_copy(nb - 1, (nb - 1) % 2).wait()

    @pl.when(e == pl.num_programs(0) - 1)
    def _():
        ybuf_ref[0] = jnp.zeros(ybuf_ref.shape[1:], ybuf_ref.dtype)
        first_unused = base // T + nb
        n_blocks = yb_ref.shape[0] // T

        def tail(b):
            return pltpu.make_async_copy(ybuf_ref.at[0], yb_ref.at[pl.ds(pl.multiple_of(b * T, T), T)], out_sems.at[0])

        lax.fori_loop(first_unused, n_blocks, lambda b, c: (tail(b).start(), c)[1], 0)
        lax.fori_loop(first_unused, n_blocks, lambda b, c: (tail(b).wait(), c)[1], 0)


def _experts(xb, region_start, region_blocks, w1, b1p, w2, b2):
    R, D = xb.shape
    E, _, F2 = w1.shape
    F = F2 // 2
    T = MOE_T
    G = 2 * LANES
    c = np.arange(G)
    src = np.where(c < LANES, 2 * c, 2 * (c - LANES) + 1)
    perm = jnp.asarray(np.arange(G)[:, None] == src[None, :], dtype=BF16)
    per_expert = lambda r, w: pl.BlockSpec((1, r, w), lambda e, *_: (e, 0, 0))
    grid_spec = pltpu.PrefetchScalarGridSpec(
        num_scalar_prefetch=2,
        grid=(E,),
        in_specs=[
            per_expert(1, F2), per_expert(1, D),
            pl.BlockSpec((G, G), lambda e, *_: (0, 0)),
            pl.BlockSpec(memory_space=pl.ANY), pl.BlockSpec(memory_space=pl.ANY), pl.BlockSpec(memory_space=pl.ANY),
        ],
        out_specs=pl.BlockSpec(memory_space=pl.ANY),
        scratch_shapes=[pltpu.VMEM((2, D, F2), F32), pltpu.VMEM((2, F, D), F32),
                        pltpu.VMEM((D, F2), BF16), pltpu.VMEM((F, D), BF16),
                        pltpu.VMEM((IN_DEPTH, T, D), F32), pltpu.VMEM((2, T, D), F32),
                        pltpu.SemaphoreType.DMA((2, 2)), pltpu.SemaphoreType.DMA((IN_DEPTH,)),
                        pltpu.SemaphoreType.DMA((2,))],
    )
    return pl.pallas_call(
        _expert_kernel,
        grid_spec=grid_spec,
        out_shape=jax.ShapeDtypeStruct((R, D), F32),
        compiler_params=_cparams(("arbitrary",), VMEM_LARGE_MIB),
        name="experts",
    )(region_start, region_blocks, b1p, b2.reshape(E, 1, D), perm, w1, w2, xb)


def _combine_kernel(seg_ref, lstart_ref, gstart_ref, total_ref, ldt_ref, wt_ref, x1_ref, yb_ref, o_ref, ybuf_ref,
                    sems):
    i = pl.program_id(0)
    n = pl.num_programs(0)
    slot = i % 2

    def fetch(tile, s):
        _segment_copies(seg_ref, lstart_ref, gstart_ref, tile, ybuf_ref.at[s], yb_ref, sems.at[s], False,
                        lambda c: c.start())

    @pl.when(i == 0)
    def _():
        ybuf_ref[...] = jnp.zeros_like(ybuf_ref)
        fetch(i, slot)

    @pl.when(i + 1 < n)
    def _():
        fetch(i + 1, 1 - slot)

    _wait_tile(total_ref, i, ybuf_ref.at[slot], yb_ref, sems.at[slot], False)

    L = ybuf_ref.shape[1]
    RT = x1_ref.shape[0]
    c_iota = lax.broadcasted_iota(I32, (RT, L), 1)
    ldt = ldt_ref[...]
    w = wt_ref[...]
    w_hi = w.astype(BF16).astype(F32)
    w_lo = w - w_hi
    g_hi = jnp.zeros((RT, L), F32)
    g_lo = jnp.zeros((RT, L), F32)
    for k in range(TOP_K):
        hit = c_iota == ldt[:, k:k + 1]
        g_hi = jnp.where(hit, w_hi[:, k:k + 1], g_hi)
        g_lo = jnp.where(hit, w_lo[:, k:k + 1], g_lo)
    y = ybuf_ref[slot].astype(BF16)
    o_ref[...] = x1_ref[...] + (_dot(g_hi.astype(BF16), y) + _dot(g_lo.astype(BF16), y))


def _combine(ldt, wts_t, x1, yb, seg, lstart, gstart, totals):
    N, D = x1.shape
    RT = ROUTE_T
    grid_spec = pltpu.PrefetchScalarGridSpec(
        num_scalar_prefetch=4,
        grid=(N // RT,),
        in_specs=[
            pl.BlockSpec((RT, TOP_K), lambda i, *_: (i, 0)),
            pl.BlockSpec((RT, TOP_K), lambda i, *_: (i, 0)),
            pl.BlockSpec((RT, D), lambda i, *_: (i, 0)),
            pl.BlockSpec(memory_space=pl.ANY),
        ],
        out_specs=pl.BlockSpec((RT, D), lambda i, *_: (i, 0)),
        scratch_shapes=[pltpu.VMEM((2, LOCAL_ROWS, D), F32), pltpu.SemaphoreType.DMA((2,))],
    )
    return pl.pallas_call(
        _combine_kernel,
        grid_spec=grid_spec,
        out_shape=jax.ShapeDtypeStruct((N, D), F32),
        compiler_params=_cparams(("arbitrary",), VMEM_MEDIUM_MIB),
        name="combine",
    )(seg, lstart, gstart, totals, ldt, wts_t, x1, yb)


def _layer(x, mem, g_mix, w_in, b_gate, g_ret_out, g_moba_q, g_moba_k, g_mem, w_mem_kv, g_mem_q, g_mem_k,
           w_br_ret, w_br_moba, w_br_mem, w_out, g_ffn, w_router, b_router, w_mlp1, b_mlp1, w_mlp2, b_mlp2):
    B, S, D = x.shape
    N = B * S
    x2 = x.reshape(N, D)
    slopes_np = np.exp2(-8.0 * (np.arange(MOBA_HEADS, dtype=np.float64) + 1.0) / MOBA_HEADS)
    assert all(float(np.log2(s)).is_integer() for s in slopes_np)
    slopes = tuple(float(s) for s in slopes_np)

    w_mix = w_in[:, :MIX_W].astype(BF16)
    w_gate = w_in[:, MIX_W:].astype(BF16)
    kmem, vmem = _memkv(mem, g_mem, w_mem_kv, g_mem_k)
    rq, rk, rv, rg, mq, mk, mv, cq, km = _inproj(x2, B, S, g_mix, w_mix, g_moba_q, g_moba_k, g_mem_q, slopes)
    o_ret = _retention(rq, rk, rv, rg, g_ret_out)
    o_moba = _moba(mq, mk, mv, km, jnp.asarray(slopes, F32)).reshape(N, MOBA_W)
    x1, xt, eidx, wts, rank, cnt = _merge(x2, S, o_ret, o_moba, cq, kmem, vmem, g_mix, w_gate, b_gate,
                                          w_br_ret, w_br_moba, w_br_mem, w_out, g_ffn, w_router, b_router)
    T = MOE_T
    tcnt = cnt[:, :, 0]
    seg = ((tcnt + SEG_ALIGN - 1) // SEG_ALIGN) * SEG_ALIGN
    region_rows = jnp.sum(seg, axis=0)
    padded = ((region_rows + T - 1) // T) * T
    pad_ends = jnp.cumsum(padded).astype(I32)
    pad_starts = pad_ends - padded
    gstart = pad_starts[None, :] + jnp.cumsum(seg, axis=0) - seg
    lstart = jnp.cumsum(seg, axis=1) - seg
    lstart_tok = jnp.broadcast_to(lstart[:, None, :], (N // ROUTE_T, ROUTE_T, N_EXPERTS)).reshape(N, N_EXPERTS)
    onehot = eidx[:, :, None] == jnp.arange(N_EXPERTS, dtype=I32)[None, None, :]
    ld = jnp.sum(jnp.where(onehot, lstart_tok[None], 0), axis=-1) + rank
    NB = -(-(N // ROUTE_T * LOCAL_ROWS) // T) + N_EXPERTS
    seg_f, lstart_f, gstart_f = (a.reshape(-1).astype(I32) for a in (seg, lstart, gstart))
    totals = jnp.sum(seg, axis=1).astype(I32)

    xb = _dispatch(xt, ld, seg_f, lstart_f, gstart_f, totals, pad_ends, padded.astype(I32), NB * T)
    F2 = w_mlp1.shape[-1]
    b1p = b_mlp1.reshape(N_EXPERTS, F2 // (2 * LANES), LANES, 2).transpose(0, 1, 3, 2).reshape(N_EXPERTS, 1, F2)
    yb = _experts(xb, pad_starts.astype(I32), (padded // T).astype(I32), w_mlp1, b1p, w_mlp2, b_mlp2)
    out = _combine(ld.T, wts.T, x1, yb, seg_f, lstart_f, gstart_f, totals)
    return out.reshape(B, S, D)


def kernel(x, mem, g_mix, w_in, b_gate, g_ret_out, g_moba_q, g_moba_k, g_mem, w_mem_kv, g_mem_q, g_mem_k, w_br_ret, w_br_moba, w_br_mem, w_out, g_ffn, w_router, b_router, w_mlp1, b_mlp1, w_mlp2, b_mlp2):
    for l in range(g_mix.shape[0]):
        x = _layer(x, mem, g_mix[l], w_in[l], b_gate[l], g_ret_out[l], g_moba_q[l], g_moba_k[l], g_mem[l],
                   w_mem_kv[l], g_mem_q[l], g_mem_k[l], w_br_ret[l], w_br_moba[l], w_br_mem[l], w_out[l],
                   g_ffn[l], w_router[l], b_router[l], w_mlp1[l], b_mlp1[l], w_mlp2[l], b_mlp2[l])
    return x
```

```python
import functools

import jax
import jax.numpy as jnp
import numpy as np
from jax import lax
from jax.experimental import pallas as pl
from jax.experimental.pallas import tpu as pltpu

F32 = jnp.float32
BF16 = jnp.bfloat16
I32 = jnp.int32

EPS = 1e-5
NEG = -1e30

RET_HEADS = 4
RET_DK = 64
RET_DV = 128
RET_CHUNK = 128
MOBA_HEADS = 8
MOBA_HD = 64
MOBA_BLOCK = 256
MOBA_TOPK = 3
MEM_HEADS = 4
MEM_HD = 128
N_BRANCH = 3
N_EXPERTS = 32
TOP_K = 4
SWIGLU_LIMIT = 7.0
SWIGLU_ALPHA = 1.702

RET_Q = RET_HEADS * RET_DK
RET_V = RET_HEADS * RET_DV
MOBA_W = MOBA_HEADS * MOBA_HD
MEM_W = MEM_HEADS * MEM_HD
MIX_W = 2 * RET_Q + 2 * RET_V + 3 * MOBA_W + MEM_W

LANES = 128
V7X_VMEM_MIB = 64
VMEM_LARGE_MIB = V7X_VMEM_MIB - 8
VMEM_MEDIUM_MIB = V7X_VMEM_MIB * 3 // 4
VMEM_SMALL_MIB = V7X_VMEM_MIB // 2
MOBA_PAIRS = MOBA_HEADS // 2
MOBA_HEADS_PER_STEP = 8
BIAS_LANE0 = MOBA_HD
ONE_LANE = 80
VSUM_LANE = MOBA_HD

MOE_T = 256
IN_DEPTH = 2
TM_PROJ = 512
TM_MERGE = 512
TQ_RET = 512
ROUTE_T = 256
ROUTE_TILES = 2
SEG_ALIGN = 8
LOCAL_ROWS = -(-(ROUTE_T * TOP_K + N_EXPERTS * (SEG_ALIGN - 1)) // LANES) * LANES
SEG_CHUNKS = tuple(2 ** p for p in range(ROUTE_T.bit_length() - 1, SEG_ALIGN.bit_length() - 2, -1))
SEG_SMALL = 64
TOTAL_CHUNKS = tuple(2 ** p for p in range(LOCAL_ROWS.bit_length() - 1, SEG_ALIGN.bit_length() - 2, -1))

_NT = (((1,), (1,)), ((), ()))
_TN = (((0,), (0,)), ((), ()))


def _rms(x, g):
    return x * lax.rsqrt(jnp.mean(x * x, axis=-1, keepdims=True) + EPS) * g


def _dot(a, b):
    return jnp.dot(a, b, preferred_element_type=F32)


def _cparams(sem, vmem_mb):
    return pltpu.CompilerParams(dimension_semantics=sem, vmem_limit_bytes=vmem_mb * 1024 * 1024)


def _memkv_kernel(mem_ref, g_ref, w_ref, gk_ref, k_ref, v_ref):
    m = _rms(mem_ref[0], g_ref[...]).astype(BF16)
    kv = _dot(m, w_ref[...])
    ks = [_rms(kv[:, h * MEM_HD:(h + 1) * MEM_HD], gk_ref[...]) for h in range(MEM_HEADS)]
    k_ref[0] = jnp.concatenate(ks, axis=-1).astype(BF16)
    v_ref[0] = kv[:, MEM_W:].astype(BF16)


def _memkv(mem, g_mem, w_mem_kv, g_mem_k):
    B, M, D = mem.shape
    return pl.pallas_call(
        _memkv_kernel,
        grid=(B,),
        in_specs=[
            pl.BlockSpec((1, M, D), lambda b: (b, 0, 0)),
            pl.BlockSpec((1, D), lambda b: (0, 0)),
            pl.BlockSpec((D, 2 * MEM_W), lambda b: (0, 0)),
            pl.BlockSpec((1, MEM_HD), lambda b: (0, 0)),
        ],
        out_specs=[
            pl.BlockSpec((1, M, MEM_W), lambda b: (b, 0, 0)),
            pl.BlockSpec((1, M, MEM_W), lambda b: (b, 0, 0)),
        ],
        out_shape=[jax.ShapeDtypeStruct((B, M, MEM_W), BF16)] * 2,
        compiler_params=_cparams(("arbitrary",), VMEM_SMALL_MIB),
        name="memkv",
    )(mem, g_mem.reshape(1, D), w_mem_kv.astype(BF16), g_mem_k.reshape(1, MEM_HD))


def _head_pair_norm(a2, g2, lane):
    sq = a2 * a2
    lo = lane < MOBA_HD
    ss_lo = jnp.sum(jnp.where(lo, sq, 0.0), axis=-1, keepdims=True)
    ss_hi = jnp.sum(jnp.where(lo, 0.0, sq), axis=-1, keepdims=True)
    inv = jnp.where(lo, lax.rsqrt(ss_lo / MOBA_HD + EPS), lax.rsqrt(ss_hi / MOBA_HD + EPS))
    return a2 * inv * g2


def _inproj_kernel(slopes, seq_tiles, x_ref, gmix_ref, w_ref, gq_ref, gk_ref, gc_ref,
                   rq_ref, rk_ref, rv_ref, rg_ref, mq_ref, mk_ref, mv_ref, cq_ref, kmean_ref):
    tm = x_ref.shape[0]
    blocks_per_tile = tm // MOBA_BLOCK
    h = _rms(x_ref[...], gmix_ref[...]).astype(BF16)
    col = [0]

    def proj(width):
        a = _dot(h, w_ref[:, col[0]:col[0] + width])
        col[0] += width
        return a

    a = proj(2 * RET_Q)
    for hh in range(RET_HEADS):
        rq_ref[0, hh] = a[:, hh * RET_DK:(hh + 1) * RET_DK].astype(BF16)
        rk_ref[0, hh] = (a[:, RET_Q + hh * RET_DK:RET_Q + (hh + 1) * RET_DK] * (RET_DK ** -0.5)).astype(BF16)
    rv_ref[...] = proj(RET_V).astype(BF16)
    rg_ref[...] = proj(RET_V).astype(BF16)

    lane = lax.broadcasted_iota(I32, (tm, LANES), 1)
    row = lax.broadcasted_iota(I32, (tm, LANES), 0)
    lo = lane < MOBA_HD
    q_tail = jnp.where(lane == ONE_LANE, 1.0, 0.0)
    a = proj(MOBA_W)
    for p in range(MOBA_PAIRS):
        n2 = _head_pair_norm(a[:, p * LANES:(p + 1) * LANES], gq_ref[...], lane) * (MOBA_HD ** -0.5)
        mq_ref[0, 2 * p] = jnp.where(lo, n2, q_tail).astype(BF16)
        mq_ref[0, 2 * p + 1] = jnp.where(lo, pltpu.roll(n2, MOBA_HD, 1), q_tail).astype(BF16)
    blk = (pl.program_id(0) % seq_tiles) * blocks_per_tile + row // MOBA_BLOCK
    onehot_tail = jnp.where(lane == BIAS_LANE0 + blk, 1.0, 0.0)
    off = (row % MOBA_BLOCK).astype(F32)
    a = proj(MOBA_W)
    for p in range(MOBA_PAIRS):
        n2 = _head_pair_norm(a[:, p * LANES:(p + 1) * LANES], gk_ref[...], lane)
        for j in range(blocks_per_tile):
            kmean_ref[0, 0, p, j:j + 1, :] = jnp.mean(n2[j * MOBA_BLOCK:(j + 1) * MOBA_BLOCK], axis=0, keepdims=True)
        for s, src in ((0, n2), (1, pltpu.roll(n2, MOBA_HD, 1))):
            tail = jnp.where(lane == ONE_LANE, slopes[2 * p + s] * off, onehot_tail)
            mk_ref[0, 2 * p + s] = jnp.where(lo, src, tail).astype(BF16)
    v_tail = jnp.where(lane == VSUM_LANE, 1.0, 0.0)
    a = proj(MOBA_W)
    for p in range(MOBA_PAIRS):
        a2 = a[:, p * LANES:(p + 1) * LANES]
        mv_ref[0, 2 * p] = jnp.where(lo, a2, v_tail).astype(BF16)
        mv_ref[0, 2 * p + 1] = jnp.where(lo, pltpu.roll(a2, MOBA_HD, 1), v_tail).astype(BF16)
    a = proj(MEM_W)
    cq = [_rms(a[:, hh * MEM_HD:(hh + 1) * MEM_HD], gc_ref[...]) for hh in range(MEM_HEADS)]
    cq_ref[...] = jnp.concatenate(cq, axis=-1).astype(BF16)


def _inproj(x2, B, S, g_mix, w_mix, g_moba_q, g_moba_k, g_mem_q, slopes):
    N, D = x2.shape
    tm = TM_PROJ
    nS = S // tm
    bpt = tm // MOBA_BLOCK
    tok = lambda i: (i, 0)
    headmaj = lambda i: (i // nS, 0, i % nS, 0)
    g2 = lambda g: jnp.concatenate([g, g]).reshape(1, LANES)
    outs = pl.pallas_call(
        functools.partial(_inproj_kernel, slopes, nS),
        grid=(N // tm,),
        in_specs=[
            pl.BlockSpec((tm, D), tok),
            pl.BlockSpec((1, D), lambda i: (0, 0)),
            pl.BlockSpec((D, MIX_W), lambda i: (0, 0)),
            pl.BlockSpec((1, LANES), lambda i: (0, 0)),
            pl.BlockSpec((1, LANES), lambda i: (0, 0)),
            pl.BlockSpec((1, MEM_HD), lambda i: (0, 0)),
        ],
        out_specs=[
            pl.BlockSpec((1, RET_HEADS, tm, RET_DK), headmaj),
            pl.BlockSpec((1, RET_HEADS, tm, RET_DK), headmaj),
            pl.BlockSpec((tm, RET_V), tok),
            pl.BlockSpec((tm, RET_V), tok),
            pl.BlockSpec((1, MOBA_HEADS, tm, LANES), headmaj),
            pl.BlockSpec((1, MOBA_HEADS, tm, LANES), headmaj),
            pl.BlockSpec((1, MOBA_HEADS, tm, LANES), headmaj),
            pl.BlockSpec((tm, MEM_W), tok),
            pl.BlockSpec((1, 1, MOBA_PAIRS, bpt, LANES), lambda i: (i // nS, i % nS, 0, 0, 0)),
        ],
        out_shape=[
            jax.ShapeDtypeStruct((B, RET_HEADS, S, RET_DK), BF16),
            jax.ShapeDtypeStruct((B, RET_HEADS, S, RET_DK), BF16),
            jax.ShapeDtypeStruct((N, RET_V), BF16),
            jax.ShapeDtypeStruct((N, RET_V), BF16),
            jax.ShapeDtypeStruct((B, MOBA_HEADS, S, LANES), BF16),
            jax.ShapeDtypeStruct((B, MOBA_HEADS, S, LANES), BF16),
            jax.ShapeDtypeStruct((B, MOBA_HEADS, S, LANES), BF16),
            jax.ShapeDtypeStruct((N, MEM_W), BF16),
            jax.ShapeDtypeStruct((B, nS, MOBA_PAIRS, bpt, LANES), F32),
        ],
        compiler_params=_cparams(("arbitrary",), VMEM_LARGE_MIB),
        name="inproj",
    )(x2, g_mix.reshape(1, D), w_mix, g2(g_moba_q), g2(g_moba_k), g_mem_q.reshape(1, MEM_HD))
    rq, rk, rv, rg, mq, mk, mv, cq, kmean_pairs = outs
    km = kmean_pairs.reshape(B, nS, MOBA_PAIRS, bpt, 2, MOBA_HD).transpose(0, 2, 4, 1, 3, 5)
    km = km.reshape(B, MOBA_HEADS, nS * bpt, MOBA_HD)
    km = jnp.pad(km, ((0, 0), (0, 0), (0, 0), (0, LANES - MOBA_HD)))
    return rq, rk, rv, rg, mq, mk, mv, cq, km


def _retention_kernel(q_ref, k_ref, v_ref, rg_ref, din_ref, dq_ref, dk_ref, dc_ref, g_ref, o_ref, state_ref):
    @pl.when(pl.program_id(1) == 0)
    def _():
        state_ref[...] = jnp.zeros_like(state_ref)

    C = RET_CHUNK
    for h in range(q_ref.shape[1]):
        cols = slice(h * RET_DV, (h + 1) * RET_DV)
        state = state_ref[h]
        for c in range(q_ref.shape[2] // C):
            rows = slice(c * C, (c + 1) * C)
            q = q_ref[0, h, rows, :]
            k = k_ref[0, h, rows, :]
            v = v_ref[rows, cols]
            scores = lax.dot_general(q, k, _NT, preferred_element_type=F32) * din_ref[h]
            intra = _dot(scores.astype(BF16), v)
            cross = _dot(q, state.astype(BF16)) * dq_ref[h]
            kd = (k.astype(F32) * dk_ref[h]).astype(BF16)
            state = dc_ref[h] * state + lax.dot_general(kd, v, _TN, preferred_element_type=F32)
            o = _rms(intra + cross, g_ref[...])
            o_ref[rows, cols] = (o * jax.nn.silu(rg_ref[rows, cols].astype(F32))).astype(BF16)
        state_ref[h] = state


def _retention_decays():
    H, C = RET_HEADS, RET_CHUNK
    log_g = jnp.log1p(-jnp.exp2(-5.0 - jnp.arange(H, dtype=F32)))
    i = jnp.arange(C, dtype=F32)
    diff = i[:, None] - i[None, :]
    decay_in = jnp.where(diff >= 0, jnp.exp(jnp.maximum(diff, 0.0)[None] * log_g[:, None, None]), 0.0)
    decay_k = jnp.exp((C - 1 - i)[None, :] * log_g[:, None])
    decay_q = jnp.exp((i + 1)[None, :] * log_g[:, None])
    decay_chunk = jnp.exp(C * log_g)
    dq = jnp.broadcast_to(decay_q[:, :, None], (H, C, RET_DV))
    dk = jnp.broadcast_to(decay_k[:, :, None], (H, C, RET_DK))
    dc = jnp.broadcast_to(decay_chunk[:, None, None], (H, RET_DK, RET_DV))
    return decay_in, dq, dk, dc


def _retention(rq, rk, rv, rg, g_ret_out):
    B, H, S, dk = rq.shape
    tq = TQ_RET
    nT = S // tq
    C = RET_CHUNK
    din, dq, dk_, dc = _retention_decays()
    qk_spec = pl.BlockSpec((1, H, tq, dk), lambda b, t: (b, 0, t, 0))
    tok_spec = pl.BlockSpec((tq, RET_V), lambda b, t: (b * nT + t, 0))
    const = lambda r, c: pl.BlockSpec((H, r, c), lambda b, t: (0, 0, 0))
    return pl.pallas_call(
        _retention_kernel,
        grid=(B, nT),
        in_specs=[qk_spec, qk_spec, tok_spec, tok_spec,
                  const(C, C), const(C, RET_DV), const(C, RET_DK), const(RET_DK, RET_DV),
                  pl.BlockSpec((1, RET_DV), lambda b, t: (0, 0))],
        out_specs=tok_spec,
        out_shape=jax.ShapeDtypeStruct((B * S, RET_V), BF16),
        scratch_shapes=[pltpu.VMEM((H, RET_DK, RET_DV), F32)],
        compiler_params=_cparams(("arbitrary", "arbitrary"), VMEM_SMALL_MIB),
        name="retention",
    )(rq, rk, rv, rg, din, dq, dk_, dc, g_ret_out.reshape(1, RET_DV))


def _moba_kernel(slopes_ref, q_ref, k_ref, v_ref, km_ref, shift_ref, mask_ref, o_ref):
    NH = q_ref.shape[1]
    head0 = pl.program_id(1) * NH
    i0 = pl.program_id(2) * 2
    BS = MOBA_BLOCK
    nb = km_ref.shape[2]
    heads = range(NH)
    n_iota = lax.broadcasted_iota(I32, (nb, 2 * BS), 0)
    q_blk = i0 + lax.broadcasted_iota(I32, (nb, 2 * BS), 1) // BS
    past = n_iota < q_blk
    qa = [q_ref[0, s] for s in heads]
    q_aug = []
    for s in heads:
        km = km_ref[0, s]
        hi = km.astype(BF16)
        mid = (km - hi.astype(F32)).astype(BF16)
        lo = (km - hi.astype(F32) - mid.astype(F32)).astype(BF16)
        g3 = lax.dot_general(jnp.concatenate([hi, mid, lo], axis=0), qa[s], _NT, preferred_element_type=F32)
        gate = (g3[:nb] + g3[nb:2 * nb]) + g3[2 * nb:]
        g = jnp.where(past, gate, -jnp.inf)
        rank = jnp.zeros((nb, 2 * BS), I32)
        for m in range(nb - 1):
            gm = g[m:m + 1, :]
            rank = rank + jnp.where(gm > g, 1, jnp.where(gm == g, jnp.where(m < n_iota, 1, 0), 0))
        sel = jnp.where(past, rank, MOBA_TOPK) < MOBA_TOPK
        bias = jnp.where(sel, (n_iota - q_blk).astype(F32) * (slopes_ref[head0 + s] * BS), NEG).astype(BF16)
        placed = lax.dot_general(bias, shift_ref[...], _TN, preferred_element_type=F32)
        q_aug.append((qa[s].astype(F32) + placed).astype(BF16))

    def rows_of(j):
        return pl.ds(pl.multiple_of(j * BS, BS), BS)

    m_lo, m_hi, acc_lo, acc_hi = [], [], [], []
    for s in heads:
        q_mix = jnp.concatenate([qa[s][:BS], q_aug[s][BS:]], axis=0)
        sc = lax.dot_general(q_mix, k_ref[0, s, rows_of(i0), :], _NT, preferred_element_type=F32) + mask_ref[...]
        m = jnp.max(sc, axis=-1, keepdims=True)
        acc = _dot(jnp.exp(sc - m).astype(BF16), v_ref[0, s, rows_of(i0), :])
        m_lo.append(m[:BS])
        acc_lo.append(acc[:BS])
        sc1 = lax.dot_general(qa[s][BS:], k_ref[0, s, rows_of(i0 + 1), :], _NT, preferred_element_type=F32)
        sc1 = sc1 + mask_ref[:BS, :]
        m1 = jnp.maximum(m[BS:], jnp.max(sc1, axis=-1, keepdims=True))
        p1 = jnp.exp(sc1 - m1).astype(BF16)
        m_hi.append(m1)
        acc_hi.append(jnp.exp(m[BS:] - m1) * acc[BS:] + _dot(p1, v_ref[0, s, rows_of(i0 + 1), :]))

    def body(j, carry):
        ms_lo, ms_hi, accs_lo, accs_hi = carry
        out = ([], [], [], [])
        for s in heads:
            sj = lax.dot_general(q_aug[s], k_ref[0, s, rows_of(j), :], _NT, preferred_element_type=F32)
            ps, alphas, ms = [], [], []
            for half, m_prev in ((slice(0, BS), ms_lo[s]), (slice(BS, 2 * BS), ms_hi[s])):
                m_new = jnp.maximum(m_prev, jnp.max(sj[half], axis=-1, keepdims=True))
                ps.append(jnp.exp(sj[half] - m_new).astype(BF16))
                alphas.append(jnp.exp(m_prev - m_new))
                ms.append(m_new)
            pv = _dot(jnp.concatenate(ps, axis=0), v_ref[0, s, rows_of(j), :])
            out[0].append(ms[0])
            out[1].append(ms[1])
            out[2].append(alphas[0] * accs_lo[s] + pv[:BS])
            out[3].append(alphas[1] * accs_hi[s] + pv[BS:])
        return tuple(tuple(o) for o in out)

    _, _, accs_lo, accs_hi = lax.fori_loop(0, i0 // 2, lambda j, c: body(2 * j + 1, body(2 * j, c)),
                                           (tuple(m_lo), tuple(m_hi), tuple(acc_lo), tuple(acc_hi)))
    lane = lax.broadcasted_iota(I32, (BS, LANES), 1)
    for t, accs in enumerate((accs_lo, accs_hi)):
        outs = [acc / acc[:, VSUM_LANE:VSUM_LANE + 1] for acc in accs]
        pairs = [jnp.where(lane < MOBA_HD, outs[2 * p], pltpu.roll(outs[2 * p + 1], MOBA_HD, 1))
                 for p in range(NH // 2)]
        o_ref[0, t * BS:(t + 1) * BS, :] = jnp.concatenate(pairs, axis=-1).astype(BF16)


def _moba(mq, mk, mv, km, slopes):
    B, H, S, _ = mq.shape
    BS = MOBA_BLOCK
    NH = MOBA_HEADS_PER_STEP
    nq = S // BS
    nb = km.shape[2]
    assert BIAS_LANE0 + nb <= ONE_LANE and nq % 2 == 0
    shift = (jnp.arange(nb)[:, None] + BIAS_LANE0 == jnp.arange(LANES)[None, :]).astype(BF16)
    causal = jnp.where(jnp.arange(BS)[:, None] >= jnp.arange(BS)[None, :], 0.0, NEG).astype(F32)
    mask = jnp.concatenate([causal, jnp.zeros((BS, BS), F32)], axis=0)
    per_head = lambda rows: pl.BlockSpec((1, NH, rows, LANES), lambda b, p, i, sl: (b, p, 0, 0))
    grid_spec = pltpu.PrefetchScalarGridSpec(
        num_scalar_prefetch=1,
        grid=(B, H // NH, nq // 2),
        in_specs=[
            pl.BlockSpec((1, NH, 2 * BS, LANES), lambda b, p, i, sl: (b, p, i, 0)),
            per_head(S), per_head(S), per_head(nb),
            pl.BlockSpec((nb, LANES), lambda b, p, i, sl: (0, 0)),
            pl.BlockSpec((2 * BS, BS), lambda b, p, i, sl: (0, 0)),
        ],
        out_specs=pl.BlockSpec((1, 2 * BS, NH * MOBA_HD), lambda b, p, i, sl: (b, i, p)),
    )
    return pl.pallas_call(
        _moba_kernel,
        grid_spec=grid_spec,
        out_shape=jax.ShapeDtypeStruct((B, S, MOBA_W), BF16),
        compiler_params=_cparams(("arbitrary", "arbitrary", "arbitrary"), VMEM_LARGE_MIB),
        name="moba",
    )(slopes, mq, mk, mv, km, shift, mask)


def _merge_kernel(x_ref, oret_ref, omoba_ref, cq_ref, kmem_ref, vmem_ref, gmix_ref, wg_ref, bg_ref,
                  wbr_ref, wbm_ref, wbc_ref, wout_ref, gffn_ref, wr_ref, br_ref, tri_ref,
                  x1_ref, xt_ref, eidx_ref, wts_ref, rank_ref, cnt_ref):
    D = x_ref.shape[1]
    tm = x_ref.shape[0]

    x = x_ref[...]
    h = _rms(x, gmix_ref[...]).astype(BF16)
    cq = cq_ref[...]
    om = []
    for hh in range(MEM_HEADS):
        cols = slice(hh * MEM_HD, (hh + 1) * MEM_HD)
        sc = lax.dot_general(cq[:, cols], kmem_ref[0, :, cols], _NT, preferred_element_type=F32) * (MEM_HD ** -0.5)
        sc = sc - jnp.max(sc, axis=-1, keepdims=True)
        p = jnp.exp(sc)
        p = p / jnp.sum(p, axis=-1, keepdims=True)
        om.append(_dot(p.astype(BF16), vmem_ref[0, :, cols]))
    omem = jnp.concatenate(om, axis=-1).astype(BF16)
    y = None
    for br, (o, w_ref) in enumerate(((oret_ref[...], wbr_ref), (omoba_ref[...], wbm_ref), (omem, wbc_ref))):
        gl = _dot(h, wg_ref[:, br * D:(br + 1) * D]) + bg_ref[:, br * D:(br + 1) * D]
        term = jax.nn.sigmoid(gl) * _dot(o, w_ref[...])
        y = term if y is None else y + term
    x1 = x + _dot(y.astype(BF16), wout_ref[...])
    x1_ref[...] = x1
    xt = _rms(x1, gffn_ref[...])
    wr = wr_ref[...]
    E = wr.shape[0]
    w_hi = wr.astype(BF16)
    w_mid = (wr - w_hi.astype(F32)).astype(BF16)
    w_lo = (wr - w_hi.astype(F32) - w_mid.astype(F32)).astype(BF16)
    xt_hi = xt.astype(BF16)
    xt_ref[...] = xt_hi
    xt_lo = (xt - xt_hi.astype(F32)).astype(BF16)
    a = lax.dot_general(jnp.concatenate([w_hi, w_mid, w_lo], axis=0), xt_hi, _NT, preferred_element_type=F32)
    b = lax.dot_general(jnp.concatenate([w_hi, w_mid], axis=0), xt_lo, _NT, preferred_element_type=F32)
    logits = (a[:E] + (a[E:2 * E] + b[:E])) + (a[2 * E:] + b[E:]) + br_ref[...]
    e_iota = lax.broadcasted_iota(I32, (E, tm), 0)
    l = logits
    vals, hots = [], []
    for k in range(TOP_K):
        m = jnp.max(l, axis=0, keepdims=True)
        idx = jnp.min(jnp.where(l == m, e_iota, E), axis=0, keepdims=True)
        hot = e_iota == idx
        l = jnp.where(hot, -jnp.inf, l)
        vals.append(m)
        hots.append(hot)
        eidx_ref[k:k + 1, :] = idx
    ex = [jnp.exp(v - vals[0]) for v in vals]
    den = ex[0]
    for k in range(1, TOP_K):
        den = den + ex[k]
    chosen = jnp.zeros((E, tm), F32)
    for k in range(TOP_K):
        wts_ref[k:k + 1, :] = ex[k] / den
        chosen = chosen + jnp.where(hots[k], 1.0, 0.0)
    RT = tri_ref.shape[0]
    for t in range(tm // RT):
        cols = slice(t * RT, (t + 1) * RT)
        prefix = _dot(chosen[:, cols].astype(BF16), tri_ref[...])
        for k in range(TOP_K):
            rank_ref[k:k + 1, cols] = jnp.sum(jnp.where(hots[k][:, cols], prefix, 0.0), axis=0, keepdims=True).astype(I32)
        counts = jnp.sum(chosen[:, cols], axis=1, keepdims=True)
        cnt_ref[t] = jnp.broadcast_to(counts, cnt_ref.shape[1:]).astype(I32)


def _merge(x2, S, o_ret, o_moba, cq, kmem, vmem, g_mix, w_gate, b_gate, w_br_ret, w_br_moba, w_br_mem, w_out,
           g_ffn, w_router, b_router):
    N, D = x2.shape
    tm = TM_MERGE
    nS = S // tm
    M = kmem.shape[1]
    E = w_router.shape[1]
    tok = lambda w: pl.BlockSpec((tm, w), lambda i: (i, 0))
    const = lambda r, c: pl.BlockSpec((r, c), lambda i: (0, 0), pipeline_mode=pl.Buffered(1))
    mem_spec = pl.BlockSpec((1, M, MEM_W), lambda i: (i // nS, 0, 0))
    lanes_tok = lambda r, dt: (pl.BlockSpec((r, tm), lambda i: (0, i)), jax.ShapeDtypeStruct((r, N), dt))
    RT = ROUTE_T
    tri = (jnp.arange(RT)[:, None] < jnp.arange(RT)[None, :]).astype(BF16)
    e_spec, e_shape = lanes_tok(TOP_K, I32)
    w_spec, w_shape = lanes_tok(TOP_K, F32)
    r_spec, r_shape = lanes_tok(TOP_K, I32)
    return pl.pallas_call(
        _merge_kernel,
        grid=(N // tm,),
        in_specs=[tok(D), tok(RET_V), tok(MOBA_W), tok(MEM_W), mem_spec, mem_spec,
                  const(1, D), const(D, N_BRANCH * D), const(1, N_BRANCH * D),
                  const(RET_V, D), const(MOBA_W, D), const(MEM_W, D), const(D, D),
                  const(1, D), const(E, D), const(E, 1), const(RT, RT)],
        out_specs=[tok(D), tok(D), e_spec, w_spec, r_spec, pl.BlockSpec((tm // RT, E, LANES), lambda i: (i, 0, 0))],
        out_shape=[jax.ShapeDtypeStruct((N, D), F32), jax.ShapeDtypeStruct((N, D), BF16),
                   e_shape, w_shape, r_shape, jax.ShapeDtypeStruct((N // RT, E, LANES), I32)],
        compiler_params=_cparams(("arbitrary",), VMEM_LARGE_MIB),
        name="merge",
    )(x2, o_ret, o_moba, cq, kmem, vmem, g_mix.reshape(1, D), w_gate, b_gate.reshape(1, N_BRANCH * D),
      w_br_ret.astype(BF16), w_br_moba.astype(BF16), w_br_mem.astype(BF16), w_out.astype(BF16),
      g_ffn.reshape(1, D), w_router.T, b_router.reshape(E, 1), tri)


def _segment_copies(seg_ref, lstart_ref, gstart_ref, tile, local_ref, slots_ref, sem, to_slots, fn):
    def per_expert(e, carry):
        idx = tile * N_EXPERTS + e
        size = seg_ref[idx]
        lstart = lstart_ref[idx]
        gstart = gstart_ref[idx]

        def copy_chunks(chunks, off):
            for chunk in chunks:
                take = (size & chunk) != 0
                lo = pl.ds(pl.multiple_of(lstart + off, SEG_ALIGN), chunk)
                gl = pl.ds(pl.multiple_of(gstart + off, SEG_ALIGN), chunk)
                src, dst = (local_ref.at[lo], slots_ref.at[gl]) if to_slots else (slots_ref.at[gl], local_ref.at[lo])

                @pl.when(take)
                def _():
                    fn(pltpu.make_async_copy(src, dst, sem))

                off = off + jnp.where(take, chunk, 0)

        large = size & -SEG_SMALL

        @pl.when(large != 0)
        def _():
            copy_chunks([c for c in SEG_CHUNKS if c >= SEG_SMALL], jnp.int32(0))

        copy_chunks([c for c in SEG_CHUNKS if c < SEG_SMALL], large)
        return carry

    lax.fori_loop(0, N_EXPERTS, per_expert, 0)


def _wait_tile(total_ref, tile, local_ref, slots_ref, sem, to_slots):
    total = total_ref[tile]
    for chunk in TOTAL_CHUNKS:
        lo, gl = local_ref.at[pl.ds(0, chunk)], slots_ref.at[pl.ds(0, chunk)]
        src, dst = (lo, gl) if to_slots else (gl, lo)

        @pl.when((total & chunk) != 0)
        def _():
            pltpu.make_async_copy(src, dst, sem).wait()


def _dispatch_kernel(seg_ref, lstart_ref, gstart_ref, total_ref, pad_end_ref, padded_ref, ld_ref, xt_ref, xb_ref,
                     ybuf_ref, zeros_ref, zsem, sems):
    T = zeros_ref.shape[0]
    i = pl.program_id(0)
    n = pl.num_programs(0)

    @pl.when(i == 0)
    def _():
        zeros_ref[...] = jnp.zeros_like(zeros_ref)

        def fill(start):
            return pltpu.make_async_copy(zeros_ref, xb_ref.at[pl.ds(pl.multiple_of(start, T), T)], zsem)

        for e in range(N_EXPERTS):
            @pl.when(padded_ref[e] > 0)
            def _():
                fill(pad_end_ref[e] - T).start()
        for e in range(N_EXPERTS):
            @pl.when(padded_ref[e] > 0)
            def _():
                fill(pad_end_ref[e] - T).wait()
        first_unused = pad_end_ref[N_EXPERTS - 1] // T
        n_blocks = xb_ref.shape[0] // T
        lax.fori_loop(first_unused, n_blocks, lambda b, c: (fill(b * T).start(), c)[1], 0)
        lax.fori_loop(first_unused, n_blocks, lambda b, c: (fill(b * T).wait(), c)[1], 0)

    U = ROUTE_TILES
    group = (i % 2) * U

    @pl.when(i >= 2)
    def _():
        for u in range(U):
            _wait_tile(total_ref, (i - 2) * U + u, ybuf_ref.at[group + u], xb_ref, sems.at[group + u], True)

    L = ybuf_ref.shape[1]
    RT = xt_ref.shape[0] // U
    r_iota = lax.broadcasted_iota(I32, (L, RT), 0)
    for u in range(U):
        ld = ld_ref[:, u * RT:(u + 1) * RT]
        onehot = jnp.zeros((L, RT), F32)
        for k in range(TOP_K):
            onehot = jnp.where(r_iota == ld[k:k + 1, :], 1.0, onehot)
        ybuf_ref[group + u] = _dot(onehot.astype(BF16), xt_ref[u * RT:(u + 1) * RT, :])
    for u in range(U):
        _segment_copies(seg_ref, lstart_ref, gstart_ref, i * U + u, ybuf_ref.at[group + u], xb_ref,
                        sems.at[group + u], True, lambda c: c.start())

    @pl.when(i == n - 1)
    def _():
        @pl.when(n >= 2)
        def _():
            for u in range(U):
                _wait_tile(total_ref, (i - 1) * U + u, ybuf_ref.at[U - group + u], xb_ref, sems.at[U - group + u], True)

        for u in range(U):
            _wait_tile(total_ref, i * U + u, ybuf_ref.at[group + u], xb_ref, sems.at[group + u], True)


def _dispatch(xt, ld, seg, lstart, gstart, totals, pad_ends, padded, R):
    N, D = xt.shape
    RT = ROUTE_T
    grid_spec = pltpu.PrefetchScalarGridSpec(
        num_scalar_prefetch=6,
        grid=(N // (RT * ROUTE_TILES),),
        in_specs=[
            pl.BlockSpec((TOP_K, RT * ROUTE_TILES), lambda i, *_: (0, i)),
            pl.BlockSpec((RT * ROUTE_TILES, D), lambda i, *_: (i, 0)),
        ],
        out_specs=pl.BlockSpec(memory_space=pl.ANY),
        scratch_shapes=[pltpu.VMEM((2 * ROUTE_TILES, LOCAL_ROWS, D), F32), pltpu.VMEM((MOE_T, D), F32),
                        pltpu.SemaphoreType.DMA(()), pltpu.SemaphoreType.DMA((2 * ROUTE_TILES,))],
    )
    return pl.pallas_call(
        _dispatch_kernel,
        grid_spec=grid_spec,
        out_shape=jax.ShapeDtypeStruct((R, D), F32),
        compiler_params=_cparams(("arbitrary",), VMEM_MEDIUM_MIB),
        name="dispatch",
    )(seg, lstart, gstart, totals, pad_ends, padded, ld, xt)


def _expert_kernel(start_ref, nblk_ref, b1_ref, b2_ref, perm_ref, w1_hbm, w2_hbm, xb_ref, yb_ref,
                   w1f_ref, w2f_ref, w1p_ref, w2b_ref, xbuf_ref, ybuf_ref, w_sems, in_sems, out_sems):
    e = pl.program_id(0)
    n_experts = pl.num_programs(0)
    ws = e % 2

    def w_copies(ex, slot):
        return (pltpu.make_async_copy(w1_hbm.at[ex], w1f_ref.at[slot], w_sems.at[slot, 0]),
                pltpu.make_async_copy(w2_hbm.at[ex], w2f_ref.at[slot], w_sems.at[slot, 1]))
    T = xbuf_ref.shape[1]
    G = 2 * LANES
    nb = nblk_ref[e]
    base = start_ref[e]

    def rows(b):
        return pl.ds(pl.multiple_of(base + b * T, T), T)

    def in_copy(b, slot):
        return pltpu.make_async_copy(xb_ref.at[rows(b)], xbuf_ref.at[slot], in_sems.at[slot])

    def out_copy(b, slot):
        return pltpu.make_async_copy(ybuf_ref.at[slot], yb_ref.at[rows(b)], out_sems.at[slot])

    for d in range(IN_DEPTH - 1):
        @pl.when(nb > d)
        def _():
            in_copy(d, d).start()

    @pl.when(e == 0)
    def _():
        for c in w_copies(0, 0):
            c.start()

    for c in w_copies(e, ws):
        c.wait()

    @pl.when(e + 1 < n_experts)
    def _():
        for c in w_copies(e + 1, 1 - ws):
            c.start(priority=1)

    @pl.when(nb > 0)
    def _():
        for g in range(w1p_ref.shape[1] // G):
            w = w1f_ref[ws, :, g * G:(g + 1) * G].astype(BF16)
            w1p_ref[:, g * G:(g + 1) * G] = _dot(w, perm_ref[...]).astype(BF16)
        w2b_ref[...] = w2f_ref[ws].astype(BF16)

        def block(b, carry):
            slot = b % 2
            islot = b % IN_DEPTH
            in_copy(b, islot).wait()

            @pl.when(b + IN_DEPTH - 1 < nb)
            def _():
                in_copy(b + IN_DEPTH - 1, (b + IN_DEPTH - 1) % IN_DEPTH).start()

            x = xbuf_ref[islot].astype(BF16)
            acts = []
            for g in range(w1p_ref.shape[1] // G):
                hg = _dot(x, w1p_ref[:, g * G:(g + 1) * G]) + b1_ref[0, :, g * G:(g + 1) * G]
                glu = jnp.minimum(hg[:, :LANES], SWIGLU_LIMIT)
                lin = jnp.clip(hg[:, LANES:], -SWIGLU_LIMIT, SWIGLU_LIMIT)
                acts.append((glu * jax.nn.sigmoid(SWIGLU_ALPHA * glu) * (lin + 1.0)).astype(BF16))
            y = _dot(jnp.concatenate(acts, axis=-1), w2b_ref[...]) + b2_ref[0]

            @pl.when(b >= 2)
            def _():
                out_copy(b - 2, slot).wait()

            ybuf_ref[slot] = y
            out_copy(b, slot).start()
            return carry

        lax.fori_loop(0, nb, block, 0)

        @pl.when(nb >= 2)
        def _():
            out_copy(nb - 2, nb % 2).wait()

        out_copy(nb - 1, (nb - 1) % 2).wait()

    @pl.when(e == pl.num_programs(0) - 1)
    def _():
        ybuf_ref[0] = jnp.zeros(ybuf_ref.shape[1:], ybuf_ref.dtype)
        first_unused = base // T + nb
        n_blocks = yb_ref.shape[0] // T

        def tail(b):
            return pltpu.make_async_copy(ybuf_ref.at[0], yb_ref.at[pl.ds(pl.multiple_of(b * T, T), T)], out_sems.at[0])

        lax.fori_loop(first_unused, n_blocks, lambda b, c: (tail(b).start(), c)[1], 0)
        lax.fori_loop(first_unused, n_blocks, lambda b, c: (tail(b).wait(), c)[1], 0)


def _experts(xb, region_start, region_blocks, w1, b1p, w2, b2):
    R, D = xb.shape
    E, _, F2 = w1.shape
    F = F2 // 2
    T = MOE_T
    G = 2 * LANES
    c = np.arange(G)
    src = np.where(c < LANES, 2 * c, 2 * (c - LANES) + 1)
    perm = jnp.asarray(np.arange(G)[:, None] == src[None, :], dtype=BF16)
    per_expert = lambda r, w: pl.BlockSpec((1, r, w), lambda e, *_: (e, 0, 0))
    grid_spec = pltpu.PrefetchScalarGridSpec(
        num_scalar_prefetch=2,
        grid=(E,),
        in_specs=[
            per_expert(1, F2), per_expert(1, D),
            pl.BlockSpec((G, G), lambda e, *_: (0, 0)),
            pl.BlockSpec(memory_space=pl.ANY), pl.BlockSpec(memory_space=pl.ANY), pl.BlockSpec(memory_space=pl.ANY),
        ],
        out_specs=pl.BlockSpec(memory_space=pl.ANY),
        scratch_shapes=[pltpu.VMEM((2, D, F2), F32), pltpu.VMEM((2, F, D), F32),
                        pltpu.VMEM((D, F2), BF16), pltpu.VMEM((F, D), BF16),
                        pltpu.VMEM((IN_DEPTH, T, D), F32), pltpu.VMEM((2, T, D), F32),
                        pltpu.SemaphoreType.DMA((2, 2)), pltpu.SemaphoreType.DMA((IN_DEPTH,)),
                        pltpu.SemaphoreType.DMA((2,))],
    )
    return pl.pallas_call(
        _expert_kernel,
        grid_spec=grid_spec,
        out_shape=jax.ShapeDtypeStruct((R, D), F32),
        compiler_params=_cparams(("arbitrary",), VMEM_LARGE_MIB),
        name="experts",
    )(region_start, region_blocks, b1p, b2.reshape(E, 1, D), perm, w1, w2, xb)


def _combine_kernel(seg_ref, lstart_ref, gstart_ref, total_ref, ldt_ref, wt_ref, x1_ref, yb_ref, o_ref, ybuf_ref,
                    sems):
    i = pl.program_id(0)
    n = pl.num_programs(0)
    U = ROUTE_TILES
    group = (i % 2) * U

    def fetch(step, g):
        for u in range(U):
            _segment_copies(seg_ref, lstart_ref, gstart_ref, step * U + u, ybuf_ref.at[g + u], yb_ref,
                            sems.at[g + u], False, lambda c: c.start())

    @pl.when(i == 0)
    def _():
        ybuf_ref[...] = jnp.zeros_like(ybuf_ref)
        fetch(i, group)

    @pl.when(i + 1 < n)
    def _():
        fetch(i + 1, U - group)

    for u in range(U):
        _wait_tile(total_ref, i * U + u, ybuf_ref.at[group + u], yb_ref, sems.at[group + u], False)

    L = ybuf_ref.shape[1]
    RT = x1_ref.shape[0] // U
    c_iota = lax.broadcasted_iota(I32, (RT, L), 1)
    for u in range(U):
        rows = slice(u * RT, (u + 1) * RT)
        ldt = ldt_ref[rows, :]
        w = wt_ref[rows, :]
        w_hi = w.astype(BF16).astype(F32)
        w_lo = w - w_hi
        g_hi = jnp.zeros((RT, L), F32)
        g_lo = jnp.zeros((RT, L), F32)
        for k in range(TOP_K):
            hit = c_iota == ldt[:, k:k + 1]
            g_hi = jnp.where(hit, w_hi[:, k:k + 1], g_hi)
            g_lo = jnp.where(hit, w_lo[:, k:k + 1], g_lo)
        y = ybuf_ref[group + u].astype(BF16)
        o_ref[rows, :] = x1_ref[rows, :] + (_dot(g_hi.astype(BF16), y) + _dot(g_lo.astype(BF16), y))


def _combine(ldt, wts_t, x1, yb, seg, lstart, gstart, totals):
    N, D = x1.shape
    RT = ROUTE_T
    grid_spec = pltpu.PrefetchScalarGridSpec(
        num_scalar_prefetch=4,
        grid=(N // (RT * ROUTE_TILES),),
        in_specs=[
            pl.BlockSpec((RT * ROUTE_TILES, TOP_K), lambda i, *_: (i, 0)),
            pl.BlockSpec((RT * ROUTE_TILES, TOP_K), lambda i, *_: (i, 0)),
            pl.BlockSpec((RT * ROUTE_TILES, D), lambda i, *_: (i, 0)),
            pl.BlockSpec(memory_space=pl.ANY),
        ],
        out_specs=pl.BlockSpec((RT * ROUTE_TILES, D), lambda i, *_: (i, 0)),
        scratch_shapes=[pltpu.VMEM((2 * ROUTE_TILES, LOCAL_ROWS, D), F32), pltpu.SemaphoreType.DMA((2 * ROUTE_TILES,))],
    )
    return pl.pallas_call(
        _combine_kernel,
        grid_spec=grid_spec,
        out_shape=jax.ShapeDtypeStruct((N, D), F32),
        compiler_params=_cparams(("arbitrary",), VMEM_MEDIUM_MIB),
        name="combine",
    )(seg, lstart, gstart, totals, ldt, wts_t, x1, yb)


def _layer(x, mem, g_mix, w_in, b_gate, g_ret_out, g_moba_q, g_moba_k, g_mem, w_mem_kv, g_mem_q, g_mem_k,
           w_br_ret, w_br_moba, w_br_mem, w_out, g_ffn, w_router, b_router, w_mlp1, b_mlp1, w_mlp2, b_mlp2):
    B, S, D = x.shape
    N = B * S
    x2 = x.reshape(N, D)
    slopes_np = np.exp2(-8.0 * (np.arange(MOBA_HEADS, dtype=np.float64) + 1.0) / MOBA_HEADS)
    assert all(float(np.log2(s)).is_integer() for s in slopes_np)
    slopes = tuple(float(s) for s in slopes_np)

    w_mix = w_in[:, :MIX_W].astype(BF16)
    w_gate = w_in[:, MIX_W:].astype(BF16)
    kmem, vmem = _memkv(mem, g_mem, w_mem_kv, g_mem_k)
    rq, rk, rv, rg, mq, mk, mv, cq, km = _inproj(x2, B, S, g_mix, w_mix, g_moba_q, g_moba_k, g_mem_q, slopes)
    o_ret = _retention(rq, rk, rv, rg, g_ret_out)
    o_moba = _moba(mq, mk, mv, km, jnp.asarray(slopes, F32)).reshape(N, MOBA_W)
    x1, xt, eidx, wts, rank, cnt = _merge(x2, S, o_ret, o_moba, cq, kmem, vmem, g_mix, w_gate, b_gate,
                                          w_br_ret, w_br_moba, w_br_mem, w_out, g_ffn, w_router, b_router)
    T = MOE_T
    tcnt = cnt[:, :, 0]
    seg = ((tcnt + SEG_ALIGN - 1) // SEG_ALIGN) * SEG_ALIGN
    region_rows = jnp.sum(seg, axis=0)
    padded = ((region_rows + T - 1) // T) * T
    pad_ends = jnp.cumsum(padded).astype(I32)
    pad_starts = pad_ends - padded
    gstart = pad_starts[None, :] + jnp.cumsum(seg, axis=0) - seg
    lstart = jnp.cumsum(seg, axis=1) - seg
    lstart_tok = jnp.broadcast_to(lstart[:, None, :], (N // ROUTE_T, ROUTE_T, N_EXPERTS)).reshape(N, N_EXPERTS)
    onehot = eidx[:, :, None] == jnp.arange(N_EXPERTS, dtype=I32)[None, None, :]
    ld = jnp.sum(jnp.where(onehot, lstart_tok[None], 0), axis=-1) + rank
    NB = -(-(N // ROUTE_T * LOCAL_ROWS) // T) + N_EXPERTS
    seg_f, lstart_f, gstart_f = (a.reshape(-1).astype(I32) for a in (seg, lstart, gstart))
    totals = jnp.sum(seg, axis=1).astype(I32)

    xb = _dispatch(xt, ld, seg_f, lstart_f, gstart_f, totals, pad_ends, padded.astype(I32), NB * T)
    F2 = w_mlp1.shape[-1]
    b1p = b_mlp1.reshape(N_EXPERTS, F2 // (2 * LANES), LANES, 2).transpose(0, 1, 3, 2).reshape(N_EXPERTS, 1, F2)
    yb = _experts(xb, pad_starts.astype(I32), (padded // T).astype(I32), w_mlp1, b1p, w_mlp2, b_mlp2)
    out = _combine(ld.T, wts.T, x1, yb, seg_f, lstart_f, gstart_f, totals)
    return out.reshape(B, S, D)


def kernel(x, mem, g_mix, w_in, b_gate, g_ret_out, g_moba_q, g_moba_k, g_mem, w_mem_kv, g_mem_q, g_mem_k, w_br_ret, w_br_moba, w_br_mem, w_out, g_ffn, w_router, b_router, w_mlp1, b_mlp1, w_mlp2, b_mlp2):
    for l in range(g_mix.shape[0]):
        x = _layer(x, mem, g_mix[l], w_in[l], b_gate[l], g_ret_out[l], g_moba_q[l], g_moba_k[l], g_mem[l],
                   w_mem_kv[l], g_mem_q[l], g_mem_k[l], w_br_ret[l], w_br_moba[l], w_br_mem[l], w_out[l],
                   g_ffn[l], w_router[l], b_router[l], w_mlp1[l], b_mlp1[l], w_mlp2[l], b_mlp2[l])
    return x
```

```python
import functools

import jax
import jax.numpy as jnp
import numpy as np
from jax import lax
from jax.experimental import pallas as pl
from jax.experimental.pallas import tpu as pltpu

F32 = jnp.float32
BF16 = jnp.bfloat16
I32 = jnp.int32

EPS = 1e-5
NEG = -1e30

RET_HEADS = 4
RET_DK = 64
RET_DV = 128
RET_CHUNK = 128
MOBA_HEADS = 8
MOBA_HD = 64
MOBA_BLOCK = 256
MOBA_TOPK = 3
MEM_HEADS = 4
MEM_HD = 128
N_BRANCH = 3
N_EXPERTS = 32
TOP_K = 4
SWIGLU_LIMIT = 7.0
SWIGLU_ALPHA = 1.702

RET_Q = RET_HEADS * RET_DK
RET_V = RET_HEADS * RET_DV
MOBA_W = MOBA_HEADS * MOBA_HD
MEM_W = MEM_HEADS * MEM_HD
MIX_W = 2 * RET_Q + 2 * RET_V + 3 * MOBA_W + MEM_W

LANES = 128
V7X_VMEM_MIB = 64
VMEM_LARGE_MIB = V7X_VMEM_MIB - 8
VMEM_MEDIUM_MIB = V7X_VMEM_MIB * 3 // 4
VMEM_SMALL_MIB = V7X_VMEM_MIB // 2
MOBA_PAIRS = MOBA_HEADS // 2
MOBA_HEADS_PER_STEP = 8
BIAS_LANE0 = MOBA_HD
ONE_LANE = 80
VSUM_LANE = MOBA_HD

MOE_T = 256
TM_PROJ = 512
TM_MERGE = 512
TQ_RET = 512
ROUTE_T = 256
ROUTE_TILES = 2
SEG_ALIGN = 8
LOCAL_ROWS = -(-(ROUTE_T * TOP_K + N_EXPERTS * (SEG_ALIGN - 1)) // LANES) * LANES
SEG_CHUNKS = tuple(2 ** p for p in range(ROUTE_T.bit_length() - 1, SEG_ALIGN.bit_length() - 2, -1))
SEG_SMALL = 64
TOTAL_CHUNKS = tuple(2 ** p for p in range(LOCAL_ROWS.bit_length() - 1, SEG_ALIGN.bit_length() - 2, -1))

_NT = (((1,), (1,)), ((), ()))
_TN = (((0,), (0,)), ((), ()))


def _rms(x, g):
    return x * lax.rsqrt(jnp.mean(x * x, axis=-1, keepdims=True) + EPS) * g


def _dot(a, b):
    return jnp.dot(a, b, preferred_element_type=F32)


def _cparams(sem, vmem_mb):
    return pltpu.CompilerParams(dimension_semantics=sem, vmem_limit_bytes=vmem_mb * 1024 * 1024)


def _memkv_kernel(mem_ref, g_ref, w_ref, gk_ref, k_ref, v_ref):
    m = _rms(mem_ref[0], g_ref[...]).astype(BF16)
    kv = _dot(m, w_ref[...])
    ks = [_rms(kv[:, h * MEM_HD:(h + 1) * MEM_HD], gk_ref[...]) for h in range(MEM_HEADS)]
    k_ref[0] = jnp.concatenate(ks, axis=-1).astype(BF16)
    v_ref[0] = kv[:, MEM_W:].astype(BF16)


def _memkv(mem, g_mem, w_mem_kv, g_mem_k):
    B, M, D = mem.shape
    return pl.pallas_call(
        _memkv_kernel,
        grid=(B,),
        in_specs=[
            pl.BlockSpec((1, M, D), lambda b: (b, 0, 0)),
            pl.BlockSpec((1, D), lambda b: (0, 0)),
            pl.BlockSpec((D, 2 * MEM_W), lambda b: (0, 0)),
            pl.BlockSpec((1, MEM_HD), lambda b: (0, 0)),
        ],
        out_specs=[
            pl.BlockSpec((1, M, MEM_W), lambda b: (b, 0, 0)),
            pl.BlockSpec((1, M, MEM_W), lambda b: (b, 0, 0)),
        ],
        out_shape=[jax.ShapeDtypeStruct((B, M, MEM_W), BF16)] * 2,
        compiler_params=_cparams(("arbitrary",), VMEM_SMALL_MIB),
        name="memkv",
    )(mem, g_mem.reshape(1, D), w_mem_kv.astype(BF16), g_mem_k.reshape(1, MEM_HD))


def _head_pair_norm(a2, g2, lane):
    sq = a2 * a2
    lo = lane < MOBA_HD
    ss_lo = jnp.sum(jnp.where(lo, sq, 0.0), axis=-1, keepdims=True)
    ss_hi = jnp.sum(jnp.where(lo, 0.0, sq), axis=-1, keepdims=True)
    inv = jnp.where(lo, lax.rsqrt(ss_lo / MOBA_HD + EPS), lax.rsqrt(ss_hi / MOBA_HD + EPS))
    return a2 * inv * g2


def _inproj_kernel(slopes, seq_tiles, x_ref, gmix_ref, w_ref, gq_ref, gk_ref, gc_ref,
                   rq_ref, rk_ref, rv_ref, rg_ref, mq_ref, mk_ref, mv_ref, cq_ref, kmean_ref):
    tm = x_ref.shape[0]
    blocks_per_tile = tm // MOBA_BLOCK
    h = _rms(x_ref[...], gmix_ref[...]).astype(BF16)
    col = [0]

    def proj(width):
        a = _dot(h, w_ref[:, col[0]:col[0] + width])
        col[0] += width
        return a

    a = proj(2 * RET_Q)
    for hh in range(RET_HEADS):
        rq_ref[0, hh] = a[:, hh * RET_DK:(hh + 1) * RET_DK].astype(BF16)
        rk_ref[0, hh] = (a[:, RET_Q + hh * RET_DK:RET_Q + (hh + 1) * RET_DK] * (RET_DK ** -0.5)).astype(BF16)
    rv_ref[...] = proj(RET_V).astype(BF16)
    rg_ref[...] = proj(RET_V).astype(BF16)

    lane = lax.broadcasted_iota(I32, (tm, LANES), 1)
    row = lax.broadcasted_iota(I32, (tm, LANES), 0)
    lo = lane < MOBA_HD
    q_tail = jnp.where(lane == ONE_LANE, 1.0, 0.0)
    a = proj(MOBA_W)
    for p in range(MOBA_PAIRS):
        n2 = _head_pair_norm(a[:, p * LANES:(p + 1) * LANES], gq_ref[...], lane) * (MOBA_HD ** -0.5)
        mq_ref[0, 2 * p] = jnp.where(lo, n2, q_tail).astype(BF16)
        mq_ref[0, 2 * p + 1] = jnp.where(lo, pltpu.roll(n2, MOBA_HD, 1), q_tail).astype(BF16)
    blk = (pl.program_id(0) % seq_tiles) * blocks_per_tile + row // MOBA_BLOCK
    onehot_tail = jnp.where(lane == BIAS_LANE0 + blk, 1.0, 0.0)
    off = (row % MOBA_BLOCK).astype(F32)
    a = proj(MOBA_W)
    for p in range(MOBA_PAIRS):
        n2 = _head_pair_norm(a[:, p * LANES:(p + 1) * LANES], gk_ref[...], lane)
        for j in range(blocks_per_tile):
            kmean_ref[0, 0, p, j:j + 1, :] = jnp.mean(n2[j * MOBA_BLOCK:(j + 1) * MOBA_BLOCK], axis=0, keepdims=True)
        for s, src in ((0, n2), (1, pltpu.roll(n2, MOBA_HD, 1))):
            tail = jnp.where(lane == ONE_LANE, slopes[2 * p + s] * off, onehot_tail)
            mk_ref[0, 2 * p + s] = jnp.where(lo, src, tail).astype(BF16)
    v_tail = jnp.where(lane == VSUM_LANE, 1.0, 0.0)
    a = proj(MOBA_W)
    for p in range(MOBA_PAIRS):
        a2 = a[:, p * LANES:(p + 1) * LANES]
        mv_ref[0, 2 * p] = jnp.where(lo, a2, v_tail).astype(BF16)
        mv_ref[0, 2 * p + 1] = jnp.where(lo, pltpu.roll(a2, MOBA_HD, 1), v_tail).astype(BF16)
    a = proj(MEM_W)
    cq = [_rms(a[:, hh * MEM_HD:(hh + 1) * MEM_HD], gc_ref[...]) for hh in range(MEM_HEADS)]
    cq_ref[...] = jnp.concatenate(cq, axis=-1).astype(BF16)


def _inproj(x2, B, S, g_mix, w_mix, g_moba_q, g_moba_k, g_mem_q, slopes):
    N, D = x2.shape
    tm = TM_PROJ
    nS = S // tm
    bpt = tm // MOBA_BLOCK
    tok = lambda i: (i, 0)
    headmaj = lambda i: (i // nS, 0, i % nS, 0)
    g2 = lambda g: jnp.concatenate([g, g]).reshape(1, LANES)
    outs = pl.pallas_call(
        functools.partial(_inproj_kernel, slopes, nS),
        grid=(N // tm,),
        in_specs=[
            pl.BlockSpec((tm, D), tok),
            pl.BlockSpec((1, D), lambda i: (0, 0)),
            pl.BlockSpec((D, MIX_W), lambda i: (0, 0)),
            pl.BlockSpec((1, LANES), lambda i: (0, 0)),
            pl.BlockSpec((1, LANES), lambda i: (0, 0)),
            pl.BlockSpec((1, MEM_HD), lambda i: (0, 0)),
        ],
        out_specs=[
            pl.BlockSpec((1, RET_HEADS, tm, RET_DK), headmaj),
            pl.BlockSpec((1, RET_HEADS, tm, RET_DK), headmaj),
            pl.BlockSpec((tm, RET_V), tok),
            pl.BlockSpec((tm, RET_V), tok),
            pl.BlockSpec((1, MOBA_HEADS, tm, LANES), headmaj),
            pl.BlockSpec((1, MOBA_HEADS, tm, LANES), headmaj),
            pl.BlockSpec((1, MOBA_HEADS, tm, LANES), headmaj),
            pl.BlockSpec((tm, MEM_W), tok),
            pl.BlockSpec((1, 1, MOBA_PAIRS, bpt, LANES), lambda i: (i // nS, i % nS, 0, 0, 0)),
        ],
        out_shape=[
            jax.ShapeDtypeStruct((B, RET_HEADS, S, RET_DK), BF16),
            jax.ShapeDtypeStruct((B, RET_HEADS, S, RET_DK), BF16),
            jax.ShapeDtypeStruct((N, RET_V), BF16),
            jax.ShapeDtypeStruct((N, RET_V), BF16),
            jax.ShapeDtypeStruct((B, MOBA_HEADS, S, LANES), BF16),
            jax.ShapeDtypeStruct((B, MOBA_HEADS, S, LANES), BF16),
            jax.ShapeDtypeStruct((B, MOBA_HEADS, S, LANES), BF16),
            jax.ShapeDtypeStruct((N, MEM_W), BF16),
            jax.ShapeDtypeStruct((B, nS, MOBA_PAIRS, bpt, LANES), F32),
        ],
        compiler_params=_cparams(("arbitrary",), VMEM_LARGE_MIB),
        name="inproj",
    )(x2, g_mix.reshape(1, D), w_mix, g2(g_moba_q), g2(g_moba_k), g_mem_q.reshape(1, MEM_HD))
    rq, rk, rv, rg, mq, mk, mv, cq, kmean_pairs = outs
    km = kmean_pairs.reshape(B, nS, MOBA_PAIRS, bpt, 2, MOBA_HD).transpose(0, 2, 4, 1, 3, 5)
    km = km.reshape(B, MOBA_HEADS, nS * bpt, MOBA_HD)
    km = jnp.pad(km, ((0, 0), (0, 0), (0, 0), (0, LANES - MOBA_HD)))
    return rq, rk, rv, rg, mq, mk, mv, cq, km


def _retention_kernel(q_ref, k_ref, v_ref, rg_ref, din_ref, dq_ref, dk_ref, dc_ref, g_ref, o_ref, state_ref):
    @pl.when(pl.program_id(1) == 0)
    def _():
        state_ref[...] = jnp.zeros_like(state_ref)

    C = RET_CHUNK
    for h in range(q_ref.shape[1]):
        cols = slice(h * RET_DV, (h + 1) * RET_DV)
        state = state_ref[h]
        for c in range(q_ref.shape[2] // C):
            rows = slice(c * C, (c + 1) * C)
            q = q_ref[0, h, rows, :]
            k = k_ref[0, h, rows, :]
            v = v_ref[rows, cols]
            scores = lax.dot_general(q, k, _NT, preferred_element_type=F32) * din_ref[h]
            intra = _dot(scores.astype(BF16), v)
            cross = _dot(q, state.astype(BF16)) * dq_ref[h]
            kd = (k.astype(F32) * dk_ref[h]).astype(BF16)
            state = dc_ref[h] * state + lax.dot_general(kd, v, _TN, preferred_element_type=F32)
            o = _rms(intra + cross, g_ref[...])
            o_ref[rows, cols] = (o * jax.nn.silu(rg_ref[rows, cols].astype(F32))).astype(BF16)
        state_ref[h] = state


def _retention_decays():
    H, C = RET_HEADS, RET_CHUNK
    log_g = jnp.log1p(-jnp.exp2(-5.0 - jnp.arange(H, dtype=F32)))
    i = jnp.arange(C, dtype=F32)
    diff = i[:, None] - i[None, :]
    decay_in = jnp.where(diff >= 0, jnp.exp(jnp.maximum(diff, 0.0)[None] * log_g[:, None, None]), 0.0)
    decay_k = jnp.exp((C - 1 - i)[None, :] * log_g[:, None])
    decay_q = jnp.exp((i + 1)[None, :] * log_g[:, None])
    decay_chunk = jnp.exp(C * log_g)
    dq = jnp.broadcast_to(decay_q[:, :, None], (H, C, RET_DV))
    dk = jnp.broadcast_to(decay_k[:, :, None], (H, C, RET_DK))
    dc = jnp.broadcast_to(decay_chunk[:, None, None], (H, RET_DK, RET_DV))
    return decay_in, dq, dk, dc


def _retention(rq, rk, rv, rg, g_ret_out):
    B, H, S, dk = rq.shape
    tq = TQ_RET
    nT = S // tq
    C = RET_CHUNK
    din, dq, dk_, dc = _retention_decays()
    qk_spec = pl.BlockSpec((1, H, tq, dk), lambda b, t: (b, 0, t, 0))
    tok_spec = pl.BlockSpec((tq, RET_V), lambda b, t: (b * nT + t, 0))
    const = lambda r, c: pl.BlockSpec((H, r, c), lambda b, t: (0, 0, 0))
    return pl.pallas_call(
        _retention_kernel,
        grid=(B, nT),
        in_specs=[qk_spec, qk_spec, tok_spec, tok_spec,
                  const(C, C), const(C, RET_DV), const(C, RET_DK), const(RET_DK, RET_DV),
                  pl.BlockSpec((1, RET_DV), lambda b, t: (0, 0))],
        out_specs=tok_spec,
        out_shape=jax.ShapeDtypeStruct((B * S, RET_V), BF16),
        scratch_shapes=[pltpu.VMEM((H, RET_DK, RET_DV), F32)],
        compiler_params=_cparams(("arbitrary", "arbitrary"), VMEM_SMALL_MIB),
        name="retention",
    )(rq, rk, rv, rg, din, dq, dk_, dc, g_ret_out.reshape(1, RET_DV))


def _moba_kernel(slopes_ref, q_ref, k_ref, v_ref, km_ref, shift_ref, mask_ref, o_ref):
    NH = q_ref.shape[1]
    head0 = pl.program_id(1) * NH
    i0 = pl.program_id(2) * 2
    BS = MOBA_BLOCK
    nb = km_ref.shape[2]
    heads = range(NH)
    n_iota = lax.broadcasted_iota(I32, (nb, 2 * BS), 0)
    q_blk = i0 + lax.broadcasted_iota(I32, (nb, 2 * BS), 1) // BS
    past = n_iota < q_blk
    qa = [q_ref[0, s] for s in heads]
    q_aug = []
    for s in heads:
        km = km_ref[0, s]
        hi = km.astype(BF16)
        mid = (km - hi.astype(F32)).astype(BF16)
        lo = (km - hi.astype(F32) - mid.astype(F32)).astype(BF16)
        g3 = lax.dot_general(jnp.concatenate([hi, mid, lo], axis=0), qa[s], _NT, preferred_element_type=F32)
        gate = (g3[:nb] + g3[nb:2 * nb]) + g3[2 * nb:]
        g = jnp.where(past, gate, -jnp.inf)
        rank = jnp.zeros((nb, 2 * BS), I32)
        for m in range(nb - 1):
            gm = g[m:m + 1, :]
            rank = rank + jnp.where(gm > g, 1, jnp.where(gm == g, jnp.where(m < n_iota, 1, 0), 0))
        sel = jnp.where(past, rank, MOBA_TOPK) < MOBA_TOPK
        bias = jnp.where(sel, (n_iota - q_blk).astype(F32) * (slopes_ref[head0 + s] * BS), NEG).astype(BF16)
        placed = lax.dot_general(bias, shift_ref[...], _TN, preferred_element_type=F32)
        q_aug.append((qa[s].astype(F32) + placed).astype(BF16))

    def rows_of(j):
        return pl.ds(pl.multiple_of(j * BS, BS), BS)

    m_lo, m_hi, acc_lo, acc_hi = [], [], [], []
    for s in heads:
        q_mix = jnp.concatenate([qa[s][:BS], q_aug[s][BS:]], axis=0)
        sc = lax.dot_general(q_mix, k_ref[0, s, rows_of(i0), :], _NT, preferred_element_type=F32) + mask_ref[...]
        m = jnp.max(sc, axis=-1, keepdims=True)
        acc = _dot(jnp.exp(sc - m).astype(BF16), v_ref[0, s, rows_of(i0), :])
        m_lo.append(m[:BS])
        acc_lo.append(acc[:BS])
        sc1 = lax.dot_general(qa[s][BS:], k_ref[0, s, rows_of(i0 + 1), :], _NT, preferred_element_type=F32)
        sc1 = sc1 + mask_ref[:BS, :]
        m1 = jnp.maximum(m[BS:], jnp.max(sc1, axis=-1, keepdims=True))
        p1 = jnp.exp(sc1 - m1).astype(BF16)
        m_hi.append(m1)
        acc_hi.append(jnp.exp(m[BS:] - m1) * acc[BS:] + _dot(p1, v_ref[0, s, rows_of(i0 + 1), :]))

    def body(j, carry):
        ms_lo, ms_hi, accs_lo, accs_hi = carry
        out = ([], [], [], [])
        for s in heads:
            sj = lax.dot_general(q_aug[s], k_ref[0, s, rows_of(j), :], _NT, preferred_element_type=F32)
            ps, alphas, ms = [], [], []
            for half, m_prev in ((slice(0, BS), ms_lo[s]), (slice(BS, 2 * BS), ms_hi[s])):
                m_new = jnp.maximum(m_prev, jnp.max(sj[half], axis=-1, keepdims=True))
                ps.append(jnp.exp(sj[half] - m_new).astype(BF16))
                alphas.append(jnp.exp(m_prev - m_new))
                ms.append(m_new)
            pv = _dot(jnp.concatenate(ps, axis=0), v_ref[0, s, rows_of(j), :])
            out[0].append(ms[0])
            out[1].append(ms[1])
            out[2].append(alphas[0] * accs_lo[s] + pv[:BS])
            out[3].append(alphas[1] * accs_hi[s] + pv[BS:])
        return tuple(tuple(o) for o in out)

    _, _, accs_lo, accs_hi = lax.fori_loop(0, i0 // 2, lambda j, c: body(2 * j + 1, body(2 * j, c)),
                                           (tuple(m_lo), tuple(m_hi), tuple(acc_lo), tuple(acc_hi)))
    lane = lax.broadcasted_iota(I32, (BS, LANES), 1)
    for t, accs in enumerate((accs_lo, accs_hi)):
        outs = [acc / acc[:, VSUM_LANE:VSUM_LANE + 1] for acc in accs]
        pairs = [jnp.where(lane < MOBA_HD, outs[2 * p], pltpu.roll(outs[2 * p + 1], MOBA_HD, 1))
                 for p in range(NH // 2)]
        o_ref[0, t * BS:(t + 1) * BS, :] = jnp.concatenate(pairs, axis=-1).astype(BF16)


def _moba(mq, mk, mv, km, slopes):
    B, H, S, _ = mq.shape
    BS = MOBA_BLOCK
    NH = MOBA_HEADS_PER_STEP
    nq = S // BS
    nb = km.shape[2]
    assert BIAS_LANE0 + nb <= ONE_LANE and nq % 2 == 0
    shift = (jnp.arange(nb)[:, None] + BIAS_LANE0 == jnp.arange(LANES)[None, :]).astype(BF16)
    causal = jnp.where(jnp.arange(BS)[:, None] >= jnp.arange(BS)[None, :], 0.0, NEG).astype(F32)
    mask = jnp.concatenate([causal, jnp.zeros((BS, BS), F32)], axis=0)
    per_head = lambda rows: pl.BlockSpec((1, NH, rows, LANES), lambda b, p, i, sl: (b, p, 0, 0))
    grid_spec = pltpu.PrefetchScalarGridSpec(
        num_scalar_prefetch=1,
        grid=(B, H // NH, nq // 2),
        in_specs=[
            pl.BlockSpec((1, NH, 2 * BS, LANES), lambda b, p, i, sl: (b, p, i, 0)),
            per_head(S), per_head(S), per_head(nb),
            pl.BlockSpec((nb, LANES), lambda b, p, i, sl: (0, 0)),
            pl.BlockSpec((2 * BS, BS), lambda b, p, i, sl: (0, 0)),
        ],
        out_specs=pl.BlockSpec((1, 2 * BS, NH * MOBA_HD), lambda b, p, i, sl: (b, i, p)),
    )
    return pl.pallas_call(
        _moba_kernel,
        grid_spec=grid_spec,
        out_shape=jax.ShapeDtypeStruct((B, S, MOBA_W), BF16),
        compiler_params=_cparams(("arbitrary", "arbitrary", "arbitrary"), VMEM_LARGE_MIB),
        name="moba",
    )(slopes, mq, mk, mv, km, shift, mask)


def _merge_kernel(x_ref, oret_ref, omoba_ref, cq_ref, kmem_ref, vmem_ref, gmix_ref, wg_ref, bg_ref,
                  wbr_ref, wbm_ref, wbc_ref, wout_ref, gffn_ref, wr_ref, br_ref, tri_ref,
                  x1_ref, xt_ref, eidx_ref, wts_ref, rank_ref, cnt_ref):
    D = x_ref.shape[1]
    tm = x_ref.shape[0]

    x = x_ref[...]
    h = _rms(x, gmix_ref[...]).astype(BF16)
    cq = cq_ref[...]
    om = []
    for hh in range(MEM_HEADS):
        cols = slice(hh * MEM_HD, (hh + 1) * MEM_HD)
        sc = lax.dot_general(cq[:, cols], kmem_ref[0, :, cols], _NT, preferred_element_type=F32) * (MEM_HD ** -0.5)
        sc = sc - jnp.max(sc, axis=-1, keepdims=True)
        p = jnp.exp(sc)
        p = p / jnp.sum(p, axis=-1, keepdims=True)
        om.append(_dot(p.astype(BF16), vmem_ref[0, :, cols]))
    omem = jnp.concatenate(om, axis=-1).astype(BF16)
    y = None
    for br, (o, w_ref) in enumerate(((oret_ref[...], wbr_ref), (omoba_ref[...], wbm_ref), (omem, wbc_ref))):
        gl = _dot(h, wg_ref[:, br * D:(br + 1) * D]) + bg_ref[:, br * D:(br + 1) * D]
        term = jax.nn.sigmoid(gl) * _dot(o, w_ref[...])
        y = term if y is None else y + term
    x1 = x + _dot(y.astype(BF16), wout_ref[...])
    x1_ref[...] = x1
    xt = _rms(x1, gffn_ref[...])
    wr = wr_ref[...]
    E = wr.shape[0]
    w_hi = wr.astype(BF16)
    w_mid = (wr - w_hi.astype(F32)).astype(BF16)
    w_lo = (wr - w_hi.astype(F32) - w_mid.astype(F32)).astype(BF16)
    xt_hi = xt.astype(BF16)
    xt_ref[...] = xt_hi
    xt_lo = (xt - xt_hi.astype(F32)).astype(BF16)
    a = lax.dot_general(jnp.concatenate([w_hi, w_mid, w_lo], axis=0), xt_hi, _NT, preferred_element_type=F32)
    b = lax.dot_general(jnp.concatenate([w_hi, w_mid], axis=0), xt_lo, _NT, preferred_element_type=F32)
    logits = (a[:E] + (a[E:2 * E] + b[:E])) + (a[2 * E:] + b[E:]) + br_ref[...]
    e_iota = lax.broadcasted_iota(I32, (E, tm), 0)
    l = logits
    vals, hots = [], []
    for k in range(TOP_K):
        m = jnp.max(l, axis=0, keepdims=True)
        idx = jnp.min(jnp.where(l == m, e_iota, E), axis=0, keepdims=True)
        hot = e_iota == idx
        l = jnp.where(hot, -jnp.inf, l)
        vals.append(m)
        hots.append(hot)
        eidx_ref[k:k + 1, :] = idx
    ex = [jnp.exp(v - vals[0]) for v in vals]
    den = ex[0]
    for k in range(1, TOP_K):
        den = den + ex[k]
    chosen = jnp.zeros((E, tm), F32)
    for k in range(TOP_K):
        wts_ref[k:k + 1, :] = ex[k] / den
        chosen = chosen + jnp.where(hots[k], 1.0, 0.0)
    RT = tri_ref.shape[0]
    for t in range(tm // RT):
        cols = slice(t * RT, (t + 1) * RT)
        prefix = _dot(chosen[:, cols].astype(BF16), tri_ref[...])
        for k in range(TOP_K):
            rank_ref[k:k + 1, cols] = jnp.sum(jnp.where(hots[k][:, cols], prefix, 0.0), axis=0, keepdims=True).astype(I32)
        counts = jnp.sum(chosen[:, cols], axis=1, keepdims=True)
        cnt_ref[t] = jnp.broadcast_to(counts, cnt_ref.shape[1:]).astype(I32)


def _merge(x2, S, o_ret, o_moba, cq, kmem, vmem, g_mix, w_gate, b_gate, w_br_ret, w_br_moba, w_br_mem, w_out,
           g_ffn, w_router, b_router):
    N, D = x2.shape
    tm = TM_MERGE
    nS = S // tm
    M = kmem.shape[1]
    E = w_router.shape[1]
    tok = lambda w: pl.BlockSpec((tm, w), lambda i: (i, 0))
    const = lambda r, c: pl.BlockSpec((r, c), lambda i: (0, 0), pipeline_mode=pl.Buffered(1))
    mem_spec = pl.BlockSpec((1, M, MEM_W), lambda i: (i // nS, 0, 0))
    lanes_tok = lambda r, dt: (pl.BlockSpec((r, tm), lambda i: (0, i)), jax.ShapeDtypeStruct((r, N), dt))
    RT = ROUTE_T
    tri = (jnp.arange(RT)[:, None] < jnp.arange(RT)[None, :]).astype(BF16)
    e_spec, e_shape = lanes_tok(TOP_K, I32)
    w_spec, w_shape = lanes_tok(TOP_K, F32)
    r_spec, r_shape = lanes_tok(TOP_K, I32)
    return pl.pallas_call(
        _merge_kernel,
        grid=(N // tm,),
        in_specs=[tok(D), tok(RET_V), tok(MOBA_W), tok(MEM_W), mem_spec, mem_spec,
                  const(1, D), const(D, N_BRANCH * D), const(1, N_BRANCH * D),
                  const(RET_V, D), const(MOBA_W, D), const(MEM_W, D), const(D, D),
                  const(1, D), const(E, D), const(E, 1), const(RT, RT)],
        out_specs=[tok(D), tok(D), e_spec, w_spec, r_spec, pl.BlockSpec((tm // RT, E, LANES), lambda i: (i, 0, 0))],
        out_shape=[jax.ShapeDtypeStruct((N, D), F32), jax.ShapeDtypeStruct((N, D), BF16),
                   e_shape, w_shape, r_shape, jax.ShapeDtypeStruct((N // RT, E, LANES), I32)],
        compiler_params=_cparams(("arbitrary",), VMEM_LARGE_MIB),
        name="merge",
    )(x2, o_ret, o_moba, cq, kmem, vmem, g_mix.reshape(1, D), w_gate, b_gate.reshape(1, N_BRANCH * D),
      w_br_ret.astype(BF16), w_br_moba.astype(BF16), w_br_mem.astype(BF16), w_out.astype(BF16),
      g_ffn.reshape(1, D), w_router.T, b_router.reshape(E, 1), tri)


def _segment_copies(seg_ref, lstart_ref, gstart_ref, tile, local_ref, slots_ref, sem, to_slots, fn):
    def per_expert(e, carry):
        idx = tile * N_EXPERTS + e
        size = seg_ref[idx]
        lstart = lstart_ref[idx]
        gstart = gstart_ref[idx]

        def copy_chunks(chunks, off):
            for chunk in chunks:
                take = (size & chunk) != 0
                lo = pl.ds(pl.multiple_of(lstart + off, SEG_ALIGN), chunk)
                gl = pl.ds(pl.multiple_of(gstart + off, SEG_ALIGN), chunk)
                src, dst = (local_ref.at[lo], slots_ref.at[gl]) if to_slots else (slots_ref.at[gl], local_ref.at[lo])

                @pl.when(take)
                def _():
                    fn(pltpu.make_async_copy(src, dst, sem))

                off = off + jnp.where(take, chunk, 0)

        large = size & -SEG_SMALL

        @pl.when(large != 0)
        def _():
            copy_chunks([c for c in SEG_CHUNKS if c >= SEG_SMALL], jnp.int32(0))

        copy_chunks([c for c in SEG_CHUNKS if c < SEG_SMALL], large)
        return carry

    lax.fori_loop(0, N_EXPERTS, per_expert, 0)


def _wait_tile(total_ref, tile, local_ref, slots_ref, sem, to_slots):
    total = total_ref[tile]
    for chunk in TOTAL_CHUNKS:
        lo, gl = local_ref.at[pl.ds(0, chunk)], slots_ref.at[pl.ds(0, chunk)]
        src, dst = (lo, gl) if to_slots else (gl, lo)

        @pl.when((total & chunk) != 0)
        def _():
            pltpu.make_async_copy(src, dst, sem).wait()


def _dispatch_kernel(seg_ref, lstart_ref, gstart_ref, total_ref, pad_end_ref, padded_ref, ld_ref, xt_ref, xb_ref,
                     ybuf_ref, zeros_ref, zsem, sems):
    T = zeros_ref.shape[0]
    i = pl.program_id(0)
    n = pl.num_programs(0)

    @pl.when(i == 0)
    def _():
        zeros_ref[...] = jnp.zeros_like(zeros_ref)

        def fill(start):
            return pltpu.make_async_copy(zeros_ref, xb_ref.at[pl.ds(pl.multiple_of(start, T), T)], zsem)

        for e in range(N_EXPERTS):
            @pl.when(padded_ref[e] > 0)
            def _():
                fill(pad_end_ref[e] - T).start()
        for e in range(N_EXPERTS):
            @pl.when(padded_ref[e] > 0)
            def _():
                fill(pad_end_ref[e] - T).wait()
        first_unused = pad_end_ref[N_EXPERTS - 1] // T
        n_blocks = xb_ref.shape[0] // T
        lax.fori_loop(first_unused, n_blocks, lambda b, c: (fill(b * T).start(), c)[1], 0)
        lax.fori_loop(first_unused, n_blocks, lambda b, c: (fill(b * T).wait(), c)[1], 0)

    U = ROUTE_TILES
    group = (i % 2) * U

    @pl.when(i >= 2)
    def _():
        for u in range(U):
            _wait_tile(total_ref, (i - 2) * U + u, ybuf_ref.at[group + u], xb_ref, sems.at[group + u], True)

    L = ybuf_ref.shape[1]
    RT = xt_ref.shape[0] // U
    r_iota = lax.broadcasted_iota(I32, (L, RT), 0)
    for u in range(U):
        ld = ld_ref[:, u * RT:(u + 1) * RT]
        onehot = jnp.zeros((L, RT), F32)
        for k in range(TOP_K):
            onehot = jnp.where(r_iota == ld[k:k + 1, :], 1.0, onehot)
        ybuf_ref[group + u] = _dot(onehot.astype(BF16), xt_ref[u * RT:(u + 1) * RT, :])
    for u in range(U):
        _segment_copies(seg_ref, lstart_ref, gstart_ref, i * U + u, ybuf_ref.at[group + u], xb_ref,
                        sems.at[group + u], True, lambda c: c.start())

    @pl.when(i == n - 1)
    def _():
        @pl.when(n >= 2)
        def _():
            for u in range(U):
                _wait_tile(total_ref, (i - 1) * U + u, ybuf_ref.at[U - group + u], xb_ref, sems.at[U - group + u], True)

        for u in range(U):
            _wait_tile(total_ref, i * U + u, ybuf_ref.at[group + u], xb_ref, sems.at[group + u], True)


def _dispatch(xt, ld, seg, lstart, gstart, totals, pad_ends, padded, R):
    N, D = xt.shape
    RT = ROUTE_T
    grid_spec = pltpu.PrefetchScalarGridSpec(
        num_scalar_prefetch=6,
        grid=(N // (RT * ROUTE_TILES),),
        in_specs=[
            pl.BlockSpec((TOP_K, RT * ROUTE_TILES), lambda i, *_: (0, i)),
            pl.BlockSpec((RT * ROUTE_TILES, D), lambda i, *_: (i, 0)),
        ],
        out_specs=pl.BlockSpec(memory_space=pl.ANY),
        scratch_shapes=[pltpu.VMEM((2 * ROUTE_TILES, LOCAL_ROWS, D), F32), pltpu.VMEM((MOE_T, D), F32),
                        pltpu.SemaphoreType.DMA(()), pltpu.SemaphoreType.DMA((2 * ROUTE_TILES,))],
    )
    return pl.pallas_call(
        _dispatch_kernel,
        grid_spec=grid_spec,
        out_shape=jax.ShapeDtypeStruct((R, D), F32),
        compiler_params=_cparams(("arbitrary",), VMEM_MEDIUM_MIB),
        name="dispatch",
    )(seg, lstart, gstart, totals, pad_ends, padded, ld, xt)


def _expert_kernel(start_ref, nblk_ref, b1_ref, b2_ref, perm_ref, w1_hbm, w2_hbm, xb_ref, yb_ref,
                   w1f_ref, w2f_ref, w1p_ref, w2b_ref, xbuf_ref, ybuf_ref, xodd_ref, yodd_ref,
                   w_sems, in_sems, out_sems, odd_sems):
    e = pl.program_id(0)
    n_experts = pl.num_programs(0)
    ws = e % 2

    def w_copies(ex, slot):
        return (pltpu.make_async_copy(w1_hbm.at[ex], w1f_ref.at[slot], w_sems.at[slot, 0]),
                pltpu.make_async_copy(w2_hbm.at[ex], w2f_ref.at[slot], w_sems.at[slot, 1]))
    T = xodd_ref.shape[0]
    G = 2 * LANES
    nb = nblk_ref[e]
    base = start_ref[e]
    pairs = nb // 2
    odd = nb % 2

    def pair_rows(p):
        return pl.ds(pl.multiple_of(base + p * (2 * T), T), 2 * T)

    def in_copy(p, slot):
        return pltpu.make_async_copy(xb_ref.at[pair_rows(p)], xbuf_ref.at[slot], in_sems.at[slot])

    def out_copy(p, slot):
        return pltpu.make_async_copy(ybuf_ref.at[slot], yb_ref.at[pair_rows(p)], out_sems.at[slot])

    odd_rows = pl.ds(pl.multiple_of(base + (nb - 1) * T, T), T)
    odd_in = pltpu.make_async_copy(xb_ref.at[odd_rows], xodd_ref, odd_sems.at[0])
    odd_out = pltpu.make_async_copy(yodd_ref, yb_ref.at[odd_rows], odd_sems.at[1])

    @pl.when(odd == 1)
    def _():
        odd_in.start()

    @pl.when(pairs > 0)
    def _():
        in_copy(0, 0).start()

    @pl.when(e == 0)
    def _():
        for c in w_copies(0, 0):
            c.start()

    for c in w_copies(e, ws):
        c.wait()

    @pl.when(e + 1 < n_experts)
    def _():
        for c in w_copies(e + 1, 1 - ws):
            c.start(priority=1)

    def mlp(x):
        x = x.astype(BF16)
        acts = []
        for g in range(w1p_ref.shape[1] // G):
            hg = _dot(x, w1p_ref[:, g * G:(g + 1) * G]) + b1_ref[0, :, g * G:(g + 1) * G]
            glu = jnp.minimum(hg[:, :LANES], SWIGLU_LIMIT)
            lin = jnp.clip(hg[:, LANES:], -SWIGLU_LIMIT, SWIGLU_LIMIT)
            acts.append((glu * jax.nn.sigmoid(SWIGLU_ALPHA * glu) * (lin + 1.0)).astype(BF16))
        return _dot(jnp.concatenate(acts, axis=-1), w2b_ref[...]) + b2_ref[0]

    @pl.when(nb > 0)
    def _():
        for g in range(w1p_ref.shape[1] // G):
            w = w1f_ref[ws, :, g * G:(g + 1) * G].astype(BF16)
            w1p_ref[:, g * G:(g + 1) * G] = _dot(w, perm_ref[...]).astype(BF16)
        w2b_ref[...] = w2f_ref[ws].astype(BF16)

        @pl.when(odd == 1)
        def _():
            odd_in.wait()
            yodd_ref[...] = mlp(xodd_ref[...])
            odd_out.start()

        def pair(p, carry):
            slot = p % 2
            in_copy(p, slot).wait()

            @pl.when(p + 1 < pairs)
            def _():
                in_copy(p + 1, 1 - slot).start()

            y = mlp(xbuf_ref[slot])

            @pl.when(p >= 2)
            def _():
                out_copy(p - 2, slot).wait()

            ybuf_ref[slot] = y
            out_copy(p, slot).start()
            return carry

        lax.fori_loop(0, pairs, pair, 0)

        @pl.when(pairs >= 2)
        def _():
            out_copy(pairs - 2, pairs % 2).wait()

        @pl.when(pairs >= 1)
        def _():
            out_copy(pairs - 1, (pairs - 1) % 2).wait()

        @pl.when(odd == 1)
        def _():
            odd_out.wait()

    @pl.when(e == pl.num_programs(0) - 1)
    def _():
        yodd_ref[...] = jnp.zeros_like(yodd_ref)
        first_unused = base // T + nb
        n_blocks = yb_ref.shape[0] // T

        def tail(b):
            return pltpu.make_async_copy(yodd_ref, yb_ref.at[pl.ds(pl.multiple_of(b * T, T), T)], odd_sems.at[1])

        lax.fori_loop(first_unused, n_blocks, lambda b, c: (tail(b).start(), c)[1], 0)
        lax.fori_loop(first_unused, n_blocks, lambda b, c: (tail(b).wait(), c)[1], 0)


def _experts(xb, region_start, region_blocks, w1, b1p, w2, b2):
    R, D = xb.shape
    E, _, F2 = w1.shape
    F = F2 // 2
    T = MOE_T
    G = 2 * LANES
    c = np.arange(G)
    src = np.where(c < LANES, 2 * c, 2 * (c - LANES) + 1)
    perm = jnp.asarray(np.arange(G)[:, None] == src[None, :], dtype=BF16)
    per_expert = lambda r, w: pl.BlockSpec((1, r, w), lambda e, *_: (e, 0, 0))
    grid_spec = pltpu.PrefetchScalarGridSpec(
        num_scalar_prefetch=2,
        grid=(E,),
        in_specs=[
            per_expert(1, F2), per_expert(1, D),
            pl.BlockSpec((G, G), lambda e, *_: (0, 0)),
            pl.BlockSpec(memory_space=pl.ANY), pl.BlockSpec(memory_space=pl.ANY), pl.BlockSpec(memory_space=pl.ANY),
        ],
        out_specs=pl.BlockSpec(memory_space=pl.ANY),
        scratch_shapes=[pltpu.VMEM((2, D, F2), F32), pltpu.VMEM((2, F, D), F32),
                        pltpu.VMEM((D, F2), BF16), pltpu.VMEM((F, D), BF16),
                        pltpu.VMEM((2, 2 * T, D), F32), pltpu.VMEM((2, 2 * T, D), F32),
                        pltpu.VMEM((T, D), F32), pltpu.VMEM((T, D), F32),
                        pltpu.SemaphoreType.DMA((2, 2)), pltpu.SemaphoreType.DMA((2,)), pltpu.SemaphoreType.DMA((2,)),
                        pltpu.SemaphoreType.DMA((2,))],
    )
    return pl.pallas_call(
        _expert_kernel,
        grid_spec=grid_spec,
        out_shape=jax.ShapeDtypeStruct((R, D), F32),
        compiler_params=_cparams(("arbitrary",), VMEM_LARGE_MIB),
        name="experts",
    )(region_start, region_blocks, b1p, b2.reshape(E, 1, D), perm, w1, w2, xb)


def _combine_kernel(seg_ref, lstart_ref, gstart_ref, total_ref, ldt_ref, wt_ref, x1_ref, yb_ref, o_ref, ybuf_ref,
                    sems):
    i = pl.program_id(0)
    n = pl.num_programs(0)
    U = ROUTE_TILES
    group = (i % 2) * U

    def fetch(step, g):
        for u in range(U):
            _segment_copies(seg_ref, lstart_ref, gstart_ref, step * U + u, ybuf_ref.at[g + u], yb_ref,
                            sems.at[g + u], False, lambda c: c.start())

    @pl.when(i == 0)
    def _():
        ybuf_ref[...] = jnp.zeros_like(ybuf_ref)
        fetch(i, group)

    @pl.when(i + 1 < n)
    def _():
        fetch(i + 1, U - group)

    for u in range(U):
        _wait_tile(total_ref, i * U + u, ybuf_ref.at[group + u], yb_ref, sems.at[group + u], False)

    L = ybuf_ref.shape[1]
    RT = x1_ref.shape[0] // U
    c_iota = lax.broadcasted_iota(I32, (RT, L), 1)
    for u in range(U):
        rows = slice(u * RT, (u + 1) * RT)
        ldt = ldt_ref[rows, :]
        w = wt_ref[rows, :]
        w_hi = w.astype(BF16).astype(F32)
        w_lo = w - w_hi
        g_hi = jnp.zeros((RT, L), F32)
        g_lo = jnp.zeros((RT, L), F32)
        for k in range(TOP_K):
            hit = c_iota == ldt[:, k:k + 1]
            g_hi = jnp.where(hit, w_hi[:, k:k + 1], g_hi)
            g_lo = jnp.where(hit, w_lo[:, k:k + 1], g_lo)
        y = ybuf_ref[group + u].astype(BF16)
        o_ref[rows, :] = x1_ref[rows, :] + (_dot(g_hi.astype(BF16), y) + _dot(g_lo.astype(BF16), y))


def _combine(ldt, wts_t, x1, yb, seg, lstart, gstart, totals):
    N, D = x1.shape
    RT = ROUTE_T
    grid_spec = pltpu.PrefetchScalarGridSpec(
        num_scalar_prefetch=4,
        grid=(N // (RT * ROUTE_TILES),),
        in_specs=[
            pl.BlockSpec((RT * ROUTE_TILES, TOP_K), lambda i, *_: (i, 0)),
            pl.BlockSpec((RT * ROUTE_TILES, TOP_K), lambda i, *_: (i, 0)),
            pl.BlockSpec((RT * ROUTE_TILES, D), lambda i, *_: (i, 0)),
            pl.BlockSpec(memory_space=pl.ANY),
        ],
        out_specs=pl.BlockSpec((RT * ROUTE_TILES, D), lambda i, *_: (i, 0)),
        scratch_shapes=[pltpu.VMEM((2 * ROUTE_TILES, LOCAL_ROWS, D), F32), pltpu.SemaphoreType.DMA((2 * ROUTE_TILES,))],
    )
    return pl.pallas_call(
        _combine_kernel,
        grid_spec=grid_spec,
        out_shape=jax.ShapeDtypeStruct((N, D), F32),
        compiler_params=_cparams(("arbitrary",), VMEM_MEDIUM_MIB),
        name="combine",
    )(seg, lstart, gstart, totals, ldt, wts_t, x1, yb)


def _layer(x, mem, g_mix, w_in, b_gate, g_ret_out, g_moba_q, g_moba_k, g_mem, w_mem_kv, g_mem_q, g_mem_k,
           w_br_ret, w_br_moba, w_br_mem, w_out, g_ffn, w_router, b_router, w_mlp1, b_mlp1, w_mlp2, b_mlp2):
    B, S, D = x.shape
    N = B * S
    x2 = x.reshape(N, D)
    slopes_np = np.exp2(-8.0 * (np.arange(MOBA_HEADS, dtype=np.float64) + 1.0) / MOBA_HEADS)
    assert all(float(np.log2(s)).is_integer() for s in slopes_np)
    slopes = tuple(float(s) for s in slopes_np)

    w_mix = w_in[:, :MIX_W].astype(BF16)
    w_gate = w_in[:, MIX_W:].astype(BF16)
    kmem, vmem = _memkv(mem, g_mem, w_mem_kv, g_mem_k)
    rq, rk, rv, rg, mq, mk, mv, cq, km = _inproj(x2, B, S, g_mix, w_mix, g_moba_q, g_moba_k, g_mem_q, slopes)
    o_ret = _retention(rq, rk, rv, rg, g_ret_out)
    o_moba = _moba(mq, mk, mv, km, jnp.asarray(slopes, F32)).reshape(N, MOBA_W)
    x1, xt, eidx, wts, rank, cnt = _merge(x2, S, o_ret, o_moba, cq, kmem, vmem, g_mix, w_gate, b_gate,
                                          w_br_ret, w_br_moba, w_br_mem, w_out, g_ffn, w_router, b_router)
    T = MOE_T
    tcnt = cnt[:, :, 0]
    seg = ((tcnt + SEG_ALIGN - 1) // SEG_ALIGN) * SEG_ALIGN
    region_rows = jnp.sum(seg, axis=0)
    padded = ((region_rows + T - 1) // T) * T
    pad_ends = jnp.cumsum(padded).astype(I32)
    pad_starts = pad_ends - padded
    gstart = pad_starts[None, :] + jnp.cumsum(seg, axis=0) - seg
    lstart = jnp.cumsum(seg, axis=1) - seg
    lstart_tok = jnp.broadcast_to(lstart[:, None, :], (N // ROUTE_T, ROUTE_T, N_EXPERTS)).reshape(N, N_EXPERTS)
    onehot = eidx[:, :, None] == jnp.arange(N_EXPERTS, dtype=I32)[None, None, :]
    ld = jnp.sum(jnp.where(onehot, lstart_tok[None], 0), axis=-1) + rank
    NB = -(-(N // ROUTE_T * LOCAL_ROWS) // T) + N_EXPERTS
    seg_f, lstart_f, gstart_f = (a.reshape(-1).astype(I32) for a in (seg, lstart, gstart))
    totals = jnp.sum(seg, axis=1).astype(I32)

    xb = _dispatch(xt, ld, seg_f, lstart_f, gstart_f, totals, pad_ends, padded.astype(I32), NB * T)
    F2 = w_mlp1.shape[-1]
    b1p = b_mlp1.reshape(N_EXPERTS, F2 // (2 * LANES), LANES, 2).transpose(0, 1, 3, 2).reshape(N_EXPERTS, 1, F2)
    yb = _experts(xb, pad_starts.astype(I32), (padded // T).astype(I32), w_mlp1, b1p, w_mlp2, b_mlp2)
    out = _combine(ld.T, wts.T, x1, yb, seg_f, lstart_f, gstart_f, totals)
    return out.reshape(B, S, D)


def kernel(x, mem, g_mix, w_in, b_gate, g_ret_out, g_moba_q, g_moba_k, g_mem, w_mem_kv, g_mem_q, g_mem_k, w_br_ret, w_br_moba, w_br_mem, w_out, g_ffn, w_router, b_router, w_mlp1, b_mlp1, w_mlp2, b_mlp2):
    for l in range(g_mix.shape[0]):
        x = _layer(x, mem, g_mix[l], w_in[l], b_gate[l], g_ret_out[l], g_moba_q[l], g_moba_k[l], g_mem[l],
                   w_mem_kv[l], g_mem_q[l], g_mem_k[l], w_br_ret[l], w_br_moba[l], w_br_mem[l], w_out[l],
                   g_ffn[l], w_router[l], b_router[l], w_mlp1[l], b_mlp1[l], w_mlp2[l], b_mlp2[l])
    return x
```

```python
import functools

import jax
import jax.numpy as jnp
import numpy as np
from jax import lax
from jax.experimental import pallas as pl
from jax.experimental.pallas import tpu as pltpu

F32 = jnp.float32
BF16 = jnp.bfloat16
I32 = jnp.int32

EPS = 1e-5
NEG = -1e30

RET_HEADS = 4
RET_DK = 64
RET_DV = 128
RET_CHUNK = 128
MOBA_HEADS = 8
MOBA_HD = 64
MOBA_BLOCK = 256
MOBA_TOPK = 3
MEM_HEADS = 4
MEM_HD = 128
N_BRANCH = 3
N_EXPERTS = 32
TOP_K = 4
SWIGLU_LIMIT = 7.0
SWIGLU_ALPHA = 1.702

RET_Q = RET_HEADS * RET_DK
RET_V = RET_HEADS * RET_DV
MOBA_W = MOBA_HEADS * MOBA_HD
MEM_W = MEM_HEADS * MEM_HD
MIX_W = 2 * RET_Q + 2 * RET_V + 3 * MOBA_W + MEM_W

LANES = 128
V7X_VMEM_MIB = 64
VMEM_LARGE_MIB = V7X_VMEM_MIB - 8
VMEM_MEDIUM_MIB = V7X_VMEM_MIB * 3 // 4
VMEM_SMALL_MIB = V7X_VMEM_MIB // 2
MOBA_PAIRS = MOBA_HEADS // 2
MOBA_HEADS_PER_STEP = 8
BIAS_LANE0 = MOBA_HD
ONE_LANE = 80
VSUM_LANE = MOBA_HD

MOE_T = 256
TM_PROJ = 512
TM_MERGE = 512
TQ_RET = 512
ROUTE_T = 256
ROUTE_TILES = 2
SEG_ALIGN = 8
LOCAL_ROWS = -(-(ROUTE_T * TOP_K + N_EXPERTS * (SEG_ALIGN - 1)) // LANES) * LANES
SEG_CHUNKS = tuple(2 ** p for p in range(ROUTE_T.bit_length() - 1, SEG_ALIGN.bit_length() - 2, -1))
SEG_SMALL = 64
TOTAL_CHUNKS = tuple(2 ** p for p in range(LOCAL_ROWS.bit_length() - 1, SEG_ALIGN.bit_length() - 2, -1))

_NT = (((1,), (1,)), ((), ()))
_TN = (((0,), (0,)), ((), ()))


def _rms(x, g):
    return x * lax.rsqrt(jnp.mean(x * x, axis=-1, keepdims=True) + EPS) * g


def _dot(a, b):
    return jnp.dot(a, b, preferred_element_type=F32)


def _cparams(sem, vmem_mb):
    return pltpu.CompilerParams(dimension_semantics=sem, vmem_limit_bytes=vmem_mb * 1024 * 1024)


def _memkv_kernel(mem_ref, g_ref, w_ref, gk_ref, k_ref, v_ref):
    m = _rms(mem_ref[0], g_ref[...]).astype(BF16)
    kv = _dot(m, w_ref[...])
    ks = [_rms(kv[:, h * MEM_HD:(h + 1) * MEM_HD], gk_ref[...]) for h in range(MEM_HEADS)]
    k_ref[0] = jnp.concatenate(ks, axis=-1).astype(BF16)
    v_ref[0] = kv[:, MEM_W:].astype(BF16)


def _memkv(mem, g_mem, w_mem_kv, g_mem_k):
    B, M, D = mem.shape
    return pl.pallas_call(
        _memkv_kernel,
        grid=(B,),
        in_specs=[
            pl.BlockSpec((1, M, D), lambda b: (b, 0, 0)),
            pl.BlockSpec((1, D), lambda b: (0, 0)),
            pl.BlockSpec((D, 2 * MEM_W), lambda b: (0, 0)),
            pl.BlockSpec((1, MEM_HD), lambda b: (0, 0)),
        ],
        out_specs=[
            pl.BlockSpec((1, M, MEM_W), lambda b: (b, 0, 0)),
            pl.BlockSpec((1, M, MEM_W), lambda b: (b, 0, 0)),
        ],
        out_shape=[jax.ShapeDtypeStruct((B, M, MEM_W), BF16)] * 2,
        compiler_params=_cparams(("arbitrary",), VMEM_SMALL_MIB),
        name="memkv",
    )(mem, g_mem.reshape(1, D), w_mem_kv.astype(BF16), g_mem_k.reshape(1, MEM_HD))


def _head_pair_norm(a2, g2, lane):
    sq = a2 * a2
    lo = lane < MOBA_HD
    ss_lo = jnp.sum(jnp.where(lo, sq, 0.0), axis=-1, keepdims=True)
    ss_hi = jnp.sum(jnp.where(lo, 0.0, sq), axis=-1, keepdims=True)
    inv = jnp.where(lo, lax.rsqrt(ss_lo / MOBA_HD + EPS), lax.rsqrt(ss_hi / MOBA_HD + EPS))
    return a2 * inv * g2


def _inproj_kernel(slopes, seq_tiles, x_ref, gmix_ref, w_ref, gq_ref, gk_ref, gc_ref,
                   rq_ref, rk_ref, rv_ref, rg_ref, mq_ref, mk_ref, mv_ref, cq_ref, kmean_ref):
    tm = x_ref.shape[0]
    blocks_per_tile = tm // MOBA_BLOCK
    h = _rms(x_ref[...], gmix_ref[...]).astype(BF16)
    col = [0]

    def proj(width):
        a = _dot(h, w_ref[:, col[0]:col[0] + width])
        col[0] += width
        return a

    a = proj(2 * RET_Q)
    for hh in range(RET_HEADS):
        rq_ref[0, hh] = a[:, hh * RET_DK:(hh + 1) * RET_DK].astype(BF16)
        rk_ref[0, hh] = (a[:, RET_Q + hh * RET_DK:RET_Q + (hh + 1) * RET_DK] * (RET_DK ** -0.5)).astype(BF16)
    rv_ref[...] = proj(RET_V).astype(BF16)
    rg_ref[...] = proj(RET_V).astype(BF16)

    lane = lax.broadcasted_iota(I32, (tm, LANES), 1)
    row = lax.broadcasted_iota(I32, (tm, LANES), 0)
    lo = lane < MOBA_HD
    q_tail = jnp.where(lane == ONE_LANE, 1.0, 0.0)
    a = proj(MOBA_W)
    for p in range(MOBA_PAIRS):
        n2 = _head_pair_norm(a[:, p * LANES:(p + 1) * LANES], gq_ref[...], lane) * (MOBA_HD ** -0.5)
        mq_ref[0, 2 * p] = jnp.where(lo, n2, q_tail).astype(BF16)
        mq_ref[0, 2 * p + 1] = jnp.where(lo, pltpu.roll(n2, MOBA_HD, 1), q_tail).astype(BF16)
    blk = (pl.program_id(0) % seq_tiles) * blocks_per_tile + row // MOBA_BLOCK
    onehot_tail = jnp.where(lane == BIAS_LANE0 + blk, 1.0, 0.0)
    off = (row % MOBA_BLOCK).astype(F32)
    a = proj(MOBA_W)
    for p in range(MOBA_PAIRS):
        n2 = _head_pair_norm(a[:, p * LANES:(p + 1) * LANES], gk_ref[...], lane)
        for j in range(blocks_per_tile):
            kmean_ref[0, 0, p, j:j + 1, :] = jnp.mean(n2[j * MOBA_BLOCK:(j + 1) * MOBA_BLOCK], axis=0, keepdims=True)
        for s, src in ((0, n2), (1, pltpu.roll(n2, MOBA_HD, 1))):
            tail = jnp.where(lane == ONE_LANE, slopes[2 * p + s] * off, onehot_tail)
            mk_ref[0, 2 * p + s] = jnp.where(lo, src, tail).astype(BF16)
    v_tail = jnp.where(lane == VSUM_LANE, 1.0, 0.0)
    a = proj(MOBA_W)
    for p in range(MOBA_PAIRS):
        a2 = a[:, p * LANES:(p + 1) * LANES]
        mv_ref[0, 2 * p] = jnp.where(lo, a2, v_tail).astype(BF16)
        mv_ref[0, 2 * p + 1] = jnp.where(lo, pltpu.roll(a2, MOBA_HD, 1), v_tail).astype(BF16)
    a = proj(MEM_W)
    cq = [_rms(a[:, hh * MEM_HD:(hh + 1) * MEM_HD], gc_ref[...]) for hh in range(MEM_HEADS)]
    cq_ref[...] = jnp.concatenate(cq, axis=-1).astype(BF16)


def _inproj(x2, B, S, g_mix, w_mix, g_moba_q, g_moba_k, g_mem_q, slopes):
    N, D = x2.shape
    tm = TM_PROJ
    nS = S // tm
    bpt = tm // MOBA_BLOCK
    tok = lambda i: (i, 0)
    headmaj = lambda i: (i // nS, 0, i % nS, 0)
    g2 = lambda g: jnp.concatenate([g, g]).reshape(1, LANES)
    outs = pl.pallas_call(
        functools.partial(_inproj_kernel, slopes, nS),
        grid=(N // tm,),
        in_specs=[
            pl.BlockSpec((tm, D), tok),
            pl.BlockSpec((1, D), lambda i: (0, 0)),
            pl.BlockSpec((D, MIX_W), lambda i: (0, 0)),
            pl.BlockSpec((1, LANES), lambda i: (0, 0)),
            pl.BlockSpec((1, LANES), lambda i: (0, 0)),
            pl.BlockSpec((1, MEM_HD), lambda i: (0, 0)),
        ],
        out_specs=[
            pl.BlockSpec((1, RET_HEADS, tm, RET_DK), headmaj),
            pl.BlockSpec((1, RET_HEADS, tm, RET_DK), headmaj),
            pl.BlockSpec((tm, RET_V), tok),
            pl.BlockSpec((tm, RET_V), tok),
            pl.BlockSpec((1, MOBA_HEADS, tm, LANES), headmaj),
            pl.BlockSpec((1, MOBA_HEADS, tm, LANES), headmaj),
            pl.BlockSpec((1, MOBA_HEADS, tm, LANES), headmaj),
            pl.BlockSpec((tm, MEM_W), tok),
            pl.BlockSpec((1, 1, MOBA_PAIRS, bpt, LANES), lambda i: (i // nS, i % nS, 0, 0, 0)),
        ],
        out_shape=[
            jax.ShapeDtypeStruct((B, RET_HEADS, S, RET_DK), BF16),
            jax.ShapeDtypeStruct((B, RET_HEADS, S, RET_DK), BF16),
            jax.ShapeDtypeStruct((N, RET_V), BF16),
            jax.ShapeDtypeStruct((N, RET_V), BF16),
            jax.ShapeDtypeStruct((B, MOBA_HEADS, S, LANES), BF16),
            jax.ShapeDtypeStruct((B, MOBA_HEADS, S, LANES), BF16),
            jax.ShapeDtypeStruct((B, MOBA_HEADS, S, LANES), BF16),
            jax.ShapeDtypeStruct((N, MEM_W), BF16),
            jax.ShapeDtypeStruct((B, nS, MOBA_PAIRS, bpt, LANES), F32),
        ],
        compiler_params=_cparams(("arbitrary",), VMEM_LARGE_MIB),
        name="inproj",
    )(x2, g_mix.reshape(1, D), w_mix, g2(g_moba_q), g2(g_moba_k), g_mem_q.reshape(1, MEM_HD))
    rq, rk, rv, rg, mq, mk, mv, cq, kmean_pairs = outs
    km = kmean_pairs.reshape(B, nS, MOBA_PAIRS, bpt, 2, MOBA_HD).transpose(0, 2, 4, 1, 3, 5)
    km = km.reshape(B, MOBA_HEADS, nS * bpt, MOBA_HD)
    km = jnp.pad(km, ((0, 0), (0, 0), (0, 0), (0, LANES - MOBA_HD)))
    return rq, rk, rv, rg, mq, mk, mv, cq, km


def _retention_kernel(q_ref, k_ref, v_ref, rg_ref, din_ref, dq_ref, dk_ref, dc_ref, g_ref, o_ref, state_ref):
    @pl.when(pl.program_id(1) == 0)
    def _():
        state_ref[...] = jnp.zeros_like(state_ref)

    C = RET_CHUNK
    for h in range(q_ref.shape[1]):
        cols = slice(h * RET_DV, (h + 1) * RET_DV)
        state = state_ref[h]
        for c in range(q_ref.shape[2] // C):
            rows = slice(c * C, (c + 1) * C)
            q = q_ref[0, h, rows, :]
            k = k_ref[0, h, rows, :]
            v = v_ref[rows, cols]
            scores = lax.dot_general(q, k, _NT, preferred_element_type=F32) * din_ref[h]
            intra = _dot(scores.astype(BF16), v)
            cross = _dot(q, state.astype(BF16)) * dq_ref[h]
            kd = (k.astype(F32) * dk_ref[h]).astype(BF16)
            state = dc_ref[h] * state + lax.dot_general(kd, v, _TN, preferred_element_type=F32)
            o = _rms(intra + cross, g_ref[...])
            o_ref[rows, cols] = (o * jax.nn.silu(rg_ref[rows, cols].astype(F32))).astype(BF16)
        state_ref[h] = state


def _retention_decays():
    H, C = RET_HEADS, RET_CHUNK
    log_g = jnp.log1p(-jnp.exp2(-5.0 - jnp.arange(H, dtype=F32)))
    i = jnp.arange(C, dtype=F32)
    diff = i[:, None] - i[None, :]
    decay_in = jnp.where(diff >= 0, jnp.exp(jnp.maximum(diff, 0.0)[None] * log_g[:, None, None]), 0.0)
    decay_k = jnp.exp((C - 1 - i)[None, :] * log_g[:, None])
    decay_q = jnp.exp((i + 1)[None, :] * log_g[:, None])
    decay_chunk = jnp.exp(C * log_g)
    dq = jnp.broadcast_to(decay_q[:, :, None], (H, C, RET_DV))
    dk = jnp.broadcast_to(decay_k[:, :, None], (H, C, RET_DK))
    dc = jnp.broadcast_to(decay_chunk[:, None, None], (H, RET_DK, RET_DV))
    return decay_in, dq, dk, dc


def _retention(rq, rk, rv, rg, g_ret_out):
    B, H, S, dk = rq.shape
    tq = TQ_RET
    nT = S // tq
    C = RET_CHUNK
    din, dq, dk_, dc = _retention_decays()
    qk_spec = pl.BlockSpec((1, H, tq, dk), lambda b, t: (b, 0, t, 0))
    tok_spec = pl.BlockSpec((tq, RET_V), lambda b, t: (b * nT + t, 0))
    const = lambda r, c: pl.BlockSpec((H, r, c), lambda b, t: (0, 0, 0))
    return pl.pallas_call(
        _retention_kernel,
        grid=(B, nT),
        in_specs=[qk_spec, qk_spec, tok_spec, tok_spec,
                  const(C, C), const(C, RET_DV), const(C, RET_DK), const(RET_DK, RET_DV),
                  pl.BlockSpec((1, RET_DV), lambda b, t: (0, 0))],
        out_specs=tok_spec,
        out_shape=jax.ShapeDtypeStruct((B * S, RET_V), BF16),
        scratch_shapes=[pltpu.VMEM((H, RET_DK, RET_DV), F32)],
        compiler_params=_cparams(("arbitrary", "arbitrary"), VMEM_SMALL_MIB),
        name="retention",
    )(rq, rk, rv, rg, din, dq, dk_, dc, g_ret_out.reshape(1, RET_DV))


def _moba_kernel(slopes_ref, q_ref, k_ref, v_ref, km_ref, shift_ref, mask_ref, o_ref):
    NH = q_ref.shape[1]
    head0 = pl.program_id(1) * NH
    i0 = pl.program_id(2) * 2
    BS = MOBA_BLOCK
    nb = km_ref.shape[2]
    heads = range(NH)
    n_iota = lax.broadcasted_iota(I32, (nb, 2 * BS), 0)
    q_blk = i0 + lax.broadcasted_iota(I32, (nb, 2 * BS), 1) // BS
    past = n_iota < q_blk
    qa = [q_ref[0, s] for s in heads]
    q_aug = []
    for s in heads:
        km = km_ref[0, s]
        hi = km.astype(BF16)
        mid = (km - hi.astype(F32)).astype(BF16)
        lo = (km - hi.astype(F32) - mid.astype(F32)).astype(BF16)
        g3 = lax.dot_general(jnp.concatenate([hi, mid, lo], axis=0), qa[s], _NT, preferred_element_type=F32)
        gate = (g3[:nb] + g3[nb:2 * nb]) + g3[2 * nb:]
        g = jnp.where(past, gate, -jnp.inf)
        rank = jnp.zeros((nb, 2 * BS), I32)
        for m in range(nb - 1):
            gm = g[m:m + 1, :]
            rank = rank + jnp.where(gm > g, 1, jnp.where(gm == g, jnp.where(m < n_iota, 1, 0), 0))
        sel = jnp.where(past, rank, MOBA_TOPK) < MOBA_TOPK
        bias = jnp.where(sel, (n_iota - q_blk).astype(F32) * (slopes_ref[head0 + s] * BS), NEG).astype(BF16)
        placed = lax.dot_general(bias, shift_ref[...], _TN, preferred_element_type=F32)
        q_aug.append((qa[s].astype(F32) + placed).astype(BF16))

    def rows_of(j):
        return pl.ds(pl.multiple_of(j * BS, BS), BS)

    m_lo, m_hi, acc_lo, acc_hi = [], [], [], []
    for s in heads:
        q_mix = jnp.concatenate([qa[s][:BS], q_aug[s][BS:]], axis=0)
        sc = lax.dot_general(q_mix, k_ref[0, s, rows_of(i0), :], _NT, preferred_element_type=F32) + mask_ref[...]
        m = jnp.max(sc, axis=-1, keepdims=True)
        acc = _dot(jnp.exp(sc - m).astype(BF16), v_ref[0, s, rows_of(i0), :])
        m_lo.append(m[:BS])
        acc_lo.append(acc[:BS])
        sc1 = lax.dot_general(qa[s][BS:], k_ref[0, s, rows_of(i0 + 1), :], _NT, preferred_element_type=F32)
        sc1 = sc1 + mask_ref[:BS, :]
        m1 = jnp.maximum(m[BS:], jnp.max(sc1, axis=-1, keepdims=True))
        p1 = jnp.exp(sc1 - m1).astype(BF16)
        m_hi.append(m1)
        acc_hi.append(jnp.exp(m[BS:] - m1) * acc[BS:] + _dot(p1, v_ref[0, s, rows_of(i0 + 1), :]))

    def body(j, carry):
        ms_lo, ms_hi, accs_lo, accs_hi = carry
        out = ([], [], [], [])
        for s in heads:
            sj = lax.dot_general(q_aug[s], k_ref[0, s, rows_of(j), :], _NT, preferred_element_type=F32)
            ps, alphas, ms = [], [], []
            for half, m_prev in ((slice(0, BS), ms_lo[s]), (slice(BS, 2 * BS), ms_hi[s])):
                m_new = jnp.maximum(m_prev, jnp.max(sj[half], axis=-1, keepdims=True))
                ps.append(jnp.exp(sj[half] - m_new).astype(BF16))
                alphas.append(jnp.exp(m_prev - m_new))
                ms.append(m_new)
            pv = _dot(jnp.concatenate(ps, axis=0), v_ref[0, s, rows_of(j), :])
            out[0].append(ms[0])
            out[1].append(ms[1])
            out[2].append(alphas[0] * accs_lo[s] + pv[:BS])
            out[3].append(alphas[1] * accs_hi[s] + pv[BS:])
        return tuple(tuple(o) for o in out)

    _, _, accs_lo, accs_hi = lax.fori_loop(0, i0 // 2, lambda j, c: body(2 * j + 1, body(2 * j, c)),
                                           (tuple(m_lo), tuple(m_hi), tuple(acc_lo), tuple(acc_hi)))
    lane = lax.broadcasted_iota(I32, (BS, LANES), 1)
    for t, accs in enumerate((accs_lo, accs_hi)):
        outs = [acc / acc[:, VSUM_LANE:VSUM_LANE + 1] for acc in accs]
        pairs = [jnp.where(lane < MOBA_HD, outs[2 * p], pltpu.roll(outs[2 * p + 1], MOBA_HD, 1))
                 for p in range(NH // 2)]
        o_ref[0, t * BS:(t + 1) * BS, :] = jnp.concatenate(pairs, axis=-1).astype(BF16)


def _moba(mq, mk, mv, km, slopes):
    B, H, S, _ = mq.shape
    BS = MOBA_BLOCK
    NH = MOBA_HEADS_PER_STEP
    nq = S // BS
    nb = km.shape[2]
    assert BIAS_LANE0 + nb <= ONE_LANE and nq % 2 == 0
    shift = (jnp.arange(nb)[:, None] + BIAS_LANE0 == jnp.arange(LANES)[None, :]).astype(BF16)
    causal = jnp.where(jnp.arange(BS)[:, None] >= jnp.arange(BS)[None, :], 0.0, NEG).astype(F32)
    mask = jnp.concatenate([causal, jnp.zeros((BS, BS), F32)], axis=0)
    per_head = lambda rows: pl.BlockSpec((1, NH, rows, LANES), lambda b, p, i, sl: (b, p, 0, 0))
    grid_spec = pltpu.PrefetchScalarGridSpec(
        num_scalar_prefetch=1,
        grid=(B, H // NH, nq // 2),
        in_specs=[
            pl.BlockSpec((1, NH, 2 * BS, LANES), lambda b, p, i, sl: (b, p, i, 0)),
            per_head(S), per_head(S), per_head(nb),
            pl.BlockSpec((nb, LANES), lambda b, p, i, sl: (0, 0)),
            pl.BlockSpec((2 * BS, BS), lambda b, p, i, sl: (0, 0)),
        ],
        out_specs=pl.BlockSpec((1, 2 * BS, NH * MOBA_HD), lambda b, p, i, sl: (b, i, p)),
    )
    return pl.pallas_call(
        _moba_kernel,
        grid_spec=grid_spec,
        out_shape=jax.ShapeDtypeStruct((B, S, MOBA_W), BF16),
        compiler_params=_cparams(("arbitrary", "arbitrary", "arbitrary"), VMEM_LARGE_MIB),
        name="moba",
    )(slopes, mq, mk, mv, km, shift, mask)


def _merge_kernel(x_ref, oret_ref, omoba_ref, cq_ref, kmem_ref, vmem_ref, gmix_ref, wg_ref, bg_ref,
                  wbr_ref, wbm_ref, wbc_ref, wout_ref, gffn_ref, wr_ref, br_ref, tri_ref,
                  x1_ref, xt_ref, eidx_ref, wts_ref, rank_ref, cnt_ref):
    D = x_ref.shape[1]
    tm = x_ref.shape[0]

    x = x_ref[...]
    h = _rms(x, gmix_ref[...]).astype(BF16)
    cq = cq_ref[...]
    om = []
    for hh in range(MEM_HEADS):
        cols = slice(hh * MEM_HD, (hh + 1) * MEM_HD)
        sc = lax.dot_general(cq[:, cols], kmem_ref[0, :, cols], _NT, preferred_element_type=F32) * (MEM_HD ** -0.5)
        sc = sc - jnp.max(sc, axis=-1, keepdims=True)
        p = jnp.exp(sc)
        p = p / jnp.sum(p, axis=-1, keepdims=True)
        om.append(_dot(p.astype(BF16), vmem_ref[0, :, cols]))
    omem = jnp.concatenate(om, axis=-1).astype(BF16)
    y = None
    for br, (o, w_ref) in enumerate(((oret_ref[...], wbr_ref), (omoba_ref[...], wbm_ref), (omem, wbc_ref))):
        gl = _dot(h, wg_ref[:, br * D:(br + 1) * D]) + bg_ref[:, br * D:(br + 1) * D]
        term = jax.nn.sigmoid(gl) * _dot(o, w_ref[...])
        y = term if y is None else y + term
    x1 = x + _dot(y.astype(BF16), wout_ref[...])
    x1_ref[...] = x1
    xt = _rms(x1, gffn_ref[...])
    wr = wr_ref[...]
    E = wr.shape[0]
    w_hi = wr.astype(BF16)
    w_mid = (wr - w_hi.astype(F32)).astype(BF16)
    w_lo = (wr - w_hi.astype(F32) - w_mid.astype(F32)).astype(BF16)
    xt_hi = xt.astype(BF16)
    xt_ref[...] = xt_hi
    xt_lo = (xt - xt_hi.astype(F32)).astype(BF16)
    a = lax.dot_general(jnp.concatenate([w_hi, w_mid, w_lo], axis=0), xt_hi, _NT, preferred_element_type=F32)
    b = lax.dot_general(jnp.concatenate([w_hi, w_mid], axis=0), xt_lo, _NT, preferred_element_type=F32)
    logits = (a[:E] + (a[E:2 * E] + b[:E])) + (a[2 * E:] + b[E:]) + br_ref[...]
    e_iota = lax.broadcasted_iota(I32, (E, tm), 0)
    l = logits
    vals, hots = [], []
    for k in range(TOP_K):
        m = jnp.max(l, axis=0, keepdims=True)
        idx = jnp.min(jnp.where(l == m, e_iota, E), axis=0, keepdims=True)
        hot = e_iota == idx
        l = jnp.where(hot, -jnp.inf, l)
        vals.append(m)
        hots.append(hot)
        eidx_ref[k:k + 1, :] = idx
    ex = [jnp.exp(v - vals[0]) for v in vals]
    den = ex[0]
    for k in range(1, TOP_K):
        den = den + ex[k]
    chosen = jnp.zeros((E, tm), F32)
    for k in range(TOP_K):
        wts_ref[k:k + 1, :] = ex[k] / den
        chosen = chosen + jnp.where(hots[k], 1.0, 0.0)
    RT = tri_ref.shape[0]
    for t in range(tm // RT):
        cols = slice(t * RT, (t + 1) * RT)
        prefix = _dot(chosen[:, cols].astype(BF16), tri_ref[...])
        for k in range(TOP_K):
            rank_ref[k:k + 1, cols] = jnp.sum(jnp.where(hots[k][:, cols], prefix, 0.0), axis=0, keepdims=True).astype(I32)
        counts = jnp.sum(chosen[:, cols], axis=1, keepdims=True)
        cnt_ref[t] = jnp.broadcast_to(counts, cnt_ref.shape[1:]).astype(I32)


def _merge(x2, S, o_ret, o_moba, cq, kmem, vmem, g_mix, w_gate, b_gate, w_br_ret, w_br_moba, w_br_mem, w_out,
           g_ffn, w_router, b_router):
    N, D = x2.shape
    tm = TM_MERGE
    nS = S // tm
    M = kmem.shape[1]
    E = w_router.shape[1]
    tok = lambda w: pl.BlockSpec((tm, w), lambda i: (i, 0))
    const = lambda r, c: pl.BlockSpec((r, c), lambda i: (0, 0), pipeline_mode=pl.Buffered(1))
    mem_spec = pl.BlockSpec((1, M, MEM_W), lambda i: (i // nS, 0, 0))
    lanes_tok = lambda r, dt: (pl.BlockSpec((r, tm), lambda i: (0, i)), jax.ShapeDtypeStruct((r, N), dt))
    RT = ROUTE_T
    tri = (jnp.arange(RT)[:, None] < jnp.arange(RT)[None, :]).astype(BF16)
    e_spec, e_shape = lanes_tok(TOP_K, I32)
    w_spec, w_shape = lanes_tok(TOP_K, F32)
    r_spec, r_shape = lanes_tok(TOP_K, I32)
    return pl.pallas_call(
        _merge_kernel,
        grid=(N // tm,),
        in_specs=[tok(D), tok(RET_V), tok(MOBA_W), tok(MEM_W), mem_spec, mem_spec,
                  const(1, D), const(D, N_BRANCH * D), const(1, N_BRANCH * D),
                  const(RET_V, D), const(MOBA_W, D), const(MEM_W, D), const(D, D),
                  const(1, D), const(E, D), const(E, 1), const(RT, RT)],
        out_specs=[tok(D), tok(D), e_spec, w_spec, r_spec, pl.BlockSpec((tm // RT, E, LANES), lambda i: (i, 0, 0))],
        out_shape=[jax.ShapeDtypeStruct((N, D), F32), jax.ShapeDtypeStruct((N, D), BF16),
                   e_shape, w_shape, r_shape, jax.ShapeDtypeStruct((N // RT, E, LANES), I32)],
        compiler_params=_cparams(("arbitrary",), VMEM_LARGE_MIB),
        name="merge",
    )(x2, o_ret, o_moba, cq, kmem, vmem, g_mix.reshape(1, D), w_gate, b_gate.reshape(1, N_BRANCH * D),
      w_br_ret.astype(BF16), w_br_moba.astype(BF16), w_br_mem.astype(BF16), w_out.astype(BF16),
      g_ffn.reshape(1, D), w_router.T, b_router.reshape(E, 1), tri)


def _segment_copies(seg_ref, lstart_ref, gstart_ref, tile, local_ref, slots_ref, sem, to_slots, fn):
    def per_expert(e, carry):
        idx = tile * N_EXPERTS + e
        size = seg_ref[idx]
        lstart = lstart_ref[idx]
        gstart = gstart_ref[idx]

        def copy_chunks(chunks, off):
            for chunk in chunks:
                take = (size & chunk) != 0
                lo = pl.ds(pl.multiple_of(lstart + off, SEG_ALIGN), chunk)
                gl = pl.ds(pl.multiple_of(gstart + off, SEG_ALIGN), chunk)
                src, dst = (local_ref.at[lo], slots_ref.at[gl]) if to_slots else (slots_ref.at[gl], local_ref.at[lo])

                @pl.when(take)
                def _():
                    fn(pltpu.make_async_copy(src, dst, sem))

                off = off + jnp.where(take, chunk, 0)

        large = size & -SEG_SMALL

        @pl.when(large != 0)
        def _():
            copy_chunks([c for c in SEG_CHUNKS if c >= SEG_SMALL], jnp.int32(0))

        copy_chunks([c for c in SEG_CHUNKS if c < SEG_SMALL], large)
        return carry

    lax.fori_loop(0, N_EXPERTS, per_expert, 0)


def _wait_tile(total_ref, tile, local_ref, slots_ref, sem, to_slots):
    total = total_ref[tile]
    for chunk in TOTAL_CHUNKS:
        lo, gl = local_ref.at[pl.ds(0, chunk)], slots_ref.at[pl.ds(0, chunk)]
        src, dst = (lo, gl) if to_slots else (gl, lo)

        @pl.when((total & chunk) != 0)
        def _():
            pltpu.make_async_copy(src, dst, sem).wait()


def _dispatch_kernel(seg_ref, lstart_ref, gstart_ref, total_ref, pad_end_ref, padded_ref, ld_ref, xt_ref, xb_ref,
                     ybuf_ref, zeros_ref, zsem, sems):
    T = zeros_ref.shape[0]
    i = pl.program_id(0)
    n = pl.num_programs(0)

    @pl.when(i == 0)
    def _():
        zeros_ref[...] = jnp.zeros_like(zeros_ref)

        def fill(start):
            return pltpu.make_async_copy(zeros_ref, xb_ref.at[pl.ds(pl.multiple_of(start, T), T)], zsem)

        for e in range(N_EXPERTS):
            @pl.when(padded_ref[e] > 0)
            def _():
                fill(pad_end_ref[e] - T).start()
        for e in range(N_EXPERTS):
            @pl.when(padded_ref[e] > 0)
            def _():
                fill(pad_end_ref[e] - T).wait()
        first_unused = pad_end_ref[N_EXPERTS - 1] // T
        n_blocks = xb_ref.shape[0] // T
        lax.fori_loop(first_unused, n_blocks, lambda b, c: (fill(b * T).start(), c)[1], 0)
        lax.fori_loop(first_unused, n_blocks, lambda b, c: (fill(b * T).wait(), c)[1], 0)

    U = ROUTE_TILES
    group = (i % 2) * U

    @pl.when(i >= 2)
    def _():
        for u in range(U):
            _wait_tile(total_ref, (i - 2) * U + u, ybuf_ref.at[group + u], xb_ref, sems.at[group + u], True)

    L = ybuf_ref.shape[1]
    RT = xt_ref.shape[0] // U
    r_iota = lax.broadcasted_iota(I32, (L, RT), 0)
    for u in range(U):
        ld = ld_ref[:, u * RT:(u + 1) * RT]
        onehot = jnp.zeros((L, RT), F32)
        for k in range(TOP_K):
            onehot = jnp.where(r_iota == ld[k:k + 1, :], 1.0, onehot)
        ybuf_ref[group + u] = _dot(onehot.astype(BF16), xt_ref[u * RT:(u + 1) * RT, :])
    for u in range(U):
        _segment_copies(seg_ref, lstart_ref, gstart_ref, i * U + u, ybuf_ref.at[group + u], xb_ref,
                        sems.at[group + u], True, lambda c: c.start())

    @pl.when(i == n - 1)
    def _():
        @pl.when(n >= 2)
        def _():
            for u in range(U):
                _wait_tile(total_ref, (i - 1) * U + u, ybuf_ref.at[U - group + u], xb_ref, sems.at[U - group + u], True)

        for u in range(U):
            _wait_tile(total_ref, i * U + u, ybuf_ref.at[group + u], xb_ref, sems.at[group + u], True)


def _dispatch(xt, ld, seg, lstart, gstart, totals, pad_ends, padded, R):
    N, D = xt.shape
    RT = ROUTE_T
    grid_spec = pltpu.PrefetchScalarGridSpec(
        num_scalar_prefetch=6,
        grid=(N // (RT * ROUTE_TILES),),
        in_specs=[
            pl.BlockSpec((TOP_K, RT * ROUTE_TILES), lambda i, *_: (0, i)),
            pl.BlockSpec((RT * ROUTE_TILES, D), lambda i, *_: (i, 0)),
        ],
        out_specs=pl.BlockSpec(memory_space=pl.ANY),
        scratch_shapes=[pltpu.VMEM((2 * ROUTE_TILES, LOCAL_ROWS, D), F32), pltpu.VMEM((MOE_T, D), F32),
                        pltpu.SemaphoreType.DMA(()), pltpu.SemaphoreType.DMA((2 * ROUTE_TILES,))],
    )
    return pl.pallas_call(
        _dispatch_kernel,
        grid_spec=grid_spec,
        out_shape=jax.ShapeDtypeStruct((R, D), F32),
        compiler_params=_cparams(("arbitrary",), VMEM_MEDIUM_MIB),
        name="dispatch",
    )(seg, lstart, gstart, totals, pad_ends, padded, ld, xt)


def _expert_kernel(start_ref, nblk_ref, b1_ref, b2_ref, perm_ref, w1_hbm, w2_hbm, xb_ref, yb_ref,
                   w1f_ref, w2f_ref, w1p_ref, w2b_ref, xbuf_ref, ybuf_ref, xodd_ref, yodd_ref,
                   w_sems, in_sems, out_sems, odd_sems):
    e = pl.program_id(0)
    n_experts = pl.num_programs(0)
    ws = e % 2

    def w_copies(ex, slot):
        return (pltpu.make_async_copy(w1_hbm.at[ex], w1f_ref.at[slot], w_sems.at[slot, 0]),
                pltpu.make_async_copy(w2_hbm.at[ex], w2f_ref.at[slot], w_sems.at[slot, 1]))
    T = xodd_ref.shape[0]
    G = 2 * LANES
    nb = nblk_ref[e]
    base = start_ref[e]
    pairs = nb // 2
    odd = nb % 2

    def pair_rows(p):
        return pl.ds(pl.multiple_of(base + p * (2 * T), T), 2 * T)

    def in_copy(p, slot):
        return pltpu.make_async_copy(xb_ref.at[pair_rows(p)], xbuf_ref.at[slot], in_sems.at[slot])

    def out_copy(p, slot):
        return pltpu.make_async_copy(ybuf_ref.at[slot], yb_ref.at[pair_rows(p)], out_sems.at[slot])

    odd_rows = pl.ds(pl.multiple_of(base + (nb - 1) * T, T), T)
    odd_in = pltpu.make_async_copy(xb_ref.at[odd_rows], xodd_ref, odd_sems.at[0])
    odd_out = pltpu.make_async_copy(yodd_ref, yb_ref.at[odd_rows], odd_sems.at[1])

    @pl.when(odd == 1)
    def _():
        odd_in.start()

    @pl.when(pairs > 0)
    def _():
        in_copy(0, 0).start()

    @pl.when(e == 0)
    def _():
        for c in w_copies(0, 0):
            c.start()

    for c in w_copies(e, ws):
        c.wait()

    @pl.when(e + 1 < n_experts)
    def _():
        for c in w_copies(e + 1, 1 - ws):
            c.start(priority=1)

    def mlp(x):
        x = x.astype(BF16)
        acts = []
        for g in range(w1p_ref.shape[1] // G):
            hg = _dot(x, w1p_ref[:, g * G:(g + 1) * G]) + b1_ref[0, :, g * G:(g + 1) * G]
            glu = jnp.minimum(hg[:, :LANES], SWIGLU_LIMIT)
            lin = jnp.clip(hg[:, LANES:], -SWIGLU_LIMIT, SWIGLU_LIMIT)
            acts.append((glu * jax.nn.sigmoid(SWIGLU_ALPHA * glu) * (lin + 1.0)).astype(BF16))
        return _dot(jnp.concatenate(acts, axis=-1), w2b_ref[...]) + b2_ref[0]

    @pl.when(nb > 0)
    def _():
        for g in range(w1p_ref.shape[1] // G):
            w = w1f_ref[ws, :, g * G:(g + 1) * G].astype(BF16)
            w1p_ref[:, g * G:(g + 1) * G] = _dot(w, perm_ref[...]).astype(BF16)
        w2b_ref[...] = w2f_ref[ws].astype(BF16)

        @pl.when(odd == 1)
        def _():
            odd_in.wait()
            yodd_ref[...] = mlp(xodd_ref[...])
            odd_out.start()

        def pair(p, carry):
            slot = p % 2
            in_copy(p, slot).wait()

            @pl.when(p + 1 < pairs)
            def _():
                in_copy(p + 1, 1 - slot).start()

            y = mlp(xbuf_ref[slot])

            @pl.when(p >= 2)
            def _():
                out_copy(p - 2, slot).wait()

            ybuf_ref[slot] = y
            out_copy(p, slot).start()
            return carry

        lax.fori_loop(0, pairs, pair, 0)

        @pl.when(pairs >= 2)
        def _():
            out_copy(pairs - 2, pairs % 2).wait()

        @pl.when(pairs >= 1)
        def _():
            out_copy(pairs - 1, (pairs - 1) % 2).wait()

        @pl.when(odd == 1)
        def _():
            odd_out.wait()


def _experts(xb, region_start, region_blocks, w1, b1p, w2, b2):
    R, D = xb.shape
    E, _, F2 = w1.shape
    F = F2 // 2
    T = MOE_T
    G = 2 * LANES
    c = np.arange(G)
    src = np.where(c < LANES, 2 * c, 2 * (c - LANES) + 1)
    perm = jnp.asarray(np.arange(G)[:, None] == src[None, :], dtype=BF16)
    per_expert = lambda r, w: pl.BlockSpec((1, r, w), lambda e, *_: (e, 0, 0))
    grid_spec = pltpu.PrefetchScalarGridSpec(
        num_scalar_prefetch=2,
        grid=(E,),
        in_specs=[
            per_expert(1, F2), per_expert(1, D),
            pl.BlockSpec((G, G), lambda e, *_: (0, 0)),
            pl.BlockSpec(memory_space=pl.ANY), pl.BlockSpec(memory_space=pl.ANY), pl.BlockSpec(memory_space=pl.ANY),
        ],
        out_specs=pl.BlockSpec(memory_space=pl.ANY),
        scratch_shapes=[pltpu.VMEM((2, D, F2), F32), pltpu.VMEM((2, F, D), F32),
                        pltpu.VMEM((D, F2), BF16), pltpu.VMEM((F, D), BF16),
                        pltpu.VMEM((2, 2 * T, D), F32), pltpu.VMEM((2, 2 * T, D), F32),
                        pltpu.VMEM((T, D), F32), pltpu.VMEM((T, D), F32),
                        pltpu.SemaphoreType.DMA((2, 2)), pltpu.SemaphoreType.DMA((2,)), pltpu.SemaphoreType.DMA((2,)),
                        pltpu.SemaphoreType.DMA((2,))],
    )
    return pl.pallas_call(
        _expert_kernel,
        grid_spec=grid_spec,
        out_shape=jax.ShapeDtypeStruct((R, D), F32),
        input_output_aliases={7: 0},
        compiler_params=_cparams(("arbitrary",), VMEM_LARGE_MIB),
        name="experts",
    )(region_start, region_blocks, b1p, b2.reshape(E, 1, D), perm, w1, w2, xb)


def _combine_kernel(seg_ref, lstart_ref, gstart_ref, total_ref, ldt_ref, wt_ref, x1_ref, yb_ref, o_ref, ybuf_ref,
                    sems):
    i = pl.program_id(0)
    n = pl.num_programs(0)
    U = ROUTE_TILES
    group = (i % 2) * U

    def fetch(step, g):
        for u in range(U):
            _segment_copies(seg_ref, lstart_ref, gstart_ref, step * U + u, ybuf_ref.at[g + u], yb_ref,
                            sems.at[g + u], False, lambda c: c.start())

    @pl.when(i == 0)
    def _():
        ybuf_ref[...] = jnp.zeros_like(ybuf_ref)
        fetch(i, group)

    @pl.when(i + 1 < n)
    def _():
        fetch(i + 1, U - group)

    for u in range(U):
        _wait_tile(total_ref, i * U + u, ybuf_ref.at[group + u], yb_ref, sems.at[group + u], False)

    L = ybuf_ref.shape[1]
    RT = x1_ref.shape[0] // U
    c_iota = lax.broadcasted_iota(I32, (RT, L), 1)
    for u in range(U):
        rows = slice(u * RT, (u + 1) * RT)
        ldt = ldt_ref[rows, :]
        w = wt_ref[rows, :]
        w_hi = w.astype(BF16).astype(F32)
        w_lo = w - w_hi
        g_hi = jnp.zeros((RT, L), F32)
        g_lo = jnp.zeros((RT, L), F32)
        for k in range(TOP_K):
            hit = c_iota == ldt[:, k:k + 1]
            g_hi = jnp.where(hit, w_hi[:, k:k + 1], g_hi)
            g_lo = jnp.where(hit, w_lo[:, k:k + 1], g_lo)
        y = ybuf_ref[group + u].astype(BF16)
        o_ref[rows, :] = x1_ref[rows, :] + (_dot(g_hi.astype(BF16), y) + _dot(g_lo.astype(BF16), y))


def _combine(ldt, wts_t, x1, yb, seg, lstart, gstart, totals):
    N, D = x1.shape
    RT = ROUTE_T
    grid_spec = pltpu.PrefetchScalarGridSpec(
        num_scalar_prefetch=4,
        grid=(N // (RT * ROUTE_TILES),),
        in_specs=[
            pl.BlockSpec((RT * ROUTE_TILES, TOP_K), lambda i, *_: (i, 0)),
            pl.BlockSpec((RT * ROUTE_TILES, TOP_K), lambda i, *_: (i, 0)),
            pl.BlockSpec((RT * ROUTE_TILES, D), lambda i, *_: (i, 0)),
            pl.BlockSpec(memory_space=pl.ANY),
        ],
        out_specs=pl.BlockSpec((RT * ROUTE_TILES, D), lambda i, *_: (i, 0)),
        scratch_shapes=[pltpu.VMEM((2 * ROUTE_TILES, LOCAL_ROWS, D), F32), pltpu.SemaphoreType.DMA((2 * ROUTE_TILES,))],
    )
    return pl.pallas_call(
        _combine_kernel,
        grid_spec=grid_spec,
        out_shape=jax.ShapeDtypeStruct((N, D), F32),
        compiler_params=_cparams(("arbitrary",), VMEM_MEDIUM_MIB),
        name="combine",
    )(seg, lstart, gstart, totals, ldt, wts_t, x1, yb)


def _layer(x, mem, g_mix, w_in, b_gate, g_ret_out, g_moba_q, g_moba_k, g_mem, w_mem_kv, g_mem_q, g_mem_k,
           w_br_ret, w_br_moba, w_br_mem, w_out, g_ffn, w_router, b_router, w_mlp1, b_mlp1, w_mlp2, b_mlp2):
    B, S, D = x.shape
    N = B * S
    x2 = x.reshape(N, D)
    slopes_np = np.exp2(-8.0 * (np.arange(MOBA_HEADS, dtype=np.float64) + 1.0) / MOBA_HEADS)
    assert all(float(np.log2(s)).is_integer() for s in slopes_np)
    slopes = tuple(float(s) for s in slopes_np)

    w_mix = w_in[:, :MIX_W].astype(BF16)
    w_gate = w_in[:, MIX_W:].astype(BF16)
    kmem, vmem = _memkv(mem, g_mem, w_mem_kv, g_mem_k)
    rq, rk, rv, rg, mq, mk, mv, cq, km = _inproj(x2, B, S, g_mix, w_mix, g_moba_q, g_moba_k, g_mem_q, slopes)
    o_ret = _retention(rq, rk, rv, rg, g_ret_out)
    o_moba = _moba(mq, mk, mv, km, jnp.asarray(slopes, F32)).reshape(N, MOBA_W)
    x1, xt, eidx, wts, rank, cnt = _merge(x2, S, o_ret, o_moba, cq, kmem, vmem, g_mix, w_gate, b_gate,
                                          w_br_ret, w_br_moba, w_br_mem, w_out, g_ffn, w_router, b_router)
    T = MOE_T
    tcnt = cnt[:, :, 0]
    seg = ((tcnt + SEG_ALIGN - 1) // SEG_ALIGN) * SEG_ALIGN
    region_rows = jnp.sum(seg, axis=0)
    padded = ((region_rows + T - 1) // T) * T
    pad_ends = jnp.cumsum(padded).astype(I32)
    pad_starts = pad_ends - padded
    gstart = pad_starts[None, :] + jnp.cumsum(seg, axis=0) - seg
    lstart = jnp.cumsum(seg, axis=1) - seg
    lstart_tok = jnp.broadcast_to(lstart[:, None, :], (N // ROUTE_T, ROUTE_T, N_EXPERTS)).reshape(N, N_EXPERTS)
    onehot = eidx[:, :, None] == jnp.arange(N_EXPERTS, dtype=I32)[None, None, :]
    ld = jnp.sum(jnp.where(onehot, lstart_tok[None], 0), axis=-1) + rank
    NB = -(-(N // ROUTE_T * LOCAL_ROWS) // T) + N_EXPERTS
    seg_f, lstart_f, gstart_f = (a.reshape(-1).astype(I32) for a in (seg, lstart, gstart))
    totals = jnp.sum(seg, axis=1).astype(I32)

    xb = _dispatch(xt, ld, seg_f, lstart_f, gstart_f, totals, pad_ends, padded.astype(I32), NB * T)
    F2 = w_mlp1.shape[-1]
    b1p = b_mlp1.reshape(N_EXPERTS, F2 // (2 * LANES), LANES, 2).transpose(0, 1, 3, 2).reshape(N_EXPERTS, 1, F2)
    yb = _experts(xb, pad_starts.astype(I32), (padded // T).astype(I32), w_mlp1, b1p, w_mlp2, b_mlp2)
    out = _combine(ld.T, wts.T, x1, yb, seg_f, lstart_f, gstart_f, totals)
    return out.reshape(B, S, D)


def kernel(x, mem, g_mix, w_in, b_gate, g_ret_out, g_moba_q, g_moba_k, g_mem, w_mem_kv, g_mem_q, g_mem_k, w_br_ret, w_br_moba, w_br_mem, w_out, g_ffn, w_router, b_router, w_mlp1, b_mlp1, w_mlp2, b_mlp2):
    for l in range(g_mix.shape[0]):
        x = _layer(x, mem, g_mix[l], w_in[l], b_gate[l], g_ret_out[l], g_moba_q[l], g_moba_k[l], g_mem[l],
                   w_mem_kv[l], g_mem_q[l], g_mem_k[l], w_br_ret[l], w_br_moba[l], w_br_mem[l], w_out[l],
                   g_ffn[l], w_router[l], b_router[l], w_mlp1[l], b_mlp1[l], w_mlp2[l], b_mlp2[l])
    return x
```

```python
import functools

import jax
import jax.numpy as jnp
import numpy as np
from jax import lax
from jax.experimental import pallas as pl
from jax.experimental.pallas import tpu as pltpu

F32 = jnp.float32
BF16 = jnp.bfloat16
I32 = jnp.int32

EPS = 1e-5
NEG = -1e30

RET_HEADS = 4
RET_DK = 64
RET_DV = 128
RET_CHUNK = 128
MOBA_HEADS = 8
MOBA_HD = 64
MOBA_BLOCK = 256
MOBA_TOPK = 3
MEM_HEADS = 4
MEM_HD = 128
N_BRANCH = 3
N_EXPERTS = 32
TOP_K = 4
SWIGLU_LIMIT = 7.0
SWIGLU_ALPHA = 1.702

RET_Q = RET_HEADS * RET_DK
RET_V = RET_HEADS * RET_DV
MOBA_W = MOBA_HEADS * MOBA_HD
MEM_W = MEM_HEADS * MEM_HD
MIX_W = 2 * RET_Q + 2 * RET_V + 3 * MOBA_W + MEM_W

LANES = 128
V7X_VMEM_MIB = 64
VMEM_LARGE_MIB = V7X_VMEM_MIB - 8
VMEM_MEDIUM_MIB = V7X_VMEM_MIB * 3 // 4
VMEM_SMALL_MIB = V7X_VMEM_MIB // 2
MOBA_PAIRS = MOBA_HEADS // 2
MOBA_HEADS_PER_STEP = 8
BIAS_LANE0 = MOBA_HD
ONE_LANE = 80
VSUM_LANE = MOBA_HD

MOE_T = 256
TM_PROJ = 512
TM_MERGE = 512
TQ_RET = 512
ROUTE_T = 256
ROUTE_TILES = 2
SEG_ALIGN = 8
LOCAL_ROWS = -(-(ROUTE_T * TOP_K + N_EXPERTS * (SEG_ALIGN - 1)) // LANES) * LANES
SEG_CHUNKS = tuple(2 ** p for p in range(ROUTE_T.bit_length() - 1, SEG_ALIGN.bit_length() - 2, -1))
SEG_SMALL = 64
TOTAL_CHUNKS = tuple(2 ** p for p in range(LOCAL_ROWS.bit_length() - 1, SEG_ALIGN.bit_length() - 2, -1))

_NT = (((1,), (1,)), ((), ()))
_TN = (((0,), (0,)), ((), ()))


def _rms(x, g):
    return x * lax.rsqrt(jnp.mean(x * x, axis=-1, keepdims=True) + EPS) * g


def _dot(a, b):
    return jnp.dot(a, b, preferred_element_type=F32)


def _cparams(sem, vmem_mb):
    return pltpu.CompilerParams(dimension_semantics=sem, vmem_limit_bytes=vmem_mb * 1024 * 1024)


def _memkv_kernel(mem_ref, g_ref, w_ref, gk_ref, k_ref, v_ref):
    m = _rms(mem_ref[0], g_ref[...]).astype(BF16)
    kv = _dot(m, w_ref[...])
    ks = [_rms(kv[:, h * MEM_HD:(h + 1) * MEM_HD], gk_ref[...]) for h in range(MEM_HEADS)]
    k_ref[0] = jnp.concatenate(ks, axis=-1).astype(BF16)
    v_ref[0] = kv[:, MEM_W:].astype(BF16)


def _memkv(mem, g_mem, w_mem_kv, g_mem_k):
    B, M, D = mem.shape
    return pl.pallas_call(
        _memkv_kernel,
        grid=(B,),
        in_specs=[
            pl.BlockSpec((1, M, D), lambda b: (b, 0, 0)),
            pl.BlockSpec((1, D), lambda b: (0, 0)),
            pl.BlockSpec((D, 2 * MEM_W), lambda b: (0, 0)),
            pl.BlockSpec((1, MEM_HD), lambda b: (0, 0)),
        ],
        out_specs=[
            pl.BlockSpec((1, M, MEM_W), lambda b: (b, 0, 0)),
            pl.BlockSpec((1, M, MEM_W), lambda b: (b, 0, 0)),
        ],
        out_shape=[jax.ShapeDtypeStruct((B, M, MEM_W), BF16)] * 2,
        compiler_params=_cparams(("arbitrary",), VMEM_SMALL_MIB),
        name="memkv",
    )(mem, g_mem.reshape(1, D), w_mem_kv.astype(BF16), g_mem_k.reshape(1, MEM_HD))


def _head_pair_norm(a2, g2, lane):
    sq = a2 * a2
    lo = lane < MOBA_HD
    ss_lo = jnp.sum(jnp.where(lo, sq, 0.0), axis=-1, keepdims=True)
    ss_hi = jnp.sum(jnp.where(lo, 0.0, sq), axis=-1, keepdims=True)
    inv = jnp.where(lo, lax.rsqrt(ss_lo / MOBA_HD + EPS), lax.rsqrt(ss_hi / MOBA_HD + EPS))
    return a2 * inv * g2


def _inproj_kernel(slopes, seq_tiles, x_ref, gmix_ref, w32_ref, gq_ref, gk_ref, gc_ref,
                   rq_ref, rk_ref, rv_ref, rg_ref, mq_ref, mk_ref, mv_ref, cq_ref, kmean_ref, w_ref):
    tm = x_ref.shape[0]

    @pl.when(pl.program_id(0) == 0)
    def _():
        for c in range(0, w_ref.shape[1], 4 * LANES):
            w_ref[:, c:c + 4 * LANES] = w32_ref[:, c:c + 4 * LANES].astype(BF16)

    blocks_per_tile = tm // MOBA_BLOCK
    h = _rms(x_ref[...], gmix_ref[...]).astype(BF16)
    col = [0]

    def proj(width):
        a = _dot(h, w_ref[:, col[0]:col[0] + width])
        col[0] += width
        return a

    a = proj(2 * RET_Q)
    for hh in range(RET_HEADS):
        rq_ref[0, hh] = a[:, hh * RET_DK:(hh + 1) * RET_DK].astype(BF16)
        rk_ref[0, hh] = (a[:, RET_Q + hh * RET_DK:RET_Q + (hh + 1) * RET_DK] * (RET_DK ** -0.5)).astype(BF16)
    rv_ref[...] = proj(RET_V).astype(BF16)
    rg_ref[...] = proj(RET_V).astype(BF16)

    lane = lax.broadcasted_iota(I32, (tm, LANES), 1)
    row = lax.broadcasted_iota(I32, (tm, LANES), 0)
    lo = lane < MOBA_HD
    q_tail = jnp.where(lane == ONE_LANE, 1.0, 0.0)
    a = proj(MOBA_W)
    for p in range(MOBA_PAIRS):
        n2 = _head_pair_norm(a[:, p * LANES:(p + 1) * LANES], gq_ref[...], lane) * (MOBA_HD ** -0.5)
        mq_ref[0, 2 * p] = jnp.where(lo, n2, q_tail).astype(BF16)
        mq_ref[0, 2 * p + 1] = jnp.where(lo, pltpu.roll(n2, MOBA_HD, 1), q_tail).astype(BF16)
    blk = (pl.program_id(0) % seq_tiles) * blocks_per_tile + row // MOBA_BLOCK
    onehot_tail = jnp.where(lane == BIAS_LANE0 + blk, 1.0, 0.0)
    off = (row % MOBA_BLOCK).astype(F32)
    a = proj(MOBA_W)
    for p in range(MOBA_PAIRS):
        n2 = _head_pair_norm(a[:, p * LANES:(p + 1) * LANES], gk_ref[...], lane)
        for j in range(blocks_per_tile):
            kmean_ref[0, 0, p, j:j + 1, :] = jnp.mean(n2[j * MOBA_BLOCK:(j + 1) * MOBA_BLOCK], axis=0, keepdims=True)
        for s, src in ((0, n2), (1, pltpu.roll(n2, MOBA_HD, 1))):
            tail = jnp.where(lane == ONE_LANE, slopes[2 * p + s] * off, onehot_tail)
            mk_ref[0, 2 * p + s] = jnp.where(lo, src, tail).astype(BF16)
    v_tail = jnp.where(lane == VSUM_LANE, 1.0, 0.0)
    a = proj(MOBA_W)
    for p in range(MOBA_PAIRS):
        a2 = a[:, p * LANES:(p + 1) * LANES]
        mv_ref[0, 2 * p] = jnp.where(lo, a2, v_tail).astype(BF16)
        mv_ref[0, 2 * p + 1] = jnp.where(lo, pltpu.roll(a2, MOBA_HD, 1), v_tail).astype(BF16)
    a = proj(MEM_W)
    cq = [_rms(a[:, hh * MEM_HD:(hh + 1) * MEM_HD], gc_ref[...]) for hh in range(MEM_HEADS)]
    cq_ref[...] = jnp.concatenate(cq, axis=-1).astype(BF16)


def _inproj(x2, B, S, g_mix, w_in, g_moba_q, g_moba_k, g_mem_q, slopes):
    N, D = x2.shape
    tm = TM_PROJ
    nS = S // tm
    bpt = tm // MOBA_BLOCK
    tok = lambda i: (i, 0)
    headmaj = lambda i: (i // nS, 0, i % nS, 0)
    g2 = lambda g: jnp.concatenate([g, g]).reshape(1, LANES)
    outs = pl.pallas_call(
        functools.partial(_inproj_kernel, slopes, nS),
        grid=(N // tm,),
        in_specs=[
            pl.BlockSpec((tm, D), tok),
            pl.BlockSpec((1, D), lambda i: (0, 0)),
            pl.BlockSpec((D, MIX_W), lambda i: (0, 0), pipeline_mode=pl.Buffered(1)),
            pl.BlockSpec((1, LANES), lambda i: (0, 0)),
            pl.BlockSpec((1, LANES), lambda i: (0, 0)),
            pl.BlockSpec((1, MEM_HD), lambda i: (0, 0)),
        ],
        out_specs=[
            pl.BlockSpec((1, RET_HEADS, tm, RET_DK), headmaj),
            pl.BlockSpec((1, RET_HEADS, tm, RET_DK), headmaj),
            pl.BlockSpec((tm, RET_V), tok),
            pl.BlockSpec((tm, RET_V), tok),
            pl.BlockSpec((1, MOBA_HEADS, tm, LANES), headmaj),
            pl.BlockSpec((1, MOBA_HEADS, tm, LANES), headmaj),
            pl.BlockSpec((1, MOBA_HEADS, tm, LANES), headmaj),
            pl.BlockSpec((tm, MEM_W), tok),
            pl.BlockSpec((1, 1, MOBA_PAIRS, bpt, LANES), lambda i: (i // nS, i % nS, 0, 0, 0)),
        ],
        out_shape=[
            jax.ShapeDtypeStruct((B, RET_HEADS, S, RET_DK), BF16),
            jax.ShapeDtypeStruct((B, RET_HEADS, S, RET_DK), BF16),
            jax.ShapeDtypeStruct((N, RET_V), BF16),
            jax.ShapeDtypeStruct((N, RET_V), BF16),
            jax.ShapeDtypeStruct((B, MOBA_HEADS, S, LANES), BF16),
            jax.ShapeDtypeStruct((B, MOBA_HEADS, S, LANES), BF16),
            jax.ShapeDtypeStruct((B, MOBA_HEADS, S, LANES), BF16),
            jax.ShapeDtypeStruct((N, MEM_W), BF16),
            jax.ShapeDtypeStruct((B, nS, MOBA_PAIRS, bpt, LANES), F32),
        ],
        scratch_shapes=[pltpu.VMEM((D, MIX_W), BF16)],
        compiler_params=_cparams(("arbitrary",), VMEM_LARGE_MIB),
        name="inproj",
    )(x2, g_mix.reshape(1, D), w_in, g2(g_moba_q), g2(g_moba_k), g_mem_q.reshape(1, MEM_HD))
    rq, rk, rv, rg, mq, mk, mv, cq, kmean_pairs = outs
    km = kmean_pairs.reshape(B, nS, MOBA_PAIRS, bpt, 2, MOBA_HD).transpose(0, 2, 4, 1, 3, 5)
    km = km.reshape(B, MOBA_HEADS, nS * bpt, MOBA_HD)
    km = jnp.pad(km, ((0, 0), (0, 0), (0, 0), (0, LANES - MOBA_HD)))
    return rq, rk, rv, rg, mq, mk, mv, cq, km


def _retention_kernel(q_ref, k_ref, v_ref, rg_ref, din_ref, dq_ref, dk_ref, dc_ref, g_ref, o_ref, state_ref):
    @pl.when(pl.program_id(1) == 0)
    def _():
        state_ref[...] = jnp.zeros_like(state_ref)

    C = RET_CHUNK
    for h in range(q_ref.shape[1]):
        cols = slice(h * RET_DV, (h + 1) * RET_DV)
        state = state_ref[h]
        for c in range(q_ref.shape[2] // C):
            rows = slice(c * C, (c + 1) * C)
            q = q_ref[0, h, rows, :]
            k = k_ref[0, h, rows, :]
            v = v_ref[rows, cols]
            scores = lax.dot_general(q, k, _NT, preferred_element_type=F32) * din_ref[h]
            intra = _dot(scores.astype(BF16), v)
            cross = _dot(q, state.astype(BF16)) * dq_ref[h]
            kd = (k.astype(F32) * dk_ref[h]).astype(BF16)
            state = dc_ref[h] * state + lax.dot_general(kd, v, _TN, preferred_element_type=F32)
            o = _rms(intra + cross, g_ref[...])
            o_ref[rows, cols] = (o * jax.nn.silu(rg_ref[rows, cols].astype(F32))).astype(BF16)
        state_ref[h] = state


def _retention_decays():
    H, C = RET_HEADS, RET_CHUNK
    log_g = jnp.log1p(-jnp.exp2(-5.0 - jnp.arange(H, dtype=F32)))
    i = jnp.arange(C, dtype=F32)
    diff = i[:, None] - i[None, :]
    decay_in = jnp.where(diff >= 0, jnp.exp(jnp.maximum(diff, 0.0)[None] * log_g[:, None, None]), 0.0)
    decay_k = jnp.exp((C - 1 - i)[None, :] * log_g[:, None])
    decay_q = jnp.exp((i + 1)[None, :] * log_g[:, None])
    decay_chunk = jnp.exp(C * log_g)
    dq = jnp.broadcast_to(decay_q[:, :, None], (H, C, RET_DV))
    dk = jnp.broadcast_to(decay_k[:, :, None], (H, C, RET_DK))
    dc = jnp.broadcast_to(decay_chunk[:, None, None], (H, RET_DK, RET_DV))
    return decay_in, dq, dk, dc


def _retention(rq, rk, rv, rg, g_ret_out):
    B, H, S, dk = rq.shape
    tq = TQ_RET
    nT = S // tq
    C = RET_CHUNK
    din, dq, dk_, dc = _retention_decays()
    qk_spec = pl.BlockSpec((1, H, tq, dk), lambda b, t: (b, 0, t, 0))
    tok_spec = pl.BlockSpec((tq, RET_V), lambda b, t: (b * nT + t, 0))
    const = lambda r, c: pl.BlockSpec((H, r, c), lambda b, t: (0, 0, 0))
    return pl.pallas_call(
        _retention_kernel,
        grid=(B, nT),
        in_specs=[qk_spec, qk_spec, tok_spec, tok_spec,
                  const(C, C), const(C, RET_DV), const(C, RET_DK), const(RET_DK, RET_DV),
                  pl.BlockSpec((1, RET_DV), lambda b, t: (0, 0))],
        out_specs=tok_spec,
        out_shape=jax.ShapeDtypeStruct((B * S, RET_V), BF16),
        scratch_shapes=[pltpu.VMEM((H, RET_DK, RET_DV), F32)],
        compiler_params=_cparams(("arbitrary", "arbitrary"), VMEM_SMALL_MIB),
        name="retention",
    )(rq, rk, rv, rg, din, dq, dk_, dc, g_ret_out.reshape(1, RET_DV))


def _moba_kernel(slopes_ref, q_ref, k_ref, v_ref, km_ref, shift_ref, mask_ref, o_ref):
    NH = q_ref.shape[1]
    head0 = pl.program_id(1) * NH
    i0 = pl.program_id(2) * 2
    BS = MOBA_BLOCK
    nb = km_ref.shape[2]
    heads = range(NH)
    n_iota = lax.broadcasted_iota(I32, (nb, 2 * BS), 0)
    q_blk = i0 + lax.broadcasted_iota(I32, (nb, 2 * BS), 1) // BS
    past = n_iota < q_blk
    qa = [q_ref[0, s] for s in heads]
    q_aug = []
    for s in heads:
        km = km_ref[0, s]
        hi = km.astype(BF16)
        mid = (km - hi.astype(F32)).astype(BF16)
        lo = (km - hi.astype(F32) - mid.astype(F32)).astype(BF16)
        g3 = lax.dot_general(jnp.concatenate([hi, mid, lo], axis=0), qa[s], _NT, preferred_element_type=F32)
        gate = (g3[:nb] + g3[nb:2 * nb]) + g3[2 * nb:]
        g = jnp.where(past, gate, -jnp.inf)
        rank = jnp.zeros((nb, 2 * BS), I32)
        for m in range(nb - 1):
            gm = g[m:m + 1, :]
            rank = rank + jnp.where(gm > g, 1, jnp.where(gm == g, jnp.where(m < n_iota, 1, 0), 0))
        sel = jnp.where(past, rank, MOBA_TOPK) < MOBA_TOPK
        bias = jnp.where(sel, (n_iota - q_blk).astype(F32) * (slopes_ref[head0 + s] * BS), NEG).astype(BF16)
        placed = lax.dot_general(bias, shift_ref[...], _TN, preferred_element_type=F32)
        q_aug.append((qa[s].astype(F32) + placed).astype(BF16))

    def rows_of(j):
        return pl.ds(pl.multiple_of(j * BS, BS), BS)

    m_lo, m_hi, acc_lo, acc_hi = [], [], [], []
    for s in heads:
        q_mix = jnp.concatenate([qa[s][:BS], q_aug[s][BS:]], axis=0)
        sc = lax.dot_general(q_mix, k_ref[0, s, rows_of(i0), :], _NT, preferred_element_type=F32) + mask_ref[...]
        m = jnp.max(sc, axis=-1, keepdims=True)
        acc = _dot(jnp.exp(sc - m).astype(BF16), v_ref[0, s, rows_of(i0), :])
        m_lo.append(m[:BS])
        acc_lo.append(acc[:BS])
        sc1 = lax.dot_general(qa[s][BS:], k_ref[0, s, rows_of(i0 + 1), :], _NT, preferred_element_type=F32)
        sc1 = sc1 + mask_ref[:BS, :]
        m1 = jnp.maximum(m[BS:], jnp.max(sc1, axis=-1, keepdims=True))
        p1 = jnp.exp(sc1 - m1).astype(BF16)
        m_hi.append(m1)
        acc_hi.append(jnp.exp(m[BS:] - m1) * acc[BS:] + _dot(p1, v_ref[0, s, rows_of(i0 + 1), :]))

    def body(j, carry):
        ms_lo, ms_hi, accs_lo, accs_hi = carry
        out = ([], [], [], [])
        for s in heads:
            sj = lax.dot_general(q_aug[s], k_ref[0, s, rows_of(j), :], _NT, preferred_element_type=F32)
            ps, alphas, ms = [], [], []
            for half, m_prev in ((slice(0, BS), ms_lo[s]), (slice(BS, 2 * BS), ms_hi[s])):
                m_new = jnp.maximum(m_prev, jnp.max(sj[half], axis=-1, keepdims=True))
                ps.append(jnp.exp(sj[half] - m_new).astype(BF16))
                alphas.append(jnp.exp(m_prev - m_new))
                ms.append(m_new)
            pv = _dot(jnp.concatenate(ps, axis=0), v_ref[0, s, rows_of(j), :])
            out[0].append(ms[0])
            out[1].append(ms[1])
            out[2].append(alphas[0] * accs_lo[s] + pv[:BS])
            out[3].append(alphas[1] * accs_hi[s] + pv[BS:])
        return tuple(tuple(o) for o in out)

    _, _, accs_lo, accs_hi = lax.fori_loop(0, i0 // 2, lambda j, c: body(2 * j + 1, body(2 * j, c)),
                                           (tuple(m_lo), tuple(m_hi), tuple(acc_lo), tuple(acc_hi)))
    lane = lax.broadcasted_iota(I32, (BS, LANES), 1)
    for t, accs in enumerate((accs_lo, accs_hi)):
        outs = [acc / acc[:, VSUM_LANE:VSUM_LANE + 1] for acc in accs]
        pairs = [jnp.where(lane < MOBA_HD, outs[2 * p], pltpu.roll(outs[2 * p + 1], MOBA_HD, 1))
                 for p in range(NH // 2)]
        o_ref[0, t * BS:(t + 1) * BS, :] = jnp.concatenate(pairs, axis=-1).astype(BF16)


def _moba(mq, mk, mv, km, slopes):
    B, H, S, _ = mq.shape
    BS = MOBA_BLOCK
    NH = MOBA_HEADS_PER_STEP
    nq = S // BS
    nb = km.shape[2]
    assert BIAS_LANE0 + nb <= ONE_LANE and nq % 2 == 0
    shift = (jnp.arange(nb)[:, None] + BIAS_LANE0 == jnp.arange(LANES)[None, :]).astype(BF16)
    causal = jnp.where(jnp.arange(BS)[:, None] >= jnp.arange(BS)[None, :], 0.0, NEG).astype(F32)
    mask = jnp.concatenate([causal, jnp.zeros((BS, BS), F32)], axis=0)
    per_head = lambda rows: pl.BlockSpec((1, NH, rows, LANES), lambda b, p, i, sl: (b, p, 0, 0))
    grid_spec = pltpu.PrefetchScalarGridSpec(
        num_scalar_prefetch=1,
        grid=(B, H // NH, nq // 2),
        in_specs=[
            pl.BlockSpec((1, NH, 2 * BS, LANES), lambda b, p, i, sl: (b, p, i, 0)),
            per_head(S), per_head(S), per_head(nb),
            pl.BlockSpec((nb, LANES), lambda b, p, i, sl: (0, 0)),
            pl.BlockSpec((2 * BS, BS), lambda b, p, i, sl: (0, 0)),
        ],
        out_specs=pl.BlockSpec((1, 2 * BS, NH * MOBA_HD), lambda b, p, i, sl: (b, i, p)),
    )
    return pl.pallas_call(
        _moba_kernel,
        grid_spec=grid_spec,
        out_shape=jax.ShapeDtypeStruct((B, S, MOBA_W), BF16),
        compiler_params=_cparams(("arbitrary", "arbitrary", "arbitrary"), VMEM_LARGE_MIB),
        name="moba",
    )(slopes, mq, mk, mv, km, shift, mask)


def _merge_kernel(x_ref, oret_ref, omoba_ref, cq_ref, kmem_ref, vmem_ref, gmix_ref, wg_ref, bg_ref,
                  wbr_ref, wbm_ref, wbc_ref, wout_ref, gffn_ref, wr_ref, br_ref, tri_ref,
                  x1_ref, xt_ref, eidx_ref, wts_ref, rank_ref, cnt_ref):
    D = x_ref.shape[1]
    tm = x_ref.shape[0]

    x = x_ref[...]
    h = _rms(x, gmix_ref[...]).astype(BF16)
    cq = cq_ref[...]
    om = []
    for hh in range(MEM_HEADS):
        cols = slice(hh * MEM_HD, (hh + 1) * MEM_HD)
        sc = lax.dot_general(cq[:, cols], kmem_ref[0, :, cols], _NT, preferred_element_type=F32) * (MEM_HD ** -0.5)
        sc = sc - jnp.max(sc, axis=-1, keepdims=True)
        p = jnp.exp(sc)
        p = p / jnp.sum(p, axis=-1, keepdims=True)
        om.append(_dot(p.astype(BF16), vmem_ref[0, :, cols]))
    omem = jnp.concatenate(om, axis=-1).astype(BF16)
    y = None
    for br, (o, w_ref) in enumerate(((oret_ref[...], wbr_ref), (omoba_ref[...], wbm_ref), (omem, wbc_ref))):
        gl = _dot(h, wg_ref[:, br * D:(br + 1) * D]) + bg_ref[:, br * D:(br + 1) * D]
        term = jax.nn.sigmoid(gl) * _dot(o, w_ref[...])
        y = term if y is None else y + term
    x1 = x + _dot(y.astype(BF16), wout_ref[...])
    x1_ref[...] = x1
    xt = _rms(x1, gffn_ref[...])
    wr = wr_ref[...]
    E = wr.shape[0]
    w_hi = wr.astype(BF16)
    w_mid = (wr - w_hi.astype(F32)).astype(BF16)
    w_lo = (wr - w_hi.astype(F32) - w_mid.astype(F32)).astype(BF16)
    xt_hi = xt.astype(BF16)
    xt_ref[...] = xt_hi
    xt_lo = (xt - xt_hi.astype(F32)).astype(BF16)
    a = lax.dot_general(jnp.concatenate([w_hi, w_mid, w_lo], axis=0), xt_hi, _NT, preferred_element_type=F32)
    b = lax.dot_general(jnp.concatenate([w_hi, w_mid], axis=0), xt_lo, _NT, preferred_element_type=F32)
    logits = (a[:E] + (a[E:2 * E] + b[:E])) + (a[2 * E:] + b[E:]) + br_ref[...]
    e_iota = lax.broadcasted_iota(I32, (E, tm), 0)
    l = logits
    vals, hots = [], []
    for k in range(TOP_K):
        m = jnp.max(l, axis=0, keepdims=True)
        idx = jnp.min(jnp.where(l == m, e_iota, E), axis=0, keepdims=True)
        hot = e_iota == idx
        l = jnp.where(hot, -jnp.inf, l)
        vals.append(m)
        hots.append(hot)
        eidx_ref[k:k + 1, :] = idx
    ex = [jnp.exp(v - vals[0]) for v in vals]
    den = ex[0]
    for k in range(1, TOP_K):
        den = den + ex[k]
    chosen = jnp.zeros((E, tm), F32)
    for k in range(TOP_K):
        wts_ref[k:k + 1, :] = ex[k] / den
        chosen = chosen + jnp.where(hots[k], 1.0, 0.0)
    RT = tri_ref.shape[0]
    for t in range(tm // RT):
        cols = slice(t * RT, (t + 1) * RT)
        prefix = _dot(chosen[:, cols].astype(BF16), tri_ref[...])
        for k in range(TOP_K):
            rank_ref[k:k + 1, cols] = jnp.sum(jnp.where(hots[k][:, cols], prefix, 0.0), axis=0, keepdims=True).astype(I32)
        counts = jnp.sum(chosen[:, cols], axis=1, keepdims=True)
        cnt_ref[t] = jnp.broadcast_to(counts, cnt_ref.shape[1:]).astype(I32)


def _merge(x2, S, o_ret, o_moba, cq, kmem, vmem, g_mix, w_gate, b_gate, w_br_ret, w_br_moba, w_br_mem, w_out,
           g_ffn, w_router, b_router):
    N, D = x2.shape
    tm = TM_MERGE
    nS = S // tm
    M = kmem.shape[1]
    E = w_router.shape[1]
    tok = lambda w: pl.BlockSpec((tm, w), lambda i: (i, 0))
    const = lambda r, c: pl.BlockSpec((r, c), lambda i: (0, 0), pipeline_mode=pl.Buffered(1))
    mem_spec = pl.BlockSpec((1, M, MEM_W), lambda i: (i // nS, 0, 0))
    lanes_tok = lambda r, dt: (pl.BlockSpec((r, tm), lambda i: (0, i)), jax.ShapeDtypeStruct((r, N), dt))
    RT = ROUTE_T
    tri = (jnp.arange(RT)[:, None] < jnp.arange(RT)[None, :]).astype(BF16)
    e_spec, e_shape = lanes_tok(TOP_K, I32)
    w_spec, w_shape = lanes_tok(TOP_K, F32)
    r_spec, r_shape = lanes_tok(TOP_K, I32)
    return pl.pallas_call(
        _merge_kernel,
        grid=(N // tm,),
        in_specs=[tok(D), tok(RET_V), tok(MOBA_W), tok(MEM_W), mem_spec, mem_spec,
                  const(1, D), const(D, N_BRANCH * D), const(1, N_BRANCH * D),
                  const(RET_V, D), const(MOBA_W, D), const(MEM_W, D), const(D, D),
                  const(1, D), const(E, D), const(E, 1), const(RT, RT)],
        out_specs=[tok(D), tok(D), e_spec, w_spec, r_spec, pl.BlockSpec((tm // RT, E, LANES), lambda i: (i, 0, 0))],
        out_shape=[jax.ShapeDtypeStruct((N, D), F32), jax.ShapeDtypeStruct((N, D), BF16),
                   e_shape, w_shape, r_shape, jax.ShapeDtypeStruct((N // RT, E, LANES), I32)],
        compiler_params=_cparams(("arbitrary",), VMEM_LARGE_MIB),
        name="merge",
    )(x2, o_ret, o_moba, cq, kmem, vmem, g_mix.reshape(1, D), w_gate, b_gate.reshape(1, N_BRANCH * D),
      w_br_ret.astype(BF16), w_br_moba.astype(BF16), w_br_mem.astype(BF16), w_out.astype(BF16),
      g_ffn.reshape(1, D), w_router.T, b_router.reshape(E, 1), tri)


def _segment_copies(seg_ref, lstart_ref, gstart_ref, tile, local_ref, slots_ref, sem, to_slots, fn):
    def per_expert(e, carry):
        idx = tile * N_EXPERTS + e
        size = seg_ref[idx]
        lstart = lstart_ref[idx]
        gstart = gstart_ref[idx]

        def copy_chunks(chunks, off):
            for chunk in chunks:
                take = (size & chunk) != 0
                lo = pl.ds(pl.multiple_of(lstart + off, SEG_ALIGN), chunk)
                gl = pl.ds(pl.multiple_of(gstart + off, SEG_ALIGN), chunk)
                src, dst = (local_ref.at[lo], slots_ref.at[gl]) if to_slots else (slots_ref.at[gl], local_ref.at[lo])

                @pl.when(take)
                def _():
                    fn(pltpu.make_async_copy(src, dst, sem))

                off = off + jnp.where(take, chunk, 0)

        large = size & -SEG_SMALL

        @pl.when(large != 0)
        def _():
            copy_chunks([c for c in SEG_CHUNKS if c >= SEG_SMALL], jnp.int32(0))

        copy_chunks([c for c in SEG_CHUNKS if c < SEG_SMALL], large)
        return carry

    lax.fori_loop(0, N_EXPERTS, per_expert, 0)


def _wait_tile(total_ref, tile, local_ref, slots_ref, sem, to_slots):
    total = total_ref[tile]
    for chunk in TOTAL_CHUNKS:
        lo, gl = local_ref.at[pl.ds(0, chunk)], slots_ref.at[pl.ds(0, chunk)]
        src, dst = (lo, gl) if to_slots else (gl, lo)

        @pl.when((total & chunk) != 0)
        def _():
            pltpu.make_async_copy(src, dst, sem).wait()


def _dispatch_kernel(seg_ref, lstart_ref, gstart_ref, total_ref, pad_end_ref, padded_ref, ld_ref, xt_ref, xb_ref,
                     ybuf_ref, zeros_ref, zsem, sems):
    T = zeros_ref.shape[0]
    i = pl.program_id(0)
    n = pl.num_programs(0)

    @pl.when(i == 0)
    def _():
        zeros_ref[...] = jnp.zeros_like(zeros_ref)

        def fill(start):
            return pltpu.make_async_copy(zeros_ref, xb_ref.at[pl.ds(pl.multiple_of(start, T), T)], zsem)

        for e in range(N_EXPERTS):
            @pl.when(padded_ref[e] > 0)
            def _():
                fill(pad_end_ref[e] - T).start()
        for e in range(N_EXPERTS):
            @pl.when(padded_ref[e] > 0)
            def _():
                fill(pad_end_ref[e] - T).wait()
        first_unused = pad_end_ref[N_EXPERTS - 1] // T
        n_blocks = xb_ref.shape[0] // T
        lax.fori_loop(first_unused, n_blocks, lambda b, c: (fill(b * T).start(), c)[1], 0)
        lax.fori_loop(first_unused, n_blocks, lambda b, c: (fill(b * T).wait(), c)[1], 0)

    U = ROUTE_TILES
    group = (i % 2) * U

    @pl.when(i >= 2)
    def _():
        for u in range(U):
            _wait_tile(total_ref, (i - 2) * U + u, ybuf_ref.at[group + u], xb_ref, sems.at[group + u], True)

    L = ybuf_ref.shape[1]
    RT = xt_ref.shape[0] // U
    r_iota = lax.broadcasted_iota(I32, (L, RT), 0)
    for u in range(U):
        ld = ld_ref[:, u * RT:(u + 1) * RT]
        onehot = jnp.zeros((L, RT), F32)
        for k in range(TOP_K):
            onehot = jnp.where(r_iota == ld[k:k + 1, :], 1.0, onehot)
        ybuf_ref[group + u] = _dot(onehot.astype(BF16), xt_ref[u * RT:(u + 1) * RT, :])
    for u in range(U):
        _segment_copies(seg_ref, lstart_ref, gstart_ref, i * U + u, ybuf_ref.at[group + u], xb_ref,
                        sems.at[group + u], True, lambda c: c.start())

    @pl.when(i == n - 1)
    def _():
        @pl.when(n >= 2)
        def _():
            for u in range(U):
                _wait_tile(total_ref, (i - 1) * U + u, ybuf_ref.at[U - group + u], xb_ref, sems.at[U - group + u], True)

        for u in range(U):
            _wait_tile(total_ref, i * U + u, ybuf_ref.at[group + u], xb_ref, sems.at[group + u], True)


def _dispatch(xt, ld, seg, lstart, gstart, totals, pad_ends, padded, R):
    N, D = xt.shape
    RT = ROUTE_T
    grid_spec = pltpu.PrefetchScalarGridSpec(
        num_scalar_prefetch=6,
        grid=(N // (RT * ROUTE_TILES),),
        in_specs=[
            pl.BlockSpec((TOP_K, RT * ROUTE_TILES), lambda i, *_: (0, i)),
            pl.BlockSpec((RT * ROUTE_TILES, D), lambda i, *_: (i, 0)),
        ],
        out_specs=pl.BlockSpec(memory_space=pl.ANY),
        scratch_shapes=[pltpu.VMEM((2 * ROUTE_TILES, LOCAL_ROWS, D), F32), pltpu.VMEM((MOE_T, D), F32),
                        pltpu.SemaphoreType.DMA(()), pltpu.SemaphoreType.DMA((2 * ROUTE_TILES,))],
    )
    return pl.pallas_call(
        _dispatch_kernel,
        grid_spec=grid_spec,
        out_shape=jax.ShapeDtypeStruct((R, D), F32),
        compiler_params=_cparams(("arbitrary",), VMEM_MEDIUM_MIB),
        name="dispatch",
    )(seg, lstart, gstart, totals, pad_ends, padded, ld, xt)


def _expert_kernel(start_ref, nblk_ref, b1_ref, b2_ref, perm_ref, w1_hbm, w2_hbm, xb_ref, yb_ref,
                   w1f_ref, w2f_ref, w1p_ref, w2b_ref, xbuf_ref, ybuf_ref, xodd_ref, yodd_ref,
                   w_sems, in_sems, out_sems, odd_sems):
    e = pl.program_id(0)
    n_experts = pl.num_programs(0)
    ws = e % 2

    def w_copies(ex, slot):
        return (pltpu.make_async_copy(w1_hbm.at[ex], w1f_ref.at[slot], w_sems.at[slot, 0]),
                pltpu.make_async_copy(w2_hbm.at[ex], w2f_ref.at[slot], w_sems.at[slot, 1]))
    T = xodd_ref.shape[0]
    G = 2 * LANES
    nb = nblk_ref[e]
    base = start_ref[e]
    pairs = nb // 2
    odd = nb % 2

    def pair_rows(p):
        return pl.ds(pl.multiple_of(base + p * (2 * T), T), 2 * T)

    def in_copy(p, slot):
        return pltpu.make_async_copy(xb_ref.at[pair_rows(p)], xbuf_ref.at[slot], in_sems.at[slot])

    def out_copy(p, slot):
        return pltpu.make_async_copy(ybuf_ref.at[slot], yb_ref.at[pair_rows(p)], out_sems.at[slot])

    odd_rows = pl.ds(pl.multiple_of(base + (nb - 1) * T, T), T)
    odd_in = pltpu.make_async_copy(xb_ref.at[odd_rows], xodd_ref, odd_sems.at[0])
    odd_out = pltpu.make_async_copy(yodd_ref, yb_ref.at[odd_rows], odd_sems.at[1])

    @pl.when(odd == 1)
    def _():
        odd_in.start()

    @pl.when(pairs > 0)
    def _():
        in_copy(0, 0).start()

    @pl.when(e == 0)
    def _():
        for c in w_copies(0, 0):
            c.start()

    for c in w_copies(e, ws):
        c.wait()

    @pl.when(e + 1 < n_experts)
    def _():
        for c in w_copies(e + 1, 1 - ws):
            c.start(priority=1)

    def mlp(x):
        x = x.astype(BF16)
        acts = []
        for g in range(w1p_ref.shape[1] // G):
            hg = _dot(x, w1p_ref[:, g * G:(g + 1) * G]) + b1_ref[0, :, g * G:(g + 1) * G]
            glu = jnp.minimum(hg[:, :LANES], SWIGLU_LIMIT)
            lin = jnp.clip(hg[:, LANES:], -SWIGLU_LIMIT, SWIGLU_LIMIT)
            acts.append((glu * jax.nn.sigmoid(SWIGLU_ALPHA * glu) * (lin + 1.0)).astype(BF16))
        return _dot(jnp.concatenate(acts, axis=-1), w2b_ref[...]) + b2_ref[0]

    @pl.when(nb > 0)
    def _():
        for g in range(w1p_ref.shape[1] // G):
            w = w1f_ref[ws, :, g * G:(g + 1) * G].astype(BF16)
            w1p_ref[:, g * G:(g + 1) * G] = _dot(w, perm_ref[...]).astype(BF16)
        w2b_ref[...] = w2f_ref[ws].astype(BF16)

        @pl.when(odd == 1)
        def _():
            odd_in.wait()
            yodd_ref[...] = mlp(xodd_ref[...])
            odd_out.start()

        def pair(p, carry):
            slot = p % 2
            in_copy(p, slot).wait()

            @pl.when(p + 1 < pairs)
            def _():
                in_copy(p + 1, 1 - slot).start()

            y = mlp(xbuf_ref[slot])

            @pl.when(p >= 2)
            def _():
                out_copy(p - 2, slot).wait()

            ybuf_ref[slot] = y
            out_copy(p, slot).start()
            return carry

        lax.fori_loop(0, pairs, pair, 0)

        @pl.when(pairs >= 2)
        def _():
            out_copy(pairs - 2, pairs % 2).wait()

        @pl.when(pairs >= 1)
        def _():
            out_copy(pairs - 1, (pairs - 1) % 2).wait()

        @pl.when(odd == 1)
        def _():
            odd_out.wait()


def _experts(xb, region_start, region_blocks, w1, b1p, w2, b2):
    R, D = xb.shape
    E, _, F2 = w1.shape
    F = F2 // 2
    T = MOE_T
    G = 2 * LANES
    c = np.arange(G)
    src = np.where(c < LANES, 2 * c, 2 * (c - LANES) + 1)
    perm = jnp.asarray(np.arange(G)[:, None] == src[None, :], dtype=BF16)
    per_expert = lambda r, w: pl.BlockSpec((1, r, w), lambda e, *_: (e, 0, 0))
    grid_spec = pltpu.PrefetchScalarGridSpec(
        num_scalar_prefetch=2,
        grid=(E,),
        in_specs=[
            per_expert(1, F2), per_expert(1, D),
            pl.BlockSpec((G, G), lambda e, *_: (0, 0)),
            pl.BlockSpec(memory_space=pl.ANY), pl.BlockSpec(memory_space=pl.ANY), pl.BlockSpec(memory_space=pl.ANY),
        ],
        out_specs=pl.BlockSpec(memory_space=pl.ANY),
        scratch_shapes=[pltpu.VMEM((2, D, F2), F32), pltpu.VMEM((2, F, D), F32),
                        pltpu.VMEM((D, F2), BF16), pltpu.VMEM((F, D), BF16),
                        pltpu.VMEM((2, 2 * T, D), F32), pltpu.VMEM((2, 2 * T, D), F32),
                        pltpu.VMEM((T, D), F32), pltpu.VMEM((T, D), F32),
                        pltpu.SemaphoreType.DMA((2, 2)), pltpu.SemaphoreType.DMA((2,)), pltpu.SemaphoreType.DMA((2,)),
                        pltpu.SemaphoreType.DMA((2,))],
    )
    return pl.pallas_call(
        _expert_kernel,
        grid_spec=grid_spec,
        out_shape=jax.ShapeDtypeStruct((R, D), F32),
        input_output_aliases={7: 0},
        compiler_params=_cparams(("arbitrary",), VMEM_LARGE_MIB),
        name="experts",
    )(region_start, region_blocks, b1p, b2.reshape(E, 1, D), perm, w1, w2, xb)


def _combine_kernel(seg_ref, lstart_ref, gstart_ref, total_ref, ldt_ref, wt_ref, x1_ref, yb_ref, o_ref, ybuf_ref,
                    sems):
    i = pl.program_id(0)
    n = pl.num_programs(0)
    U = ROUTE_TILES
    group = (i % 2) * U

    def fetch(step, g):
        for u in range(U):
            _segment_copies(seg_ref, lstart_ref, gstart_ref, step * U + u, ybuf_ref.at[g + u], yb_ref,
                            sems.at[g + u], False, lambda c: c.start())

    @pl.when(i == 0)
    def _():
        ybuf_ref[...] = jnp.zeros_like(ybuf_ref)
        fetch(i, group)

    @pl.when(i + 1 < n)
    def _():
        fetch(i + 1, U - group)

    for u in range(U):
        _wait_tile(total_ref, i * U + u, ybuf_ref.at[group + u], yb_ref, sems.at[group + u], False)

    L = ybuf_ref.shape[1]
    RT = x1_ref.shape[0] // U
    c_iota = lax.broadcasted_iota(I32, (RT, L), 1)
    for u in range(U):
        rows = slice(u * RT, (u + 1) * RT)
        ldt = ldt_ref[rows, :]
        w = wt_ref[rows, :]
        w_hi = w.astype(BF16).astype(F32)
        w_lo = w - w_hi
        g_hi = jnp.zeros((RT, L), F32)
        g_lo = jnp.zeros((RT, L), F32)
        for k in range(TOP_K):
            hit = c_iota == ldt[:, k:k + 1]
            g_hi = jnp.where(hit, w_hi[:, k:k + 1], g_hi)
            g_lo = jnp.where(hit, w_lo[:, k:k + 1], g_lo)
        y = ybuf_ref[group + u].astype(BF16)
        o_ref[rows, :] = x1_ref[rows, :] + (_dot(g_hi.astype(BF16), y) + _dot(g_lo.astype(BF16), y))


def _combine(ldt, wts_t, x1, yb, seg, lstart, gstart, totals):
    N, D = x1.shape
    RT = ROUTE_T
    grid_spec = pltpu.PrefetchScalarGridSpec(
        num_scalar_prefetch=4,
        grid=(N // (RT * ROUTE_TILES),),
        in_specs=[
            pl.BlockSpec((RT * ROUTE_TILES, TOP_K), lambda i, *_: (i, 0)),
            pl.BlockSpec((RT * ROUTE_TILES, TOP_K), lambda i, *_: (i, 0)),
            pl.BlockSpec((RT * ROUTE_TILES, D), lambda i, *_: (i, 0)),
            pl.BlockSpec(memory_space=pl.ANY),
        ],
        out_specs=pl.BlockSpec((RT * ROUTE_TILES, D), lambda i, *_: (i, 0)),
        scratch_shapes=[pltpu.VMEM((2 * ROUTE_TILES, LOCAL_ROWS, D), F32), pltpu.SemaphoreType.DMA((2 * ROUTE_TILES,))],
    )
    return pl.pallas_call(
        _combine_kernel,
        grid_spec=grid_spec,
        out_shape=jax.ShapeDtypeStruct((N, D), F32),
        compiler_params=_cparams(("arbitrary",), VMEM_MEDIUM_MIB),
        name="combine",
    )(seg, lstart, gstart, totals, ldt, wts_t, x1, yb)


def _layer(x, mem, g_mix, w_in, b_gate, g_ret_out, g_moba_q, g_moba_k, g_mem, w_mem_kv, g_mem_q, g_mem_k,
           w_br_ret, w_br_moba, w_br_mem, w_out, g_ffn, w_router, b_router, w_mlp1, b_mlp1, w_mlp2, b_mlp2):
    B, S, D = x.shape
    N = B * S
    x2 = x.reshape(N, D)
    slopes_np = np.exp2(-8.0 * (np.arange(MOBA_HEADS, dtype=np.float64) + 1.0) / MOBA_HEADS)
    assert all(float(np.log2(s)).is_integer() for s in slopes_np)
    slopes = tuple(float(s) for s in slopes_np)

    w_gate = w_in[:, MIX_W:].astype(BF16)
    kmem, vmem = _memkv(mem, g_mem, w_mem_kv, g_mem_k)
    rq, rk, rv, rg, mq, mk, mv, cq, km = _inproj(x2, B, S, g_mix, w_in, g_moba_q, g_moba_k, g_mem_q, slopes)
    o_ret = _retention(rq, rk, rv, rg, g_ret_out)
    o_moba = _moba(mq, mk, mv, km, jnp.asarray(slopes, F32)).reshape(N, MOBA_W)
    x1, xt, eidx, wts, rank, cnt = _merge(x2, S, o_ret, o_moba, cq, kmem, vmem, g_mix, w_gate, b_gate,
                                          w_br_ret, w_br_moba, w_br_mem, w_out, g_ffn, w_router, b_router)
    T = MOE_T
    tcnt = cnt[:, :, 0]
    seg = ((tcnt + SEG_ALIGN - 1) // SEG_ALIGN) * SEG_ALIGN
    region_rows = jnp.sum(seg, axis=0)
    padded = ((region_rows + T - 1) // T) * T
    pad_ends = jnp.cumsum(padded).astype(I32)
    pad_starts = pad_ends - padded
    gstart = pad_starts[None, :] + jnp.cumsum(seg, axis=0) - seg
    lstart = jnp.cumsum(seg, axis=1) - seg
    lstart_tok = jnp.broadcast_to(lstart[:, None, :], (N // ROUTE_T, ROUTE_T, N_EXPERTS)).reshape(N, N_EXPERTS)
    onehot = eidx[:, :, None] == jnp.arange(N_EXPERTS, dtype=I32)[None, None, :]
    ld = jnp.sum(jnp.where(onehot, lstart_tok[None], 0), axis=-1) + rank
    NB = -(-(N // ROUTE_T * LOCAL_ROWS) // T) + N_EXPERTS
    seg_f, lstart_f, gstart_f = (a.reshape(-1).astype(I32) for a in (seg, lstart, gstart))
    totals = jnp.sum(seg, axis=1).astype(I32)

    xb = _dispatch(xt, ld, seg_f, lstart_f, gstart_f, totals, pad_ends, padded.astype(I32), NB * T)
    F2 = w_mlp1.shape[-1]
    b1p = b_mlp1.reshape(N_EXPERTS, F2 // (2 * LANES), LANES, 2).transpose(0, 1, 3, 2).reshape(N_EXPERTS, 1, F2)
    yb = _experts(xb, pad_starts.astype(I32), (padded // T).astype(I32), w_mlp1, b1p, w_mlp2, b_mlp2)
    out = _combine(ld.T, wts.T, x1, yb, seg_f, lstart_f, gstart_f, totals)
    return out.reshape(B, S, D)


def kernel(x, mem, g_mix, w_in, b_gate, g_ret_out, g_moba_q, g_moba_k, g_mem, w_mem_kv, g_mem_q, g_mem_k, w_br_ret, w_br_moba, w_br_mem, w_out, g_ffn, w_router, b_router, w_mlp1, b_mlp1, w_mlp2, b_mlp2):
    for l in range(g_mix.shape[0]):
        x = _layer(x, mem, g_mix[l], w_in[l], b_gate[l], g_ret_out[l], g_moba_q[l], g_moba_k[l], g_mem[l],
                   w_mem_kv[l], g_mem_q[l], g_mem_k[l], w_br_ret[l], w_br_moba[l], w_br_mem[l], w_out[l],
                   g_ffn[l], w_router[l], b_router[l], w_mlp1[l], b_mlp1[l], w_mlp2[l], b_mlp2[l])
    return x
```
